```python
import math
import jax
import jax.numpy as jnp
from jax import lax
import numpy as np

D_MODEL = 1024
BATCH = 16
SEQ = 2048
DEPTH = 2

GRID_W = 64
CTX_LEN = 256
NORM_EPS = 1e-6
F32 = jnp.float32

DN_HEADS = 4
DN_DK = 128
DN_DV = 128
DN_CONV = 5
DN_CHUNK = 64
DN_QK_W = DN_HEADS * DN_DK
DN_V_W = DN_HEADS * DN_DV

S5_WIDTH = 512
S5_GROUP = 16
S5_GROUPS = S5_WIDTH // S5_GROUP
S5_STATE = 64

AB_SIZES = (DN_QK_W, DN_QK_W, DN_V_W, DN_V_W, 2 * DN_HEADS, 2 * DN_HEADS, S5_WIDTH)
AB_IN_W = sum(AB_SIZES)
MIX_W = DN_V_W + S5_WIDTH

MLA_HEADS = 8
MLA_Q_RANK = 384
MLA_KV_RANK = 256
MLA_NOPE = 128
MLA_ROPE = 64
MLA_V = 128
MLA_IN_W = MLA_Q_RANK + MLA_KV_RANK + MLA_ROPE
MLA_SCALE = (MLA_NOPE + MLA_ROPE) ** -0.5
ROPE_THETA = 10000.0
Q_BLOCK = 128

N_EXPERTS = 16
N_EXPERT_GROUPS = 4
EXPERTS_PER_GROUP = N_EXPERTS // N_EXPERT_GROUPS
TOP_K = 2
D_EXPERT = 512

kernel_name = 'hybrid_deltanet_s5_mla_moe_dit'


def rmsnorm(x, g):
    xf = x.astype(F32)
    y = xf * lax.rsqrt(jnp.mean(xf * xf, axis=-1, keepdims=True) + NORM_EPS)
    return (y * g.astype(F32)).astype(x.dtype)


def l2norm(x):
    xf = x.astype(F32)
    return xf * lax.rsqrt(jnp.sum(xf * xf, axis=-1, keepdims=True) + NORM_EPS)


def adaln(cond, w, b):
    m = jax.nn.silu(cond) @ w + b
    return [t[:, None, :] for t in jnp.split(m, 6, axis=-1)]


def modulate(x, g, shift, scale):
    return rmsnorm(x, g) * (1.0 + scale) + shift


def split_cols(x, sizes):
    offsets, acc = [], 0
    for s in sizes[:-1]:
        acc += s
        offsets.append(acc)
    return jnp.split(x, offsets, axis=-1)


def flip_t(t):
    return jnp.flip(t, axis=1)


def identity_t(t):
    return t


def short_conv(x, w):
    out = lax.conv_general_dilated(
        x, w[:, None, :].astype(x.dtype), window_strides=(1,),
        padding=[(DN_CONV // 2, DN_CONV // 2)],
        dimension_numbers=('NWC', 'WIO', 'NWC'), feature_group_count=x.shape[-1])
    return jax.nn.silu(out)


def dn_prepare(pq, pk, pv, pa, pb, conv_w, A_log, dt_bias):
    Bn, T = pq.shape[:2]
    qkv = short_conv(jnp.concatenate([pq, pk, pv], axis=-1), conv_w)
    q, k, v = jnp.split(qkv, [DN_QK_W, 2 * DN_QK_W], axis=-1)
    q = l2norm(q.reshape(Bn, T, DN_HEADS, DN_DK)) * (DN_DK ** -0.5)
    k = l2norm(k.reshape(Bn, T, DN_HEADS, DN_DK))
    v = v.reshape(Bn, T, DN_HEADS, DN_DV).astype(F32)
    a = pa.reshape(Bn, T, 2, DN_HEADS).astype(F32)
    beta = jax.nn.sigmoid(pb.reshape(Bn, T, 2, DN_HEADS).astype(F32))
    g = -jnp.exp(A_log.astype(F32)) * jax.nn.softplus(a + dt_bias.astype(F32))
    return q, k, v, g, beta


def chunk_gated_delta(q, k, v, g, beta, s0):
    Bn, T, H, K = q.shape
    Vd = v.shape[-1]
    n = T // DN_CHUNK

    def chunks(t):
        t = t.reshape((Bn, n, DN_CHUNK, H) + t.shape[3:])
        return jnp.moveaxis(jnp.moveaxis(t, 1, 0), 3, 2)

    q, k, v, g, beta = (chunks(t) for t in (q, k, v, g, beta))
    gc = jnp.cumsum(g, axis=-1)
    pos = jnp.arange(DN_CHUNK)
    incl = pos[:, None] >= pos[None, :]
    strict = pos[:, None] > pos[None, :]
    diff = gc[..., :, None] - gc[..., None, :]
    decay = jnp.where(incl, jnp.exp(jnp.where(incl, diff, 0.0)), 0.0)
    k_beta = k * beta[..., None]
    a_low = jnp.where(strict, jnp.einsum('nbhik,nbhjk->nbhij', k_beta, k) * decay, 0.0)
    lower = a_low + jnp.eye(DN_CHUNK, dtype=F32)
    rhs = jnp.concatenate([v * beta[..., None], k_beta * jnp.exp(gc)[..., None]], axis=-1)
    sol = lax.linalg.triangular_solve(lower, rhs, left_side=True, lower=True, unit_diagonal=True)
    u, w = sol[..., :Vd], sol[..., Vd:]
    qk = jnp.where(incl, jnp.einsum('nbhik,nbhjk->nbhij', q, k) * decay, 0.0)
    q_dec = q * jnp.exp(gc)[..., None]
    k_dec = k * jnp.exp(gc[..., -1:] - gc)[..., None]
    g_last = jnp.exp(gc[..., -1])

    def step(S, xs):
        u_i, w_i, qk_i, qd_i, kd_i, gl_i = xs
        v_new = u_i - jnp.einsum('bhck,bhkv->bhcv', w_i, S)
        o_i = jnp.einsum('bhck,bhkv->bhcv', qd_i, S) + jnp.einsum('bhij,bhjv->bhiv', qk_i, v_new)
        S = S * gl_i[..., None, None] + jnp.einsum('bhck,bhcv->bhkv', kd_i, v_new)
        return S, o_i

    S, o = lax.scan(step, s0, (u, w, qk, q_dec, k_dec, g_last))
    o = jnp.moveaxis(jnp.moveaxis(o, 2, 3), 0, 1).reshape(Bn, T, H, Vd)
    return o, S


def bidir_delta(ctx_in, lat_in):
    qc, kc, vc, gc, bc = ctx_in
    ql, kl, vl, gl, bl = lat_in
    s0 = jnp.zeros((qc.shape[0], DN_HEADS, DN_DK, DN_DV), F32)
    outs = []
    for d in range(2):
        orient = identity_t if d == 0 else flip_t
        o_c, s_c = chunk_gated_delta(orient(qc), orient(kc), orient(vc),
                                     orient(gc[:, :, d]), orient(bc[:, :, d]), s0)
        o_l, _ = chunk_gated_delta(orient(ql), orient(kl), orient(vl),
                                   orient(gl[:, :, d]), orient(bl[:, :, d]), s_c)
        outs.append((orient(o_c), orient(o_l)))
    return outs[0][0] + outs[1][0], outs[0][1] + outs[1][1]


def dn_output(o, z, norm_g):
    Bn, T = o.shape[:2]
    gated = rmsnorm(o, norm_g) * jax.nn.silu(z.reshape(Bn, T, DN_HEADS, DN_DV).astype(F32))
    return gated.reshape(Bn, T, DN_V_W)


def cmul(ar, ai, br, bi):
    return ar * br - ai * bi, ar * bi + ai * br


def s5_discretise(A_re, A_im, log_dt):
    A_re, A_im = A_re.astype(F32), A_im.astype(F32)
    dt = jnp.exp(log_dt.astype(F32))[..., None]
    mag = jnp.exp(A_re * dt)
    lam_re, lam_im = mag * jnp.cos(A_im * dt), mag * jnp.sin(A_im * dt)
    den = A_re * A_re + A_im * A_im
    nr, ni = lam_re - 1.0, lam_im
    coef_re = (nr * A_re + ni * A_im) / den
    coef_im = (ni * A_re - nr * A_im) / den
    return lam_re, lam_im, coef_re, coef_im


def s5_scan(b_re, b_im, lam_re, lam_im, h_re, h_im):
    hr, hi = cmul(lam_re, lam_im, h_re, h_im)
    b_re = b_re.at[:, 0].add(hr)
    b_im = b_im.at[:, 0].add(hi)
    T = b_re.shape[1]
    a_re = jnp.broadcast_to(lam_re, (1, T) + lam_re.shape)
    a_im = jnp.broadcast_to(lam_im, (1, T) + lam_im.shape)

    def combine(e1, e2):
        a1r, a1i, b1r, b1i = e1
        a2r, a2i, b2r, b2i = e2
        ar, ai = cmul(a2r, a2i, a1r, a1i)
        br, bi = cmul(a2r, a2i, b1r, b1i)
        return ar, ai, br + b2r, bi + b2i

    _, _, x_re, x_im = lax.associative_scan(combine, (a_re, a_im, b_re, b_im), axis=1)
    return x_re, x_im


def s5_states(u_c, u_l, A_re, A_im, log_dt, B_re, B_im):
    lam_re, lam_im, coef_re, coef_im = s5_discretise(A_re, A_im, log_dt)
    B_re, B_im = B_re.astype(F32), B_im.astype(F32)

    def drive(u):
        ug = u.reshape(u.shape[:2] + (S5_GROUPS, S5_GROUP)).astype(F32)
        return jnp.einsum('btgh,gph->btgp', ug, B_re), jnp.einsum('btgh,gph->btgp', ug, B_im)

    bc_re, bc_im = drive(u_c)
    bl_re, bl_im = drive(u_l)
    h0 = jnp.zeros((u_c.shape[0], S5_GROUPS, S5_STATE), F32)
    per_dir = []
    for d in range(2):
        orient = identity_t if d == 0 else flip_t
        dc = cmul(coef_re[d], coef_im[d], orient(bc_re), orient(bc_im))
        xcr, xci = s5_scan(dc[0], dc[1], lam_re[d], lam_im[d], h0, h0)
        dl = cmul(coef_re[d], coef_im[d], orient(bl_re), orient(bl_im))
        xlr, xli = s5_scan(dl[0], dl[1], lam_re[d], lam_im[d], xcr[:, -1], xci[:, -1])
        per_dir.append((orient(xcr), orient(xci), orient(xlr), orient(xli)))
    xc_re, xc_im, xl_re, xl_im = (a + b for a, b in zip(per_dir[0], per_dir[1]))
    return xc_re, xc_im, xl_re, xl_im


def s5_readout(u, x_re, x_im, C_re, C_im, D_skip, glu_w, glu_b):
    Bn, T = u.shape[:2]
    ug = u.reshape(Bn, T, S5_GROUPS, S5_GROUP).astype(F32)
    y = (jnp.einsum('btgp,ghp->btgh', x_re, C_re.astype(F32))
         - jnp.einsum('btgp,ghp->btgh', x_im, C_im.astype(F32))
         + D_skip.astype(F32) * ug)
    y = jax.nn.gelu(y.reshape(Bn, T, S5_WIDTH))
    return y * jax.nn.sigmoid(y @ glu_w.astype(F32) + glu_b.astype(F32))


def mixer_ab(h_ctx, h_lat, w_in, conv_w, A_log, dt_bias, dn_norm_g, A_re, A_im, log_dt,
             B_re, B_im, C_re, C_im, D_skip, glu_w, glu_b, w_out, need_ctx):
    pc = split_cols(h_ctx @ w_in, AB_SIZES)
    pl = split_cols(h_lat @ w_in, AB_SIZES)
    dc = dn_prepare(pc[0], pc[1], pc[2], pc[4], pc[5], conv_w, A_log, dt_bias)
    dl = dn_prepare(pl[0], pl[1], pl[2], pl[4], pl[5], conv_w, A_log, dt_bias)
    o_dn_c, o_dn_l = bidir_delta(dc, dl)
    xc_re, xc_im, xl_re, xl_im = s5_states(pc[6], pl[6], A_re, A_im, log_dt, B_re, B_im)

    def merge(o_dn, z, u, x_re, x_im, h):
        a_out = dn_output(o_dn, z, dn_norm_g)
        b_out = s5_readout(u, x_re, x_im, C_re, C_im, D_skip, glu_w, glu_b)
        return jnp.concatenate([a_out, b_out], axis=-1).astype(h.dtype) @ w_out

    o_lat = merge(o_dn_l, pl[3], pl[6], xl_re, xl_im, h_lat)
    o_ctx = merge(o_dn_c, pc[3], pc[6], xc_re, xc_im, h_ctx) if need_ctx else None
    return o_ctx, o_lat


def axial_rope(T):
    rows = T // GRID_W
    row = jnp.repeat(jnp.arange(rows, dtype=F32), GRID_W)
    col = jnp.tile(jnp.arange(GRID_W, dtype=F32), rows)
    n_freq = MLA_ROPE // 4
    inv = ROPE_THETA ** (-jnp.arange(n_freq, dtype=F32) / n_freq)
    ang = jnp.concatenate([row[:, None] * inv, col[:, None] * inv], axis=-1)
    return jnp.cos(ang), jnp.sin(ang)


def apply_rope(x, cos, sin):
    shape = (1, x.shape[1]) + (1,) * (x.ndim - 3) + (cos.shape[-1],)
    cos, sin = cos.reshape(shape), sin.reshape(shape)
    x1, x2 = jnp.split(x.astype(F32), 2, axis=-1)
    return jnp.concatenate([x1 * cos - x2 * sin, x1 * sin + x2 * cos], axis=-1).astype(x.dtype)


def softmax_attend(q, k, v):
    s = jnp.einsum('bqhd,bkhd->bhqk', q, k).astype(F32) * MLA_SCALE
    p = jax.nn.softmax(s, axis=-1).astype(v.dtype)
    return jnp.einsum('bhqk,bkhd->bqhd', p, v)


def mixer_mla(h_ctx, h_lat, w_in, q_norm_g, w_q_up, kv_norm_g, w_kv_up, w_out, cos, sin, need_ctx):
    def queries(cq, rope):
        Bn, T = cq.shape[:2]
        q = (rmsnorm(cq, q_norm_g) @ w_q_up).reshape(Bn, T, MLA_HEADS, MLA_NOPE + MLA_ROPE)
        q_rope = apply_rope(q[..., MLA_NOPE:], cos, sin) if rope else q[..., MLA_NOPE:]
        return jnp.concatenate([q[..., :MLA_NOPE], q_rope], axis=-1)

    def keys_values(ckv, k_rope, rope):
        Bn, T = ckv.shape[:2]
        kv = (rmsnorm(ckv, kv_norm_g) @ w_kv_up).reshape(Bn, T, MLA_HEADS, MLA_NOPE + MLA_V)
        if rope:
            k_rope = apply_rope(k_rope, cos, sin)
        k_rope = jnp.broadcast_to(k_rope[:, :, None, :], (Bn, T, MLA_HEADS, MLA_ROPE))
        return jnp.concatenate([kv[..., :MLA_NOPE], k_rope], axis=-1), kv[..., MLA_NOPE:]

    cq_c, ckv_c, kr_c = split_cols(h_ctx @ w_in, (MLA_Q_RANK, MLA_KV_RANK, MLA_ROPE))
    cq_l, ckv_l, kr_l = split_cols(h_lat @ w_in, (MLA_Q_RANK, MLA_KV_RANK, MLA_ROPE))
    k_c, v_c = keys_values(ckv_c, kr_c, False)
    k_l, v_l = keys_values(ckv_l, kr_l, True)
    q_l = queries(cq_l, True)
    k_all = jnp.concatenate([k_c, k_l], axis=1)
    v_all = jnp.concatenate([v_c, v_l], axis=1)
    Bn, T = q_l.shape[:2]
    nb = T // Q_BLOCK
    q_blocks = jnp.moveaxis(q_l.reshape(Bn, nb, Q_BLOCK, MLA_HEADS, MLA_NOPE + MLA_ROPE), 1, 0)
    o_blocks = lax.map(lambda qb: softmax_attend(qb, k_all, v_all), q_blocks)
    o_l = jnp.moveaxis(o_blocks, 0, 1).reshape(Bn, T, MLA_HEADS * MLA_V) @ w_out
    o_c = None
    if need_ctx:
        q_c = queries(cq_c, False)
        o_c = softmax_attend(q_c, k_c, v_c).reshape(Bn, -1, MLA_HEADS * MLA_V) @ w_out
    return o_c, o_l


def grouped_moe(h, router_w, router_bias, w_gate, w_up, w_down):
    tok = h.reshape(-1, h.shape[-1])
    scores = jax.nn.sigmoid((tok @ router_w).astype(F32))
    choice = scores + router_bias.astype(F32)
    grouped = choice.reshape(-1, N_EXPERT_GROUPS, EXPERTS_PER_GROUP)
    group_score = jnp.sum(lax.top_k(grouped, 2)[0], axis=-1)
    best_group = jnp.argmax(group_score, axis=-1)
    in_group = (jnp.arange(N_EXPERTS) // EXPERTS_PER_GROUP)[None, :] == best_group[:, None]
    _, idx = lax.top_k(jnp.where(in_group, choice, -jnp.inf), TOP_K)
    wts = jnp.take_along_axis(scores, idx, axis=-1)
    wts = wts / jnp.sum(wts, axis=-1, keepdims=True)
    gates = jnp.sum(jax.nn.one_hot(idx, N_EXPERTS, dtype=F32) * wts[..., None], axis=1)
    out = jnp.zeros(tok.shape, F32)
    for e in range(N_EXPERTS):
        act = jax.nn.silu(tok @ w_gate[e]) * (tok @ w_up[e])
        out = out + gates[:, e:e + 1] * (act @ w_down[e])
    return out.reshape(h.shape).astype(h.dtype)


def setup_inputs(seed: int = 0) -> dict:
    key = jax.random.key(seed)
    keys = iter(jax.random.split(key, 40))

    def normal(shape, scale):
        return scale * jax.random.normal(next(keys), shape, F32)

    def gain(shape):
        return 1.0 + normal(shape, 0.05)

    def log_uniform(shape, lo, hi):
        return jax.random.uniform(next(keys), shape, F32, math.log(lo), math.log(hi))

    NE, NO = (DEPTH + 1) // 2, DEPTH // 2
    D = D_MODEL
    dn_dt = jnp.exp(log_uniform((NE, 2, DN_HEADS), 1e-3, 1e-1))
    state_idx = jnp.arange(S5_STATE, dtype=F32)
    return {
        'x': normal((BATCH, SEQ, D), 1.0),
        'c': normal((BATCH, D), 1.0),
        'ctx': normal((BATCH, CTX_LEN, D), 1.0),
        'c_ctx': normal((D,), 1.0),
        'ada_w': normal((DEPTH, D, 6 * D), 0.5 * D ** -0.5),
        'ada_b': normal((DEPTH, 6 * D), 0.02),
        'norm1_g': gain((DEPTH, D)),
        'norm2_g': gain((DEPTH, D)),
        'ab_w_in': normal((NE, D, AB_IN_W), D ** -0.5),
        'dn_conv_w': normal((NE, DN_CONV, 2 * DN_QK_W + DN_V_W), DN_CONV ** -0.5),
        'dn_A_log': jnp.log(jax.random.uniform(next(keys), (NE, 2, DN_HEADS), F32, 1.0, 16.0)),
        'dn_dt_bias': dn_dt + jnp.log(-jnp.expm1(-dn_dt)),
        'dn_norm_g': gain((NE, DN_DV)),
        's5_A_re': -0.5 + normal((NE, 2, S5_GROUPS, S5_STATE), 0.01),
        's5_A_im': math.pi * state_idx + normal((NE, 2, S5_GROUPS, S5_STATE), 0.01),
        's5_log_dt': log_uniform((NE, 2, S5_GROUPS), 1e-3, 1e-1),
        's5_B_re': normal((NE, S5_GROUPS, S5_STATE, S5_GROUP), (2 * S5_GROUP) ** -0.5),
        's5_B_im': normal((NE, S5_GROUPS, S5_STATE, S5_GROUP), (2 * S5_GROUP) ** -0.5),
        's5_C_re': normal((NE, S5_GROUPS, S5_GROUP, S5_STATE), (2 * S5_STATE) ** -0.5),
        's5_C_im': normal((NE, S5_GROUPS, S5_GROUP, S5_STATE), (2 * S5_STATE) ** -0.5),
        's5_D': normal((NE, S5_GROUPS, S5_GROUP), 0.5),
        's5_glu_w': normal((NE, S5_WIDTH, S5_WIDTH), S5_WIDTH ** -0.5),
        's5_glu_b': normal((NE, S5_WIDTH), 0.01),
        'ab_w_out': normal((NE, MIX_W, D), MIX_W ** -0.5),
        'mla_w_in': normal((NO, D, MLA_IN_W), D ** -0.5),
        'mla_q_norm_g': gain((NO, MLA_Q_RANK)),
        'mla_w_q_up': normal((NO, MLA_Q_RANK, MLA_HEADS * (MLA_NOPE + MLA_ROPE)), MLA_Q_RANK ** -0.5),
        'mla_kv_norm_g': gain((NO, MLA_KV_RANK)),
        'mla_w_kv_up': normal((NO, MLA_KV_RANK, MLA_HEADS * (MLA_NOPE + MLA_V)), MLA_KV_RANK ** -0.5),
        'mla_w_out': normal((NO, MLA_HEADS * MLA_V, D), (MLA_HEADS * MLA_V) ** -0.5),
        'router_w': normal((D, N_EXPERTS), D ** -0.5),
        'router_bias': normal((N_EXPERTS,), 0.01),
        'moe_w_gate': normal((DEPTH, N_EXPERTS, D, D_EXPERT), D ** -0.5),
        'moe_w_up': normal((DEPTH, N_EXPERTS, D, D_EXPERT), D ** -0.5),
        'moe_w_down': normal((DEPTH, N_EXPERTS, D_EXPERT, D), D_EXPERT ** -0.5),
        'final_norm_g': gain((D,)),
    }


def reference(x, c, ctx, c_ctx, ada_w, ada_b, norm1_g, norm2_g, ab_w_in, dn_conv_w, dn_A_log,
              dn_dt_bias, dn_norm_g, s5_A_re, s5_A_im, s5_log_dt, s5_B_re, s5_B_im, s5_C_re,
              s5_C_im, s5_D, s5_glu_w, s5_glu_b, ab_w_out, mla_w_in, mla_q_norm_g, mla_w_q_up,
              mla_kv_norm_g, mla_w_kv_up, mla_w_out, router_w, router_bias, moe_w_gate, moe_w_up,
              moe_w_down, final_norm_g):
    T = x.shape[1]
    cos, sin = axial_rope(T)
    cond_ctx = c_ctx[None, :]
    for i in range(DEPTH):
        last = i == DEPTH - 1
        j = i // 2
        sh1, sc1, g1, sh2, sc2, g2 = adaln(c, ada_w[i], ada_b[i])
        sh1c, sc1c, g1c, sh2c, sc2c, g2c = adaln(cond_ctx, ada_w[i], ada_b[i])
        h_l = modulate(x, norm1_g[i], sh1, sc1)
        h_c = modulate(ctx, norm1_g[i], sh1c, sc1c)
        if i % 2 == 0:
            o_c, o_l = mixer_ab(h_c, h_l, ab_w_in[j], dn_conv_w[j], dn_A_log[j], dn_dt_bias[j],
                                dn_norm_g[j], s5_A_re[j], s5_A_im[j], s5_log_dt[j], s5_B_re[j],
                                s5_B_im[j], s5_C_re[j], s5_C_im[j], s5_D[j], s5_glu_w[j],
                                s5_glu_b[j], ab_w_out[j], not last)
        else:
            o_c, o_l = mixer_mla(h_c, h_l, mla_w_in[j], mla_q_norm_g[j], mla_w_q_up[j],
                                 mla_kv_norm_g[j], mla_w_kv_up[j], mla_w_out[j], cos, sin, not last)
        x = x + g1 * o_l
        x = x + g2 * grouped_moe(modulate(x, norm2_g[i], sh2, sc2), router_w, router_bias,
                                 moe_w_gate[i], moe_w_up[i], moe_w_down[i])
        if not last:
            ctx = ctx + g1c * o_c
            ctx = ctx + g2c * grouped_moe(modulate(ctx, norm2_g[i], sh2c, sc2c), router_w, router_bias,
                                          moe_w_gate[i], moe_w_up[i], moe_w_down[i])
    return rmsnorm(x, final_norm_g)
```

```python
import functools
import math

import jax
import jax.numpy as jnp
from jax import lax
from jax.experimental import pallas as pl
from jax.experimental.pallas import tpu as pltpu

F32 = jnp.float32
BF16 = jnp.bfloat16
HIGHEST = lax.Precision.HIGHEST

NORM_EPS = 1e-6
GRID_W = 64
ROPE_THETA = 10000.0

DN_HEADS = 4
DN_DK = 128
DN_DV = 128
DN_CONV = 5
DN_CHUNK = 64
DN_W = DN_HEADS * DN_DK

S5_WIDTH = 512
S5_GROUP = 16
S5_GROUPS = 32
S5_STATE = 64
S5_NSTATE = S5_GROUPS * S5_STATE
S5_GBLK = 8
S5_NBLK = S5_GROUPS // S5_GBLK
S5_SBLK = S5_GBLK * S5_STATE

MLA_HEADS = 8
MLA_Q_RANK = 384
MLA_KV_RANK = 256
MLA_NOPE = 128
MLA_ROPE = 64
MLA_V = 128
MLA_SCALE = (MLA_NOPE + MLA_ROPE) ** -0.5

N_EXPERTS = 16
N_EXPERT_GROUPS = 4
EXPERTS_PER_GROUP = 4
D_EXPERT = 512

LANES = 128
VMEM_LIMIT_BYTES = 56 * 1024 * 1024


def _tile(n, pref):
    t = min(pref, n)
    while n % t or t % 8:
        t -= 1
    return t


def _cparams(*sem):
    return pltpu.CompilerParams(dimension_semantics=sem, vmem_limit_bytes=VMEM_LIMIT_BYTES)


def _silu(x):
    return x * jax.nn.sigmoid(x)


def _softplus(x):
    return jnp.maximum(x, 0.0) + jnp.log(1.0 + jnp.exp(-jnp.abs(x)))


def _gelu_tanh(x):
    return 0.5 * x * (1.0 + jnp.tanh(math.sqrt(2.0 / math.pi) * (x + 0.044715 * (x * x * x))))


def _rms(x):
    return x * lax.rsqrt(jnp.mean(x * x, axis=-1, keepdims=True) + NORM_EPS)


def _dot(a, b):
    return jnp.dot(a, b, preferred_element_type=F32)


def _dot_nt(a, b, precision=None):
    return lax.dot_general(a, b, (((1,), (1,)), ((), ())), preferred_element_type=F32,
                           precision=precision)


def _dot_tn(a, b):
    return lax.dot_general(a, b, (((0,), (0,)), ((), ())), preferred_element_type=F32)


def _ada_body(c_ref, w_ref, b_ref, o_ref):
    c = c_ref[...]
    o_ref[0] = _dot(_silu(c).astype(BF16), w_ref[0].astype(BF16)) + b_ref[0]


def _adaln_all(cond, ada_w, ada_b):
    L, D, D6 = ada_w.shape
    R = cond.shape[0]
    tn = 1536
    return pl.pallas_call(
        _ada_body,
        grid=(L, D6 // tn),
        in_specs=[pl.BlockSpec((R, D), lambda l, j: (0, 0)),
                  pl.BlockSpec((1, D, tn), lambda l, j: (l, 0, j)),
                  pl.BlockSpec((1, 1, tn), lambda l, j: (l, 0, j))],
        out_specs=pl.BlockSpec((1, R, tn), lambda l, j: (l, 0, j)),
        out_shape=jax.ShapeDtypeStruct((L, R, D6), F32),
        compiler_params=_cparams("parallel", "parallel"),
        name="adaln",
    )(cond, ada_w, ada_b.reshape(L, 1, D6))


def _modmm_body(x_ref, mod_ref, g_ref, *refs, n_out, shift_row):
    w_refs, o_refs = refs[:n_out], refs[n_out:]
    h = _rms(x_ref[0]) * g_ref[...]
    h = h * (1.0 + mod_ref[0, shift_row + 1:shift_row + 2, :]) + mod_ref[0, shift_row:shift_row + 1, :]
    hb = h.astype(BF16)
    for w_ref, o_ref in zip(w_refs, o_refs):
        r = _dot(hb, w_ref[...]).astype(o_ref.dtype)
        if len(o_ref.shape) == 3:
            o_ref[0] = r
        else:
            o_ref[...] = r


def _modmm(x, mod, g, ws, out_dtypes, tb_layout, shift_row, tm):
    Bx, T, D = x.shape
    tm = _tile(T, tm)
    per_batch_mod = mod.shape[0] != 1
    in_specs = [pl.BlockSpec((1, tm, D), lambda b, i: (b, i, 0)),
                pl.BlockSpec((1, 6, D), (lambda b, i: (b, 0, 0)) if per_batch_mod else (lambda b, i: (0, 0, 0))),
                pl.BlockSpec((1, D), lambda b, i: (0, 0))]
    out_specs, out_shape = [], []
    for w, dt, tb in zip(ws, out_dtypes, tb_layout):
        n = w.shape[1]
        in_specs.append(pl.BlockSpec((D, n), lambda b, i: (0, 0)))
        if tb:
            out_specs.append(pl.BlockSpec((tm, n), lambda b, i: (i, b)))
            out_shape.append(jax.ShapeDtypeStruct((T, Bx * n), dt))
        else:
            out_specs.append(pl.BlockSpec((1, tm, n), lambda b, i: (b, i, 0)))
            out_shape.append(jax.ShapeDtypeStruct((Bx, T, n), dt))
    return pl.pallas_call(
        functools.partial(_modmm_body, n_out=len(ws), shift_row=shift_row),
        grid=(Bx, T // tm),
        in_specs=in_specs, out_specs=out_specs, out_shape=out_shape,
        compiler_params=_cparams("parallel", "parallel"),
        name="modmm",
    )(x, mod, g.reshape(1, D), *ws)


def _dn_prep_body(x_ref, xp_ref, xn_ref, w_ref, o_ref, buf, *, tm, nt):
    i = pl.program_id(1)
    j = pl.program_id(2)
    buf[0:8, :] = jnp.where(i == 0, 0.0, xp_ref[0])
    buf[8:8 + tm, :] = x_ref[0]
    buf[8 + tm:16 + tm, :] = jnp.where(i == nt - 1, 0.0, xn_ref[0])
    acc = buf[pl.ds(8 - DN_CONV // 2, tm), :] * w_ref[0:1, :]
    for kk in range(1, DN_CONV):
        acc = acc + buf[pl.ds(8 - DN_CONV // 2 + kk, tm), :] * w_ref[kk:kk + 1, :]
    y = _silu(acc)
    q_scale = jnp.where(j == 0, DN_DK ** -0.5, 1.0)
    for h in range(DN_HEADS):
        yh = y[:, h * DN_DK:(h + 1) * DN_DK]
        r = lax.rsqrt(jnp.sum(yh * yh, axis=-1, keepdims=True) + NORM_EPS) * q_scale
        o_ref[0, :, h * DN_DK:(h + 1) * DN_DK] = yh * jnp.where(j == 2, 1.0, r)


def _dn_prep(qkv, conv_w, tm):
    B, T, W3 = qkv.shape
    tm = _tile(T, tm)
    nt = T // tm
    r8 = tm // 8
    return pl.pallas_call(
        functools.partial(_dn_prep_body, tm=tm, nt=nt),
        grid=(B, nt, 3),
        in_specs=[pl.BlockSpec((1, tm, DN_W), lambda b, i, j: (b, i, j)),
                  pl.BlockSpec((1, 8, DN_W), lambda b, i, j: (b, jnp.maximum(i * r8 - 1, 0), j)),
                  pl.BlockSpec((1, 8, DN_W), lambda b, i, j: (b, jnp.minimum((i + 1) * r8, T // 8 - 1), j)),
                  pl.BlockSpec((DN_CONV, DN_W), lambda b, i, j: (0, j))],
        out_specs=pl.BlockSpec((1, tm, DN_W), lambda b, i, j: (b, i, j)),
        out_shape=jax.ShapeDtypeStruct((B, T, W3), F32),
        scratch_shapes=[pltpu.VMEM((tm + 16, DN_W), F32)],
        compiler_params=_cparams("parallel", "parallel", "parallel"),
        name="dn_prep",
    )(qkv, qkv, qkv, conv_w)


def _dn_scan_body(q_ref, k_ref, v_ref, ab_ref, alog_ref, dtb_ref, s0_ref, o_ref, sout_ref, S_ref,
                  *, bb, C, nch):
    d = pl.program_id(0)
    c = pl.program_id(2)

    @pl.when(c == 0)
    def _():
        S_ref[...] = s0_ref[...]

    fwd = d == 0
    row = lax.broadcasted_iota(jnp.int32, (C, C), 0)
    col = lax.broadcasted_iota(jnp.int32, (C, C), 1)
    order = (row - col) * (1 - 2 * d)
    incl = order >= 0
    strict = order > 0
    cum_mat = incl.astype(F32)
    eye = (row == col).astype(F32)
    lane = lax.broadcasted_iota(jnp.int32, (C, LANES), 1)
    neg_a = -jnp.exp(alog_ref[...])
    n_double = int(math.log2(C)) - 1

    for b in range(bb):
        ab = ab_ref[b]
        g_all = neg_a * _softplus(ab + dtb_ref[...])
        beta_all = jax.nn.sigmoid(ab)
        gc_all = jnp.dot(cum_mat, g_all, preferred_element_type=F32, precision=HIGHEST)
        gtot_all = jnp.sum(g_all, axis=0, keepdims=True)
        for h in range(DN_HEADS):
            gc = jnp.where(fwd, gc_all[:, h:h + 1], gc_all[:, DN_HEADS + h:DN_HEADS + h + 1])
            gtot = jnp.where(fwd, gtot_all[:, h:h + 1], gtot_all[:, DN_HEADS + h:DN_HEADS + h + 1])
            beta = jnp.where(fwd, beta_all[:, 2 * DN_HEADS + h:2 * DN_HEADS + h + 1],
                             beta_all[:, 3 * DN_HEADS + h:3 * DN_HEADS + h + 1])
            lhs = jnp.where(lane == 0, gc, jnp.where(lane == 1, 1.0, 0.0))
            rhs = jnp.where(lane == 0, 1.0, jnp.where(lane == 1, -gc, 0.0))
            diff = _dot_nt(lhs, rhs, precision=HIGHEST)
            decay = jnp.where(incl, jnp.exp(jnp.where(incl, diff, 0.0)), 0.0)
            egc = jnp.exp(gc)
            sl = slice(h * DN_DK, (h + 1) * DN_DK)
            qh = q_ref[b, :, sl]
            kh = k_ref[b, :, sl]
            vh = v_ref[b, :, sl]
            kb = kh * beta
            khb = kh.astype(BF16)
            a_low = jnp.where(strict, _dot_nt(kb.astype(BF16), khb) * decay, 0.0)
            pw = -a_low
            inv = eye + pw
            for _ in range(n_double):
                pwb = pw.astype(BF16)
                pw = _dot(pwb, pwb)
                inv = inv + _dot(inv.astype(BF16), pw.astype(BF16))
            rhs_uw = jnp.concatenate([vh * beta, kb * egc], axis=1).astype(BF16)
            sol = _dot(inv.astype(BF16), rhs_uw)
            u = sol[:, :DN_DV]
            w = sol[:, DN_DV:]
            qk = jnp.where(incl, _dot_nt(qh.astype(BF16), khb) * decay, 0.0)
            S = S_ref[b, h]
            Sb = S.astype(BF16)
            v_new = u - _dot(w.astype(BF16), Sb)
            v_newb = v_new.astype(BF16)
            o = _dot((qh * egc).astype(BF16), Sb) + _dot(qk.astype(BF16), v_newb)
            k_dec = kh * jnp.exp(gtot - gc)
            S_ref[b, h] = S * jnp.exp(gtot) + _dot_tn(k_dec.astype(BF16), v_newb)
            o_ref[0, b, :, sl] = o

    @pl.when(c == nch - 1)
    def _():
        sout_ref[...] = S_ref[...]


def _dn_scan(qkv, ab, a_log, dt_bias, s0, bb):
    B, T, _ = qkv.shape
    C = DN_CHUNK
    nch = T // C
    pad = jnp.zeros((1, LANES), F32)
    alog_row = pad.at[0, :2 * DN_HEADS].set(a_log.reshape(-1))
    dtb_row = pad.at[0, :2 * DN_HEADS].set(dt_bias.reshape(-1))

    def cidx(d, c):
        return c + d * (nch - 1 - 2 * c)

    def qkv_spec(j):
        return pl.BlockSpec((bb, C, DN_W), lambda d, b, c: (b, cidx(d, c), j))

    s_spec = pl.BlockSpec((None, bb, DN_HEADS, DN_DK, DN_DV), lambda d, b, c: (d, b, 0, 0, 0))
    return pl.pallas_call(
        functools.partial(_dn_scan_body, bb=bb, C=C, nch=nch),
        grid=(2, B // bb, nch),
        in_specs=[qkv_spec(0), qkv_spec(1), qkv_spec(2),
                  pl.BlockSpec((bb, C, LANES), lambda d, b, c: (b, cidx(d, c), 0)),
                  pl.BlockSpec((1, LANES), lambda d, b, c: (0, 0)),
                  pl.BlockSpec((1, LANES), lambda d, b, c: (0, 0)),
                  s_spec],
        out_specs=[pl.BlockSpec((1, bb, C, DN_W), lambda d, b, c: (d, b, cidx(d, c), 0)), s_spec],
        out_shape=[jax.ShapeDtypeStruct((2, B, T, DN_W), F32),
                   jax.ShapeDtypeStruct((2, B, DN_HEADS, DN_DK, DN_DV), F32)],
        scratch_shapes=[pltpu.VMEM((bb, DN_HEADS, DN_DK, DN_DV), F32)],
        compiler_params=_cparams("parallel", "parallel", "arbitrary"),
        name="dn_scan",
    )(qkv, qkv, qkv, ab, alog_row, dtb_row, s0)


def _s5_param_body(are_ref, aim_ref, ldt_ref, bre_ref, bim_ref, lam_ref, bd_ref):
    a_re = are_ref[...]
    a_im = aim_ref[...]
    dt = jnp.exp(ldt_ref[...])
    mag = jnp.exp(a_re * dt)
    lam_re = mag * jnp.cos(a_im * dt)
    lam_im = mag * jnp.sin(a_im * dt)
    den = a_re * a_re + a_im * a_im
    nr = lam_re - 1.0
    ni = lam_im
    coef_re = (nr * a_re + ni * a_im) / den
    coef_im = (ni * a_re - nr * a_im) / den
    lam_ref[0] = lam_re
    lam_ref[1] = lam_im
    b_re = bre_ref[...]
    b_im = bim_ref[...]
    for d in range(2):
        cr = coef_re[d:d + 1, :]
        ci = coef_im[d:d + 1, :]
        bd_ref[d, 0] = cr * b_re - ci * b_im
        bd_ref[d, 1] = cr * b_im + ci * b_re


def _s5_params(A_re, A_im, log_dt, B_re, B_im):
    a_re = A_re.reshape(2, S5_NSTATE)
    a_im = A_im.reshape(2, S5_NSTATE)
    ldt = jnp.repeat(log_dt, S5_STATE, axis=1)
    b_re_t = jnp.transpose(B_re, (2, 0, 1)).reshape(S5_GROUP, S5_NSTATE)
    b_im_t = jnp.transpose(B_im, (2, 0, 1)).reshape(S5_GROUP, S5_NSTATE)
    lam, bd = pl.pallas_call(
        _s5_param_body,
        out_shape=[jax.ShapeDtypeStruct((2, 2, S5_NSTATE), F32),
                   jax.ShapeDtypeStruct((2, 2, S5_GROUP, S5_NSTATE), F32)],
        name="s5_params",
    )(a_re, a_im, ldt, b_re_t, b_im_t)
    return lam, bd


def _s5_block_weights(bd, C_re, C_im):
    eye = jnp.eye(S5_GBLK, dtype=F32)
    bd6 = bd.reshape(2, 2, S5_GROUP, S5_NBLK, S5_GBLK, S5_STATE)
    w = jnp.einsum('dchjmp,lm->djlhcmp', bd6, eye)
    w_drive = w.reshape(2, S5_NBLK, S5_GBLK * S5_GROUP, 2 * S5_SBLK).astype(BF16)
    cc = jnp.stack([C_re, -C_im], axis=0).reshape(2, S5_NBLK, S5_GBLK, S5_GROUP, S5_STATE)
    cm = jnp.einsum('cjmhp,lm->jcmplh', cc, eye)
    w_read = cm.reshape(S5_NBLK, 2, S5_SBLK, S5_GBLK * S5_GROUP).astype(BF16)
    return w_drive, w_read


def _s5_scan_body(u_ref, wd_ref, wr_ref, lam_ref, h0_ref, y_ref, hout_ref, xre, xim, hst,
                  *, B, Tc, nch, lb):
    d = pl.program_id(0)
    c = pl.program_id(1)

    @pl.when(c == 0)
    def _():
        hst[...] = h0_ref[0]

    ub = u_ref[...].astype(BF16)
    for j in range(S5_NBLK):
        drv = _dot(ub[:, j * LANES:(j + 1) * LANES], wd_ref[0, j])
        xre[:, j * S5_SBLK:(j + 1) * S5_SBLK] = drv[:, :S5_SBLK]
        xim[:, j * S5_SBLK:(j + 1) * S5_SBLK] = drv[:, S5_SBLK:]

    for lbi in range(S5_NSTATE // lb):
        ls = slice(lbi * lb, (lbi + 1) * lb)
        lr = jnp.broadcast_to(lam_ref[0, 0, :, ls], (B, lb))
        li = jnp.broadcast_to(lam_ref[1, 0, :, ls], (B, lb))

        def step(s, carry):
            hr, hi = carry
            t = s + d * (Tc - 1 - 2 * s)
            r0 = pl.multiple_of(t * B, B)
            nr = lr * hr - li * hi + xre[pl.ds(r0, B), ls]
            ni = lr * hi + li * hr + xim[pl.ds(r0, B), ls]
            xre[pl.ds(r0, B), ls] = nr
            xim[pl.ds(r0, B), ls] = ni
            return nr, ni

        hr, hi = lax.fori_loop(0, Tc, step, (hst[0, :, ls], hst[1, :, ls]), unroll=4)
        hst[0, :, ls] = hr
        hst[1, :, ls] = hi

    for j in range(S5_NBLK):
        ss = slice(j * S5_SBLK, (j + 1) * S5_SBLK)
        y = _dot(xre[:, ss].astype(BF16), wr_ref[j, 0]) + _dot(xim[:, ss].astype(BF16), wr_ref[j, 1])
        y_ref[0, :, j * LANES:(j + 1) * LANES] = y

    @pl.when(c == nch - 1)
    def _():
        hout_ref[0] = hst[...]


def _s5_scan(u_tb, w_drive, w_read, lam, h0, B, Tc):
    TB = u_tb.shape[0]
    T = TB // B
    Tc = _tile(T, Tc)
    nch = T // Tc
    lam4 = lam.reshape(2, 2, 1, S5_NSTATE)

    def cidx(d, c):
        return c + d * (nch - 1 - 2 * c)

    return pl.pallas_call(
        functools.partial(_s5_scan_body, B=B, Tc=Tc, nch=nch, lb=256),
        grid=(2, nch),
        in_specs=[pl.BlockSpec((Tc * B, S5_WIDTH), lambda d, c: (cidx(d, c), 0)),
                  pl.BlockSpec((1, S5_NBLK, LANES, 2 * S5_SBLK), lambda d, c: (d, 0, 0, 0)),
                  pl.BlockSpec((S5_NBLK, 2, S5_SBLK, LANES), lambda d, c: (0, 0, 0, 0)),
                  pl.BlockSpec((2, 1, 1, S5_NSTATE), lambda d, c: (0, d, 0, 0)),
                  pl.BlockSpec((1, 2, B, S5_NSTATE), lambda d, c: (d, 0, 0, 0))],
        out_specs=[pl.BlockSpec((1, Tc * B, S5_WIDTH), lambda d, c: (d, cidx(d, c), 0)),
                   pl.BlockSpec((1, 2, B, S5_NSTATE), lambda d, c: (d, 0, 0, 0))],
        out_shape=[jax.ShapeDtypeStruct((2, TB, S5_WIDTH), F32),
                   jax.ShapeDtypeStruct((2, 2, B, S5_NSTATE), F32)],
        scratch_shapes=[pltpu.VMEM((Tc * B, S5_NSTATE), F32),
                        pltpu.VMEM((Tc * B, S5_NSTATE), F32),
                        pltpu.VMEM((2, B, S5_NSTATE), F32)],
        compiler_params=_cparams("parallel", "arbitrary"),
        name="s5_scan",
    )(u_tb, w_drive, w_read, lam4, h0)


def _ab_out_body(x_ref, mod_ref, o_ref, z_ref, u_ref, y_ref, ng_ref, dsk_ref, gw_ref, gb_ref,
                 woa_ref, wob_ref, out_ref):
    o = o_ref[0, 0] + o_ref[1, 0]
    z = z_ref[0]
    parts = []
    for h in range(DN_HEADS):
        sl = slice(h * DN_DV, (h + 1) * DN_DV)
        parts.append(_rms(o[:, sl]) * ng_ref[...] * _silu(z[:, sl]))
    a_out = jnp.concatenate(parts, axis=1)
    y = y_ref[0] + y_ref[1] + dsk_ref[...] * u_ref[...]
    y = _gelu_tanh(y)
    b_out = y * jax.nn.sigmoid(_dot(y.astype(BF16), gw_ref[...]) + gb_ref[...])
    mix = _dot(a_out.astype(BF16), woa_ref[...]) + _dot(b_out.astype(BF16), wob_ref[...])
    out_ref[0] = x_ref[0] + mod_ref[0, 2:3, :] * mix


def _ab_out(x, mod, o_dn, z, u_tb, y_tb, dn_norm_g, d_skip, glu_w, glu_b, w_out, tm):
    B, T, D = x.shape
    tm = _tile(T, tm)
    per_batch_mod = mod.shape[0] != 1
    y3 = y_tb.reshape(2, T, B * S5_WIDTH)
    full = lambda *shape: pl.BlockSpec(shape, lambda b, i: (0,) * len(shape))
    return pl.pallas_call(
        _ab_out_body,
        grid=(B, T // tm),
        in_specs=[pl.BlockSpec((1, tm, D), lambda b, i: (b, i, 0)),
                  pl.BlockSpec((1, 6, D), (lambda b, i: (b, 0, 0)) if per_batch_mod else (lambda b, i: (0, 0, 0))),
                  pl.BlockSpec((2, 1, tm, DN_W), lambda b, i: (0, b, i, 0)),
                  pl.BlockSpec((1, tm, DN_W), lambda b, i: (b, i, 0)),
                  pl.BlockSpec((tm, S5_WIDTH), lambda b, i: (i, b)),
                  pl.BlockSpec((2, tm, S5_WIDTH), lambda b, i: (0, i, b)),
                  full(1, DN_DV), full(1, S5_WIDTH), full(S5_WIDTH, S5_WIDTH), full(1, S5_WIDTH),
                  full(DN_W, D), full(S5_WIDTH, D)],
        out_specs=pl.BlockSpec((1, tm, D), lambda b, i: (b, i, 0)),
        out_shape=jax.ShapeDtypeStruct((B, T, D), F32),
        compiler_params=_cparams("parallel", "parallel"),
        name="ab_out",
    )(x, mod, o_dn, z, u_tb, y3, dn_norm_g.reshape(1, DN_DV), d_skip.reshape(1, S5_WIDTH),
      glu_w.astype(BF16), glu_b.reshape(1, S5_WIDTH), w_out[:DN_W].astype(BF16), w_out[DN_W:].astype(BF16))


def _route(h, rw_ref, rb_ref):
    logits = _dot_nt(rw_ref[...], h, precision=HIGHEST)
    scores = jax.nn.sigmoid(logits)
    choice = scores + rb_ref[...]
    rows = [choice[e:e + 1, :] for e in range(N_EXPERTS)]
    neg_inf = jnp.float32(-jnp.inf)
    gs = []
    for g in range(N_EXPERT_GROUPS):
        r = rows[g * EXPERTS_PER_GROUP:(g + 1) * EXPERTS_PER_GROUP]
        best = None
        for a in range(EXPERTS_PER_GROUP):
            for b in range(a + 1, EXPERTS_PER_GROUP):
                s = r[a] + r[b]
                best = s if best is None else jnp.maximum(best, s)
        gs.append(best)
    best_val = gs[0]
    best_g = jnp.zeros_like(best_val, dtype=jnp.int32)
    for g in range(1, N_EXPERT_GROUPS):
        better = gs[g] > best_val
        best_val = jnp.where(better, gs[g], best_val)
        best_g = jnp.where(better, g, best_g)
    masked = [jnp.where(best_g == e // EXPERTS_PER_GROUP, rows[e], neg_inf) for e in range(N_EXPERTS)]
    m1 = masked[0]
    for e in range(1, N_EXPERTS):
        m1 = jnp.maximum(m1, masked[e])
    i1 = jnp.full_like(best_g, N_EXPERTS)
    for e in reversed(range(N_EXPERTS)):
        i1 = jnp.where(masked[e] == m1, e, i1)
    rest = [jnp.where(i1 == e, neg_inf, masked[e]) for e in range(N_EXPERTS)]
    m2 = rest[0]
    for e in range(1, N_EXPERTS):
        m2 = jnp.maximum(m2, rest[e])
    i2 = jnp.full_like(best_g, N_EXPERTS)
    for e in reversed(range(N_EXPERTS)):
        i2 = jnp.where(rest[e] == m2, e, i2)
    eidx = lax.broadcasted_iota(jnp.int32, scores.shape, 0)
    sel1 = eidx == i1
    sel2 = eidx == i2
    w1 = jnp.sum(jnp.where(sel1, scores, 0.0), axis=0, keepdims=True)
    w2 = jnp.sum(jnp.where(sel2, scores, 0.0), axis=0, keepdims=True)
    inv = 1.0 / (w1 + w2)
    return jnp.where(sel1, w1 * inv, 0.0) + jnp.where(sel2, w2 * inv, 0.0)


def _moe_body(x_ref, mod_ref, g_ref, rw_ref, rb_ref, wg_ref, wu_ref, wd_ref, fg_ref, out_ref,
              h_scr, gate_scr, acc, *, final_norm):
    e = pl.program_id(2)

    @pl.when(e == 0)
    def _():
        h = _rms(x_ref[0]) * g_ref[...]
        h = h * (1.0 + mod_ref[0, 4:5, :]) + mod_ref[0, 3:4, :]
        h_scr[...] = h.astype(BF16)
        gates_t = _route(h, rw_ref, rb_ref)
        pad = jnp.zeros((LANES - N_EXPERTS, gates_t.shape[1]), F32)
        gate_scr[...] = jnp.transpose(jnp.concatenate([gates_t, pad], axis=0))
        acc[...] = jnp.zeros_like(acc)

    hb = h_scr[...]
    act = _silu(_dot(hb, wg_ref[0])) * _dot(hb, wu_ref[0])
    lane = lax.broadcasted_iota(jnp.int32, gate_scr.shape, 1)
    ge = jnp.sum(jnp.where(lane == e, gate_scr[...], 0.0), axis=1, keepdims=True)
    acc[...] += ge * _dot(act.astype(BF16), wd_ref[0])

    @pl.when(e == N_EXPERTS - 1)
    def _():
        r = x_ref[0] + mod_ref[0, 5:6, :] * acc[...]
        if final_norm:
            r = _rms(r) * fg_ref[...]
        out_ref[0] = r


def _moe(x, mod, norm_g, router_w, router_bias, wg, wu, wd, final_g, tm):
    Bx, T, D = x.shape
    tm = _tile(T, tm)
    per_batch_mod = mod.shape[0] != 1
    final_norm = final_g is not None
    fg = (final_g if final_norm else jnp.ones((D,), F32)).reshape(1, D)
    full = lambda *shape: pl.BlockSpec(shape, lambda b, i, e: (0,) * len(shape))
    return pl.pallas_call(
        functools.partial(_moe_body, final_norm=final_norm),
        grid=(Bx, T // tm, N_EXPERTS),
        in_specs=[pl.BlockSpec((1, tm, D), lambda b, i, e: (b, i, 0)),
                  pl.BlockSpec((1, 6, D), (lambda b, i, e: (b, 0, 0)) if per_batch_mod else (lambda b, i, e: (0, 0, 0))),
                  full(1, D), full(N_EXPERTS, D), full(N_EXPERTS, 1),
                  pl.BlockSpec((1, D, D_EXPERT), lambda b, i, e: (e, 0, 0)),
                  pl.BlockSpec((1, D, D_EXPERT), lambda b, i, e: (e, 0, 0)),
                  pl.BlockSpec((1, D_EXPERT, D), lambda b, i, e: (e, 0, 0)),
                  full(1, D)],
        out_specs=pl.BlockSpec((1, tm, D), lambda b, i, e: (b, i, 0)),
        out_shape=jax.ShapeDtypeStruct((Bx, T, D), F32),
        scratch_shapes=[pltpu.VMEM((tm, D), BF16), pltpu.VMEM((tm, LANES), F32), pltpu.VMEM((tm, D), F32)],
        compiler_params=_cparams("parallel", "parallel", "arbitrary"),
        name="moe",
    )(x, mod, norm_g.reshape(1, D), router_w.T, router_bias.reshape(N_EXPERTS, 1), wg, wu, wd, fg)


def _mla_q_body(cq_ref, g_ref, wn_ref, wa_ref, wb_ref, cos_ref, sin_ref, qn_ref, qr_ref):
    hb = (_rms(cq_ref[0]) * g_ref[...]).astype(BF16)
    qn_ref[0] = (_dot(hb, wn_ref[...]) * MLA_SCALE).astype(BF16)
    ra = _dot(hb, wa_ref[...])
    rb = _dot(hb, wb_ref[...])
    cos = cos_ref[...]
    sin = sin_ref[...]
    for p in range(MLA_HEADS // 2):
        sl = slice(p * LANES, (p + 1) * LANES)
        qr_ref[0, :, sl] = ((ra[:, sl] * cos + rb[:, sl] * sin) * MLA_SCALE).astype(BF16)


def _mla_q(cq, g, wn, wa, wb, cos_t, sin_t, tm):
    B, T, R = cq.shape
    tm = _tile(T, tm)
    full = lambda *shape: pl.BlockSpec(shape, lambda b, i: (0,) * len(shape))
    return pl.pallas_call(
        _mla_q_body,
        grid=(B, T // tm),
        in_specs=[pl.BlockSpec((1, tm, R), lambda b, i: (b, i, 0)), full(1, R),
                  full(R, MLA_HEADS * MLA_NOPE), full(R, MLA_HEADS * MLA_ROPE), full(R, MLA_HEADS * MLA_ROPE),
                  pl.BlockSpec((tm, LANES), lambda b, i: (i, 0)),
                  pl.BlockSpec((tm, LANES), lambda b, i: (i, 0))],
        out_specs=[pl.BlockSpec((1, tm, MLA_HEADS * MLA_NOPE), lambda b, i: (b, i, 0)),
                   pl.BlockSpec((1, tm, MLA_HEADS * MLA_ROPE), lambda b, i: (b, i, 0))],
        out_shape=[jax.ShapeDtypeStruct((B, T, MLA_HEADS * MLA_NOPE), BF16),
                   jax.ShapeDtypeStruct((B, T, MLA_HEADS * MLA_ROPE), BF16)],
        compiler_params=_cparams("parallel", "parallel"),
        name="mla_q",
    )(cq, g.reshape(1, R), wn, wa, wb, cos_t, sin_t)


def _mla_kv_body(ckv_ref, ka_ref, kb_ref, g_ref, wk_ref, wv_ref, cos_ref, sin_ref, kn_ref, v_ref, kr_ref):
    hb = (_rms(ckv_ref[0]) * g_ref[...]).astype(BF16)
    kn_ref[0] = _dot(hb, wk_ref[...]).astype(BF16)
    v_ref[0] = _dot(hb, wv_ref[...]).astype(BF16)
    kr_ref[0] = (ka_ref[0] * cos_ref[...] + kb_ref[0] * sin_ref[...]).astype(BF16)


def _mla_kv(ckv, kra, krb, g, wk, wv, cos_t, sin_t, tm):
    B, T, R = ckv.shape
    tm = _tile(T, tm)
    W = MLA_HEADS * MLA_NOPE
    full = lambda *shape: pl.BlockSpec(shape, lambda b, i: (0,) * len(shape))
    tok = lambda n: pl.BlockSpec((1, tm, n), lambda b, i: (b, i, 0))
    return pl.pallas_call(
        _mla_kv_body,
        grid=(B, T // tm),
        in_specs=[tok(R), tok(LANES), tok(LANES), full(1, R), full(R, W), full(R, W),
                  pl.BlockSpec((tm, LANES), lambda b, i: (i, 0)),
                  pl.BlockSpec((tm, LANES), lambda b, i: (i, 0))],
        out_specs=[tok(W), tok(W), tok(LANES)],
        out_shape=[jax.ShapeDtypeStruct((B, T, W), BF16), jax.ShapeDtypeStruct((B, T, W), BF16),
                   jax.ShapeDtypeStruct((B, T, LANES), BF16)],
        compiler_params=_cparams("parallel", "parallel"),
        name="mla_kv",
    )(ckv, kra, krb, g.reshape(1, R), wk, wv, cos_t, sin_t)


def _mla_attn_body(qn_ref, qr_ref, kn_ref, kr_ref, v_ref, o_ref):
    tq = qn_ref.shape[1]
    lane = lax.broadcasted_iota(jnp.int32, (tq, LANES), 1)
    kr = kr_ref[0]
    for h in range(MLA_HEADS):
        sl = slice(h * MLA_NOPE, (h + 1) * MLA_NOPE)
        p = h // 2
        qr = qr_ref[0, :, p * LANES:(p + 1) * LANES]
        mine = (lane < MLA_ROPE) if h % 2 == 0 else (lane >= MLA_ROPE)
        qr = jnp.where(mine, qr, jnp.zeros_like(qr))
        s = _dot_nt(qn_ref[0, :, sl], kn_ref[0, :, sl]) + _dot_nt(qr, kr)
        m = jnp.max(s, axis=-1, keepdims=True)
        e = jnp.exp(s - m)
        l = jnp.sum(e, axis=-1, keepdims=True)
        o = _dot(e.astype(BF16), v_ref[0, :, sl]) / l
        o_ref[0, :, sl] = o.astype(o_ref.dtype)


def _mla_attn(qn, qr, kn, kr, v, tq):
    B, T, W = qn.shape
    Tk = kn.shape[1]
    tq = _tile(T, tq)
    return pl.pallas_call(
        _mla_attn_body,
        grid=(B, T // tq),
        in_specs=[pl.BlockSpec((1, tq, W), lambda b, i: (b, i, 0)),
                  pl.BlockSpec((1, tq, MLA_HEADS * MLA_ROPE), lambda b, i: (b, i, 0)),
                  pl.BlockSpec((1, Tk, W), lambda b, i: (b, 0, 0)),
                  pl.BlockSpec((1, Tk, LANES), lambda b, i: (b, 0, 0)),
                  pl.BlockSpec((1, Tk, W), lambda b, i: (b, 0, 0))],
        out_specs=pl.BlockSpec((1, tq, W), lambda b, i: (b, i, 0)),
        out_shape=jax.ShapeDtypeStruct((B, T, W), BF16),
        compiler_params=_cparams("parallel", "arbitrary"),
        name="mla_attn",
    )(qn, qr, kn, kr, v)


def _proj_res_body(x_ref, mod_ref, a_ref, w_ref, out_ref):
    out_ref[0] = x_ref[0] + mod_ref[0, 2:3, :] * _dot(a_ref[0], w_ref[...])


def _proj_res(x, mod, a, w, tm):
    B, T, D = x.shape
    K = a.shape[2]
    tm = _tile(T, tm)
    per_batch_mod = mod.shape[0] != 1
    return pl.pallas_call(
        _proj_res_body,
        grid=(B, T // tm),
        in_specs=[pl.BlockSpec((1, tm, D), lambda b, i: (b, i, 0)),
                  pl.BlockSpec((1, 6, D), (lambda b, i: (b, 0, 0)) if per_batch_mod else (lambda b, i: (0, 0, 0))),
                  pl.BlockSpec((1, tm, K), lambda b, i: (b, i, 0)),
                  pl.BlockSpec((K, D), lambda b, i: (0, 0))],
        out_specs=pl.BlockSpec((1, tm, D), lambda b, i: (b, i, 0)),
        out_shape=jax.ShapeDtypeStruct((B, T, D), F32),
        compiler_params=_cparams("parallel", "parallel"),
        name="proj_res",
    )(x, mod, a, w)


def _rope_tables(T, n_ctx):
    rows = T // GRID_W
    row = jnp.repeat(jnp.arange(rows, dtype=F32), GRID_W)
    col = jnp.tile(jnp.arange(GRID_W, dtype=F32), rows)
    n_freq = MLA_ROPE // 4
    inv = ROPE_THETA ** (-jnp.arange(n_freq, dtype=F32) / n_freq)
    ang = jnp.concatenate([row[:, None] * inv, col[:, None] * inv], axis=-1)
    cos, sin = jnp.cos(ang), jnp.sin(ang)
    cos_t = jnp.concatenate([cos, cos, cos, cos], axis=-1)
    sin_t = jnp.concatenate([-sin, sin, -sin, sin], axis=-1)
    cos_k = jnp.concatenate([jnp.ones((n_ctx, LANES), F32), cos_t], axis=0)
    sin_k = jnp.concatenate([jnp.zeros((n_ctx, LANES), F32), sin_t], axis=0)
    return cos_t, sin_t, cos_k, sin_k


def _layer_ab(x, ctx, mod_l, mod_c, norm1_g, w_in, conv_w, a_log, dt_bias, dn_norm_g, A_re, A_im, log_dt,
              B_re, B_im, C_re, C_im, D_skip, glu_w, glu_b, w_out):
    B, T, D = x.shape
    Tc = ctx.shape[1]
    q0, k0, v0, z0, a0, b0, u0 = 0, 512, 1024, 1536, 2048, 2056, 2064
    w_qkv = w_in[:, q0:z0].astype(BF16)
    w_z = w_in[:, z0:a0].astype(BF16)
    w_ab = jnp.zeros((D, LANES), F32).at[:, :16].set(w_in[:, a0:u0]).astype(BF16)
    w_u = w_in[:, u0:].astype(BF16)
    ws = [w_qkv, w_z, w_ab, w_u]
    dts = [F32, F32, F32, F32]
    tbl = [False, False, False, True]

    lam, bd = _s5_params(A_re, A_im, log_dt, B_re, B_im)
    w_drive, w_read = _s5_block_weights(bd, C_re, C_im)

    streams = []
    dn_state = jnp.zeros((2, B, DN_HEADS, DN_DK, DN_DV), F32)
    s5_state = jnp.zeros((2, 2, B, S5_NSTATE), F32)
    for xs, mod in ((ctx, mod_c), (x, mod_l)):
        qkv, z, ab, u_tb = _modmm(xs, mod, norm1_g, ws, dts, tbl, 0, 512)
        qkv = _dn_prep(qkv, conv_w, 512)
        o_dn, dn_state = _dn_scan(qkv, ab, a_log, dt_bias, dn_state, 2)
        u2 = u_tb.reshape(xs.shape[1] * B, S5_WIDTH)
        y_tb, s5_state = _s5_scan(u2, w_drive, w_read, lam, s5_state, B, 32)
        streams.append(_ab_out(xs, mod, o_dn, z, u_tb, y_tb, dn_norm_g, D_skip.reshape(-1), glu_w, glu_b,
                               w_out, 512))
    return streams[1], streams[0]


def _layer_mla(x, ctx, mod_l, mod_c, norm1_g, w_in, q_norm_g, w_q_up, kv_norm_g, w_kv_up, w_out, need_ctx):
    assert not need_ctx, "context attention output is only needed when a later layer follows"
    B, T, D = x.shape
    n_ctx = ctx.shape[1]
    qr, kvr = MLA_Q_RANK, MLA_KV_RANK
    half = MLA_ROPE // 2
    w_cq = w_in[:, :qr].astype(BF16)
    w_ckv = w_in[:, qr:qr + kvr].astype(BF16)
    wk1 = w_in[:, qr + kvr:qr + kvr + half]
    wk2 = w_in[:, qr + kvr + half:]
    w_ka = jnp.concatenate([wk1, wk2, wk1, wk2], axis=1).astype(BF16)
    w_kb = jnp.concatenate([wk2, wk1, wk2, wk1], axis=1).astype(BF16)
    ws = [w_cq, w_ckv, w_ka, w_kb]
    cq_l, ckv_l, ka_l, kb_l = _modmm(x, mod_l, norm1_g, ws, [F32] * 4, [False] * 4, 0, 512)
    _, ckv_c, ka_c, kb_c = _modmm(ctx, mod_c, norm1_g, ws, [F32] * 4, [False] * 4, 0, 512)

    wq = w_q_up.reshape(qr, MLA_HEADS, MLA_NOPE + MLA_ROPE)
    wq_n = wq[:, :, :MLA_NOPE].reshape(qr, MLA_HEADS * MLA_NOPE).astype(BF16)
    x1 = wq[:, :, MLA_NOPE:MLA_NOPE + half]
    x2 = wq[:, :, MLA_NOPE + half:]
    wq_a = jnp.concatenate([x1, x2], axis=2).reshape(qr, MLA_HEADS * MLA_ROPE).astype(BF16)
    wq_b = jnp.concatenate([x2, x1], axis=2).reshape(qr, MLA_HEADS * MLA_ROPE).astype(BF16)
    wkv = w_kv_up.reshape(kvr, MLA_HEADS, MLA_NOPE + MLA_V)
    wk_n = wkv[:, :, :MLA_NOPE].reshape(kvr, MLA_HEADS * MLA_NOPE).astype(BF16)
    wv = wkv[:, :, MLA_NOPE:].reshape(kvr, MLA_HEADS * MLA_V).astype(BF16)

    cos_t, sin_t, cos_k, sin_k = _rope_tables(T, n_ctx)
    qn, qrope = _mla_q(cq_l, q_norm_g, wq_n, wq_a, wq_b, cos_t, sin_t, 512)
    ckv = jnp.concatenate([ckv_c, ckv_l], axis=1)
    ka = jnp.concatenate([ka_c, ka_l], axis=1)
    kb = jnp.concatenate([kb_c, kb_l], axis=1)
    kn, v, kr = _mla_kv(ckv, ka, kb, kv_norm_g, wk_n, wv, cos_k, sin_k, 256)
    o = _mla_attn(qn, qrope, kn, kr, v, 256)
    return _proj_res(x, mod_l, o, w_out.astype(BF16), 512)


def kernel(x, c, ctx, c_ctx, ada_w, ada_b, norm1_g, norm2_g, ab_w_in, dn_conv_w, dn_A_log, dn_dt_bias, dn_norm_g, s5_A_re, s5_A_im, s5_log_dt, s5_B_re, s5_B_im, s5_C_re, s5_C_im, s5_D, s5_glu_w, s5_glu_b, ab_w_out, mla_w_in, mla_q_norm_g, mla_w_q_up, mla_kv_norm_g, mla_w_kv_up, mla_w_out, router_w, router_bias, moe_w_gate, moe_w_up, moe_w_down, final_norm_g):
    B, T, D = x.shape
    n_ctx = ctx.shape[1]
    depth = ada_w.shape[0]
    n_cond = -(-(B + 1) // 8) * 8
    cond = jnp.zeros((n_cond, D), F32).at[:B].set(c).at[B].set(c_ctx)
    mods = _adaln_all(cond, ada_w, ada_b).reshape(depth, n_cond, 6, D)
    wg = moe_w_gate.astype(BF16)
    wu = moe_w_up.astype(BF16)
    wd = moe_w_down.astype(BF16)
    for i in range(depth):
        last = i == depth - 1
        j = i // 2
        mod_l = mods[i, :B]
        mod_c = mods[i, B:B + 1]
        if i % 2 == 0:
            x, ctx_new = _layer_ab(x, ctx, mod_l, mod_c, norm1_g[i], ab_w_in[j], dn_conv_w[j], dn_A_log[j],
                                   dn_dt_bias[j], dn_norm_g[j], s5_A_re[j], s5_A_im[j], s5_log_dt[j],
                                   s5_B_re[j], s5_B_im[j], s5_C_re[j], s5_C_im[j], s5_D[j], s5_glu_w[j],
                                   s5_glu_b[j], ab_w_out[j])
        else:
            x = _layer_mla(x, ctx, mod_l, mod_c, norm1_g[i], mla_w_in[j], mla_q_norm_g[j], mla_w_q_up[j],
                           mla_kv_norm_g[j], mla_w_kv_up[j], mla_w_out[j], not last)
            ctx_new = None
        x = _moe(x, mod_l, norm2_g[i], router_w, router_bias, wg[i], wu[i], wd[i],
                 final_norm_g if last else None, 1024)
        if not last:
            ctx_flat = _moe(ctx_new.reshape(1, B * n_ctx, D), mod_c, norm2_g[i], router_w, router_bias,
                            wg[i], wu[i], wd[i], None, 1024)
            ctx = ctx_flat.reshape(B, n_ctx, D)
    return x
```

```python
import functools
import math

import jax
import jax.numpy as jnp
from jax import lax
from jax.experimental import pallas as pl
from jax.experimental.pallas import tpu as pltpu

F32 = jnp.float32
BF16 = jnp.bfloat16
HIGHEST = lax.Precision.HIGHEST

NORM_EPS = 1e-6
GRID_W = 64
ROPE_THETA = 10000.0

DN_HEADS = 4
DN_DK = 128
DN_DV = 128
DN_CONV = 5
DN_CHUNK = 64
DN_W = DN_HEADS * DN_DK

S5_WIDTH = 512
S5_GROUP = 16
S5_GROUPS = 32
S5_STATE = 64
S5_NSTATE = S5_GROUPS * S5_STATE
S5_GBLK = 8
S5_NBLK = S5_GROUPS // S5_GBLK
S5_SBLK = S5_GBLK * S5_STATE

MLA_HEADS = 8
MLA_Q_RANK = 384
MLA_KV_RANK = 256
MLA_NOPE = 128
MLA_ROPE = 64
MLA_V = 128
MLA_SCALE = (MLA_NOPE + MLA_ROPE) ** -0.5

N_EXPERTS = 16
N_EXPERT_GROUPS = 4
EXPERTS_PER_GROUP = 4
D_EXPERT = 512

LANES = 128
VMEM_LIMIT_BYTES = 56 * 1024 * 1024


def _tile(n, pref):
    t = min(pref, n)
    while n % t or t % 8:
        t -= 1
    return t


def _cparams(*sem):
    return pltpu.CompilerParams(dimension_semantics=sem, vmem_limit_bytes=VMEM_LIMIT_BYTES)


def _silu(x):
    return x * jax.nn.sigmoid(x)


def _softplus(x):
    return jnp.maximum(x, 0.0) + jnp.log(1.0 + jnp.exp(-jnp.abs(x)))


def _gelu_tanh(x):
    return 0.5 * x * (1.0 + jnp.tanh(math.sqrt(2.0 / math.pi) * (x + 0.044715 * (x * x * x))))


def _rms(x):
    return x * lax.rsqrt(jnp.mean(x * x, axis=-1, keepdims=True) + NORM_EPS)


def _dot(a, b):
    return jnp.dot(a, b, preferred_element_type=F32)


def _dot_nt(a, b, precision=None):
    return lax.dot_general(a, b, (((1,), (1,)), ((), ())), preferred_element_type=F32,
                           precision=precision)


def _dot_tn(a, b):
    return lax.dot_general(a, b, (((0,), (0,)), ((), ())), preferred_element_type=F32)


def _ada_body(c_ref, w_ref, b_ref, o_ref):
    c = c_ref[...]
    o_ref[0] = _dot(_silu(c).astype(BF16), w_ref[0].astype(BF16)) + b_ref[0]


def _adaln_all(cond, ada_w, ada_b):
    L, D, D6 = ada_w.shape
    R = cond.shape[0]
    tn = 1536
    return pl.pallas_call(
        _ada_body,
        grid=(L, D6 // tn),
        in_specs=[pl.BlockSpec((R, D), lambda l, j: (0, 0)),
                  pl.BlockSpec((1, D, tn), lambda l, j: (l, 0, j)),
                  pl.BlockSpec((1, 1, tn), lambda l, j: (l, 0, j))],
        out_specs=pl.BlockSpec((1, R, tn), lambda l, j: (l, 0, j)),
        out_shape=jax.ShapeDtypeStruct((L, R, D6), F32),
        compiler_params=_cparams("parallel", "parallel"),
        name="adaln",
    )(cond, ada_w, ada_b.reshape(L, 1, D6))


def _modmm_body(x_ref, mod_ref, g_ref, *refs, kinds, shift_row):
    n_out = len(kinds)
    w_refs, o_refs = refs[:n_out], refs[n_out:]
    h = _rms(x_ref[0]) * g_ref[...]
    h = h * (1.0 + mod_ref[0, shift_row + 1:shift_row + 2, :]) + mod_ref[0, shift_row:shift_row + 1, :]
    hb = h.astype(BF16)
    for w_ref, o_ref, kind in zip(w_refs, o_refs, kinds):
        if kind == "t":
            o_ref[0] = _dot_nt(w_ref[...], hb).astype(o_ref.dtype)
        elif kind == "tb":
            o_ref[...] = _dot(hb, w_ref[...]).astype(o_ref.dtype)
        else:
            o_ref[0] = _dot(hb, w_ref[...]).astype(o_ref.dtype)


def _modmm(x, mod, g, ws, out_dtypes, kinds, shift_row, tm):
    Bx, T, D = x.shape
    tm = _tile(T, tm)
    per_batch_mod = mod.shape[0] != 1
    in_specs = [pl.BlockSpec((1, tm, D), lambda b, i: (b, i, 0)),
                pl.BlockSpec((1, 6, D), (lambda b, i: (b, 0, 0)) if per_batch_mod else (lambda b, i: (0, 0, 0))),
                pl.BlockSpec((1, D), lambda b, i: (0, 0))]
    out_specs, out_shape = [], []
    for w, dt, kind in zip(ws, out_dtypes, kinds):
        in_specs.append(pl.BlockSpec(w.shape, lambda b, i: (0, 0)))
        if kind == "t":
            n = w.shape[0]
            out_specs.append(pl.BlockSpec((1, n, tm), lambda b, i: (b, 0, i)))
            out_shape.append(jax.ShapeDtypeStruct((Bx, n, T), dt))
        elif kind == "tb":
            n = w.shape[1]
            out_specs.append(pl.BlockSpec((tm, n), lambda b, i: (i, b)))
            out_shape.append(jax.ShapeDtypeStruct((T, Bx * n), dt))
        else:
            n = w.shape[1]
            out_specs.append(pl.BlockSpec((1, tm, n), lambda b, i: (b, i, 0)))
            out_shape.append(jax.ShapeDtypeStruct((Bx, T, n), dt))
    return pl.pallas_call(
        functools.partial(_modmm_body, kinds=tuple(kinds), shift_row=shift_row),
        grid=(Bx, T // tm),
        in_specs=in_specs, out_specs=out_specs, out_shape=out_shape,
        compiler_params=_cparams("parallel", "parallel"),
        name="modmm",
    )(x, mod, g.reshape(1, D), *ws)


def _dn_prep_body(x_ref, xp_ref, xn_ref, w_ref, o_ref, buf, *, tm, nt):
    i = pl.program_id(1)
    j = pl.program_id(2)
    buf[0:8, :] = jnp.where(i == 0, 0.0, xp_ref[0])
    buf[8:8 + tm, :] = x_ref[0]
    buf[8 + tm:16 + tm, :] = jnp.where(i == nt - 1, 0.0, xn_ref[0])
    acc = buf[pl.ds(8 - DN_CONV // 2, tm), :] * w_ref[0:1, :]
    for kk in range(1, DN_CONV):
        acc = acc + buf[pl.ds(8 - DN_CONV // 2 + kk, tm), :] * w_ref[kk:kk + 1, :]
    y = _silu(acc)
    q_scale = jnp.where(j == 0, DN_DK ** -0.5, 1.0)
    for h in range(DN_HEADS):
        yh = y[:, h * DN_DK:(h + 1) * DN_DK]
        r = lax.rsqrt(jnp.sum(yh * yh, axis=-1, keepdims=True) + NORM_EPS) * q_scale
        o_ref[0, :, h * DN_DK:(h + 1) * DN_DK] = yh * jnp.where(j == 2, 1.0, r)


def _dn_prep(qkv, conv_w, tm):
    B, T, W3 = qkv.shape
    tm = _tile(T, tm)
    nt = T // tm
    r8 = tm // 8
    return pl.pallas_call(
        functools.partial(_dn_prep_body, tm=tm, nt=nt),
        grid=(B, nt, 3),
        in_specs=[pl.BlockSpec((1, tm, DN_W), lambda b, i, j: (b, i, j)),
                  pl.BlockSpec((1, 8, DN_W), lambda b, i, j: (b, jnp.maximum(i * r8 - 1, 0), j)),
                  pl.BlockSpec((1, 8, DN_W), lambda b, i, j: (b, jnp.minimum((i + 1) * r8, T // 8 - 1), j)),
                  pl.BlockSpec((DN_CONV, DN_W), lambda b, i, j: (0, j))],
        out_specs=pl.BlockSpec((1, tm, DN_W), lambda b, i, j: (b, i, j)),
        out_shape=jax.ShapeDtypeStruct((B, T, W3), F32),
        scratch_shapes=[pltpu.VMEM((tm + 16, DN_W), F32)],
        compiler_params=_cparams("parallel", "parallel", "parallel"),
        name="dn_prep",
    )(qkv, qkv, qkv, conv_w)


def _heads_to_lanes(cols, width):
    return jnp.concatenate([jnp.broadcast_to(c, (c.shape[0], width)) for c in cols], axis=1)


def _block_diag(x, nblk):
    C, W = x.shape
    w = W // nblk
    t = jnp.concatenate([x] * nblk, axis=0)
    rb = lax.broadcasted_iota(jnp.int32, t.shape, 0) // C
    cb = lax.broadcasted_iota(jnp.int32, t.shape, 1) // w
    return jnp.where(rb == cb, t, jnp.zeros_like(t))


def _dn_chunk_body(q_ref, k_ref, v_ref, ab_ref, abt_ref, arow_ref, drow_ref, acol_ref, dcol_ref,
                   uw_ref, qk_ref, gl_ref, *, G):
    C, H = DN_CHUNK, DN_HEADS
    Tg = G * C
    gl_ref[...] = jnp.zeros_like(gl_ref)
    ab = ab_ref[0]
    abt = abt_ref[0]
    g_all = -jnp.exp(arow_ref[...]) * _softplus(ab + drow_ref[...])
    gt_all = -jnp.exp(acol_ref[...]) * _softplus(abt + dcol_ref[...])
    beta_all = jax.nn.sigmoid(ab)
    pos_s = lax.broadcasted_iota(jnp.int32, (Tg, LANES), 0) % C
    pos_l = lax.broadcasted_iota(jnp.int32, (2 * H * 2, Tg), 1) % C
    gc_all, gct_all = g_all, gt_all
    s = 1
    while s < C:
        gc_all = gc_all + jnp.where(pos_s >= s, pltpu.roll(gc_all, s, 0), 0.0)
        gct_all = gct_all + jnp.where(pos_l >= s, pltpu.roll(gct_all, s, 1), 0.0)
        s *= 2

    ri = lax.broadcasted_iota(jnp.int32, (C, H * C), 0)
    cj = lax.broadcasted_iota(jnp.int32, (C, H * C), 1) % C
    eye_side = (ri == cj).astype(F32)

    chains = []
    for ci in range(G):
        rows = slice(ci * C, (ci + 1) * C)
        q = q_ref[0, rows, :]
        k = k_ref[0, rows, :]
        v = v_ref[0, rows, :]
        kbd = _block_diag(k.astype(BF16), H)
        g_c, gc_f, beta_c = g_all[rows], gc_all[rows], beta_all[rows]
        gt_c, gct_f = gt_all[:, rows], gct_all[:, rows]
        gtot = gc_f[C - 1:C, :]
        gtot_t = gct_f[:, C - 1:C]
        for d in range(2):
            if d == 0:
                gc, gct = gc_f, gct_f
                incl, strict = ri >= cj, ri > cj
            else:
                gc, gct = gtot - gc_f + g_c, gtot_t - gct_f + gt_c
                incl, strict = ri <= cj, ri < cj
            lanes = [d * H + h for h in range(H)]
            gcol = [gc[:, l:l + 1] for l in lanes]
            diff = _heads_to_lanes(gcol, C) - jnp.concatenate([gct[l:l + 1, :] for l in lanes], axis=1)
            decay = jnp.where(incl, jnp.exp(jnp.where(incl, diff, 0.0)), 0.0)
            beta_b = _heads_to_lanes([beta_c[:, 2 * H + l:2 * H + l + 1] for l in lanes], DN_DK)
            egc_b = _heads_to_lanes([jnp.exp(c) for c in gcol], DN_DK)
            ekd_b = _heads_to_lanes([jnp.exp(gtot[:, l:l + 1] - gc[:, l:l + 1]) for l in lanes], DN_DK)
            kb = k * beta_b
            a_low = jnp.where(strict, _dot_nt(kb.astype(BF16), kbd) * decay, 0.0)
            qk_ref[d, 0, rows, :] = jnp.where(incl, _dot_nt(q.astype(BF16), kbd) * decay, 0.0).astype(BF16)
            uw_ref[d, 0, rows, 2 * DN_W:3 * DN_W] = (q * egc_b).astype(BF16)
            uw_ref[d, 0, rows, 3 * DN_W:4 * DN_W] = (k * ekd_b).astype(BF16)
            gl_ref[d, 0, 0, ci:ci + 1, :] = jnp.exp(gtot)
            rhs = jnp.concatenate([v * beta_b, kb * egc_b], axis=1).astype(BF16)
            chains.append(dict(d=d, rows=rows, pw=-a_low, inv=eye_side - a_low, rhs=rhs))

    n_lvl = int(math.log2(C)) - 1
    for ch in chains:
        pwb = ch["pw"].astype(BF16)
        ch["pw"] = _dot(pwb, _block_diag(pwb, H))
    for lvl in range(1, n_lvl + 1):
        for ch in chains:
            pwb = ch["pw"].astype(BF16)
            pbd = _block_diag(pwb, H)
            if lvl < n_lvl:
                st = _dot(jnp.concatenate([ch["inv"].astype(BF16), pwb], axis=0), pbd)
                ch["inv"] = ch["inv"] + st[:C]
                ch["pw"] = st[C:]
            else:
                ch["inv"] = ch["inv"] + _dot(ch["inv"].astype(BF16), pbd)
    rb = lax.broadcasted_iota(jnp.int32, (H * C, 2 * DN_W), 0) // C
    cb = (lax.broadcasted_iota(jnp.int32, (H * C, 2 * DN_W), 1) // DN_DK) % H
    for ch in chains:
        rhs_bd = jnp.where(rb == cb, jnp.concatenate([ch["rhs"]] * H, axis=0), jnp.zeros((), BF16))
        sol = _dot(ch["inv"].astype(BF16), rhs_bd)
        uw_ref[ch["d"], 0, ch["rows"], 0:2 * DN_W] = sol.astype(BF16)


def _dn_chunk(qkv, ab, abt, a_log, dt_bias, G):
    B, T, _ = qkv.shape
    Tg = G * DN_CHUNK
    ns = T // Tg
    nl = 2 * DN_HEADS
    alog = a_log.reshape(-1)
    dtb = dt_bias.reshape(-1)
    arow = jnp.zeros((1, LANES), F32).at[0, :nl].set(alog)
    drow = jnp.zeros((1, LANES), F32).at[0, :nl].set(dtb)
    acol = jnp.zeros((2 * nl, 1), F32).at[:nl, 0].set(alog)
    dcol = jnp.zeros((2 * nl, 1), F32).at[:nl, 0].set(dtb)
    full = lambda *shape: pl.BlockSpec(shape, lambda b, i: (0,) * len(shape))
    return pl.pallas_call(
        functools.partial(_dn_chunk_body, G=G),
        grid=(B, ns),
        in_specs=[pl.BlockSpec((1, Tg, DN_W), lambda b, i: (b, i, 0)),
                  pl.BlockSpec((1, Tg, DN_W), lambda b, i: (b, i, 1)),
                  pl.BlockSpec((1, Tg, DN_W), lambda b, i: (b, i, 2)),
                  pl.BlockSpec((1, Tg, LANES), lambda b, i: (b, i, 0)),
                  pl.BlockSpec((1, 2 * nl, Tg), lambda b, i: (b, 0, i)),
                  full(1, LANES), full(1, LANES), full(2 * nl, 1), full(2 * nl, 1)],
        out_specs=[pl.BlockSpec((2, 1, Tg, 4 * DN_W), lambda b, i: (0, b, i, 0)),
                   pl.BlockSpec((2, 1, Tg, DN_HEADS * DN_CHUNK), lambda b, i: (0, b, i, 0)),
                   pl.BlockSpec((2, 1, 1, 8, LANES), lambda b, i: (0, b, i, 0, 0))],
        out_shape=[jax.ShapeDtypeStruct((2, B, T, 4 * DN_W), BF16),
                   jax.ShapeDtypeStruct((2, B, T, DN_HEADS * DN_CHUNK), BF16),
                   jax.ShapeDtypeStruct((2, B, ns, 8, LANES), F32)],
        compiler_params=_cparams("parallel", "parallel"),
        name="dn_chunk",
    )(qkv, qkv, qkv, ab, abt, arow, drow, acol, dcol)


def _dn_rec_body(uw_ref, qk_ref, gl_ref, s0_ref, o_ref, sout_ref, S_ref, *, bb, nch, G):
    C, H = DN_CHUNK, DN_HEADS
    d = pl.program_id(0)
    c = pl.program_id(2)

    @pl.when(c == 0)
    def _():
        S_ref[...] = s0_ref[...]

    r = (c + d * (nch - 1 - 2 * c)) % G
    heads = [(b, h) for b in range(bb) for h in range(H)]
    ts = []
    for b, h in heads:
        wq = jnp.concatenate([uw_ref[b, :, DN_W + h * DN_DK:DN_W + (h + 1) * DN_DK],
                              uw_ref[b, :, 2 * DN_W + h * DN_DK:2 * DN_W + (h + 1) * DN_DK]], axis=0)
        ts.append(_dot(wq, S_ref[b, h].astype(BF16)))
    for (b, h), t in zip(heads, ts):
        sl = slice(h * DN_DV, (h + 1) * DN_DV)
        v_new = (uw_ref[b, :, sl].astype(F32) - t[:C]).astype(BF16)
        o_ref[b, :, sl] = t[C:] + _dot(qk_ref[b, :, h * C:(h + 1) * C], v_new)
        gl_row = gl_ref[b, 0, pl.ds(r, 1), :]
        gl = jnp.where(d == 0, gl_row[:, h:h + 1], gl_row[:, H + h:H + h + 1])
        kd = uw_ref[b, :, 3 * DN_W + h * DN_DK:3 * DN_W + (h + 1) * DN_DK]
        S_ref[b, h] = S_ref[b, h] * gl + _dot_tn(kd, v_new)

    @pl.when(c == nch - 1)
    def _():
        sout_ref[...] = S_ref[...]


def _dn_rec(uw, qk, gl, s0, bb, G):
    _, B, T, _ = uw.shape
    C = DN_CHUNK
    nch = T // C

    def cidx(d, c):
        return c + d * (nch - 1 - 2 * c)

    s_spec = pl.BlockSpec((None, bb, DN_HEADS, DN_DK, DN_DV), lambda d, b, c: (d, b, 0, 0, 0))
    return pl.pallas_call(
        functools.partial(_dn_rec_body, bb=bb, nch=nch, G=G),
        grid=(2, B // bb, nch),
        in_specs=[pl.BlockSpec((None, bb, C, 4 * DN_W), lambda d, b, c: (d, b, cidx(d, c), 0)),
                  pl.BlockSpec((None, bb, C, DN_HEADS * C), lambda d, b, c: (d, b, cidx(d, c), 0)),
                  pl.BlockSpec((None, bb, 1, 8, LANES), lambda d, b, c: (d, b, cidx(d, c) // G, 0, 0)),
                  s_spec],
        out_specs=[pl.BlockSpec((None, bb, C, DN_W), lambda d, b, c: (d, b, cidx(d, c), 0)), s_spec],
        out_shape=[jax.ShapeDtypeStruct((2, B, T, DN_W), F32),
                   jax.ShapeDtypeStruct((2, B, DN_HEADS, DN_DK, DN_DV), F32)],
        scratch_shapes=[pltpu.VMEM((bb, DN_HEADS, DN_DK, DN_DV), F32)],
        compiler_params=_cparams("parallel", "parallel", "arbitrary"),
        name="dn_rec",
    )(uw, qk, gl, s0)


def _s5_param_body(are_ref, aim_ref, ldt_ref, bre_ref, bim_ref, lam_ref, bd_ref):
    a_re = are_ref[...]
    a_im = aim_ref[...]
    dt = jnp.exp(ldt_ref[...])
    mag = jnp.exp(a_re * dt)
    lam_re = mag * jnp.cos(a_im * dt)
    lam_im = mag * jnp.sin(a_im * dt)
    den = a_re * a_re + a_im * a_im
    nr = lam_re - 1.0
    ni = lam_im
    coef_re = (nr * a_re + ni * a_im) / den
    coef_im = (ni * a_re - nr * a_im) / den
    lam_ref[0] = lam_re
    lam_ref[1] = lam_im
    b_re = bre_ref[...]
    b_im = bim_ref[...]
    for d in range(2):
        cr = coef_re[d:d + 1, :]
        ci = coef_im[d:d + 1, :]
        bd_ref[d, 0] = cr * b_re - ci * b_im
        bd_ref[d, 1] = cr * b_im + ci * b_re


def _s5_params(A_re, A_im, log_dt, B_re, B_im):
    a_re = A_re.reshape(2, S5_NSTATE)
    a_im = A_im.reshape(2, S5_NSTATE)
    ldt = jnp.repeat(log_dt, S5_STATE, axis=1)
    b_re_t = jnp.transpose(B_re, (2, 0, 1)).reshape(S5_GROUP, S5_NSTATE)
    b_im_t = jnp.transpose(B_im, (2, 0, 1)).reshape(S5_GROUP, S5_NSTATE)
    lam, bd = pl.pallas_call(
        _s5_param_body,
        out_shape=[jax.ShapeDtypeStruct((2, 2, S5_NSTATE), F32),
                   jax.ShapeDtypeStruct((2, 2, S5_GROUP, S5_NSTATE), F32)],
        name="s5_params",
    )(a_re, a_im, ldt, b_re_t, b_im_t)
    return lam, bd


def _s5_block_weights(bd, C_re, C_im):
    eye = jnp.eye(S5_GBLK, dtype=F32)
    bd6 = bd.reshape(2, 2, S5_GROUP, S5_NBLK, S5_GBLK, S5_STATE)
    w = jnp.einsum('dchjmp,lm->djlhcmp', bd6, eye)
    w_drive = w.reshape(2, S5_NBLK, S5_GBLK * S5_GROUP, 2 * S5_SBLK).astype(BF16)
    cc = jnp.stack([C_re, -C_im], axis=0).reshape(2, S5_NBLK, S5_GBLK, S5_GROUP, S5_STATE)
    cm = jnp.einsum('cjmhp,lm->jcmplh', cc, eye)
    w_read = cm.reshape(S5_NBLK, 2, S5_SBLK, S5_GBLK * S5_GROUP).astype(BF16)
    return w_drive, w_read


def _s5_scan_body(u_ref, wd_ref, wr_ref, lam_ref, h0_ref, y_ref, hout_ref, xre, xim, hst,
                  *, B, Tc, nch, lb):
    d = pl.program_id(0)
    c = pl.program_id(1)

    @pl.when(c == 0)
    def _():
        hst[...] = h0_ref[0]

    ub = u_ref[...].astype(BF16)
    for j in range(S5_NBLK):
        drv = _dot(ub[:, j * LANES:(j + 1) * LANES], wd_ref[0, j])
        xre[:, j * S5_SBLK:(j + 1) * S5_SBLK] = drv[:, :S5_SBLK]
        xim[:, j * S5_SBLK:(j + 1) * S5_SBLK] = drv[:, S5_SBLK:]

    for lbi in range(S5_NSTATE // lb):
        ls = slice(lbi * lb, (lbi + 1) * lb)
        lr = jnp.broadcast_to(lam_ref[0, 0, :, ls], (B, lb))
        li = jnp.broadcast_to(lam_ref[1, 0, :, ls], (B, lb))

        def step(s, carry):
            hr, hi = carry
            t = s + d * (Tc - 1 - 2 * s)
            r0 = pl.multiple_of(t * B, B)
            nr = lr * hr - li * hi + xre[pl.ds(r0, B), ls]
            ni = lr * hi + li * hr + xim[pl.ds(r0, B), ls]
            xre[pl.ds(r0, B), ls] = nr
            xim[pl.ds(r0, B), ls] = ni
            return nr, ni

        hr, hi = lax.fori_loop(0, Tc, step, (hst[0, :, ls], hst[1, :, ls]), unroll=4)
        hst[0, :, ls] = hr
        hst[1, :, ls] = hi

    for j in range(S5_NBLK):
        ss = slice(j * S5_SBLK, (j + 1) * S5_SBLK)
        y = _dot(xre[:, ss].astype(BF16), wr_ref[j, 0]) + _dot(xim[:, ss].astype(BF16), wr_ref[j, 1])
        y_ref[0, :, j * LANES:(j + 1) * LANES] = y

    @pl.when(c == nch - 1)
    def _():
        hout_ref[0] = hst[...]


def _s5_scan(u_tb, w_drive, w_read, lam, h0, B, Tc):
    TB = u_tb.shape[0]
    T = TB // B
    Tc = _tile(T, Tc)
    nch = T // Tc
    lam4 = lam.reshape(2, 2, 1, S5_NSTATE)

    def cidx(d, c):
        return c + d * (nch - 1 - 2 * c)

    return pl.pallas_call(
        functools.partial(_s5_scan_body, B=B, Tc=Tc, nch=nch, lb=256),
        grid=(2, nch),
        in_specs=[pl.BlockSpec((Tc * B, S5_WIDTH), lambda d, c: (cidx(d, c), 0)),
                  pl.BlockSpec((1, S5_NBLK, LANES, 2 * S5_SBLK), lambda d, c: (d, 0, 0, 0)),
                  pl.BlockSpec((S5_NBLK, 2, S5_SBLK, LANES), lambda d, c: (0, 0, 0, 0)),
                  pl.BlockSpec((2, 1, 1, S5_NSTATE), lambda d, c: (0, d, 0, 0)),
                  pl.BlockSpec((1, 2, B, S5_NSTATE), lambda d, c: (d, 0, 0, 0))],
        out_specs=[pl.BlockSpec((1, Tc * B, S5_WIDTH), lambda d, c: (d, cidx(d, c), 0)),
                   pl.BlockSpec((1, 2, B, S5_NSTATE), lambda d, c: (d, 0, 0, 0))],
        out_shape=[jax.ShapeDtypeStruct((2, TB, S5_WIDTH), F32),
                   jax.ShapeDtypeStruct((2, 2, B, S5_NSTATE), F32)],
        scratch_shapes=[pltpu.VMEM((Tc * B, S5_NSTATE), F32),
                        pltpu.VMEM((Tc * B, S5_NSTATE), F32),
                        pltpu.VMEM((2, B, S5_NSTATE), F32)],
        compiler_params=_cparams("parallel", "arbitrary"),
        name="s5_scan",
    )(u_tb, w_drive, w_read, lam4, h0)


def _ab_out_body(x_ref, mod_ref, o_ref, z_ref, u_ref, y_ref, ng_ref, dsk_ref, gw_ref, gb_ref,
                 woa_ref, wob_ref, out_ref):
    o = o_ref[0, 0] + o_ref[1, 0]
    z = z_ref[0]
    parts = []
    for h in range(DN_HEADS):
        sl = slice(h * DN_DV, (h + 1) * DN_DV)
        parts.append(_rms(o[:, sl]) * ng_ref[...] * _silu(z[:, sl]))
    a_out = jnp.concatenate(parts, axis=1)
    y = y_ref[0] + y_ref[1] + dsk_ref[...] * u_ref[...]
    y = _gelu_tanh(y)
    b_out = y * jax.nn.sigmoid(_dot(y.astype(BF16), gw_ref[...]) + gb_ref[...])
    mix = _dot(a_out.astype(BF16), woa_ref[...]) + _dot(b_out.astype(BF16), wob_ref[...])
    out_ref[0] = x_ref[0] + mod_ref[0, 2:3, :] * mix


def _ab_out(x, mod, o_dn, z, u_tb, y_tb, dn_norm_g, d_skip, glu_w, glu_b, w_out, tm):
    B, T, D = x.shape
    tm = _tile(T, tm)
    per_batch_mod = mod.shape[0] != 1
    y3 = y_tb.reshape(2, T, B * S5_WIDTH)
    full = lambda *shape: pl.BlockSpec(shape, lambda b, i: (0,) * len(shape))
    return pl.pallas_call(
        _ab_out_body,
        grid=(B, T // tm),
        in_specs=[pl.BlockSpec((1, tm, D), lambda b, i: (b, i, 0)),
                  pl.BlockSpec((1, 6, D), (lambda b, i: (b, 0, 0)) if per_batch_mod else (lambda b, i: (0, 0, 0))),
                  pl.BlockSpec((2, 1, tm, DN_W), lambda b, i: (0, b, i, 0)),
                  pl.BlockSpec((1, tm, DN_W), lambda b, i: (b, i, 0)),
                  pl.BlockSpec((tm, S5_WIDTH), lambda b, i: (i, b)),
                  pl.BlockSpec((2, tm, S5_WIDTH), lambda b, i: (0, i, b)),
                  full(1, DN_DV), full(1, S5_WIDTH), full(S5_WIDTH, S5_WIDTH), full(1, S5_WIDTH),
                  full(DN_W, D), full(S5_WIDTH, D)],
        out_specs=pl.BlockSpec((1, tm, D), lambda b, i: (b, i, 0)),
        out_shape=jax.ShapeDtypeStruct((B, T, D), F32),
        compiler_params=_cparams("parallel", "parallel"),
        name="ab_out",
    )(x, mod, o_dn, z, u_tb, y3, dn_norm_g.reshape(1, DN_DV), d_skip.reshape(1, S5_WIDTH),
      glu_w.astype(BF16), glu_b.reshape(1, S5_WIDTH), w_out[:DN_W].astype(BF16), w_out[DN_W:].astype(BF16))


def _route(h, rw_ref, rb_ref):
    logits = _dot_nt(rw_ref[...], h, precision=HIGHEST)
    scores = jax.nn.sigmoid(logits)
    choice = scores + rb_ref[...]
    rows = [choice[e:e + 1, :] for e in range(N_EXPERTS)]
    neg_inf = jnp.float32(-jnp.inf)
    gs = []
    for g in range(N_EXPERT_GROUPS):
        r = rows[g * EXPERTS_PER_GROUP:(g + 1) * EXPERTS_PER_GROUP]
        best = None
        for a in range(EXPERTS_PER_GROUP):
            for b in range(a + 1, EXPERTS_PER_GROUP):
                s = r[a] + r[b]
                best = s if best is None else jnp.maximum(best, s)
        gs.append(best)
    best_val = gs[0]
    best_g = jnp.zeros_like(best_val, dtype=jnp.int32)
    for g in range(1, N_EXPERT_GROUPS):
        better = gs[g] > best_val
        best_val = jnp.where(better, gs[g], best_val)
        best_g = jnp.where(better, g, best_g)
    masked = [jnp.where(best_g == e // EXPERTS_PER_GROUP, rows[e], neg_inf) for e in range(N_EXPERTS)]
    m1 = masked[0]
    for e in range(1, N_EXPERTS):
        m1 = jnp.maximum(m1, masked[e])
    i1 = jnp.full_like(best_g, N_EXPERTS)
    for e in reversed(range(N_EXPERTS)):
        i1 = jnp.where(masked[e] == m1, e, i1)
    rest = [jnp.where(i1 == e, neg_inf, masked[e]) for e in range(N_EXPERTS)]
    m2 = rest[0]
    for e in range(1, N_EXPERTS):
        m2 = jnp.maximum(m2, rest[e])
    i2 = jnp.full_like(best_g, N_EXPERTS)
    for e in reversed(range(N_EXPERTS)):
        i2 = jnp.where(rest[e] == m2, e, i2)
    eidx = lax.broadcasted_iota(jnp.int32, scores.shape, 0)
    sel1 = eidx == i1
    sel2 = eidx == i2
    w1 = jnp.sum(jnp.where(sel1, scores, 0.0), axis=0, keepdims=True)
    w2 = jnp.sum(jnp.where(sel2, scores, 0.0), axis=0, keepdims=True)
    inv = 1.0 / (w1 + w2)
    return jnp.where(sel1, w1 * inv, 0.0) + jnp.where(sel2, w2 * inv, 0.0)


def _moe_body(x_ref, mod_ref, g_ref, rw_ref, rb_ref, wg_ref, wu_ref, wd_ref, fg_ref, out_ref,
              h_scr, gate_scr, acc, *, final_norm):
    e = pl.program_id(2)

    @pl.when(e == 0)
    def _():
        h = _rms(x_ref[0]) * g_ref[...]
        h = h * (1.0 + mod_ref[0, 4:5, :]) + mod_ref[0, 3:4, :]
        h_scr[...] = h.astype(BF16)
        gates_t = _route(h, rw_ref, rb_ref)
        pad = jnp.zeros((LANES - N_EXPERTS, gates_t.shape[1]), F32)
        gate_scr[...] = jnp.transpose(jnp.concatenate([gates_t, pad], axis=0))
        acc[...] = jnp.zeros_like(acc)

    hb = h_scr[...]
    act = _silu(_dot(hb, wg_ref[0])) * _dot(hb, wu_ref[0])
    lane = lax.broadcasted_iota(jnp.int32, gate_scr.shape, 1)
    ge = jnp.sum(jnp.where(lane == e, gate_scr[...], 0.0), axis=1, keepdims=True)
    acc[...] += ge * _dot(act.astype(BF16), wd_ref[0])

    @pl.when(e == N_EXPERTS - 1)
    def _():
        r = x_ref[0] + mod_ref[0, 5:6, :] * acc[...]
        if final_norm:
            r = _rms(r) * fg_ref[...]
        out_ref[0] = r


def _moe(x, mod, norm_g, router_w, router_bias, wg, wu, wd, final_g, tm):
    Bx, T, D = x.shape
    tm = _tile(T, tm)
    per_batch_mod = mod.shape[0] != 1
    final_norm = final_g is not None
    fg = (final_g if final_norm else jnp.ones((D,), F32)).reshape(1, D)
    full = lambda *shape: pl.BlockSpec(shape, lambda b, i, e: (0,) * len(shape))
    return pl.pallas_call(
        functools.partial(_moe_body, final_norm=final_norm),
        grid=(Bx, T // tm, N_EXPERTS),
        in_specs=[pl.BlockSpec((1, tm, D), lambda b, i, e: (b, i, 0)),
                  pl.BlockSpec((1, 6, D), (lambda b, i, e: (b, 0, 0)) if per_batch_mod else (lambda b, i, e: (0, 0, 0))),
                  full(1, D), full(N_EXPERTS, D), full(N_EXPERTS, 1),
                  pl.BlockSpec((1, D, D_EXPERT), lambda b, i, e: (e, 0, 0)),
                  pl.BlockSpec((1, D, D_EXPERT), lambda b, i, e: (e, 0, 0)),
                  pl.BlockSpec((1, D_EXPERT, D), lambda b, i, e: (e, 0, 0)),
                  full(1, D)],
        out_specs=pl.BlockSpec((1, tm, D), lambda b, i, e: (b, i, 0)),
        out_shape=jax.ShapeDtypeStruct((Bx, T, D), F32),
        scratch_shapes=[pltpu.VMEM((tm, D), BF16), pltpu.VMEM((tm, LANES), F32), pltpu.VMEM((tm, D), F32)],
        compiler_params=_cparams("parallel", "parallel", "arbitrary"),
        name="moe",
    )(x, mod, norm_g.reshape(1, D), router_w.T, router_bias.reshape(N_EXPERTS, 1), wg, wu, wd, fg)


def _mla_q_body(cq_ref, g_ref, wn_ref, wa_ref, wb_ref, cos_ref, sin_ref, qn_ref, qr_ref):
    hb = (_rms(cq_ref[0]) * g_ref[...]).astype(BF16)
    qn_ref[0] = (_dot(hb, wn_ref[...]) * MLA_SCALE).astype(BF16)
    ra = _dot(hb, wa_ref[...])
    rb = _dot(hb, wb_ref[...])
    cos = cos_ref[...]
    sin = sin_ref[...]
    for p in range(MLA_HEADS // 2):
        sl = slice(p * LANES, (p + 1) * LANES)
        qr_ref[0, :, sl] = ((ra[:, sl] * cos + rb[:, sl] * sin) * MLA_SCALE).astype(BF16)


def _mla_q(cq, g, wn, wa, wb, cos_t, sin_t, tm):
    B, T, R = cq.shape
    tm = _tile(T, tm)
    full = lambda *shape: pl.BlockSpec(shape, lambda b, i: (0,) * len(shape))
    return pl.pallas_call(
        _mla_q_body,
        grid=(B, T // tm),
        in_specs=[pl.BlockSpec((1, tm, R), lambda b, i: (b, i, 0)), full(1, R),
                  full(R, MLA_HEADS * MLA_NOPE), full(R, MLA_HEADS * MLA_ROPE), full(R, MLA_HEADS * MLA_ROPE),
                  pl.BlockSpec((tm, LANES), lambda b, i: (i, 0)),
                  pl.BlockSpec((tm, LANES), lambda b, i: (i, 0))],
        out_specs=[pl.BlockSpec((1, tm, MLA_HEADS * MLA_NOPE), lambda b, i: (b, i, 0)),
                   pl.BlockSpec((1, tm, MLA_HEADS * MLA_ROPE), lambda b, i: (b, i, 0))],
        out_shape=[jax.ShapeDtypeStruct((B, T, MLA_HEADS * MLA_NOPE), BF16),
                   jax.ShapeDtypeStruct((B, T, MLA_HEADS * MLA_ROPE), BF16)],
        compiler_params=_cparams("parallel", "parallel"),
        name="mla_q",
    )(cq, g.reshape(1, R), wn, wa, wb, cos_t, sin_t)


def _mla_kv_body(ckv_ref, ka_ref, kb_ref, g_ref, wk_ref, wv_ref, cos_ref, sin_ref, kn_ref, v_ref, kr_ref):
    hb = (_rms(ckv_ref[0]) * g_ref[...]).astype(BF16)
    kn_ref[0] = _dot(hb, wk_ref[...]).astype(BF16)
    v_ref[0] = _dot(hb, wv_ref[...]).astype(BF16)
    kr_ref[0] = (ka_ref[0] * cos_ref[...] + kb_ref[0] * sin_ref[...]).astype(BF16)


def _mla_kv(ckv, kra, krb, g, wk, wv, cos_t, sin_t, tm):
    B, T, R = ckv.shape
    tm = _tile(T, tm)
    W = MLA_HEADS * MLA_NOPE
    full = lambda *shape: pl.BlockSpec(shape, lambda b, i: (0,) * len(shape))
    tok = lambda n: pl.BlockSpec((1, tm, n), lambda b, i: (b, i, 0))
    return pl.pallas_call(
        _mla_kv_body,
        grid=(B, T // tm),
        in_specs=[tok(R), tok(LANES), tok(LANES), full(1, R), full(R, W), full(R, W),
                  pl.BlockSpec((tm, LANES), lambda b, i: (i, 0)),
                  pl.BlockSpec((tm, LANES), lambda b, i: (i, 0))],
        out_specs=[tok(W), tok(W), tok(LANES)],
        out_shape=[jax.ShapeDtypeStruct((B, T, W), BF16), jax.ShapeDtypeStruct((B, T, W), BF16),
                   jax.ShapeDtypeStruct((B, T, LANES), BF16)],
        compiler_params=_cparams("parallel", "parallel"),
        name="mla_kv",
    )(ckv, kra, krb, g.reshape(1, R), wk, wv, cos_t, sin_t)


def _mla_attn_body(qn_ref, qr_ref, kn_ref, kr_ref, v_ref, o_ref):
    tq = qn_ref.shape[1]
    lane = lax.broadcasted_iota(jnp.int32, (tq, LANES), 1)
    kr = kr_ref[0]
    for h in range(MLA_HEADS):
        sl = slice(h * MLA_NOPE, (h + 1) * MLA_NOPE)
        p = h // 2
        qr = qr_ref[0, :, p * LANES:(p + 1) * LANES]
        mine = (lane < MLA_ROPE) if h % 2 == 0 else (lane >= MLA_ROPE)
        qr = jnp.where(mine, qr, jnp.zeros_like(qr))
        s = _dot_nt(qn_ref[0, :, sl], kn_ref[0, :, sl]) + _dot_nt(qr, kr)
        m = jnp.max(s, axis=-1, keepdims=True)
        e = jnp.exp(s - m)
        l = jnp.sum(e, axis=-1, keepdims=True)
        o = _dot(e.astype(BF16), v_ref[0, :, sl]) / l
        o_ref[0, :, sl] = o.astype(o_ref.dtype)


def _mla_attn(qn, qr, kn, kr, v, tq):
    B, T, W = qn.shape
    Tk = kn.shape[1]
    tq = _tile(T, tq)
    return pl.pallas_call(
        _mla_attn_body,
        grid=(B, T // tq),
        in_specs=[pl.BlockSpec((1, tq, W), lambda b, i: (b, i, 0)),
                  pl.BlockSpec((1, tq, MLA_HEADS * MLA_ROPE), lambda b, i: (b, i, 0)),
                  pl.BlockSpec((1, Tk, W), lambda b, i: (b, 0, 0)),
                  pl.BlockSpec((1, Tk, LANES), lambda b, i: (b, 0, 0)),
                  pl.BlockSpec((1, Tk, W), lambda b, i: (b, 0, 0))],
        out_specs=pl.BlockSpec((1, tq, W), lambda b, i: (b, i, 0)),
        out_shape=jax.ShapeDtypeStruct((B, T, W), BF16),
        compiler_params=_cparams("parallel", "arbitrary"),
        name="mla_attn",
    )(qn, qr, kn, kr, v)


def _proj_res_body(x_ref, mod_ref, a_ref, w_ref, out_ref):
    out_ref[0] = x_ref[0] + mod_ref[0, 2:3, :] * _dot(a_ref[0], w_ref[...])


def _proj_res(x, mod, a, w, tm):
    B, T, D = x.shape
    K = a.shape[2]
    tm = _tile(T, tm)
    per_batch_mod = mod.shape[0] != 1
    return pl.pallas_call(
        _proj_res_body,
        grid=(B, T // tm),
        in_specs=[pl.BlockSpec((1, tm, D), lambda b, i: (b, i, 0)),
                  pl.BlockSpec((1, 6, D), (lambda b, i: (b, 0, 0)) if per_batch_mod else (lambda b, i: (0, 0, 0))),
                  pl.BlockSpec((1, tm, K), lambda b, i: (b, i, 0)),
                  pl.BlockSpec((K, D), lambda b, i: (0, 0))],
        out_specs=pl.BlockSpec((1, tm, D), lambda b, i: (b, i, 0)),
        out_shape=jax.ShapeDtypeStruct((B, T, D), F32),
        compiler_params=_cparams("parallel", "parallel"),
        name="proj_res",
    )(x, mod, a, w)


def _rope_tables(T, n_ctx):
    rows = T // GRID_W
    row = jnp.repeat(jnp.arange(rows, dtype=F32), GRID_W)
    col = jnp.tile(jnp.arange(GRID_W, dtype=F32), rows)
    n_freq = MLA_ROPE // 4
    inv = ROPE_THETA ** (-jnp.arange(n_freq, dtype=F32) / n_freq)
    ang = jnp.concatenate([row[:, None] * inv, col[:, None] * inv], axis=-1)
    cos, sin = jnp.cos(ang), jnp.sin(ang)
    cos_t = jnp.concatenate([cos, cos, cos, cos], axis=-1)
    sin_t = jnp.concatenate([-sin, sin, -sin, sin], axis=-1)
    cos_k = jnp.concatenate([jnp.ones((n_ctx, LANES), F32), cos_t], axis=0)
    sin_k = jnp.concatenate([jnp.zeros((n_ctx, LANES), F32), sin_t], axis=0)
    return cos_t, sin_t, cos_k, sin_k


def _layer_ab(x, ctx, mod_l, mod_c, norm1_g, w_in, conv_w, a_log, dt_bias, dn_norm_g, A_re, A_im, log_dt,
              B_re, B_im, C_re, C_im, D_skip, glu_w, glu_b, w_out):
    B, T, D = x.shape
    Tc = ctx.shape[1]
    q0, k0, v0, z0, a0, b0, u0 = 0, 512, 1024, 1536, 2048, 2056, 2064
    w_qkv = w_in[:, q0:z0].astype(BF16)
    w_z = w_in[:, z0:a0].astype(BF16)
    w_ab = jnp.zeros((D, LANES), F32).at[:, :16].set(w_in[:, a0:u0]).astype(BF16)
    w_u = w_in[:, u0:].astype(BF16)
    w_abt = w_in[:, a0:u0].T.astype(BF16)
    ws = [w_qkv, w_z, w_ab, w_abt, w_u]
    dts = [F32] * 5
    kinds = ["n", "n", "n", "t", "tb"]
    dn_group = 4

    lam, bd = _s5_params(A_re, A_im, log_dt, B_re, B_im)
    w_drive, w_read = _s5_block_weights(bd, C_re, C_im)

    streams = []
    dn_state = jnp.zeros((2, B, DN_HEADS, DN_DK, DN_DV), F32)
    s5_state = jnp.zeros((2, 2, B, S5_NSTATE), F32)
    for xs, mod in ((ctx, mod_c), (x, mod_l)):
        qkv, z, ab, abt, u_tb = _modmm(xs, mod, norm1_g, ws, dts, kinds, 0, 512)
        qkv = _dn_prep(qkv, conv_w, 512)
        uw, qk, gl = _dn_chunk(qkv, ab, abt, a_log, dt_bias, dn_group)
        o_dn, dn_state = _dn_rec(uw, qk, gl, dn_state, 4, dn_group)
        u2 = u_tb.reshape(xs.shape[1] * B, S5_WIDTH)
        y_tb, s5_state = _s5_scan(u2, w_drive, w_read, lam, s5_state, B, 32)
        streams.append(_ab_out(xs, mod, o_dn, z, u_tb, y_tb, dn_norm_g, D_skip.reshape(-1), glu_w, glu_b,
                               w_out, 512))
    return streams[1], streams[0]


def _layer_mla(x, ctx, mod_l, mod_c, norm1_g, w_in, q_norm_g, w_q_up, kv_norm_g, w_kv_up, w_out, need_ctx):
    assert not need_ctx, "context attention output is only needed when a later layer follows"
    B, T, D = x.shape
    n_ctx = ctx.shape[1]
    qr, kvr = MLA_Q_RANK, MLA_KV_RANK
    half = MLA_ROPE // 2
    w_cq = w_in[:, :qr].astype(BF16)
    w_ckv = w_in[:, qr:qr + kvr].astype(BF16)
    wk1 = w_in[:, qr + kvr:qr + kvr + half]
    wk2 = w_in[:, qr + kvr + half:]
    w_ka = jnp.concatenate([wk1, wk2, wk1, wk2], axis=1).astype(BF16)
    w_kb = jnp.concatenate([wk2, wk1, wk2, wk1], axis=1).astype(BF16)
    ws = [w_cq, w_ckv, w_ka, w_kb]
    cq_l, ckv_l, ka_l, kb_l = _modmm(x, mod_l, norm1_g, ws, [F32] * 4, ["n"] * 4, 0, 512)
    _, ckv_c, ka_c, kb_c = _modmm(ctx, mod_c, norm1_g, ws, [F32] * 4, ["n"] * 4, 0, 512)

    wq = w_q_up.reshape(qr, MLA_HEADS, MLA_NOPE + MLA_ROPE)
    wq_n = wq[:, :, :MLA_NOPE].reshape(qr, MLA_HEADS * MLA_NOPE).astype(BF16)
    x1 = wq[:, :, MLA_NOPE:MLA_NOPE + half]
    x2 = wq[:, :, MLA_NOPE + half:]
    wq_a = jnp.concatenate([x1, x2], axis=2).reshape(qr, MLA_HEADS * MLA_ROPE).astype(BF16)
    wq_b = jnp.concatenate([x2, x1], axis=2).reshape(qr, MLA_HEADS * MLA_ROPE).astype(BF16)
    wkv = w_kv_up.reshape(kvr, MLA_HEADS, MLA_NOPE + MLA_V)
    wk_n = wkv[:, :, :MLA_NOPE].reshape(kvr, MLA_HEADS * MLA_NOPE).astype(BF16)
    wv = wkv[:, :, MLA_NOPE:].reshape(kvr, MLA_HEADS * MLA_V).astype(BF16)

    cos_t, sin_t, cos_k, sin_k = _rope_tables(T, n_ctx)
    qn, qrope = _mla_q(cq_l, q_norm_g, wq_n, wq_a, wq_b, cos_t, sin_t, 512)
    ckv = jnp.concatenate([ckv_c, ckv_l], axis=1)
    ka = jnp.concatenate([ka_c, ka_l], axis=1)
    kb = jnp.concatenate([kb_c, kb_l], axis=1)
    kn, v, kr = _mla_kv(ckv, ka, kb, kv_norm_g, wk_n, wv, cos_k, sin_k, 256)
    o = _mla_attn(qn, qrope, kn, kr, v, 256)
    return _proj_res(x, mod_l, o, w_out.astype(BF16), 512)


def kernel(x, c, ctx, c_ctx, ada_w, ada_b, norm1_g, norm2_g, ab_w_in, dn_conv_w, dn_A_log, dn_dt_bias, dn_norm_g, s5_A_re, s5_A_im, s5_log_dt, s5_B_re, s5_B_im, s5_C_re, s5_C_im, s5_D, s5_glu_w, s5_glu_b, ab_w_out, mla_w_in, mla_q_norm_g, mla_w_q_up, mla_kv_norm_g, mla_w_kv_up, mla_w_out, router_w, router_bias, moe_w_gate, moe_w_up, moe_w_down, final_norm_g):
    B, T, D = x.shape
    n_ctx = ctx.shape[1]
    depth = ada_w.shape[0]
    n_cond = -(-(B + 1) // 8) * 8
    cond = jnp.zeros((n_cond, D), F32).at[:B].set(c).at[B].set(c_ctx)
    mods = _adaln_all(cond, ada_w, ada_b).reshape(depth, n_cond, 6, D)
    wg = moe_w_gate.astype(BF16)
    wu = moe_w_up.astype(BF16)
    wd = moe_w_down.astype(BF16)
    for i in range(depth):
        last = i == depth - 1
        j = i // 2
        mod_l = mods[i, :B]
        mod_c = mods[i, B:B + 1]
        if i % 2 == 0:
            x, ctx_new = _layer_ab(x, ctx, mod_l, mod_c, norm1_g[i], ab_w_in[j], dn_conv_w[j], dn_A_log[j],
                                   dn_dt_bias[j], dn_norm_g[j], s5_A_re[j], s5_A_im[j], s5_log_dt[j],
                                   s5_B_re[j], s5_B_im[j], s5_C_re[j], s5_C_im[j], s5_D[j], s5_glu_w[j],
                                   s5_glu_b[j], ab_w_out[j])
        else:
            x = _layer_mla(x, ctx, mod_l, mod_c, norm1_g[i], mla_w_in[j], mla_q_norm_g[j], mla_w_q_up[j],
                           mla_kv_norm_g[j], mla_w_kv_up[j], mla_w_out[j], not last)
            ctx_new = None
        x = _moe(x, mod_l, norm2_g[i], router_w, router_bias, wg[i], wu[i], wd[i],
                 final_norm_g if last else None, 1024)
        if not last:
            ctx_flat = _moe(ctx_new.reshape(1, B * n_ctx, D), mod_c, norm2_g[i], router_w, router_bias,
                            wg[i], wu[i], wd[i], None, 1024)
            ctx = ctx_flat.reshape(B, n_ctx, D)
    return x
```

```python
import functools
import math

import jax
import jax.numpy as jnp
from jax import lax
from jax.experimental import pallas as pl
from jax.experimental.pallas import tpu as pltpu
from jax.experimental.pallas import tpu_sc as plsc

F32 = jnp.float32
BF16 = jnp.bfloat16
HIGHEST = lax.Precision.HIGHEST

NORM_EPS = 1e-6
GRID_W = 64
ROPE_THETA = 10000.0

DN_HEADS = 4
DN_DK = 128
DN_DV = 128
DN_CONV = 5
DN_CHUNK = 64
DN_W = DN_HEADS * DN_DK

S5_WIDTH = 512
S5_GROUP = 16
S5_GROUPS = 32
S5_STATE = 64
S5_NSTATE = S5_GROUPS * S5_STATE
S5_GBLK = 8
S5_NBLK = S5_GROUPS // S5_GBLK
S5_SBLK = S5_GBLK * S5_STATE

MLA_HEADS = 8
MLA_Q_RANK = 384
MLA_KV_RANK = 256
MLA_NOPE = 128
MLA_ROPE = 64
MLA_V = 128
MLA_SCALE = (MLA_NOPE + MLA_ROPE) ** -0.5

N_EXPERTS = 16
N_EXPERT_GROUPS = 4
EXPERTS_PER_GROUP = 4
D_EXPERT = 512
MOE_TILE = 512

N_PLANES = 8
SC_CORES = 2
SC_WORKERS = 32
SC_CHUNK_ROWS = 128

LANES = 128
VMEM_LIMIT_BYTES = 56 * 1024 * 1024


def _tile(n, pref):
    t = min(pref, n)
    while n % t or t % 8:
        t -= 1
    return t


def _cparams(*sem):
    return pltpu.CompilerParams(dimension_semantics=sem, vmem_limit_bytes=VMEM_LIMIT_BYTES)


def _silu(x):
    return x * jax.nn.sigmoid(x)


def _softplus(x):
    return jnp.maximum(x, 0.0) + jnp.log(1.0 + jnp.exp(-jnp.abs(x)))


def _gelu_tanh(x):
    return 0.5 * x * (1.0 + jnp.tanh(math.sqrt(2.0 / math.pi) * (x + 0.044715 * (x * x * x))))


def _rms(x):
    return x * lax.rsqrt(jnp.mean(x * x, axis=-1, keepdims=True) + NORM_EPS)


def _dot(a, b):
    return jnp.dot(a, b, preferred_element_type=F32)


def _dot_nt(a, b, precision=None):
    return lax.dot_general(a, b, (((1,), (1,)), ((), ())), preferred_element_type=F32,
                           precision=precision)


def _dot_tn(a, b):
    return lax.dot_general(a, b, (((0,), (0,)), ((), ())), preferred_element_type=F32)


def _ada_body(c_ref, w_ref, b_ref, o_ref):
    c = c_ref[...]
    o_ref[0] = _dot(_silu(c).astype(BF16), w_ref[0].astype(BF16)) + b_ref[0]


def _adaln_all(cond, ada_w, ada_b):
    L, D, D6 = ada_w.shape
    R = cond.shape[0]
    tn = 1536
    return pl.pallas_call(
        _ada_body,
        grid=(L, D6 // tn),
        in_specs=[pl.BlockSpec((R, D), lambda l, j: (0, 0)),
                  pl.BlockSpec((1, D, tn), lambda l, j: (l, 0, j)),
                  pl.BlockSpec((1, 1, tn), lambda l, j: (l, 0, j))],
        out_specs=pl.BlockSpec((1, R, tn), lambda l, j: (l, 0, j)),
        out_shape=jax.ShapeDtypeStruct((L, R, D6), F32),
        compiler_params=_cparams("parallel", "parallel"),
        name="adaln",
    )(cond, ada_w, ada_b.reshape(L, 1, D6))


def _modmm_body(x_ref, mod_ref, g_ref, *refs, kinds, shift_row):
    n_out = len(kinds)
    w_refs, o_refs = refs[:n_out], refs[n_out:]
    h = _rms(x_ref[0]) * g_ref[...]
    h = h * (1.0 + mod_ref[0, shift_row + 1:shift_row + 2, :]) + mod_ref[0, shift_row:shift_row + 1, :]
    hb = h.astype(BF16)
    for w_ref, o_ref, kind in zip(w_refs, o_refs, kinds):
        if kind == "t":
            o_ref[0] = _dot_nt(w_ref[...], hb).astype(o_ref.dtype)
        elif kind == "tb":
            o_ref[...] = _dot(hb, w_ref[...]).astype(o_ref.dtype)
        else:
            o_ref[0] = _dot(hb, w_ref[...]).astype(o_ref.dtype)


def _modmm(x, mod, g, ws, out_dtypes, kinds, shift_row, tm):
    Bx, T, D = x.shape
    tm = _tile(T, tm)
    per_batch_mod = mod.shape[0] != 1
    in_specs = [pl.BlockSpec((1, tm, D), lambda b, i: (b, i, 0)),
                pl.BlockSpec((1, 6, D), (lambda b, i: (b, 0, 0)) if per_batch_mod else (lambda b, i: (0, 0, 0))),
                pl.BlockSpec((1, D), lambda b, i: (0, 0))]
    out_specs, out_shape = [], []
    for w, dt, kind in zip(ws, out_dtypes, kinds):
        in_specs.append(pl.BlockSpec(w.shape, lambda b, i: (0, 0)))
        if kind == "t":
            n = w.shape[0]
            out_specs.append(pl.BlockSpec((1, n, tm), lambda b, i: (b, 0, i)))
            out_shape.append(jax.ShapeDtypeStruct((Bx, n, T), dt))
        elif kind == "tb":
            n = w.shape[1]
            out_specs.append(pl.BlockSpec((tm, n), lambda b, i: (i, b)))
            out_shape.append(jax.ShapeDtypeStruct((T, Bx * n), dt))
        else:
            n = w.shape[1]
            out_specs.append(pl.BlockSpec((1, tm, n), lambda b, i: (b, i, 0)))
            out_shape.append(jax.ShapeDtypeStruct((Bx, T, n), dt))
    return pl.pallas_call(
        functools.partial(_modmm_body, kinds=tuple(kinds), shift_row=shift_row),
        grid=(Bx, T // tm),
        in_specs=in_specs, out_specs=out_specs, out_shape=out_shape,
        compiler_params=_cparams("parallel", "parallel"),
        name="modmm",
    )(x, mod, g.reshape(1, D), *ws)


def _dn_prep_body(x_ref, xp_ref, xn_ref, w_ref, o_ref, buf, *, tm, nt):
    i = pl.program_id(1)
    j = pl.program_id(2)
    buf[0:8, :] = jnp.where(i == 0, 0.0, xp_ref[0])
    buf[8:8 + tm, :] = x_ref[0]
    buf[8 + tm:16 + tm, :] = jnp.where(i == nt - 1, 0.0, xn_ref[0])
    acc = buf[pl.ds(8 - DN_CONV // 2, tm), :] * w_ref[0:1, :]
    for kk in range(1, DN_CONV):
        acc = acc + buf[pl.ds(8 - DN_CONV // 2 + kk, tm), :] * w_ref[kk:kk + 1, :]
    y = _silu(acc)
    q_scale = jnp.where(j == 0, DN_DK ** -0.5, 1.0)
    for h in range(DN_HEADS):
        yh = y[:, h * DN_DK:(h + 1) * DN_DK]
        r = lax.rsqrt(jnp.sum(yh * yh, axis=-1, keepdims=True) + NORM_EPS) * q_scale
        o_ref[0, :, h * DN_DK:(h + 1) * DN_DK] = yh * jnp.where(j == 2, 1.0, r)


def _dn_prep(qkv, conv_w, tm):
    B, T, W3 = qkv.shape
    tm = _tile(T, tm)
    nt = T // tm
    r8 = tm // 8
    return pl.pallas_call(
        functools.partial(_dn_prep_body, tm=tm, nt=nt),
        grid=(B, nt, 3),
        in_specs=[pl.BlockSpec((1, tm, DN_W), lambda b, i, j: (b, i, j)),
                  pl.BlockSpec((1, 8, DN_W), lambda b, i, j: (b, jnp.maximum(i * r8 - 1, 0), j)),
                  pl.BlockSpec((1, 8, DN_W), lambda b, i, j: (b, jnp.minimum((i + 1) * r8, T // 8 - 1), j)),
                  pl.BlockSpec((DN_CONV, DN_W), lambda b, i, j: (0, j))],
        out_specs=pl.BlockSpec((1, tm, DN_W), lambda b, i, j: (b, i, j)),
        out_shape=jax.ShapeDtypeStruct((B, T, W3), F32),
        scratch_shapes=[pltpu.VMEM((tm + 16, DN_W), F32)],
        compiler_params=_cparams("parallel", "parallel", "parallel"),
        name="dn_prep",
    )(qkv, qkv, qkv, conv_w)


def _heads_to_lanes(cols, width):
    return jnp.concatenate([jnp.broadcast_to(c, (c.shape[0], width)) for c in cols], axis=1)


def _block_diag(x, nblk):
    C, W = x.shape
    w = W // nblk
    t = jnp.concatenate([x] * nblk, axis=0)
    rb = lax.broadcasted_iota(jnp.int32, t.shape, 0) // C
    cb = lax.broadcasted_iota(jnp.int32, t.shape, 1) // w
    return jnp.where(rb == cb, t, jnp.zeros_like(t))


def _dn_chunk_body(q_ref, k_ref, v_ref, ab_ref, abt_ref, arow_ref, drow_ref, acol_ref, dcol_ref,
                   uw_ref, qk_ref, gl_ref, *, G):
    C, H = DN_CHUNK, DN_HEADS
    Tg = G * C
    gl_ref[...] = jnp.zeros_like(gl_ref)
    ab = ab_ref[0]
    abt = abt_ref[0]
    g_all = -jnp.exp(arow_ref[...]) * _softplus(ab + drow_ref[...])
    gt_all = -jnp.exp(acol_ref[...]) * _softplus(abt + dcol_ref[...])
    beta_all = jax.nn.sigmoid(ab)
    pos_s = lax.broadcasted_iota(jnp.int32, (Tg, LANES), 0) % C
    pos_l = lax.broadcasted_iota(jnp.int32, (2 * H * 2, Tg), 1) % C
    gc_all, gct_all = g_all, gt_all
    s = 1
    while s < C:
        gc_all = gc_all + jnp.where(pos_s >= s, pltpu.roll(gc_all, s, 0), 0.0)
        gct_all = gct_all + jnp.where(pos_l >= s, pltpu.roll(gct_all, s, 1), 0.0)
        s *= 2

    ri = lax.broadcasted_iota(jnp.int32, (C, H * C), 0)
    cj = lax.broadcasted_iota(jnp.int32, (C, H * C), 1) % C
    eye_side = (ri == cj).astype(F32)

    chains = []
    for ci in range(G):
        rows = slice(ci * C, (ci + 1) * C)
        q = q_ref[0, rows, :]
        k = k_ref[0, rows, :]
        v = v_ref[0, rows, :]
        kbd = _block_diag(k.astype(BF16), H)
        g_c, gc_f, beta_c = g_all[rows], gc_all[rows], beta_all[rows]
        gt_c, gct_f = gt_all[:, rows], gct_all[:, rows]
        gtot = gc_f[C - 1:C, :]
        gtot_t = gct_f[:, C - 1:C]
        for d in range(2):
            if d == 0:
                gc, gct = gc_f, gct_f
                incl, strict = ri >= cj, ri > cj
            else:
                gc, gct = gtot - gc_f + g_c, gtot_t - gct_f + gt_c
                incl, strict = ri <= cj, ri < cj
            lanes = [d * H + h for h in range(H)]
            gcol = [gc[:, l:l + 1] for l in lanes]
            diff = _heads_to_lanes(gcol, C) - jnp.concatenate([gct[l:l + 1, :] for l in lanes], axis=1)
            decay = jnp.where(incl, jnp.exp(jnp.where(incl, diff, 0.0)), 0.0)
            beta_b = _heads_to_lanes([beta_c[:, 2 * H + l:2 * H + l + 1] for l in lanes], DN_DK)
            egc_b = _heads_to_lanes([jnp.exp(c) for c in gcol], DN_DK)
            ekd_b = _heads_to_lanes([jnp.exp(gtot[:, l:l + 1] - gc[:, l:l + 1]) for l in lanes], DN_DK)
            kb = k * beta_b
            a_low = jnp.where(strict, _dot_nt(kb.astype(BF16), kbd) * decay, 0.0)
            qk_ref[d, 0, rows, :] = jnp.where(incl, _dot_nt(q.astype(BF16), kbd) * decay, 0.0).astype(BF16)
            uw_ref[d, 0, rows, 2 * DN_W:3 * DN_W] = (q * egc_b).astype(BF16)
            uw_ref[d, 0, rows, 3 * DN_W:4 * DN_W] = (k * ekd_b).astype(BF16)
            gl_ref[d, 0, 0, ci:ci + 1, :] = jnp.exp(gtot)
            rhs = jnp.concatenate([v * beta_b, kb * egc_b], axis=1).astype(BF16)
            chains.append(dict(d=d, rows=rows, pw=-a_low, inv=eye_side - a_low, rhs=rhs))

    n_lvl = int(math.log2(C)) - 1
    for ch in chains:
        pwb = ch["pw"].astype(BF16)
        ch["pw"] = _dot(pwb, _block_diag(pwb, H))
    for lvl in range(1, n_lvl + 1):
        for ch in chains:
            pwb = ch["pw"].astype(BF16)
            pbd = _block_diag(pwb, H)
            if lvl < n_lvl:
                st = _dot(jnp.concatenate([ch["inv"].astype(BF16), pwb], axis=0), pbd)
                ch["inv"] = ch["inv"] + st[:C]
                ch["pw"] = st[C:]
            else:
                ch["inv"] = ch["inv"] + _dot(ch["inv"].astype(BF16), pbd)
    rb = lax.broadcasted_iota(jnp.int32, (H * C, 2 * DN_W), 0) // C
    cb = (lax.broadcasted_iota(jnp.int32, (H * C, 2 * DN_W), 1) // DN_DK) % H
    for ch in chains:
        rhs_bd = jnp.where(rb == cb, jnp.concatenate([ch["rhs"]] * H, axis=0), jnp.zeros((), BF16))
        sol = _dot(ch["inv"].astype(BF16), rhs_bd)
        uw_ref[ch["d"], 0, ch["rows"], 0:2 * DN_W] = sol.astype(BF16)


def _dn_chunk(qkv, ab, abt, a_log, dt_bias, G):
    B, T, _ = qkv.shape
    Tg = G * DN_CHUNK
    ns = T // Tg
    nl = 2 * DN_HEADS
    alog = a_log.reshape(-1)
    dtb = dt_bias.reshape(-1)
    arow = jnp.zeros((1, LANES), F32).at[0, :nl].set(alog)
    drow = jnp.zeros((1, LANES), F32).at[0, :nl].set(dtb)
    acol = jnp.zeros((2 * nl, 1), F32).at[:nl, 0].set(alog)
    dcol = jnp.zeros((2 * nl, 1), F32).at[:nl, 0].set(dtb)
    full = lambda *shape: pl.BlockSpec(shape, lambda b, i: (0,) * len(shape))
    return pl.pallas_call(
        functools.partial(_dn_chunk_body, G=G),
        grid=(B, ns),
        in_specs=[pl.BlockSpec((1, Tg, DN_W), lambda b, i: (b, i, 0)),
                  pl.BlockSpec((1, Tg, DN_W), lambda b, i: (b, i, 1)),
                  pl.BlockSpec((1, Tg, DN_W), lambda b, i: (b, i, 2)),
                  pl.BlockSpec((1, Tg, LANES), lambda b, i: (b, i, 0)),
                  pl.BlockSpec((1, 2 * nl, Tg), lambda b, i: (b, 0, i)),
                  full(1, LANES), full(1, LANES), full(2 * nl, 1), full(2 * nl, 1)],
        out_specs=[pl.BlockSpec((2, 1, Tg, 4 * DN_W), lambda b, i: (0, b, i, 0)),
                   pl.BlockSpec((2, 1, Tg, DN_HEADS * DN_CHUNK), lambda b, i: (0, b, i, 0)),
                   pl.BlockSpec((2, 1, 1, 8, LANES), lambda b, i: (0, b, i, 0, 0))],
        out_shape=[jax.ShapeDtypeStruct((2, B, T, 4 * DN_W), BF16),
                   jax.ShapeDtypeStruct((2, B, T, DN_HEADS * DN_CHUNK), BF16),
                   jax.ShapeDtypeStruct((2, B, ns, 8, LANES), F32)],
        compiler_params=_cparams("parallel", "parallel"),
        name="dn_chunk",
    )(qkv, qkv, qkv, ab, abt, arow, drow, acol, dcol)


def _dn_rec_body(uw_ref, qk_ref, gl_ref, s0_ref, o_ref, sout_ref, S_ref, *, bb, nch, G):
    C, H = DN_CHUNK, DN_HEADS
    d = pl.program_id(0)
    c = pl.program_id(2)

    @pl.when(c == 0)
    def _():
        S_ref[...] = s0_ref[...]

    r = (c + d * (nch - 1 - 2 * c)) % G
    heads = [(b, h) for b in range(bb) for h in range(H)]
    ts = []
    for b, h in heads:
        wq = jnp.concatenate([uw_ref[b, :, DN_W + h * DN_DK:DN_W + (h + 1) * DN_DK],
                              uw_ref[b, :, 2 * DN_W + h * DN_DK:2 * DN_W + (h + 1) * DN_DK]], axis=0)
        ts.append(_dot(wq, S_ref[b, h].astype(BF16)))
    for (b, h), t in zip(heads, ts):
        sl = slice(h * DN_DV, (h + 1) * DN_DV)
        v_new = (uw_ref[b, :, sl].astype(F32) - t[:C]).astype(BF16)
        o_ref[b, :, sl] = t[C:] + _dot(qk_ref[b, :, h * C:(h + 1) * C], v_new)
        gl_row = gl_ref[b, 0, pl.ds(r, 1), :]
        gl = jnp.where(d == 0, gl_row[:, h:h + 1], gl_row[:, H + h:H + h + 1])
        kd = uw_ref[b, :, 3 * DN_W + h * DN_DK:3 * DN_W + (h + 1) * DN_DK]
        S_ref[b, h] = S_ref[b, h] * gl + _dot_tn(kd, v_new)

    @pl.when(c == nch - 1)
    def _():
        sout_ref[...] = S_ref[...]


def _dn_rec(uw, qk, gl, s0, bb, G):
    _, B, T, _ = uw.shape
    C = DN_CHUNK
    nch = T // C

    def cidx(d, c):
        return c + d * (nch - 1 - 2 * c)

    s_spec = pl.BlockSpec((None, bb, DN_HEADS, DN_DK, DN_DV), lambda d, b, c: (d, b, 0, 0, 0))
    return pl.pallas_call(
        functools.partial(_dn_rec_body, bb=bb, nch=nch, G=G),
        grid=(2, B // bb, nch),
        in_specs=[pl.BlockSpec((None, bb, C, 4 * DN_W), lambda d, b, c: (d, b, cidx(d, c), 0)),
                  pl.BlockSpec((None, bb, C, DN_HEADS * C), lambda d, b, c: (d, b, cidx(d, c), 0)),
                  pl.BlockSpec((None, bb, 1, 8, LANES), lambda d, b, c: (d, b, cidx(d, c) // G, 0, 0)),
                  s_spec],
        out_specs=[pl.BlockSpec((None, bb, C, DN_W), lambda d, b, c: (d, b, cidx(d, c), 0)), s_spec],
        out_shape=[jax.ShapeDtypeStruct((2, B, T, DN_W), F32),
                   jax.ShapeDtypeStruct((2, B, DN_HEADS, DN_DK, DN_DV), F32)],
        scratch_shapes=[pltpu.VMEM((bb, DN_HEADS, DN_DK, DN_DV), F32)],
        compiler_params=_cparams("parallel", "parallel", "arbitrary"),
        name="dn_rec",
    )(uw, qk, gl, s0)


def _s5_param_body(are_ref, aim_ref, ldt_ref, bre_ref, bim_ref, lam_ref, bd_ref):
    a_re = are_ref[...]
    a_im = aim_ref[...]
    dt = jnp.exp(ldt_ref[...])
    mag = jnp.exp(a_re * dt)
    lam_re = mag * jnp.cos(a_im * dt)
    lam_im = mag * jnp.sin(a_im * dt)
    den = a_re * a_re + a_im * a_im
    nr = lam_re - 1.0
    ni = lam_im
    coef_re = (nr * a_re + ni * a_im) / den
    coef_im = (ni * a_re - nr * a_im) / den
    lam_ref[0] = lam_re
    lam_ref[1] = lam_im
    b_re = bre_ref[...]
    b_im = bim_ref[...]
    for d in range(2):
        cr = coef_re[d:d + 1, :]
        ci = coef_im[d:d + 1, :]
        bd_ref[d, 0] = cr * b_re - ci * b_im
        bd_ref[d, 1] = cr * b_im + ci * b_re


def _s5_params(A_re, A_im, log_dt, B_re, B_im):
    a_re = A_re.reshape(2, S5_NSTATE)
    a_im = A_im.reshape(2, S5_NSTATE)
    ldt = jnp.repeat(log_dt, S5_STATE, axis=1)
    b_re_t = jnp.transpose(B_re, (2, 0, 1)).reshape(S5_GROUP, S5_NSTATE)
    b_im_t = jnp.transpose(B_im, (2, 0, 1)).reshape(S5_GROUP, S5_NSTATE)
    lam, bd = pl.pallas_call(
        _s5_param_body,
        out_shape=[jax.ShapeDtypeStruct((2, 2, S5_NSTATE), F32),
                   jax.ShapeDtypeStruct((2, 2, S5_GROUP, S5_NSTATE), F32)],
        name="s5_params",
    )(a_re, a_im, ldt, b_re_t, b_im_t)
    return lam, bd


def _s5_block_weights(bd, C_re, C_im):
    eye = jnp.eye(S5_GBLK, dtype=F32)
    bd6 = bd.reshape(2, 2, S5_GROUP, S5_NBLK, S5_GBLK, S5_STATE)
    w = jnp.einsum('dchjmp,lm->djlhcmp', bd6, eye)
    w_drive = w.reshape(2, S5_NBLK, S5_GBLK * S5_GROUP, 2 * S5_SBLK).astype(BF16)
    cc = jnp.stack([C_re, -C_im], axis=0).reshape(2, S5_NBLK, S5_GBLK, S5_GROUP, S5_STATE)
    cm = jnp.einsum('cjmhp,lm->jcmplh', cc, eye)
    w_read = cm.reshape(S5_NBLK, 2, S5_SBLK, S5_GBLK * S5_GROUP).astype(BF16)
    return w_drive, w_read


def _s5_scan_body(u_ref, wd_ref, wr_ref, lam_ref, h0_ref, y_ref, hout_ref, xre, xim, hst,
                  *, B, Tc, nch, lb):
    d = pl.program_id(0)
    c = pl.program_id(1)

    @pl.when(c == 0)
    def _():
        hst[...] = h0_ref[0]

    ub = u_ref[...].astype(BF16)
    for j in range(S5_NBLK):
        drv = _dot(ub[:, j * LANES:(j + 1) * LANES], wd_ref[0, j])
        xre[:, j * S5_SBLK:(j + 1) * S5_SBLK] = drv[:, :S5_SBLK]
        xim[:, j * S5_SBLK:(j + 1) * S5_SBLK] = drv[:, S5_SBLK:]

    for lbi in range(S5_NSTATE // lb):
        ls = slice(lbi * lb, (lbi + 1) * lb)
        lr = jnp.broadcast_to(lam_ref[0, 0, :, ls], (B, lb))
        li = jnp.broadcast_to(lam_ref[1, 0, :, ls], (B, lb))

        def step(s, carry):
            hr, hi = carry
            t = s + d * (Tc - 1 - 2 * s)
            r0 = pl.multiple_of(t * B, B)
            nr = lr * hr - li * hi + xre[pl.ds(r0, B), ls]
            ni = lr * hi + li * hr + xim[pl.ds(r0, B), ls]
            xre[pl.ds(r0, B), ls] = nr
            xim[pl.ds(r0, B), ls] = ni
            return nr, ni

        hr, hi = lax.fori_loop(0, Tc, step, (hst[0, :, ls], hst[1, :, ls]), unroll=4)
        hst[0, :, ls] = hr
        hst[1, :, ls] = hi

    for j in range(S5_NBLK):
        ss = slice(j * S5_SBLK, (j + 1) * S5_SBLK)
        y = _dot(xre[:, ss].astype(BF16), wr_ref[j, 0]) + _dot(xim[:, ss].astype(BF16), wr_ref[j, 1])
        y_ref[0, :, j * LANES:(j + 1) * LANES] = y

    @pl.when(c == nch - 1)
    def _():
        hout_ref[0] = hst[...]


def _s5_scan(u_tb, w_drive, w_read, lam, h0, B, Tc):
    TB = u_tb.shape[0]
    T = TB // B
    Tc = _tile(T, Tc)
    nch = T // Tc
    lam4 = lam.reshape(2, 2, 1, S5_NSTATE)

    def cidx(d, c):
        return c + d * (nch - 1 - 2 * c)

    return pl.pallas_call(
        functools.partial(_s5_scan_body, B=B, Tc=Tc, nch=nch, lb=256),
        grid=(2, nch),
        in_specs=[pl.BlockSpec((Tc * B, S5_WIDTH), lambda d, c: (cidx(d, c), 0)),
                  pl.BlockSpec((1, S5_NBLK, LANES, 2 * S5_SBLK), lambda d, c: (d, 0, 0, 0)),
                  pl.BlockSpec((S5_NBLK, 2, S5_SBLK, LANES), lambda d, c: (0, 0, 0, 0)),
                  pl.BlockSpec((2, 1, 1, S5_NSTATE), lambda d, c: (0, d, 0, 0)),
                  pl.BlockSpec((1, 2, B, S5_NSTATE), lambda d, c: (d, 0, 0, 0))],
        out_specs=[pl.BlockSpec((1, Tc * B, S5_WIDTH), lambda d, c: (d, cidx(d, c), 0)),
                   pl.BlockSpec((1, 2, B, S5_NSTATE), lambda d, c: (d, 0, 0, 0))],
        out_shape=[jax.ShapeDtypeStruct((2, TB, S5_WIDTH), F32),
                   jax.ShapeDtypeStruct((2, 2, B, S5_NSTATE), F32)],
        scratch_shapes=[pltpu.VMEM((Tc * B, S5_NSTATE), F32),
                        pltpu.VMEM((Tc * B, S5_NSTATE), F32),
                        pltpu.VMEM((2, B, S5_NSTATE), F32)],
        compiler_params=_cparams("parallel", "arbitrary"),
        name="s5_scan",
    )(u_tb, w_drive, w_read, lam4, h0)


def _ab_out_body(x_ref, mod_ref, o_ref, z_ref, u_ref, y_ref, ng_ref, dsk_ref, gw_ref, gb_ref,
                 woa_ref, wob_ref, out_ref):
    o = o_ref[0, 0] + o_ref[1, 0]
    z = z_ref[0]
    parts = []
    for h in range(DN_HEADS):
        sl = slice(h * DN_DV, (h + 1) * DN_DV)
        parts.append(_rms(o[:, sl]) * ng_ref[...] * _silu(z[:, sl]))
    a_out = jnp.concatenate(parts, axis=1)
    y = y_ref[0] + y_ref[1] + dsk_ref[...] * u_ref[...]
    y = _gelu_tanh(y)
    b_out = y * jax.nn.sigmoid(_dot(y.astype(BF16), gw_ref[...]) + gb_ref[...])
    mix = _dot(a_out.astype(BF16), woa_ref[...]) + _dot(b_out.astype(BF16), wob_ref[...])
    out_ref[0] = x_ref[0] + mod_ref[0, 2:3, :] * mix


def _ab_out(x, mod, o_dn, z, u_tb, y_tb, dn_norm_g, d_skip, glu_w, glu_b, w_out, tm):
    B, T, D = x.shape
    tm = _tile(T, tm)
    per_batch_mod = mod.shape[0] != 1
    y3 = y_tb.reshape(2, T, B * S5_WIDTH)
    full = lambda *shape: pl.BlockSpec(shape, lambda b, i: (0,) * len(shape))
    return pl.pallas_call(
        _ab_out_body,
        grid=(B, T // tm),
        in_specs=[pl.BlockSpec((1, tm, D), lambda b, i: (b, i, 0)),
                  pl.BlockSpec((1, 6, D), (lambda b, i: (b, 0, 0)) if per_batch_mod else (lambda b, i: (0, 0, 0))),
                  pl.BlockSpec((2, 1, tm, DN_W), lambda b, i: (0, b, i, 0)),
                  pl.BlockSpec((1, tm, DN_W), lambda b, i: (b, i, 0)),
                  pl.BlockSpec((tm, S5_WIDTH), lambda b, i: (i, b)),
                  pl.BlockSpec((2, tm, S5_WIDTH), lambda b, i: (0, i, b)),
                  full(1, DN_DV), full(1, S5_WIDTH), full(S5_WIDTH, S5_WIDTH), full(1, S5_WIDTH),
                  full(DN_W, D), full(S5_WIDTH, D)],
        out_specs=pl.BlockSpec((1, tm, D), lambda b, i: (b, i, 0)),
        out_shape=jax.ShapeDtypeStruct((B, T, D), F32),
        compiler_params=_cparams("parallel", "parallel"),
        name="ab_out",
    )(x, mod, o_dn, z, u_tb, y3, dn_norm_g.reshape(1, DN_DV), d_skip.reshape(1, S5_WIDTH),
      glu_w.astype(BF16), glu_b.reshape(1, S5_WIDTH), w_out[:DN_W].astype(BF16), w_out[DN_W:].astype(BF16))


def _route(h, rw_ref, rb_ref):
    logits = _dot_nt(rw_ref[...], h, precision=HIGHEST)
    scores = jax.nn.sigmoid(logits)
    choice = scores + rb_ref[...]
    rows = [choice[e:e + 1, :] for e in range(N_EXPERTS)]
    neg_inf = jnp.float32(-jnp.inf)
    gs = []
    for g in range(N_EXPERT_GROUPS):
        r = rows[g * EXPERTS_PER_GROUP:(g + 1) * EXPERTS_PER_GROUP]
        best = None
        for a in range(EXPERTS_PER_GROUP):
            for b in range(a + 1, EXPERTS_PER_GROUP):
                s = r[a] + r[b]
                best = s if best is None else jnp.maximum(best, s)
        gs.append(best)
    best_val = gs[0]
    best_g = jnp.zeros_like(best_val, dtype=jnp.int32)
    for g in range(1, N_EXPERT_GROUPS):
        better = gs[g] > best_val
        best_val = jnp.where(better, gs[g], best_val)
        best_g = jnp.where(better, g, best_g)
    masked = [jnp.where(best_g == e // EXPERTS_PER_GROUP, rows[e], neg_inf) for e in range(N_EXPERTS)]
    m1 = masked[0]
    for e in range(1, N_EXPERTS):
        m1 = jnp.maximum(m1, masked[e])
    i1 = jnp.full_like(best_g, N_EXPERTS)
    for e in reversed(range(N_EXPERTS)):
        i1 = jnp.where(masked[e] == m1, e, i1)
    rest = [jnp.where(i1 == e, neg_inf, masked[e]) for e in range(N_EXPERTS)]
    m2 = rest[0]
    for e in range(1, N_EXPERTS):
        m2 = jnp.maximum(m2, rest[e])
    i2 = jnp.full_like(best_g, N_EXPERTS)
    for e in reversed(range(N_EXPERTS)):
        i2 = jnp.where(rest[e] == m2, e, i2)
    eidx = lax.broadcasted_iota(jnp.int32, scores.shape, 0)
    sel1 = eidx == i1
    sel2 = eidx == i2
    w1 = jnp.sum(jnp.where(sel1, scores, 0.0), axis=0, keepdims=True)
    w2 = jnp.sum(jnp.where(sel2, scores, 0.0), axis=0, keepdims=True)
    inv = 1.0 / (w1 + w2)
    return jnp.where(sel1, w1 * inv, 0.0) + jnp.where(sel2, w2 * inv, 0.0), best_g


def _moe_route_body(x_ref, mod_ref, g_ref, rw_ref, rb_ref, tab_ref, grp_ref):
    h = _rms(x_ref[0]) * g_ref[...]
    h = h * (1.0 + mod_ref[0, 4:5, :]) + mod_ref[0, 3:4, :]
    for c in range(N_PLANES):
        tab_ref[c] = h[:, c * LANES:(c + 1) * LANES]
    gates_t, best_g = _route(h, rw_ref, rb_ref)
    rows = []
    for k in range(EXPERTS_PER_GROUP):
        gk = jnp.zeros_like(gates_t[0:1, :])
        for g in range(N_EXPERT_GROUPS):
            e = g * EXPERTS_PER_GROUP + k
            gk = jnp.where(best_g == g, gates_t[e:e + 1, :], gk)
        rows.append(gk)
    rows.append(jnp.zeros((LANES - EXPERTS_PER_GROUP, gates_t.shape[1]), F32))
    tab_ref[N_PLANES] = jnp.transpose(jnp.concatenate(rows, axis=0))
    grp_ref[...] = jnp.broadcast_to(best_g, grp_ref.shape)


def _moe_route(x, mod, norm_g, router_w, router_bias, tm):
    Bx, T, D = x.shape
    tm = _tile(T, tm)
    nt = T // tm
    N = Bx * T
    per_batch_mod = mod.shape[0] != 1
    full = lambda *shape: pl.BlockSpec(shape, lambda b, i: (0,) * len(shape))
    return pl.pallas_call(
        _moe_route_body,
        grid=(Bx, nt),
        in_specs=[pl.BlockSpec((1, tm, D), lambda b, i: (b, i, 0)),
                  pl.BlockSpec((1, 6, D), (lambda b, i: (b, 0, 0)) if per_batch_mod else (lambda b, i: (0, 0, 0))),
                  full(1, D), full(N_EXPERTS, D), full(N_EXPERTS, 1)],
        out_specs=[pl.BlockSpec((N_PLANES + 1, tm, LANES), lambda b, i: (0, b * nt + i, 0)),
                   pl.BlockSpec((8, tm), lambda b, i: (0, b * nt + i))],
        out_shape=[jax.ShapeDtypeStruct((N_PLANES + 1, N, LANES), F32),
                   jax.ShapeDtypeStruct((8, N), jnp.int32)],
        compiler_params=_cparams("parallel", "parallel"),
        name="moe_route",
    )(x, mod, norm_g.reshape(1, D), router_w.T, router_bias.reshape(N_EXPERTS, 1))


def _sc_move_rows(table, idx, n_out, scatter):
    n_planes = table.shape[0]
    n = idx.shape[0]
    ch = SC_CHUNK_ROWS if n % (2 * SC_WORKERS * SC_CHUNK_ROWS) == 0 else SC_CHUNK_ROWS // 2
    n_chunks = n // (SC_WORKERS * ch)
    assert n_chunks * SC_WORKERS * ch == n and n_chunks % 2 == 0
    assert n_out == n if not scatter else table.shape[1] == n
    items = [(dj, c) for dj in range(2) for c in range(n_planes)]
    mesh = plsc.VectorSubcoreMesh(core_axis_name="c", subcore_axis_name="s")

    @functools.partial(
        pl.kernel, mesh=mesh,
        out_type=jax.ShapeDtypeStruct((n_planes, n_out, LANES), table.dtype),
        scratch_types=[pltpu.VMEM((n_chunks, ch), jnp.int32),
                       pltpu.VMEM((2, ch, LANES), table.dtype),
                       pltpu.SemaphoreType.DMA((2,)), pltpu.SemaphoreType.DMA((2,))])
    def move_kernel(table_hbm, idx_hbm, out_hbm, idx_v, rows_v, isem, osem):
        wid = lax.axis_index("s") * SC_CORES + lax.axis_index("c")
        wbase = wid * (n_chunks * ch)
        pltpu.sync_copy(idx_hbm.at[pl.ds(wid * n_chunks, n_chunks)], idx_v)

        def load(j, c, slot):
            rows = idx_v.at[j] if not scatter else pl.ds(wbase + j * ch, ch)
            return pltpu.make_async_copy(table_hbm.at[c].at[rows], rows_v.at[slot], isem.at[slot])

        def store(j, c, slot):
            rows = idx_v.at[j] if scatter else pl.ds(wbase + j * ch, ch)
            return pltpu.make_async_copy(rows_v.at[slot], out_hbm.at[c].at[rows], osem.at[slot])

        load(0, 0, 0).start()

        @pl.loop(0, n_chunks, step=2)
        def _(j):
            for it, (dj, c) in enumerate(items):
                s = it % 2
                load(j + dj, c, s).wait()

                def refill(it=it, s=s):
                    if it == 0:
                        @pl.when(j > 0)
                        def _():
                            store(j - 1, n_planes - 1, 1 - s).wait()
                    else:
                        store(j + items[it - 1][0], items[it - 1][1], 1 - s).wait()
                    if it + 1 < len(items):
                        load(j + items[it + 1][0], items[it + 1][1], 1 - s).start()
                    else:
                        load(j + 2, 0, 1 - s).start()

                if it + 1 < len(items):
                    refill()
                else:
                    pl.when(j + 2 < n_chunks)(refill)
                store(j + dj, c, s).start()

        store(n_chunks - 2 + items[-2][0], items[-2][1], 0).wait()
        store(n_chunks - 1, n_planes - 1, 1).wait()

    return move_kernel(table, idx.reshape(SC_WORKERS * n_chunks, ch))


def _moe_ffn_body(te_ref, tn_ref, xs_ref, wg_ref, wu_ref, wd_ref, ys_ref, xb, gcol, acc):
    t = pl.program_id(0)
    k = pl.program_id(1)
    n_rows = tn_ref[t]
    live = n_rows > 0

    @pl.when(jnp.logical_and(live, k == 0))
    def _():
        keep = lax.broadcasted_iota(jnp.int32, (xs_ref.shape[1], LANES), 0) < n_rows
        for c in range(N_PLANES):
            xb[:, c * LANES:(c + 1) * LANES] = jnp.where(keep, xs_ref[c], 0.0).astype(BF16)
        gcol[...] = jnp.where(keep, xs_ref[N_PLANES], 0.0)
        acc[...] = jnp.zeros_like(acc)

    @pl.when(live)
    def _():
        hb = xb[...]
        act = _silu(_dot(hb, wg_ref[0])) * _dot(hb, wu_ref[0])
        lane = lax.broadcasted_iota(jnp.int32, gcol.shape, 1)
        ge = jnp.sum(jnp.where(lane == k, gcol[...], 0.0), axis=1, keepdims=True)
        acc[...] += _dot((act * ge).astype(BF16), wd_ref[0])

    @pl.when(k == EXPERTS_PER_GROUP - 1)
    def _():
        for c in range(N_PLANES):
            ys_ref[c] = jnp.where(live, acc[:, c * LANES:(c + 1) * LANES], 0.0)


def _moe_ffn(tile_expert0, tile_rows, xs, wg, wu, wd, tm):
    P = xs.shape[1]
    D = N_PLANES * LANES
    nt = P // tm

    def wspec(shape):
        return pl.BlockSpec(shape, lambda t, k, te, tn: (te[t] + k, 0, 0))

    return pl.pallas_call(
        _moe_ffn_body,
        grid_spec=pltpu.PrefetchScalarGridSpec(
            num_scalar_prefetch=2,
            grid=(nt, EXPERTS_PER_GROUP),
            in_specs=[pl.BlockSpec((N_PLANES + 1, tm, LANES), lambda t, k, te, tn: (0, t, 0)),
                      wspec((1, D, D_EXPERT)), wspec((1, D, D_EXPERT)), wspec((1, D_EXPERT, D))],
            out_specs=pl.BlockSpec((N_PLANES, tm, LANES), lambda t, k, te, tn: (0, t, 0)),
            scratch_shapes=[pltpu.VMEM((tm, D), BF16), pltpu.VMEM((tm, LANES), F32), pltpu.VMEM((tm, D), F32)]),
        out_shape=jax.ShapeDtypeStruct((N_PLANES, P, LANES), F32),
        compiler_params=_cparams("parallel", "arbitrary"),
        name="moe_ffn",
    )(tile_expert0, tile_rows, xs, wg, wu, wd)


def _moe_combine_body(x_ref, mod_ref, y_ref, fg_ref, out_ref, *, final_norm):
    y = jnp.concatenate([y_ref[c] for c in range(N_PLANES)], axis=1)
    r = x_ref[0] + mod_ref[0, 5:6, :] * y
    if final_norm:
        r = _rms(r) * fg_ref[...]
    out_ref[0] = r


def _moe_combine(x, mod, y_tok, final_g, tm):
    Bx, T, D = x.shape
    tm = _tile(T, tm)
    nt = T // tm
    per_batch_mod = mod.shape[0] != 1
    final_norm = final_g is not None
    fg = (final_g if final_norm else jnp.ones((D,), F32)).reshape(1, D)
    return pl.pallas_call(
        functools.partial(_moe_combine_body, final_norm=final_norm),
        grid=(Bx, nt),
        in_specs=[pl.BlockSpec((1, tm, D), lambda b, i: (b, i, 0)),
                  pl.BlockSpec((1, 6, D), (lambda b, i: (b, 0, 0)) if per_batch_mod else (lambda b, i: (0, 0, 0))),
                  pl.BlockSpec((N_PLANES, tm, LANES), lambda b, i: (0, b * nt + i, 0)),
                  pl.BlockSpec((1, D), lambda b, i: (0, 0))],
        out_specs=pl.BlockSpec((1, tm, D), lambda b, i: (b, i, 0)),
        out_shape=jax.ShapeDtypeStruct((Bx, T, D), F32),
        compiler_params=_cparams("parallel", "parallel"),
        name="moe_combine",
    )(x, mod, y_tok, fg)


def _moe(x, mod, norm_g, router_w, router_bias, wg, wu, wd, expert_base, final_g):
    Bx, T, D = x.shape
    N = Bx * T
    tm = MOE_TILE
    G = N_EXPERT_GROUPS
    table, grp = _moe_route(x, mod, norm_g, router_w, router_bias, 512)
    g = grp[0]
    onehot = (g[:, None] == jnp.arange(G, dtype=jnp.int32)[None, :]).astype(jnp.int32)
    csum = jnp.cumsum(onehot, axis=0)
    counts = csum[-1]
    padded = (counts + tm - 1) // tm * tm
    ends = jnp.cumsum(padded)
    starts = ends - padded
    rank = jnp.sum(csum * onehot, axis=1) - 1
    pos = (jnp.sum(starts[None, :] * onehot, axis=1) + rank).astype(jnp.int32)
    P = N + G * tm
    tile_start = jnp.arange(P // tm, dtype=jnp.int32) * tm
    tile_group = jnp.minimum(jnp.sum(tile_start[:, None] >= ends[None, :], axis=1), G - 1)
    tile_expert0 = (expert_base + tile_group * EXPERTS_PER_GROUP).astype(jnp.int32)
    filled = (starts + counts)[tile_group]
    tile_rows = jnp.clip(filled - tile_start, 0, tm).astype(jnp.int32)
    xs = _sc_move_rows(table, pos, P, scatter=True)
    ys = _moe_ffn(tile_expert0, tile_rows, xs, wg, wu, wd, tm)
    y_tok = _sc_move_rows(ys, pos, N, scatter=False)
    return _moe_combine(x, mod, y_tok, final_g, 512)


def _mla_q_body(cq_ref, g_ref, wn_ref, wa_ref, wb_ref, cos_ref, sin_ref, qn_ref, qr_ref):
    hb = (_rms(cq_ref[0]) * g_ref[...]).astype(BF16)
    qn_ref[0] = (_dot(hb, wn_ref[...]) * MLA_SCALE).astype(BF16)
    ra = _dot(hb, wa_ref[...])
    rb = _dot(hb, wb_ref[...])
    cos = cos_ref[...]
    sin = sin_ref[...]
    for p in range(MLA_HEADS // 2):
        sl = slice(p * LANES, (p + 1) * LANES)
        qr_ref[0, :, sl] = ((ra[:, sl] * cos + rb[:, sl] * sin) * MLA_SCALE).astype(BF16)


def _mla_q(cq, g, wn, wa, wb, cos_t, sin_t, tm):
    B, T, R = cq.shape
    tm = _tile(T, tm)
    full = lambda *shape: pl.BlockSpec(shape, lambda b, i: (0,) * len(shape))
    return pl.pallas_call(
        _mla_q_body,
        grid=(B, T // tm),
        in_specs=[pl.BlockSpec((1, tm, R), lambda b, i: (b, i, 0)), full(1, R),
                  full(R, MLA_HEADS * MLA_NOPE), full(R, MLA_HEADS * MLA_ROPE), full(R, MLA_HEADS * MLA_ROPE),
                  pl.BlockSpec((tm, LANES), lambda b, i: (i, 0)),
                  pl.BlockSpec((tm, LANES), lambda b, i: (i, 0))],
        out_specs=[pl.BlockSpec((1, tm, MLA_HEADS * MLA_NOPE), lambda b, i: (b, i, 0)),
                   pl.BlockSpec((1, tm, MLA_HEADS * MLA_ROPE), lambda b, i: (b, i, 0))],
        out_shape=[jax.ShapeDtypeStruct((B, T, MLA_HEADS * MLA_NOPE), BF16),
                   jax.ShapeDtypeStruct((B, T, MLA_HEADS * MLA_ROPE), BF16)],
        compiler_params=_cparams("parallel", "parallel"),
        name="mla_q",
    )(cq, g.reshape(1, R), wn, wa, wb, cos_t, sin_t)


def _mla_kv_body(ckv_ref, ka_ref, kb_ref, g_ref, wk_ref, wv_ref, cos_ref, sin_ref, kn_ref, v_ref, kr_ref):
    hb = (_rms(ckv_ref[0]) * g_ref[...]).astype(BF16)
    kn_ref[0] = _dot(hb, wk_ref[...]).astype(BF16)
    v_ref[0] = _dot(hb, wv_ref[...]).astype(BF16)
    kr_ref[0] = (ka_ref[0] * cos_ref[...] + kb_ref[0] * sin_ref[...]).astype(BF16)


def _mla_kv(ckv, kra, krb, g, wk, wv, cos_t, sin_t, tm):
    B, T, R = ckv.shape
    tm = _tile(T, tm)
    W = MLA_HEADS * MLA_NOPE
    full = lambda *shape: pl.BlockSpec(shape, lambda b, i: (0,) * len(shape))
    tok = lambda n: pl.BlockSpec((1, tm, n), lambda b, i: (b, i, 0))
    return pl.pallas_call(
        _mla_kv_body,
        grid=(B, T // tm),
        in_specs=[tok(R), tok(LANES), tok(LANES), full(1, R), full(R, W), full(R, W),
                  pl.BlockSpec((tm, LANES), lambda b, i: (i, 0)),
                  pl.BlockSpec((tm, LANES), lambda b, i: (i, 0))],
        out_specs=[tok(W), tok(W), tok(LANES)],
        out_shape=[jax.ShapeDtypeStruct((B, T, W), BF16), jax.ShapeDtypeStruct((B, T, W), BF16),
                   jax.ShapeDtypeStruct((B, T, LANES), BF16)],
        compiler_params=_cparams("parallel", "parallel"),
        name="mla_kv",
    )(ckv, kra, krb, g.reshape(1, R), wk, wv, cos_t, sin_t)


def _mla_attn_body(qn_ref, qr_ref, kn_ref, kr_ref, v_ref, o_ref):
    tq = qn_ref.shape[1]
    lane = lax.broadcasted_iota(jnp.int32, (tq, LANES), 1)
    kr = kr_ref[0]
    for h in range(MLA_HEADS):
        sl = slice(h * MLA_NOPE, (h + 1) * MLA_NOPE)
        p = h // 2
        qr = qr_ref[0, :, p * LANES:(p + 1) * LANES]
        mine = (lane < MLA_ROPE) if h % 2 == 0 else (lane >= MLA_ROPE)
        qr = jnp.where(mine, qr, jnp.zeros_like(qr))
        s = _dot_nt(qn_ref[0, :, sl], kn_ref[0, :, sl]) + _dot_nt(qr, kr)
        m = jnp.max(s, axis=-1, keepdims=True)
        e = jnp.exp(s - m)
        l = jnp.sum(e, axis=-1, keepdims=True)
        o = _dot(e.astype(BF16), v_ref[0, :, sl]) / l
        o_ref[0, :, sl] = o.astype(o_ref.dtype)


def _mla_attn(qn, qr, kn, kr, v, tq):
    B, T, W = qn.shape
    Tk = kn.shape[1]
    tq = _tile(T, tq)
    return pl.pallas_call(
        _mla_attn_body,
        grid=(B, T // tq),
        in_specs=[pl.BlockSpec((1, tq, W), lambda b, i: (b, i, 0)),
                  pl.BlockSpec((1, tq, MLA_HEADS * MLA_ROPE), lambda b, i: (b, i, 0)),
                  pl.BlockSpec((1, Tk, W), lambda b, i: (b, 0, 0)),
                  pl.BlockSpec((1, Tk, LANES), lambda b, i: (b, 0, 0)),
                  pl.BlockSpec((1, Tk, W), lambda b, i: (b, 0, 0))],
        out_specs=pl.BlockSpec((1, tq, W), lambda b, i: (b, i, 0)),
        out_shape=jax.ShapeDtypeStruct((B, T, W), BF16),
        compiler_params=_cparams("parallel", "arbitrary"),
        name="mla_attn",
    )(qn, qr, kn, kr, v)


def _proj_res_body(x_ref, mod_ref, a_ref, w_ref, out_ref):
    out_ref[0] = x_ref[0] + mod_ref[0, 2:3, :] * _dot(a_ref[0], w_ref[...])


def _proj_res(x, mod, a, w, tm):
    B, T, D = x.shape
    K = a.shape[2]
    tm = _tile(T, tm)
    per_batch_mod = mod.shape[0] != 1
    return pl.pallas_call(
        _proj_res_body,
        grid=(B, T // tm),
        in_specs=[pl.BlockSpec((1, tm, D), lambda b, i: (b, i, 0)),
                  pl.BlockSpec((1, 6, D), (lambda b, i: (b, 0, 0)) if per_batch_mod else (lambda b, i: (0, 0, 0))),
                  pl.BlockSpec((1, tm, K), lambda b, i: (b, i, 0)),
                  pl.BlockSpec((K, D), lambda b, i: (0, 0))],
        out_specs=pl.BlockSpec((1, tm, D), lambda b, i: (b, i, 0)),
        out_shape=jax.ShapeDtypeStruct((B, T, D), F32),
        compiler_params=_cparams("parallel", "parallel"),
        name="proj_res",
    )(x, mod, a, w)


def _rope_tables(T, n_ctx):
    rows = T // GRID_W
    row = jnp.repeat(jnp.arange(rows, dtype=F32), GRID_W)
    col = jnp.tile(jnp.arange(GRID_W, dtype=F32), rows)
    n_freq = MLA_ROPE // 4
    inv = ROPE_THETA ** (-jnp.arange(n_freq, dtype=F32) / n_freq)
    ang = jnp.concatenate([row[:, None] * inv, col[:, None] * inv], axis=-1)
    cos, sin = jnp.cos(ang), jnp.sin(ang)
    cos_t = jnp.concatenate([cos, cos, cos, cos], axis=-1)
    sin_t = jnp.concatenate([-sin, sin, -sin, sin], axis=-1)
    cos_k = jnp.concatenate([jnp.ones((n_ctx, LANES), F32), cos_t], axis=0)
    sin_k = jnp.concatenate([jnp.zeros((n_ctx, LANES), F32), sin_t], axis=0)
    return cos_t, sin_t, cos_k, sin_k


def _layer_ab(x, ctx, mod_l, mod_c, norm1_g, w_in, conv_w, a_log, dt_bias, dn_norm_g, A_re, A_im, log_dt,
              B_re, B_im, C_re, C_im, D_skip, glu_w, glu_b, w_out):
    B, T, D = x.shape
    Tc = ctx.shape[1]
    q0, k0, v0, z0, a0, b0, u0 = 0, 512, 1024, 1536, 2048, 2056, 2064
    w_qkv = w_in[:, q0:z0].astype(BF16)
    w_z = w_in[:, z0:a0].astype(BF16)
    w_ab = jnp.zeros((D, LANES), F32).at[:, :16].set(w_in[:, a0:u0]).astype(BF16)
    w_u = w_in[:, u0:].astype(BF16)
    w_abt = w_in[:, a0:u0].T.astype(BF16)
    ws = [w_qkv, w_z, w_ab, w_abt, w_u]
    dts = [F32] * 5
    kinds = ["n", "n", "n", "t", "tb"]
    dn_group = 4

    lam, bd = _s5_params(A_re, A_im, log_dt, B_re, B_im)
    w_drive, w_read = _s5_block_weights(bd, C_re, C_im)

    streams = []
    dn_state = jnp.zeros((2, B, DN_HEADS, DN_DK, DN_DV), F32)
    s5_state = jnp.zeros((2, 2, B, S5_NSTATE), F32)
    for xs, mod in ((ctx, mod_c), (x, mod_l)):
        qkv, z, ab, abt, u_tb = _modmm(xs, mod, norm1_g, ws, dts, kinds, 0, 512)
        qkv = _dn_prep(qkv, conv_w, 512)
        uw, qk, gl = _dn_chunk(qkv, ab, abt, a_log, dt_bias, dn_group)
        o_dn, dn_state = _dn_rec(uw, qk, gl, dn_state, 4, dn_group)
        u2 = u_tb.reshape(xs.shape[1] * B, S5_WIDTH)
        y_tb, s5_state = _s5_scan(u2, w_drive, w_read, lam, s5_state, B, 32)
        streams.append(_ab_out(xs, mod, o_dn, z, u_tb, y_tb, dn_norm_g, D_skip.reshape(-1), glu_w, glu_b,
                               w_out, 512))
    return streams[1], streams[0]


def _layer_mla(x, ctx, mod_l, mod_c, norm1_g, w_in, q_norm_g, w_q_up, kv_norm_g, w_kv_up, w_out, need_ctx):
    assert not need_ctx, "context attention output is only needed when a later layer follows"
    B, T, D = x.shape
    n_ctx = ctx.shape[1]
    qr, kvr = MLA_Q_RANK, MLA_KV_RANK
    half = MLA_ROPE // 2
    w_cq = w_in[:, :qr].astype(BF16)
    w_ckv = w_in[:, qr:qr + kvr].astype(BF16)
    wk1 = w_in[:, qr + kvr:qr + kvr + half]
    wk2 = w_in[:, qr + kvr + half:]
    w_ka = jnp.concatenate([wk1, wk2, wk1, wk2], axis=1).astype(BF16)
    w_kb = jnp.concatenate([wk2, wk1, wk2, wk1], axis=1).astype(BF16)
    ws = [w_cq, w_ckv, w_ka, w_kb]
    cq_l, ckv_l, ka_l, kb_l = _modmm(x, mod_l, norm1_g, ws, [F32] * 4, ["n"] * 4, 0, 512)
    _, ckv_c, ka_c, kb_c = _modmm(ctx, mod_c, norm1_g, ws, [F32] * 4, ["n"] * 4, 0, 512)

    wq = w_q_up.reshape(qr, MLA_HEADS, MLA_NOPE + MLA_ROPE)
    wq_n = wq[:, :, :MLA_NOPE].reshape(qr, MLA_HEADS * MLA_NOPE).astype(BF16)
    x1 = wq[:, :, MLA_NOPE:MLA_NOPE + half]
    x2 = wq[:, :, MLA_NOPE + half:]
    wq_a = jnp.concatenate([x1, x2], axis=2).reshape(qr, MLA_HEADS * MLA_ROPE).astype(BF16)
    wq_b = jnp.concatenate([x2, x1], axis=2).reshape(qr, MLA_HEADS * MLA_ROPE).astype(BF16)
    wkv = w_kv_up.reshape(kvr, MLA_HEADS, MLA_NOPE + MLA_V)
    wk_n = wkv[:, :, :MLA_NOPE].reshape(kvr, MLA_HEADS * MLA_NOPE).astype(BF16)
    wv = wkv[:, :, MLA_NOPE:].reshape(kvr, MLA_HEADS * MLA_V).astype(BF16)

    cos_t, sin_t, cos_k, sin_k = _rope_tables(T, n_ctx)
    qn, qrope = _mla_q(cq_l, q_norm_g, wq_n, wq_a, wq_b, cos_t, sin_t, 512)
    ckv = jnp.concatenate([ckv_c, ckv_l], axis=1)
    ka = jnp.concatenate([ka_c, ka_l], axis=1)
    kb = jnp.concatenate([kb_c, kb_l], axis=1)
    kn, v, kr = _mla_kv(ckv, ka, kb, kv_norm_g, wk_n, wv, cos_k, sin_k, 256)
    o = _mla_attn(qn, qrope, kn, kr, v, 256)
    return _proj_res(x, mod_l, o, w_out.astype(BF16), 512)


def kernel(x, c, ctx, c_ctx, ada_w, ada_b, norm1_g, norm2_g, ab_w_in, dn_conv_w, dn_A_log, dn_dt_bias, dn_norm_g, s5_A_re, s5_A_im, s5_log_dt, s5_B_re, s5_B_im, s5_C_re, s5_C_im, s5_D, s5_glu_w, s5_glu_b, ab_w_out, mla_w_in, mla_q_norm_g, mla_w_q_up, mla_kv_norm_g, mla_w_kv_up, mla_w_out, router_w, router_bias, moe_w_gate, moe_w_up, moe_w_down, final_norm_g):
    B, T, D = x.shape
    n_ctx = ctx.shape[1]
    depth = ada_w.shape[0]
    n_cond = -(-(B + 1) // 8) * 8
    cond = jnp.zeros((n_cond, D), F32).at[:B].set(c).at[B].set(c_ctx)
    mods = _adaln_all(cond, ada_w, ada_b).reshape(depth, n_cond, 6, D)
    n_exp = moe_w_gate.shape[1]
    wg = moe_w_gate.astype(BF16).reshape((depth * n_exp,) + moe_w_gate.shape[2:])
    wu = moe_w_up.astype(BF16).reshape((depth * n_exp,) + moe_w_up.shape[2:])
    wd = moe_w_down.astype(BF16).reshape((depth * n_exp,) + moe_w_down.shape[2:])
    for i in range(depth):
        last = i == depth - 1
        j = i // 2
        mod_l = mods[i, :B]
        mod_c = mods[i, B:B + 1]
        if i % 2 == 0:
            x, ctx_new = _layer_ab(x, ctx, mod_l, mod_c, norm1_g[i], ab_w_in[j], dn_conv_w[j], dn_A_log[j],
                                   dn_dt_bias[j], dn_norm_g[j], s5_A_re[j], s5_A_im[j], s5_log_dt[j],
                                   s5_B_re[j], s5_B_im[j], s5_C_re[j], s5_C_im[j], s5_D[j], s5_glu_w[j],
                                   s5_glu_b[j], ab_w_out[j])
        else:
            x = _layer_mla(x, ctx, mod_l, mod_c, norm1_g[i], mla_w_in[j], mla_q_norm_g[j], mla_w_q_up[j],
                           mla_kv_norm_g[j], mla_w_kv_up[j], mla_w_out[j], not last)
            ctx_new = None
        x = _moe(x, mod_l, norm2_g[i], router_w, router_bias, wg, wu, wd, i * n_exp,
                 final_norm_g if last else None)
        if not last:
            ctx_flat = _moe(ctx_new.reshape(1, B * n_ctx, D), mod_c, norm2_g[i], router_w, router_bias,
                            wg, wu, wd, i * n_exp, None)
            ctx = ctx_flat.reshape(B, n_ctx, D)
    return x
```

```python
import functools
import math

import jax
import jax.numpy as jnp
from jax import lax
from jax.experimental import pallas as pl
from jax.experimental.pallas import tpu as pltpu
from jax.experimental.pallas import tpu_sc as plsc

F32 = jnp.float32
BF16 = jnp.bfloat16
HIGHEST = lax.Precision.HIGHEST

NORM_EPS = 1e-6
GRID_W = 64
ROPE_THETA = 10000.0

DN_HEADS = 4
DN_DK = 128
DN_DV = 128
DN_CONV = 5
DN_CHUNK = 64
DN_W = DN_HEADS * DN_DK

S5_WIDTH = 512
S5_GROUP = 16
S5_GROUPS = 32
S5_STATE = 64
S5_NSTATE = S5_GROUPS * S5_STATE
S5_GBLK = 8
S5_NBLK = S5_GROUPS // S5_GBLK
S5_SBLK = S5_GBLK * S5_STATE

MLA_HEADS = 8
MLA_Q_RANK = 384
MLA_KV_RANK = 256
MLA_NOPE = 128
MLA_ROPE = 64
MLA_V = 128
MLA_Q_SCALE = (MLA_NOPE + MLA_ROPE) ** -0.5 * math.log2(math.e)

N_EXPERTS = 16
N_EXPERT_GROUPS = 4
EXPERTS_PER_GROUP = 4
D_EXPERT = 512
MOE_TILE = 1024

N_PLANES = 8
SC_CORES = 2
SC_WORKERS = 32
SC_CHUNK_ROWS = 128

LANES = 128
MLA_QK = MLA_NOPE + LANES
VMEM_LIMIT_BYTES = 56 * 1024 * 1024


def _tile(n, pref):
    t = min(pref, n)
    while n % t or t % 8:
        t -= 1
    return t


def _cparams(*sem):
    return pltpu.CompilerParams(dimension_semantics=sem, vmem_limit_bytes=VMEM_LIMIT_BYTES)


def _silu(x):
    return x * jax.nn.sigmoid(x)


def _softplus(x):
    return jnp.maximum(x, 0.0) + jnp.log(1.0 + jnp.exp(-jnp.abs(x)))


def _gelu_tanh(x):
    return 0.5 * x * (1.0 + jnp.tanh(math.sqrt(2.0 / math.pi) * (x + 0.044715 * (x * x * x))))


def _rms(x):
    return x * lax.rsqrt(jnp.mean(x * x, axis=-1, keepdims=True) + NORM_EPS)


def _dot(a, b):
    return jnp.dot(a, b, preferred_element_type=F32)


def _dot_nt(a, b, precision=None):
    return lax.dot_general(a, b, (((1,), (1,)), ((), ())), preferred_element_type=F32,
                           precision=precision)


def _dot_tn(a, b):
    return lax.dot_general(a, b, (((0,), (0,)), ((), ())), preferred_element_type=F32)


def _ada_body(c_ref, w_ref, b_ref, o_ref):
    c = c_ref[...]
    o_ref[0] = _dot(_silu(c).astype(BF16), w_ref[0].astype(BF16)) + b_ref[0]


def _adaln_all(cond, ada_w, ada_b):
    L, D, D6 = ada_w.shape
    R = cond.shape[0]
    tn = 1536
    return pl.pallas_call(
        _ada_body,
        grid=(L, D6 // tn),
        in_specs=[pl.BlockSpec((R, D), lambda l, j: (0, 0)),
                  pl.BlockSpec((1, D, tn), lambda l, j: (l, 0, j)),
                  pl.BlockSpec((1, 1, tn), lambda l, j: (l, 0, j))],
        out_specs=pl.BlockSpec((1, R, tn), lambda l, j: (l, 0, j)),
        out_shape=jax.ShapeDtypeStruct((L, R, D6), F32),
        compiler_params=_cparams("parallel", "parallel"),
        name="adaln",
    )(cond, ada_w, ada_b.reshape(L, 1, D6))


def _modmm_body(x_ref, mod_ref, g_ref, *refs, kinds, shift_row):
    n_out = len(kinds)
    w_refs, o_refs = refs[:n_out], refs[n_out:]
    h = _rms(x_ref[0]) * g_ref[...]
    h = h * (1.0 + mod_ref[0, shift_row + 1:shift_row + 2, :]) + mod_ref[0, shift_row:shift_row + 1, :]
    hb = h.astype(BF16)
    for w_ref, o_ref, kind in zip(w_refs, o_refs, kinds):
        if kind == "t":
            o_ref[0] = _dot_nt(w_ref[...], hb).astype(o_ref.dtype)
        else:
            o_ref[0] = _dot(hb, w_ref[...]).astype(o_ref.dtype)


def _modmm(x, mod, g, ws, out_dtypes, kinds, shift_row, tm):
    Bx, T, D = x.shape
    tm = _tile(T, tm)
    per_batch_mod = mod.shape[0] != 1
    in_specs = [pl.BlockSpec((1, tm, D), lambda b, i: (b, i, 0)),
                pl.BlockSpec((1, 6, D), (lambda b, i: (b, 0, 0)) if per_batch_mod else (lambda b, i: (0, 0, 0))),
                pl.BlockSpec((1, D), lambda b, i: (0, 0))]
    out_specs, out_shape = [], []
    for w, dt, kind in zip(ws, out_dtypes, kinds):
        in_specs.append(pl.BlockSpec(w.shape, lambda b, i: (0, 0)))
        if kind == "t":
            n = w.shape[0]
            out_specs.append(pl.BlockSpec((1, n, tm), lambda b, i: (b, 0, i)))
            out_shape.append(jax.ShapeDtypeStruct((Bx, n, T), dt))
        else:
            n = w.shape[1]
            out_specs.append(pl.BlockSpec((1, tm, n), lambda b, i: (b, i, 0)))
            out_shape.append(jax.ShapeDtypeStruct((Bx, T, n), dt))
    return pl.pallas_call(
        functools.partial(_modmm_body, kinds=tuple(kinds), shift_row=shift_row),
        grid=(Bx, T // tm),
        in_specs=in_specs, out_specs=out_specs, out_shape=out_shape,
        compiler_params=_cparams("parallel", "parallel"),
        name="modmm",
    )(x, mod, g.reshape(1, D), *ws)


def _dn_prep_body(x_ref, xp_ref, xn_ref, w_ref, o_ref, buf, *, tm, nt):
    i = pl.program_id(1)
    j = pl.program_id(2)
    buf[0:8, :] = jnp.where(i == 0, 0.0, xp_ref[0])
    buf[8:8 + tm, :] = x_ref[0]
    buf[8 + tm:16 + tm, :] = jnp.where(i == nt - 1, 0.0, xn_ref[0])
    acc = buf[pl.ds(8 - DN_CONV // 2, tm), :] * w_ref[0:1, :]
    for kk in range(1, DN_CONV):
        acc = acc + buf[pl.ds(8 - DN_CONV // 2 + kk, tm), :] * w_ref[kk:kk + 1, :]
    y = _silu(acc)
    q_scale = jnp.where(j == 0, DN_DK ** -0.5, 1.0)
    for h in range(DN_HEADS):
        yh = y[:, h * DN_DK:(h + 1) * DN_DK]
        r = lax.rsqrt(jnp.sum(yh * yh, axis=-1, keepdims=True) + NORM_EPS) * q_scale
        o_ref[0, :, h * DN_DK:(h + 1) * DN_DK] = yh * jnp.where(j == 2, 1.0, r)


def _dn_prep(qkv, conv_w, tm):
    B, T, W3 = qkv.shape
    tm = _tile(T, tm)
    nt = T // tm
    r8 = tm // 8
    return pl.pallas_call(
        functools.partial(_dn_prep_body, tm=tm, nt=nt),
        grid=(B, nt, 3),
        in_specs=[pl.BlockSpec((1, tm, DN_W), lambda b, i, j: (b, i, j)),
                  pl.BlockSpec((1, 8, DN_W), lambda b, i, j: (b, jnp.maximum(i * r8 - 1, 0), j)),
                  pl.BlockSpec((1, 8, DN_W), lambda b, i, j: (b, jnp.minimum((i + 1) * r8, T // 8 - 1), j)),
                  pl.BlockSpec((DN_CONV, DN_W), lambda b, i, j: (0, j))],
        out_specs=pl.BlockSpec((1, tm, DN_W), lambda b, i, j: (b, i, j)),
        out_shape=jax.ShapeDtypeStruct((B, T, W3), F32),
        scratch_shapes=[pltpu.VMEM((tm + 16, DN_W), F32)],
        compiler_params=_cparams("parallel", "parallel", "parallel"),
        name="dn_prep",
    )(qkv, qkv, qkv, conv_w)


def _heads_to_lanes(cols, width):
    return jnp.concatenate([jnp.broadcast_to(c, (c.shape[0], width)) for c in cols], axis=1)


def _block_diag(x, nblk):
    C, W = x.shape
    w = W // nblk
    t = jnp.concatenate([x] * nblk, axis=0)
    rb = lax.broadcasted_iota(jnp.int32, t.shape, 0) // C
    cb = lax.broadcasted_iota(jnp.int32, t.shape, 1) // w
    return jnp.where(rb == cb, t, jnp.zeros_like(t))


def _dn_chunk_body(q_ref, k_ref, v_ref, ab_ref, abt_ref, arow_ref, drow_ref, acol_ref, dcol_ref,
                   uw_ref, qk_ref, gl_ref, *, G):
    C, H = DN_CHUNK, DN_HEADS
    Tg = G * C
    gl_ref[...] = jnp.zeros_like(gl_ref)
    ab = ab_ref[0]
    abt = abt_ref[0]
    g_all = -jnp.exp(arow_ref[...]) * _softplus(ab + drow_ref[...])
    gt_all = -jnp.exp(acol_ref[...]) * _softplus(abt + dcol_ref[...])
    beta_all = jax.nn.sigmoid(ab)
    pos_s = lax.broadcasted_iota(jnp.int32, (Tg, LANES), 0) % C
    pos_l = lax.broadcasted_iota(jnp.int32, (2 * H * 2, Tg), 1) % C
    gc_all, gct_all = g_all, gt_all
    s = 1
    while s < C:
        gc_all = gc_all + jnp.where(pos_s >= s, pltpu.roll(gc_all, s, 0), 0.0)
        gct_all = gct_all + jnp.where(pos_l >= s, pltpu.roll(gct_all, s, 1), 0.0)
        s *= 2

    ri = lax.broadcasted_iota(jnp.int32, (C, H * C), 0)
    cj = lax.broadcasted_iota(jnp.int32, (C, H * C), 1) % C
    eye_side = (ri == cj).astype(F32)

    chains = []
    for ci in range(G):
        rows = slice(ci * C, (ci + 1) * C)
        q = q_ref[0, rows, :]
        k = k_ref[0, rows, :]
        v = v_ref[0, rows, :]
        kbd = _block_diag(k.astype(BF16), H)
        g_c, gc_f, beta_c = g_all[rows], gc_all[rows], beta_all[rows]
        gt_c, gct_f = gt_all[:, rows], gct_all[:, rows]
        gtot = gc_f[C - 1:C, :]
        gtot_t = gct_f[:, C - 1:C]
        for d in range(2):
            if d == 0:
                gc, gct = gc_f, gct_f
                incl, strict = ri >= cj, ri > cj
            else:
                gc, gct = gtot - gc_f + g_c, gtot_t - gct_f + gt_c
                incl, strict = ri <= cj, ri < cj
            lanes = [d * H + h for h in range(H)]
            gcol = [gc[:, l:l + 1] for l in lanes]
            diff = _heads_to_lanes(gcol, C) - jnp.concatenate([gct[l:l + 1, :] for l in lanes], axis=1)
            decay = jnp.where(incl, jnp.exp(jnp.where(incl, diff, 0.0)), 0.0)
            beta_b = _heads_to_lanes([beta_c[:, 2 * H + l:2 * H + l + 1] for l in lanes], DN_DK)
            egc_b = _heads_to_lanes([jnp.exp(c) for c in gcol], DN_DK)
            ekd_b = _heads_to_lanes([jnp.exp(gtot[:, l:l + 1] - gc[:, l:l + 1]) for l in lanes], DN_DK)
            kb = k * beta_b
            a_low = jnp.where(strict, _dot_nt(kb.astype(BF16), kbd) * decay, 0.0)
            qk_ref[d, 0, rows, :] = jnp.where(incl, _dot_nt(q.astype(BF16), kbd) * decay, 0.0).astype(BF16)
            uw_ref[d, 0, rows, 2 * DN_W:3 * DN_W] = (q * egc_b).astype(BF16)
            uw_ref[d, 0, rows, 3 * DN_W:4 * DN_W] = (k * ekd_b).astype(BF16)
            gl_ref[d, 0, 0, ci:ci + 1, :] = jnp.exp(gtot)
            rhs = jnp.concatenate([v * beta_b, kb * egc_b], axis=1).astype(BF16)
            chains.append(dict(d=d, rows=rows, pw=-a_low, inv=eye_side - a_low, rhs=rhs))

    n_lvl = int(math.log2(C)) - 1
    for ch in chains:
        pwb = ch["pw"].astype(BF16)
        ch["pw"] = _dot(pwb, _block_diag(pwb, H))
    for lvl in range(1, n_lvl + 1):
        for ch in chains:
            pwb = ch["pw"].astype(BF16)
            pbd = _block_diag(pwb, H)
            if lvl < n_lvl:
                st = _dot(jnp.concatenate([ch["inv"].astype(BF16), pwb], axis=0), pbd)
                ch["inv"] = ch["inv"] + st[:C]
                ch["pw"] = st[C:]
            else:
                ch["inv"] = ch["inv"] + _dot(ch["inv"].astype(BF16), pbd)
    rb = lax.broadcasted_iota(jnp.int32, (H * C, 2 * DN_W), 0) // C
    cb = (lax.broadcasted_iota(jnp.int32, (H * C, 2 * DN_W), 1) // DN_DK) % H
    for ch in chains:
        rhs_bd = jnp.where(rb == cb, jnp.concatenate([ch["rhs"]] * H, axis=0), jnp.zeros((), BF16))
        sol = _dot(ch["inv"].astype(BF16), rhs_bd)
        uw_ref[ch["d"], 0, ch["rows"], 0:2 * DN_W] = sol.astype(BF16)


def _dn_chunk(qkv, ab, abt, a_log, dt_bias, G):
    B, T, _ = qkv.shape
    Tg = G * DN_CHUNK
    ns = T // Tg
    nl = 2 * DN_HEADS
    alog = a_log.reshape(-1)
    dtb = dt_bias.reshape(-1)
    arow = jnp.zeros((1, LANES), F32).at[0, :nl].set(alog)
    drow = jnp.zeros((1, LANES), F32).at[0, :nl].set(dtb)
    acol = jnp.zeros((2 * nl, 1), F32).at[:nl, 0].set(alog)
    dcol = jnp.zeros((2 * nl, 1), F32).at[:nl, 0].set(dtb)
    full = lambda *shape: pl.BlockSpec(shape, lambda b, i: (0,) * len(shape))
    return pl.pallas_call(
        functools.partial(_dn_chunk_body, G=G),
        grid=(B, ns),
        in_specs=[pl.BlockSpec((1, Tg, DN_W), lambda b, i: (b, i, 0)),
                  pl.BlockSpec((1, Tg, DN_W), lambda b, i: (b, i, 1)),
                  pl.BlockSpec((1, Tg, DN_W), lambda b, i: (b, i, 2)),
                  pl.BlockSpec((1, Tg, LANES), lambda b, i: (b, i, 0)),
                  pl.BlockSpec((1, 2 * nl, Tg), lambda b, i: (b, 0, i)),
                  full(1, LANES), full(1, LANES), full(2 * nl, 1), full(2 * nl, 1)],
        out_specs=[pl.BlockSpec((2, 1, Tg, 4 * DN_W), lambda b, i: (0, b, i, 0)),
                   pl.BlockSpec((2, 1, Tg, DN_HEADS * DN_CHUNK), lambda b, i: (0, b, i, 0)),
                   pl.BlockSpec((2, 1, 1, 8, LANES), lambda b, i: (0, b, i, 0, 0))],
        out_shape=[jax.ShapeDtypeStruct((2, B, T, 4 * DN_W), BF16),
                   jax.ShapeDtypeStruct((2, B, T, DN_HEADS * DN_CHUNK), BF16),
                   jax.ShapeDtypeStruct((2, B, ns, 8, LANES), F32)],
        compiler_params=_cparams("parallel", "parallel"),
        name="dn_chunk",
    )(qkv, qkv, qkv, ab, abt, arow, drow, acol, dcol)


def _dn_rec_body(uw_ref, qk_ref, gl_ref, s0_ref, o_ref, sout_ref, S_ref, *, bb, nch, G):
    C, H = DN_CHUNK, DN_HEADS
    d = pl.program_id(0)
    c = pl.program_id(2)

    @pl.when(c == 0)
    def _():
        S_ref[...] = s0_ref[...]

    r = (c + d * (nch - 1 - 2 * c)) % G
    heads = [(b, h) for b in range(bb) for h in range(H)]
    ts = []
    for b, h in heads:
        wq = jnp.concatenate([uw_ref[b, :, DN_W + h * DN_DK:DN_W + (h + 1) * DN_DK],
                              uw_ref[b, :, 2 * DN_W + h * DN_DK:2 * DN_W + (h + 1) * DN_DK]], axis=0)
        ts.append(_dot(wq, S_ref[b, h].astype(BF16)))
    for (b, h), t in zip(heads, ts):
        sl = slice(h * DN_DV, (h + 1) * DN_DV)
        v_new = (uw_ref[b, :, sl].astype(F32) - t[:C]).astype(BF16)
        o_ref[b, :, sl] = t[C:] + _dot(qk_ref[b, :, h * C:(h + 1) * C], v_new)
        gl_row = gl_ref[b, 0, pl.ds(r, 1), :]
        gl = jnp.where(d == 0, gl_row[:, h:h + 1], gl_row[:, H + h:H + h + 1])
        kd = uw_ref[b, :, 3 * DN_W + h * DN_DK:3 * DN_W + (h + 1) * DN_DK]
        S_ref[b, h] = S_ref[b, h] * gl + _dot_tn(kd, v_new)

    @pl.when(c == nch - 1)
    def _():
        sout_ref[...] = S_ref[...]


def _dn_rec(uw, qk, gl, s0, bb, G):
    _, B, T, _ = uw.shape
    C = DN_CHUNK
    nch = T // C

    def cidx(d, c):
        return c + d * (nch - 1 - 2 * c)

    s_spec = pl.BlockSpec((None, bb, DN_HEADS, DN_DK, DN_DV), lambda d, b, c: (d, b, 0, 0, 0))
    return pl.pallas_call(
        functools.partial(_dn_rec_body, bb=bb, nch=nch, G=G),
        grid=(2, B // bb, nch),
        in_specs=[pl.BlockSpec((None, bb, C, 4 * DN_W), lambda d, b, c: (d, b, cidx(d, c), 0)),
                  pl.BlockSpec((None, bb, C, DN_HEADS * C), lambda d, b, c: (d, b, cidx(d, c), 0)),
                  pl.BlockSpec((None, bb, 1, 8, LANES), lambda d, b, c: (d, b, cidx(d, c) // G, 0, 0)),
                  s_spec],
        out_specs=[pl.BlockSpec((None, bb, C, DN_W), lambda d, b, c: (d, b, cidx(d, c), 0)), s_spec],
        out_shape=[jax.ShapeDtypeStruct((2, B, T, DN_W), F32),
                   jax.ShapeDtypeStruct((2, B, DN_HEADS, DN_DK, DN_DV), F32)],
        scratch_shapes=[pltpu.VMEM((bb, DN_HEADS, DN_DK, DN_DV), F32)],
        compiler_params=_cparams("parallel", "parallel", "arbitrary"),
        name="dn_rec",
    )(uw, qk, gl, s0)


def _s5_param_body(are_ref, aim_ref, ldt_ref, bre_ref, bim_ref, lam_ref, bd_ref):
    a_re = are_ref[...]
    a_im = aim_ref[...]
    dt = jnp.exp(ldt_ref[...])
    mag = jnp.exp(a_re * dt)
    lam_re = mag * jnp.cos(a_im * dt)
    lam_im = mag * jnp.sin(a_im * dt)
    den = a_re * a_re + a_im * a_im
    nr = lam_re - 1.0
    ni = lam_im
    coef_re = (nr * a_re + ni * a_im) / den
    coef_im = (ni * a_re - nr * a_im) / den
    lam_ref[0] = lam_re
    lam_ref[1] = lam_im
    b_re = bre_ref[...]
    b_im = bim_ref[...]
    for d in range(2):
        cr = coef_re[d:d + 1, :]
        ci = coef_im[d:d + 1, :]
        bd_ref[d, 0] = cr * b_re - ci * b_im
        bd_ref[d, 1] = cr * b_im + ci * b_re


def _s5_params(A_re, A_im, log_dt, B_re, B_im):
    a_re = A_re.reshape(2, S5_NSTATE)
    a_im = A_im.reshape(2, S5_NSTATE)
    ldt = jnp.repeat(log_dt, S5_STATE, axis=1)
    b_re_t = jnp.transpose(B_re, (2, 0, 1)).reshape(S5_GROUP, S5_NSTATE)
    b_im_t = jnp.transpose(B_im, (2, 0, 1)).reshape(S5_GROUP, S5_NSTATE)
    lam, bd = pl.pallas_call(
        _s5_param_body,
        out_shape=[jax.ShapeDtypeStruct((2, 2, S5_NSTATE), F32),
                   jax.ShapeDtypeStruct((2, 2, S5_GROUP, S5_NSTATE), F32)],
        name="s5_params",
    )(a_re, a_im, ldt, b_re_t, b_im_t)
    return lam, bd


def _s5_block_weights(bd, C_re, C_im):
    eye = jnp.eye(S5_GBLK, dtype=F32)
    bd6 = bd.reshape(2, 2, S5_GROUP, S5_NBLK, S5_GBLK, S5_STATE)
    w = jnp.einsum('dchjmp,lm->djlhcmp', bd6, eye)
    w_drive = w.reshape(2, S5_NBLK, S5_GBLK * S5_GROUP, 2 * S5_SBLK).astype(BF16)
    cc = jnp.stack([C_re, -C_im], axis=0).reshape(2, S5_NBLK, S5_GBLK, S5_GROUP, S5_STATE)
    cm = jnp.einsum('cjmhp,lm->jcmplh', cc, eye)
    w_read = cm.reshape(S5_NBLK, 2, S5_SBLK, S5_GBLK * S5_GROUP).astype(BF16)
    return w_drive, w_read


def _s5_scan_body(u_ref, pin_ref, pout_ref, wd_ref, wr_ref, lam_ref, h0_ref, y_ref, hout_ref, xre, xim, hst, ytb,
                  *, B, Tc, nch, lb):
    d = pl.program_id(0)
    c = pl.program_id(1)

    @pl.when(c == 0)
    def _():
        hst[...] = h0_ref[0]

    n = Tc * B
    ub = _dot(pin_ref[...], u_ref[...].reshape(n, S5_WIDTH).astype(BF16)).astype(BF16)
    for j in range(S5_NBLK):
        drv = _dot(ub[:, j * LANES:(j + 1) * LANES], wd_ref[0, j])
        xre[:, j * S5_SBLK:(j + 1) * S5_SBLK] = drv[:, :S5_SBLK]
        xim[:, j * S5_SBLK:(j + 1) * S5_SBLK] = drv[:, S5_SBLK:]

    for lbi in range(S5_NSTATE // lb):
        ls = slice(lbi * lb, (lbi + 1) * lb)
        lr = jnp.broadcast_to(lam_ref[0, 0, :, ls], (B, lb))
        li = jnp.broadcast_to(lam_ref[1, 0, :, ls], (B, lb))

        def step(s, carry):
            hr, hi = carry
            t = s + d * (Tc - 1 - 2 * s)
            r0 = pl.multiple_of(t * B, B)
            nr = lr * hr - li * hi + xre[pl.ds(r0, B), ls]
            ni = lr * hi + li * hr + xim[pl.ds(r0, B), ls]
            xre[pl.ds(r0, B), ls] = nr
            xim[pl.ds(r0, B), ls] = ni
            return nr, ni

        hr, hi = lax.fori_loop(0, Tc, step, (hst[0, :, ls], hst[1, :, ls]), unroll=4)
        hst[0, :, ls] = hr
        hst[1, :, ls] = hi

    for j in range(S5_NBLK):
        ss = slice(j * S5_SBLK, (j + 1) * S5_SBLK)
        y = _dot(xre[:, ss].astype(BF16), wr_ref[j, 0]) + _dot(xim[:, ss].astype(BF16), wr_ref[j, 1])
        ytb[:, j * LANES:(j + 1) * LANES] = y.astype(BF16)
    y_ref[0] = _dot(pout_ref[...], ytb[...]).astype(BF16).reshape(B, Tc, S5_WIDTH)

    @pl.when(c == nch - 1)
    def _():
        hout_ref[0] = hst[...]


def _s5_scan(u, w_drive, w_read, lam, h0, Tc):
    B, T, _ = u.shape
    Tc = _tile(T, Tc)
    nch = T // Tc
    lam4 = lam.reshape(2, 2, 1, S5_NSTATE)
    n = Tc * B
    r = jnp.arange(n, dtype=jnp.int32)
    p_in = ((r[:, None] // B == r[None, :] % Tc) & (r[:, None] % B == r[None, :] // Tc)).astype(BF16)
    p_out = p_in.T

    def cidx(d, c):
        return c + d * (nch - 1 - 2 * c)

    return pl.pallas_call(
        functools.partial(_s5_scan_body, B=B, Tc=Tc, nch=nch, lb=256),
        grid=(2, nch),
        in_specs=[pl.BlockSpec((B, Tc, S5_WIDTH), lambda d, c: (0, cidx(d, c), 0)),
                  pl.BlockSpec((n, n), lambda d, c: (0, 0)),
                  pl.BlockSpec((n, n), lambda d, c: (0, 0)),
                  pl.BlockSpec((1, S5_NBLK, LANES, 2 * S5_SBLK), lambda d, c: (d, 0, 0, 0)),
                  pl.BlockSpec((S5_NBLK, 2, S5_SBLK, LANES), lambda d, c: (0, 0, 0, 0)),
                  pl.BlockSpec((2, 1, 1, S5_NSTATE), lambda d, c: (0, d, 0, 0)),
                  pl.BlockSpec((1, 2, B, S5_NSTATE), lambda d, c: (d, 0, 0, 0))],
        out_specs=[pl.BlockSpec((1, B, Tc, S5_WIDTH), lambda d, c: (d, 0, cidx(d, c), 0)),
                   pl.BlockSpec((1, 2, B, S5_NSTATE), lambda d, c: (d, 0, 0, 0))],
        out_shape=[jax.ShapeDtypeStruct((2, B, T, S5_WIDTH), BF16),
                   jax.ShapeDtypeStruct((2, 2, B, S5_NSTATE), F32)],
        scratch_shapes=[pltpu.VMEM((n, S5_NSTATE), F32),
                        pltpu.VMEM((n, S5_NSTATE), F32),
                        pltpu.VMEM((2, B, S5_NSTATE), F32),
                        pltpu.VMEM((n, S5_WIDTH), BF16)],
        compiler_params=_cparams("parallel", "arbitrary"),
        name="s5_scan",
    )(u, p_in, p_out, w_drive, w_read, lam4, h0)


def _ab_out_body(x_ref, mod_ref, o_ref, z_ref, u_ref, y_ref, ng_ref, dsk_ref, gw_ref, gb_ref,
                 woa_ref, wob_ref, out_ref):
    o = o_ref[0, 0] + o_ref[1, 0]
    z = z_ref[0]
    parts = []
    for h in range(DN_HEADS):
        sl = slice(h * DN_DV, (h + 1) * DN_DV)
        parts.append(_rms(o[:, sl]) * ng_ref[...] * _silu(z[:, sl]))
    a_out = jnp.concatenate(parts, axis=1)
    y = y_ref[0, 0].astype(F32) + y_ref[1, 0].astype(F32) + dsk_ref[...] * u_ref[0]
    y = _gelu_tanh(y)
    b_out = y * jax.nn.sigmoid(_dot(y.astype(BF16), gw_ref[...]) + gb_ref[...])
    mix = _dot(a_out.astype(BF16), woa_ref[...]) + _dot(b_out.astype(BF16), wob_ref[...])
    out_ref[0] = x_ref[0] + mod_ref[0, 2:3, :] * mix


def _ab_out(x, mod, o_dn, z, u, y_s5, dn_norm_g, d_skip, glu_w, glu_b, w_out, tm):
    B, T, D = x.shape
    tm = _tile(T, tm)
    per_batch_mod = mod.shape[0] != 1
    full = lambda *shape: pl.BlockSpec(shape, lambda b, i: (0,) * len(shape))
    return pl.pallas_call(
        _ab_out_body,
        grid=(B, T // tm),
        in_specs=[pl.BlockSpec((1, tm, D), lambda b, i: (b, i, 0)),
                  pl.BlockSpec((1, 6, D), (lambda b, i: (b, 0, 0)) if per_batch_mod else (lambda b, i: (0, 0, 0))),
                  pl.BlockSpec((2, 1, tm, DN_W), lambda b, i: (0, b, i, 0)),
                  pl.BlockSpec((1, tm, DN_W), lambda b, i: (b, i, 0)),
                  pl.BlockSpec((1, tm, S5_WIDTH), lambda b, i: (b, i, 0)),
                  pl.BlockSpec((2, 1, tm, S5_WIDTH), lambda b, i: (0, b, i, 0)),
                  full(1, DN_DV), full(1, S5_WIDTH), full(S5_WIDTH, S5_WIDTH), full(1, S5_WIDTH),
                  full(DN_W, D), full(S5_WIDTH, D)],
        out_specs=pl.BlockSpec((1, tm, D), lambda b, i: (b, i, 0)),
        out_shape=jax.ShapeDtypeStruct((B, T, D), F32),
        compiler_params=_cparams("parallel", "parallel"),
        name="ab_out",
    )(x, mod, o_dn, z, u, y_s5, dn_norm_g.reshape(1, DN_DV), d_skip.reshape(1, S5_WIDTH),
      glu_w.astype(BF16), glu_b.reshape(1, S5_WIDTH), w_out[:DN_W].astype(BF16), w_out[DN_W:].astype(BF16))


def _route(h, rw_ref, rb_ref):
    logits = _dot_nt(rw_ref[...], h, precision=HIGHEST)
    scores = jax.nn.sigmoid(logits)
    choice = scores + rb_ref[...]
    rows = [choice[e:e + 1, :] for e in range(N_EXPERTS)]
    neg_inf = jnp.float32(-jnp.inf)
    gs = []
    for g in range(N_EXPERT_GROUPS):
        r = rows[g * EXPERTS_PER_GROUP:(g + 1) * EXPERTS_PER_GROUP]
        best = None
        for a in range(EXPERTS_PER_GROUP):
            for b in range(a + 1, EXPERTS_PER_GROUP):
                s = r[a] + r[b]
                best = s if best is None else jnp.maximum(best, s)
        gs.append(best)
    best_val = gs[0]
    best_g = jnp.zeros_like(best_val, dtype=jnp.int32)
    for g in range(1, N_EXPERT_GROUPS):
        better = gs[g] > best_val
        best_val = jnp.where(better, gs[g], best_val)
        best_g = jnp.where(better, g, best_g)
    masked = [jnp.where(best_g == e // EXPERTS_PER_GROUP, rows[e], neg_inf) for e in range(N_EXPERTS)]
    m1 = masked[0]
    for e in range(1, N_EXPERTS):
        m1 = jnp.maximum(m1, masked[e])
    i1 = jnp.full_like(best_g, N_EXPERTS)
    for e in reversed(range(N_EXPERTS)):
        i1 = jnp.where(masked[e] == m1, e, i1)
    rest = [jnp.where(i1 == e, neg_inf, masked[e]) for e in range(N_EXPERTS)]
    m2 = rest[0]
    for e in range(1, N_EXPERTS):
        m2 = jnp.maximum(m2, rest[e])
    i2 = jnp.full_like(best_g, N_EXPERTS)
    for e in reversed(range(N_EXPERTS)):
        i2 = jnp.where(rest[e] == m2, e, i2)
    eidx = lax.broadcasted_iota(jnp.int32, scores.shape, 0)
    sel1 = eidx == i1
    sel2 = eidx == i2
    w1 = jnp.sum(jnp.where(sel1, scores, 0.0), axis=0, keepdims=True)
    w2 = jnp.sum(jnp.where(sel2, scores, 0.0), axis=0, keepdims=True)
    inv = 1.0 / (w1 + w2)
    return jnp.where(sel1, w1 * inv, 0.0) + jnp.where(sel2, w2 * inv, 0.0), best_g


def _moe_route_body(x_ref, mod_ref, g_ref, rw_ref, rb_ref, tab_ref, grp_ref):
    h = _rms(x_ref[0]) * g_ref[...]
    h = h * (1.0 + mod_ref[0, 4:5, :]) + mod_ref[0, 3:4, :]
    for c in range(N_PLANES):
        tab_ref[c] = h[:, c * LANES:(c + 1) * LANES]
    gates_t, best_g = _route(h, rw_ref, rb_ref)
    rows = []
    for k in range(EXPERTS_PER_GROUP):
        gk = jnp.zeros_like(gates_t[0:1, :])
        for g in range(N_EXPERT_GROUPS):
            e = g * EXPERTS_PER_GROUP + k
            gk = jnp.where(best_g == g, gates_t[e:e + 1, :], gk)
        rows.append(gk)
    rows.append(jnp.zeros((LANES - EXPERTS_PER_GROUP, gates_t.shape[1]), F32))
    tab_ref[N_PLANES] = jnp.transpose(jnp.concatenate(rows, axis=0))
    grp_ref[...] = jnp.broadcast_to(best_g, grp_ref.shape)


def _moe_route(x, mod, norm_g, router_w, router_bias, tm):
    Bx, T, D = x.shape
    tm = _tile(T, tm)
    nt = T // tm
    N = Bx * T
    per_batch_mod = mod.shape[0] != 1
    full = lambda *shape: pl.BlockSpec(shape, lambda b, i: (0,) * len(shape))
    return pl.pallas_call(
        _moe_route_body,
        grid=(Bx, nt),
        in_specs=[pl.BlockSpec((1, tm, D), lambda b, i: (b, i, 0)),
                  pl.BlockSpec((1, 6, D), (lambda b, i: (b, 0, 0)) if per_batch_mod else (lambda b, i: (0, 0, 0))),
                  full(1, D), full(N_EXPERTS, D), full(N_EXPERTS, 1)],
        out_specs=[pl.BlockSpec((N_PLANES + 1, tm, LANES), lambda b, i: (0, b * nt + i, 0)),
                   pl.BlockSpec((8, tm), lambda b, i: (0, b * nt + i))],
        out_shape=[jax.ShapeDtypeStruct((N_PLANES + 1, N, LANES), F32),
                   jax.ShapeDtypeStruct((8, N), jnp.int32)],
        compiler_params=_cparams("parallel", "parallel"),
        name="moe_route",
    )(x, mod, norm_g.reshape(1, D), router_w.T, router_bias.reshape(N_EXPERTS, 1))


def _sc_move_rows(table, idx, n_out, scatter):
    n_planes = table.shape[0]
    n = idx.shape[0]
    ch = SC_CHUNK_ROWS if n % (2 * SC_WORKERS * SC_CHUNK_ROWS) == 0 else SC_CHUNK_ROWS // 2
    n_chunks = n // (SC_WORKERS * ch)
    assert n_chunks * SC_WORKERS * ch == n and n_chunks % 2 == 0
    assert n_out == n if not scatter else table.shape[1] == n
    items = [(dj, c) for dj in range(2) for c in range(n_planes)]
    mesh = plsc.VectorSubcoreMesh(core_axis_name="c", subcore_axis_name="s")

    @functools.partial(
        pl.kernel, mesh=mesh,
        out_type=jax.ShapeDtypeStruct((n_planes, n_out, LANES), table.dtype),
        scratch_types=[pltpu.VMEM((n_chunks, ch), jnp.int32),
                       pltpu.VMEM((2, ch, LANES), table.dtype),
                       pltpu.SemaphoreType.DMA((2,)), pltpu.SemaphoreType.DMA((2,))])
    def move_kernel(table_hbm, idx_hbm, out_hbm, idx_v, rows_v, isem, osem):
        wid = lax.axis_index("s") * SC_CORES + lax.axis_index("c")
        wbase = wid * (n_chunks * ch)
        pltpu.sync_copy(idx_hbm.at[pl.ds(wid * n_chunks, n_chunks)], idx_v)

        def load(j, c, slot):
            rows = idx_v.at[j] if not scatter else pl.ds(wbase + j * ch, ch)
            return pltpu.make_async_copy(table_hbm.at[c].at[rows], rows_v.at[slot], isem.at[slot])

        def store(j, c, slot):
            rows = idx_v.at[j] if scatter else pl.ds(wbase + j * ch, ch)
            return pltpu.make_async_copy(rows_v.at[slot], out_hbm.at[c].at[rows], osem.at[slot])

        load(0, 0, 0).start()

        @pl.loop(0, n_chunks, step=2)
        def _(j):
            for it, (dj, c) in enumerate(items):
                s = it % 2
                load(j + dj, c, s).wait()

                def refill(it=it, s=s):
                    if it == 0:
                        @pl.when(j > 0)
                        def _():
                            store(j - 1, n_planes - 1, 1 - s).wait()
                    else:
                        store(j + items[it - 1][0], items[it - 1][1], 1 - s).wait()
                    if it + 1 < len(items):
                        load(j + items[it + 1][0], items[it + 1][1], 1 - s).start()
                    else:
                        load(j + 2, 0, 1 - s).start()

                if it + 1 < len(items):
                    refill()
                else:
                    pl.when(j + 2 < n_chunks)(refill)
                store(j + dj, c, s).start()

        store(n_chunks - 2 + items[-2][0], items[-2][1], 0).wait()
        store(n_chunks - 1, n_planes - 1, 1).wait()

    return move_kernel(table, idx.reshape(SC_WORKERS * n_chunks, ch))


def _moe_ffn_body(te_ref, tn_ref, xs_ref, wg_ref, wu_ref, wd_ref, ys_ref, xb, gcol, acc):
    t = pl.program_id(0)
    k = pl.program_id(1)
    n_rows = tn_ref[t]
    live = n_rows > 0

    @pl.when(jnp.logical_and(live, k == 0))
    def _():
        keep = lax.broadcasted_iota(jnp.int32, (xs_ref.shape[1], LANES), 0) < n_rows
        for c in range(N_PLANES):
            xb[:, c * LANES:(c + 1) * LANES] = jnp.where(keep, xs_ref[c], 0.0).astype(BF16)
        gcol[...] = jnp.where(keep, xs_ref[N_PLANES], 0.0)
        acc[...] = jnp.zeros_like(acc)

    @pl.when(live)
    def _():
        hb = xb[...]
        act = _silu(_dot(hb, wg_ref[0])) * _dot(hb, wu_ref[0])
        lane = lax.broadcasted_iota(jnp.int32, gcol.shape, 1)
        ge = jnp.sum(jnp.where(lane == k, gcol[...], 0.0), axis=1, keepdims=True)
        acc[...] += _dot((act * ge).astype(BF16), wd_ref[0])

    @pl.when(k == EXPERTS_PER_GROUP - 1)
    def _():
        for c in range(N_PLANES):
            ys_ref[c] = jnp.where(live, acc[:, c * LANES:(c + 1) * LANES], 0.0)


def _moe_ffn(tile_expert0, tile_rows, xs, wg, wu, wd, tm):
    P = xs.shape[1]
    D = N_PLANES * LANES
    nt = P // tm

    def wspec(shape):
        return pl.BlockSpec(shape, lambda t, k, te, tn: (te[t] + k, 0, 0))

    return pl.pallas_call(
        _moe_ffn_body,
        grid_spec=pltpu.PrefetchScalarGridSpec(
            num_scalar_prefetch=2,
            grid=(nt, EXPERTS_PER_GROUP),
            in_specs=[pl.BlockSpec((N_PLANES + 1, tm, LANES), lambda t, k, te, tn: (0, t, 0)),
                      wspec((1, D, D_EXPERT)), wspec((1, D, D_EXPERT)), wspec((1, D_EXPERT, D))],
            out_specs=pl.BlockSpec((N_PLANES, tm, LANES), lambda t, k, te, tn: (0, t, 0)),
            scratch_shapes=[pltpu.VMEM((tm, D), BF16), pltpu.VMEM((tm, LANES), F32), pltpu.VMEM((tm, D), F32)]),
        out_shape=jax.ShapeDtypeStruct((N_PLANES, P, LANES), F32),
        compiler_params=_cparams("parallel", "arbitrary"),
        name="moe_ffn",
    )(tile_expert0, tile_rows, xs, wg, wu, wd)


def _moe_combine_body(x_ref, mod_ref, y_ref, fg_ref, out_ref, *, final_norm):
    y = jnp.concatenate([y_ref[c] for c in range(N_PLANES)], axis=1)
    r = x_ref[0] + mod_ref[0, 5:6, :] * y
    if final_norm:
        r = _rms(r) * fg_ref[...]
    out_ref[0] = r


def _moe_combine(x, mod, y_tok, final_g, tm):
    Bx, T, D = x.shape
    tm = _tile(T, tm)
    nt = T // tm
    per_batch_mod = mod.shape[0] != 1
    final_norm = final_g is not None
    fg = (final_g if final_norm else jnp.ones((D,), F32)).reshape(1, D)
    return pl.pallas_call(
        functools.partial(_moe_combine_body, final_norm=final_norm),
        grid=(Bx, nt),
        in_specs=[pl.BlockSpec((1, tm, D), lambda b, i: (b, i, 0)),
                  pl.BlockSpec((1, 6, D), (lambda b, i: (b, 0, 0)) if per_batch_mod else (lambda b, i: (0, 0, 0))),
                  pl.BlockSpec((N_PLANES, tm, LANES), lambda b, i: (0, b * nt + i, 0)),
                  pl.BlockSpec((1, D), lambda b, i: (0, 0))],
        out_specs=pl.BlockSpec((1, tm, D), lambda b, i: (b, i, 0)),
        out_shape=jax.ShapeDtypeStruct((Bx, T, D), F32),
        compiler_params=_cparams("parallel", "parallel"),
        name="moe_combine",
    )(x, mod, y_tok, fg)


def _moe(x, mod, norm_g, router_w, router_bias, wg, wu, wd, expert_base, final_g):
    Bx, T, D = x.shape
    N = Bx * T
    tm = MOE_TILE
    G = N_EXPERT_GROUPS
    table, grp = _moe_route(x, mod, norm_g, router_w, router_bias, 512)
    g = grp[0]
    onehot = (g[:, None] == jnp.arange(G, dtype=jnp.int32)[None, :]).astype(jnp.int32)
    csum = jnp.cumsum(onehot, axis=0)
    counts = csum[-1]
    padded = (counts + tm - 1) // tm * tm
    ends = jnp.cumsum(padded)
    starts = ends - padded
    rank = jnp.sum(csum * onehot, axis=1) - 1
    pos = (jnp.sum(starts[None, :] * onehot, axis=1) + rank).astype(jnp.int32)
    P = N + G * tm
    tile_start = jnp.arange(P // tm, dtype=jnp.int32) * tm
    tile_group = jnp.minimum(jnp.sum(tile_start[:, None] >= ends[None, :], axis=1), G - 1)
    tile_expert0 = (expert_base + tile_group * EXPERTS_PER_GROUP).astype(jnp.int32)
    filled = (starts + counts)[tile_group]
    tile_rows = jnp.clip(filled - tile_start, 0, tm).astype(jnp.int32)
    xs = _sc_move_rows(table, pos, P, scatter=True)
    ys = _moe_ffn(tile_expert0, tile_rows, xs, wg, wu, wd, tm)
    y_tok = _sc_move_rows(ys, pos, N, scatter=False)
    return _moe_combine(x, mod, y_tok, final_g, 512)


def _mla_q_body(cq_ref, g_ref, wn_ref, wa_ref, wb_ref, cos_ref, sin_ref, q_ref):
    hb = (_rms(cq_ref[0]) * g_ref[...]).astype(BF16)
    qn = _dot(hb, wn_ref[...]) * MLA_Q_SCALE
    ra = _dot(hb, wa_ref[...])
    rb = _dot(hb, wb_ref[...])
    cos = cos_ref[...]
    sin = sin_ref[...]
    lane = lax.broadcasted_iota(jnp.int32, cos.shape, 1)
    for h in range(MLA_HEADS):
        p = h // 2
        rot = (ra[:, p * LANES:(p + 1) * LANES] * cos + rb[:, p * LANES:(p + 1) * LANES] * sin) * MLA_Q_SCALE
        mine = (lane < MLA_ROPE) if h % 2 == 0 else (lane >= MLA_ROPE)
        q_ref[0, :, h * MLA_QK:h * MLA_QK + MLA_NOPE] = qn[:, h * MLA_NOPE:(h + 1) * MLA_NOPE].astype(BF16)
        q_ref[0, :, h * MLA_QK + MLA_NOPE:(h + 1) * MLA_QK] = jnp.where(mine, rot, 0.0).astype(BF16)


def _mla_q(cq, g, wn, wa, wb, cos_t, sin_t, tm):
    B, T, R = cq.shape
    tm = _tile(T, tm)
    full = lambda *shape: pl.BlockSpec(shape, lambda b, i: (0,) * len(shape))
    return pl.pallas_call(
        _mla_q_body,
        grid=(B, T // tm),
        in_specs=[pl.BlockSpec((1, tm, R), lambda b, i: (b, i, 0)), full(1, R),
                  full(R, MLA_HEADS * MLA_NOPE), full(R, MLA_HEADS * MLA_ROPE), full(R, MLA_HEADS * MLA_ROPE),
                  pl.BlockSpec((tm, LANES), lambda b, i: (i, 0)),
                  pl.BlockSpec((tm, LANES), lambda b, i: (i, 0))],
        out_specs=pl.BlockSpec((1, tm, MLA_HEADS * MLA_QK), lambda b, i: (b, i, 0)),
        out_shape=jax.ShapeDtypeStruct((B, T, MLA_HEADS * MLA_QK), BF16),
        compiler_params=_cparams("parallel", "parallel"),
        name="mla_q",
    )(cq, g.reshape(1, R), wn, wa, wb, cos_t, sin_t)


def _mla_kv_body(ckv_ref, ka_ref, kb_ref, g_ref, wk_ref, wv_ref, cos_ref, sin_ref, k_ref, v_ref):
    hb = (_rms(ckv_ref[0]) * g_ref[...]).astype(BF16)
    kn = _dot(hb, wk_ref[...]).astype(BF16)
    v_ref[0] = _dot(hb, wv_ref[...]).astype(BF16)
    kr =(ka_ref[0] * cos_ref[...] + kb_ref[0] * sin_ref[...]).astype(BF16)
    for h in range(MLA_HEADS):
        k_ref[0, :, h * MLA_QK:h * MLA_QK + MLA_NOPE] = kn[:, h * MLA_NOPE:(h + 1) * MLA_NOPE]
        k_ref[0, :, h * MLA_QK + MLA_NOPE:(h + 1) * MLA_QK] = kr


def _mla_kv(ckv, kra, krb, g, wk, wv, cos_t, sin_t, tm):
    B, T, R = ckv.shape
    tm = _tile(T, tm)
    W = MLA_HEADS * MLA_NOPE
    full = lambda *shape: pl.BlockSpec(shape, lambda b, i: (0,) * len(shape))
    tok = lambda n: pl.BlockSpec((1, tm, n), lambda b, i: (b, i, 0))
    return pl.pallas_call(
        _mla_kv_body,
        grid=(B, T // tm),
        in_specs=[tok(R), tok(LANES), tok(LANES), full(1, R), full(R, W), full(R, W),
                  pl.BlockSpec((tm, LANES), lambda b, i: (i, 0)),
                  pl.BlockSpec((tm, LANES), lambda b, i: (i, 0))],
        out_specs=[tok(MLA_HEADS * MLA_QK), tok(W)],
        out_shape=[jax.ShapeDtypeStruct((B, T, MLA_HEADS * MLA_QK), BF16), jax.ShapeDtypeStruct((B, T, W), BF16)],
        compiler_params=_cparams("parallel", "parallel"),
        name="mla_kv",
    )(ckv, kra, krb, g.reshape(1, R), wk, wv, cos_t, sin_t)


def _mla_attn_body(q_ref, k_ref, v_ref, o_ref):
    for h in range(MLA_HEADS):
        qk = slice(h * MLA_QK, (h + 1) * MLA_QK)
        sl = slice(h * MLA_V, (h + 1) * MLA_V)
        s = _dot_nt(q_ref[0, :, qk], k_ref[0, :, qk])
        m = jnp.max(s, axis=-1, keepdims=True)
        e = jnp.exp2(s - m)
        l = jnp.sum(e, axis=-1, keepdims=True)
        o = _dot(e.astype(BF16), v_ref[0, :, sl]) / l
        o_ref[0, :, sl] = o.astype(o_ref.dtype)


def _mla_attn(q, k, v, tq):
    B, T, WQ = q.shape
    Tk, W = v.shape[1], v.shape[2]
    tq = _tile(T, tq)
    return pl.pallas_call(
        _mla_attn_body,
        grid=(B, T // tq),
        in_specs=[pl.BlockSpec((1, tq, WQ), lambda b, i: (b, i, 0)),
                  pl.BlockSpec((1, Tk, WQ), lambda b, i: (b, 0, 0)),
                  pl.BlockSpec((1, Tk, W), lambda b, i: (b, 0, 0))],
        out_specs=pl.BlockSpec((1, tq, W), lambda b, i: (b, i, 0)),
        out_shape=jax.ShapeDtypeStruct((B, T, W), BF16),
        compiler_params=_cparams("parallel", "arbitrary"),
        name="mla_attn",
    )(q, k, v)


def _proj_res_body(x_ref, mod_ref, a_ref, w_ref, out_ref):
    out_ref[0] = x_ref[0] + mod_ref[0, 2:3, :] * _dot(a_ref[0], w_ref[...])


def _proj_res(x, mod, a, w, tm):
    B, T, D = x.shape
    K = a.shape[2]
    tm = _tile(T, tm)
    per_batch_mod = mod.shape[0] != 1
    return pl.pallas_call(
        _proj_res_body,
        grid=(B, T // tm),
        in_specs=[pl.BlockSpec((1, tm, D), lambda b, i: (b, i, 0)),
                  pl.BlockSpec((1, 6, D), (lambda b, i: (b, 0, 0)) if per_batch_mod else (lambda b, i: (0, 0, 0))),
                  pl.BlockSpec((1, tm, K), lambda b, i: (b, i, 0)),
                  pl.BlockSpec((K, D), lambda b, i: (0, 0))],
        out_specs=pl.BlockSpec((1, tm, D), lambda b, i: (b, i, 0)),
        out_shape=jax.ShapeDtypeStruct((B, T, D), F32),
        compiler_params=_cparams("parallel", "parallel"),
        name="proj_res",
    )(x, mod, a, w)


def _rope_tables(T, n_ctx):
    rows = T // GRID_W
    row = jnp.repeat(jnp.arange(rows, dtype=F32), GRID_W)
    col = jnp.tile(jnp.arange(GRID_W, dtype=F32), rows)
    n_freq = MLA_ROPE // 4
    inv = ROPE_THETA ** (-jnp.arange(n_freq, dtype=F32) / n_freq)
    ang = jnp.concatenate([row[:, None] * inv, col[:, None] * inv], axis=-1)
    cos, sin = jnp.cos(ang), jnp.sin(ang)
    cos_t = jnp.concatenate([cos, cos, cos, cos], axis=-1)
    sin_t = jnp.concatenate([-sin, sin, -sin, sin], axis=-1)
    cos_k = jnp.concatenate([jnp.ones((n_ctx, LANES), F32), cos_t], axis=0)
    sin_k = jnp.concatenate([jnp.zeros((n_ctx, LANES), F32), sin_t], axis=0)
    return cos_t, sin_t, cos_k, sin_k


def _layer_ab(x, ctx, mod_l, mod_c, norm1_g, w_in, conv_w, a_log, dt_bias, dn_norm_g, A_re, A_im, log_dt,
              B_re, B_im, C_re, C_im, D_skip, glu_w, glu_b, w_out):
    B, T, D = x.shape
    Tc = ctx.shape[1]
    q0, k0, v0, z0, a0, b0, u0 = 0, 512, 1024, 1536, 2048, 2056, 2064
    w_qkv = w_in[:, q0:z0].astype(BF16)
    w_z = w_in[:, z0:a0].astype(BF16)
    w_ab = jnp.zeros((D, LANES), F32).at[:, :16].set(w_in[:, a0:u0]).astype(BF16)
    w_u = w_in[:, u0:].astype(BF16)
    w_abt = w_in[:, a0:u0].T.astype(BF16)
    ws = [w_qkv, w_z, w_ab, w_abt, w_u]
    dts = [F32] * 5
    kinds = ["n", "n", "n", "t", "n"]
    dn_group = 4

    lam, bd = _s5_params(A_re, A_im, log_dt, B_re, B_im)
    w_drive, w_read = _s5_block_weights(bd, C_re, C_im)

    streams = []
    dn_state = jnp.zeros((2, B, DN_HEADS, DN_DK, DN_DV), F32)
    s5_state = jnp.zeros((2, 2, B, S5_NSTATE), F32)
    for xs, mod in ((ctx, mod_c), (x, mod_l)):
        qkv, z, ab, abt, u = _modmm(xs, mod, norm1_g, ws, dts, kinds, 0, 512)
        qkv = _dn_prep(qkv, conv_w, 512)
        uw, qk, gl = _dn_chunk(qkv, ab, abt, a_log, dt_bias, dn_group)
        o_dn, dn_state = _dn_rec(uw, qk, gl, dn_state, 8, dn_group)
        y_s5, s5_state = _s5_scan(u, w_drive, w_read, lam, s5_state, 32)
        streams.append(_ab_out(xs, mod, o_dn, z, u, y_s5, dn_norm_g, D_skip.reshape(-1), glu_w, glu_b,
                               w_out, 512))
    return streams[1], streams[0]


def _layer_mla(x, ctx, mod_l, mod_c, norm1_g, w_in, q_norm_g, w_q_up, kv_norm_g, w_kv_up, w_out, need_ctx):
    assert not need_ctx, "context attention output is only needed when a later layer follows"
    B, T, D = x.shape
    n_ctx = ctx.shape[1]
    qr, kvr = MLA_Q_RANK, MLA_KV_RANK
    half = MLA_ROPE // 2
    w_cq = w_in[:, :qr].astype(BF16)
    w_ckv = w_in[:, qr:qr + kvr].astype(BF16)
    wk1 = w_in[:, qr + kvr:qr + kvr + half]
    wk2 = w_in[:, qr + kvr + half:]
    w_ka = jnp.concatenate([wk1, wk2, wk1, wk2], axis=1).astype(BF16)
    w_kb = jnp.concatenate([wk2, wk1, wk2, wk1], axis=1).astype(BF16)
    ws = [w_cq, w_ckv, w_ka, w_kb]
    cq_l, ckv_l, ka_l, kb_l = _modmm(x, mod_l, norm1_g, ws, [F32] * 4, ["n"] * 4, 0, 512)
    _, ckv_c, ka_c, kb_c = _modmm(ctx, mod_c, norm1_g, ws, [F32] * 4, ["n"] * 4, 0, 512)

    wq = w_q_up.reshape(qr, MLA_HEADS, MLA_NOPE + MLA_ROPE)
    wq_n = wq[:, :, :MLA_NOPE].reshape(qr, MLA_HEADS * MLA_NOPE).astype(BF16)
    x1 = wq[:, :, MLA_NOPE:MLA_NOPE + half]
    x2 = wq[:, :, MLA_NOPE + half:]
    wq_a = jnp.concatenate([x1, x2], axis=2).reshape(qr, MLA_HEADS * MLA_ROPE).astype(BF16)
    wq_b = jnp.concatenate([x2, x1], axis=2).reshape(qr, MLA_HEADS * MLA_ROPE).astype(BF16)
    wkv = w_kv_up.reshape(kvr, MLA_HEADS, MLA_NOPE + MLA_V)
    wk_n = wkv[:, :, :MLA_NOPE].reshape(kvr, MLA_HEADS * MLA_NOPE).astype(BF16)
    wv = wkv[:, :, MLA_NOPE:].reshape(kvr, MLA_HEADS * MLA_V).astype(BF16)

    cos_t, sin_t, cos_k, sin_k = _rope_tables(T, n_ctx)
    q = _mla_q(cq_l, q_norm_g, wq_n, wq_a, wq_b, cos_t, sin_t, 512)
    ckv = jnp.concatenate([ckv_c, ckv_l], axis=1)
    ka = jnp.concatenate([ka_c, ka_l], axis=1)
    kb = jnp.concatenate([kb_c, kb_l], axis=1)
    k, v = _mla_kv(ckv, ka, kb, kv_norm_g, wk_n, wv, cos_k, sin_k, 256)
    o = _mla_attn(q, k, v, 256)
    return _proj_res(x, mod_l, o, w_out.astype(BF16), 512)


def kernel(x, c, ctx, c_ctx, ada_w, ada_b, norm1_g, norm2_g, ab_w_in, dn_conv_w, dn_A_log, dn_dt_bias, dn_norm_g, s5_A_re, s5_A_im, s5_log_dt, s5_B_re, s5_B_im, s5_C_re, s5_C_im, s5_D, s5_glu_w, s5_glu_b, ab_w_out, mla_w_in, mla_q_norm_g, mla_w_q_up, mla_kv_norm_g, mla_w_kv_up, mla_w_out, router_w, router_bias, moe_w_gate, moe_w_up, moe_w_down, final_norm_g):
    B, T, D = x.shape
    n_ctx = ctx.shape[1]
    depth = ada_w.shape[0]
    n_cond = -(-(B + 1) // 8) * 8
    cond = jnp.zeros((n_cond, D), F32).at[:B].set(c).at[B].set(c_ctx)
    mods = _adaln_all(cond, ada_w, ada_b).reshape(depth, n_cond, 6, D)
    n_exp = moe_w_gate.shape[1]
    wg = moe_w_gate.astype(BF16).reshape((depth * n_exp,) + moe_w_gate.shape[2:])
    wu = moe_w_up.astype(BF16).reshape((depth * n_exp,) + moe_w_up.shape[2:])
    wd = moe_w_down.astype(BF16).reshape((depth * n_exp,) + moe_w_down.shape[2:])
    for i in range(depth):
        last = i == depth - 1
        j = i // 2
        mod_l = mods[i, :B]
        mod_c = mods[i, B:B + 1]
        if i % 2 == 0:
            x, ctx_new = _layer_ab(x, ctx, mod_l, mod_c, norm1_g[i], ab_w_in[j], dn_conv_w[j], dn_A_log[j],
                                   dn_dt_bias[j], dn_norm_g[j], s5_A_re[j], s5_A_im[j], s5_log_dt[j],
                                   s5_B_re[j], s5_B_im[j], s5_C_re[j], s5_C_im[j], s5_D[j], s5_glu_w[j],
                                   s5_glu_b[j], ab_w_out[j])
        else:
            x = _layer_mla(x, ctx, mod_l, mod_c, norm1_g[i], mla_w_in[j], mla_q_norm_g[j], mla_w_q_up[j],
                           mla_kv_norm_g[j], mla_w_kv_up[j], mla_w_out[j], not last)
            ctx_new = None
        x = _moe(x, mod_l, norm2_g[i], router_w, router_bias, wg, wu, wd, i * n_exp,
                 final_norm_g if last else None)
        if not last:
            ctx_flat = _moe(ctx_new.reshape(1, B * n_ctx, D), mod_c, norm2_g[i], router_w, router_bias,
                            wg, wu, wd, i * n_exp, None)
            ctx = ctx_flat.reshape(B, n_ctx, D)
    return x
```

```python
import functools
import math

import jax
import jax.numpy as jnp
from jax import lax
from jax.experimental import pallas as pl
from jax.experimental.pallas import tpu as pltpu
from jax.experimental.pallas import tpu_sc as plsc

F32 = jnp.float32
BF16 = jnp.bfloat16
HIGHEST = lax.Precision.HIGHEST

NORM_EPS = 1e-6
GRID_W = 64
ROPE_THETA = 10000.0

DN_HEADS = 4
DN_DK = 128
DN_DV = 128
DN_CONV = 5
DN_CHUNK = 64
DN_W = DN_HEADS * DN_DK

S5_WIDTH = 512
S5_GROUP = 16
S5_GROUPS = 32
S5_STATE = 64
S5_NSTATE = S5_GROUPS * S5_STATE
S5_GBLK = 8
S5_NBLK = S5_GROUPS // S5_GBLK
S5_SBLK = S5_GBLK * S5_STATE

MLA_HEADS = 8
MLA_Q_RANK = 384
MLA_KV_RANK = 256
MLA_NOPE = 128
MLA_ROPE = 64
MLA_V = 128
MLA_Q_SCALE = (MLA_NOPE + MLA_ROPE) ** -0.5 * math.log2(math.e)

N_EXPERTS = 16
N_EXPERT_GROUPS = 4
EXPERTS_PER_GROUP = 4
D_EXPERT = 512
PAIRS_PER_GROUP = EXPERTS_PER_GROUP * (EXPERTS_PER_GROUP - 1) // 2
MOE_TILE = 1024

N_PLANES = 8
SC_CORES = 2
SC_WORKERS = 32
SC_CHUNK_ROWS = 128

LANES = 128
MLA_QK = MLA_NOPE + LANES
VMEM_LIMIT_BYTES = 56 * 1024 * 1024


def _tile(n, pref):
    t = min(pref, n)
    while n % t or t % 8:
        t -= 1
    return t


def _cparams(*sem):
    return pltpu.CompilerParams(dimension_semantics=sem, vmem_limit_bytes=VMEM_LIMIT_BYTES)


def _silu(x):
    return x * jax.nn.sigmoid(x)


def _softplus(x):
    return jnp.maximum(x, 0.0) + jnp.log(1.0 + jnp.exp(-jnp.abs(x)))


def _gelu_tanh(x):
    return 0.5 * x * (1.0 + jnp.tanh(math.sqrt(2.0 / math.pi) * (x + 0.044715 * (x * x * x))))


def _rms(x):
    return x * lax.rsqrt(jnp.mean(x * x, axis=-1, keepdims=True) + NORM_EPS)


def _dot(a, b):
    return jnp.dot(a, b, preferred_element_type=F32)


def _dot_nt(a, b, precision=None):
    return lax.dot_general(a, b, (((1,), (1,)), ((), ())), preferred_element_type=F32,
                           precision=precision)


def _dot_tn(a, b):
    return lax.dot_general(a, b, (((0,), (0,)), ((), ())), preferred_element_type=F32)


def _ada_body(c_ref, w_ref, b_ref, o_ref):
    c = c_ref[...]
    o_ref[0] = _dot(_silu(c).astype(BF16), w_ref[0].astype(BF16)) + b_ref[0]


def _adaln_all(cond, ada_w, ada_b):
    L, D, D6 = ada_w.shape
    R = cond.shape[0]
    tn = 1536
    return pl.pallas_call(
        _ada_body,
        grid=(L, D6 // tn),
        in_specs=[pl.BlockSpec((R, D), lambda l, j: (0, 0)),
                  pl.BlockSpec((1, D, tn), lambda l, j: (l, 0, j)),
                  pl.BlockSpec((1, 1, tn), lambda l, j: (l, 0, j))],
        out_specs=pl.BlockSpec((1, R, tn), lambda l, j: (l, 0, j)),
        out_shape=jax.ShapeDtypeStruct((L, R, D6), F32),
        compiler_params=_cparams("parallel", "parallel"),
        name="adaln",
    )(cond, ada_w, ada_b.reshape(L, 1, D6))


def _modmm_body(x_ref, mod_ref, g_ref, *refs, kinds, shift_row):
    n_out = len(kinds)
    w_refs, o_refs = refs[:n_out], refs[n_out:]
    h = _rms(x_ref[0]) * g_ref[...]
    h = h * (1.0 + mod_ref[0, shift_row + 1:shift_row + 2, :]) + mod_ref[0, shift_row:shift_row + 1, :]
    hb = h.astype(BF16)
    for w_ref, o_ref, kind in zip(w_refs, o_refs, kinds):
        if kind == "t":
            o_ref[0] = _dot_nt(w_ref[...], hb).astype(o_ref.dtype)
        else:
            o_ref[0] = _dot(hb, w_ref[...]).astype(o_ref.dtype)


def _modmm(x, mod, g, ws, out_dtypes, kinds, shift_row, tm):
    Bx, T, D = x.shape
    tm = _tile(T, tm)
    per_batch_mod = mod.shape[0] != 1
    in_specs = [pl.BlockSpec((1, tm, D), lambda b, i: (b, i, 0)),
                pl.BlockSpec((1, 6, D), (lambda b, i: (b, 0, 0)) if per_batch_mod else (lambda b, i: (0, 0, 0))),
                pl.BlockSpec((1, D), lambda b, i: (0, 0))]
    out_specs, out_shape = [], []
    for w, dt, kind in zip(ws, out_dtypes, kinds):
        in_specs.append(pl.BlockSpec(w.shape, lambda b, i: (0, 0)))
        if kind == "t":
            n = w.shape[0]
            out_specs.append(pl.BlockSpec((1, n, tm), lambda b, i: (b, 0, i)))
            out_shape.append(jax.ShapeDtypeStruct((Bx, n, T), dt))
        else:
            n = w.shape[1]
            out_specs.append(pl.BlockSpec((1, tm, n), lambda b, i: (b, i, 0)))
            out_shape.append(jax.ShapeDtypeStruct((Bx, T, n), dt))
    return pl.pallas_call(
        functools.partial(_modmm_body, kinds=tuple(kinds), shift_row=shift_row),
        grid=(Bx, T // tm),
        in_specs=in_specs, out_specs=out_specs, out_shape=out_shape,
        compiler_params=_cparams("parallel", "parallel"),
        name="modmm",
    )(x, mod, g.reshape(1, D), *ws)


def _dn_prep_body(x_ref, xp_ref, xn_ref, w_ref, o_ref, buf, *, tm, nt):
    i = pl.program_id(1)
    j = pl.program_id(2)
    buf[0:8, :] = jnp.where(i == 0, 0.0, xp_ref[0])
    buf[8:8 + tm, :] = x_ref[0]
    buf[8 + tm:16 + tm, :] = jnp.where(i == nt - 1, 0.0, xn_ref[0])
    acc = buf[pl.ds(8 - DN_CONV // 2, tm), :] * w_ref[0:1, :]
    for kk in range(1, DN_CONV):
        acc = acc + buf[pl.ds(8 - DN_CONV // 2 + kk, tm), :] * w_ref[kk:kk + 1, :]
    y = _silu(acc)
    q_scale = jnp.where(j == 0, DN_DK ** -0.5, 1.0)
    for h in range(DN_HEADS):
        yh = y[:, h * DN_DK:(h + 1) * DN_DK]
        r = lax.rsqrt(jnp.sum(yh * yh, axis=-1, keepdims=True) + NORM_EPS) * q_scale
        o_ref[0, :, h * DN_DK:(h + 1) * DN_DK] = yh * jnp.where(j == 2, 1.0, r)


def _dn_prep(qkv, conv_w, tm):
    B, T, W3 = qkv.shape
    tm = _tile(T, tm)
    nt = T // tm
    r8 = tm // 8
    return pl.pallas_call(
        functools.partial(_dn_prep_body, tm=tm, nt=nt),
        grid=(B, nt, 3),
        in_specs=[pl.BlockSpec((1, tm, DN_W), lambda b, i, j: (b, i, j)),
                  pl.BlockSpec((1, 8, DN_W), lambda b, i, j: (b, jnp.maximum(i * r8 - 1, 0), j)),
                  pl.BlockSpec((1, 8, DN_W), lambda b, i, j: (b, jnp.minimum((i + 1) * r8, T // 8 - 1), j)),
                  pl.BlockSpec((DN_CONV, DN_W), lambda b, i, j: (0, j))],
        out_specs=pl.BlockSpec((1, tm, DN_W), lambda b, i, j: (b, i, j)),
        out_shape=jax.ShapeDtypeStruct((B, T, W3), F32),
        scratch_shapes=[pltpu.VMEM((tm + 16, DN_W), F32)],
        compiler_params=_cparams("parallel", "parallel", "parallel"),
        name="dn_prep",
    )(qkv, qkv, qkv, conv_w)


def _heads_to_lanes(cols, width):
    return jnp.concatenate([jnp.broadcast_to(c, (c.shape[0], width)) for c in cols], axis=1)


def _block_diag(x, nblk):
    C, W = x.shape
    w = W // nblk
    t = jnp.concatenate([x] * nblk, axis=0)
    rb = lax.broadcasted_iota(jnp.int32, t.shape, 0) // C
    cb = lax.broadcasted_iota(jnp.int32, t.shape, 1) // w
    return jnp.where(rb == cb, t, jnp.zeros_like(t))


def _dn_chunk_body(q_ref, k_ref, v_ref, ab_ref, abt_ref, arow_ref, drow_ref, acol_ref, dcol_ref,
                   uw_ref, qk_ref, gl_ref, *, G):
    C, H = DN_CHUNK, DN_HEADS
    Tg = G * C
    gl_ref[...] = jnp.zeros_like(gl_ref)
    ab = ab_ref[0]
    abt = abt_ref[0]
    g_all = -jnp.exp(arow_ref[...]) * _softplus(ab + drow_ref[...])
    gt_all = -jnp.exp(acol_ref[...]) * _softplus(abt + dcol_ref[...])
    beta_all = jax.nn.sigmoid(ab)
    pos_s = lax.broadcasted_iota(jnp.int32, (Tg, LANES), 0) % C
    pos_l = lax.broadcasted_iota(jnp.int32, (2 * H * 2, Tg), 1) % C
    gc_all, gct_all = g_all, gt_all
    s = 1
    while s < C:
        gc_all = gc_all + jnp.where(pos_s >= s, pltpu.roll(gc_all, s, 0), 0.0)
        gct_all = gct_all + jnp.where(pos_l >= s, pltpu.roll(gct_all, s, 1), 0.0)
        s *= 2

    ri = lax.broadcasted_iota(jnp.int32, (C, H * C), 0)
    cj = lax.broadcasted_iota(jnp.int32, (C, H * C), 1) % C
    eye_side = (ri == cj).astype(F32)

    chains = []
    for ci in range(G):
        rows = slice(ci * C, (ci + 1) * C)
        q = q_ref[0, rows, :]
        k = k_ref[0, rows, :]
        v = v_ref[0, rows, :]
        kbd = _block_diag(k.astype(BF16), H)
        g_c, gc_f, beta_c = g_all[rows], gc_all[rows], beta_all[rows]
        gt_c, gct_f = gt_all[:, rows], gct_all[:, rows]
        gtot = gc_f[C - 1:C, :]
        gtot_t = gct_f[:, C - 1:C]
        for d in range(2):
            if d == 0:
                gc, gct = gc_f, gct_f
                incl, strict = ri >= cj, ri > cj
            else:
                gc, gct = gtot - gc_f + g_c, gtot_t - gct_f + gt_c
                incl, strict = ri <= cj, ri < cj
            lanes = [d * H + h for h in range(H)]
            gcol = [gc[:, l:l + 1] for l in lanes]
            diff = _heads_to_lanes(gcol, C) - jnp.concatenate([gct[l:l + 1, :] for l in lanes], axis=1)
            decay = jnp.where(incl, jnp.exp(jnp.where(incl, diff, 0.0)), 0.0)
            beta_b = _heads_to_lanes([beta_c[:, 2 * H + l:2 * H + l + 1] for l in lanes], DN_DK)
            egc_b = _heads_to_lanes([jnp.exp(c) for c in gcol], DN_DK)
            ekd_b = _heads_to_lanes([jnp.exp(gtot[:, l:l + 1] - gc[:, l:l + 1]) for l in lanes], DN_DK)
            kb = k * beta_b
            a_low = jnp.where(strict, _dot_nt(kb.astype(BF16), kbd) * decay, 0.0)
            qk_ref[d, 0, rows, :] = jnp.where(incl, _dot_nt(q.astype(BF16), kbd) * decay, 0.0).astype(BF16)
            uw_ref[d, 0, rows, 2 * DN_W:3 * DN_W] = (q * egc_b).astype(BF16)
            uw_ref[d, 0, rows, 3 * DN_W:4 * DN_W] = (k * ekd_b).astype(BF16)
            gl_ref[d, 0, 0, ci:ci + 1, :] = jnp.exp(gtot)
            rhs = jnp.concatenate([v * beta_b, kb * egc_b], axis=1).astype(BF16)
            chains.append(dict(d=d, rows=rows, pw=-a_low, inv=eye_side - a_low, rhs=rhs))

    n_lvl = int(math.log2(C)) - 1
    for ch in chains:
        pwb = ch["pw"].astype(BF16)
        ch["pw"] = _dot(pwb, _block_diag(pwb, H))
    for lvl in range(1, n_lvl + 1):
        for ch in chains:
            pwb = ch["pw"].astype(BF16)
            pbd = _block_diag(pwb, H)
            if lvl < n_lvl:
                st = _dot(jnp.concatenate([ch["inv"].astype(BF16), pwb], axis=0), pbd)
                ch["inv"] = ch["inv"] + st[:C]
                ch["pw"] = st[C:]
            else:
                ch["inv"] = ch["inv"] + _dot(ch["inv"].astype(BF16), pbd)
    rb = lax.broadcasted_iota(jnp.int32, (H * C, 2 * DN_W), 0) // C
    cb = (lax.broadcasted_iota(jnp.int32, (H * C, 2 * DN_W), 1) // DN_DK) % H
    for ch in chains:
        rhs_bd = jnp.where(rb == cb, jnp.concatenate([ch["rhs"]] * H, axis=0), jnp.zeros((), BF16))
        sol = _dot(ch["inv"].astype(BF16), rhs_bd)
        uw_ref[ch["d"], 0, ch["rows"], 0:2 * DN_W] = sol.astype(BF16)


def _dn_chunk(qkv, ab, abt, a_log, dt_bias, G):
    B, T, _ = qkv.shape
    Tg = G * DN_CHUNK
    ns = T // Tg
    nl = 2 * DN_HEADS
    alog = a_log.reshape(-1)
    dtb = dt_bias.reshape(-1)
    arow = jnp.zeros((1, LANES), F32).at[0, :nl].set(alog)
    drow = jnp.zeros((1, LANES), F32).at[0, :nl].set(dtb)
    acol = jnp.zeros((2 * nl, 1), F32).at[:nl, 0].set(alog)
    dcol = jnp.zeros((2 * nl, 1), F32).at[:nl, 0].set(dtb)
    full = lambda *shape: pl.BlockSpec(shape, lambda b, i: (0,) * len(shape))
    return pl.pallas_call(
        functools.partial(_dn_chunk_body, G=G),
        grid=(B, ns),
        in_specs=[pl.BlockSpec((1, Tg, DN_W), lambda b, i: (b, i, 0)),
                  pl.BlockSpec((1, Tg, DN_W), lambda b, i: (b, i, 1)),
                  pl.BlockSpec((1, Tg, DN_W), lambda b, i: (b, i, 2)),
                  pl.BlockSpec((1, Tg, LANES), lambda b, i: (b, i, 0)),
                  pl.BlockSpec((1, 2 * nl, Tg), lambda b, i: (b, 0, i)),
                  full(1, LANES), full(1, LANES), full(2 * nl, 1), full(2 * nl, 1)],
        out_specs=[pl.BlockSpec((2, 1, Tg, 4 * DN_W), lambda b, i: (0, b, i, 0)),
                   pl.BlockSpec((2, 1, Tg, DN_HEADS * DN_CHUNK), lambda b, i: (0, b, i, 0)),
                   pl.BlockSpec((2, 1, 1, 8, LANES), lambda b, i: (0, b, i, 0, 0))],
        out_shape=[jax.ShapeDtypeStruct((2, B, T, 4 * DN_W), BF16),
                   jax.ShapeDtypeStruct((2, B, T, DN_HEADS * DN_CHUNK), BF16),
                   jax.ShapeDtypeStruct((2, B, ns, 8, LANES), F32)],
        compiler_params=_cparams("parallel", "parallel"),
        name="dn_chunk",
    )(qkv, qkv, qkv, ab, abt, arow, drow, acol, dcol)


def _dn_rec_body(uw_ref, qk_ref, gl_ref, s0_ref, o_ref, sout_ref, S_ref, *, bb, nch, G):
    C, H = DN_CHUNK, DN_HEADS
    d = pl.program_id(0)
    c = pl.program_id(2)

    @pl.when(c == 0)
    def _():
        S_ref[...] = s0_ref[...]

    r = (c + d * (nch - 1 - 2 * c)) % G
    heads = [(b, h) for b in range(bb) for h in range(H)]
    ts = []
    for b, h in heads:
        wq = jnp.concatenate([uw_ref[b, :, DN_W + h * DN_DK:DN_W + (h + 1) * DN_DK],
                              uw_ref[b, :, 2 * DN_W + h * DN_DK:2 * DN_W + (h + 1) * DN_DK]], axis=0)
        ts.append(_dot(wq, S_ref[b, h].astype(BF16)))
    for (b, h), t in zip(heads, ts):
        sl = slice(h * DN_DV, (h + 1) * DN_DV)
        v_new = (uw_ref[b, :, sl].astype(F32) - t[:C]).astype(BF16)
        o_ref[b, :, sl] = t[C:] + _dot(qk_ref[b, :, h * C:(h + 1) * C], v_new)
        gl_row = gl_ref[b, 0, pl.ds(r, 1), :]
        gl = jnp.where(d == 0, gl_row[:, h:h + 1], gl_row[:, H + h:H + h + 1])
        kd = uw_ref[b, :, 3 * DN_W + h * DN_DK:3 * DN_W + (h + 1) * DN_DK]
        S_ref[b, h] = S_ref[b, h] * gl + _dot_tn(kd, v_new)

    @pl.when(c == nch - 1)
    def _():
        sout_ref[...] = S_ref[...]


def _dn_rec(uw, qk, gl, s0, bb, G):
    _, B, T, _ = uw.shape
    C = DN_CHUNK
    nch = T // C

    def cidx(d, c):
        return c + d * (nch - 1 - 2 * c)

    s_spec = pl.BlockSpec((None, bb, DN_HEADS, DN_DK, DN_DV), lambda d, b, c: (d, b, 0, 0, 0))
    return pl.pallas_call(
        functools.partial(_dn_rec_body, bb=bb, nch=nch, G=G),
        grid=(2, B // bb, nch),
        in_specs=[pl.BlockSpec((None, bb, C, 4 * DN_W), lambda d, b, c: (d, b, cidx(d, c), 0)),
                  pl.BlockSpec((None, bb, C, DN_HEADS * C), lambda d, b, c: (d, b, cidx(d, c), 0)),
                  pl.BlockSpec((None, bb, 1, 8, LANES), lambda d, b, c: (d, b, cidx(d, c) // G, 0, 0)),
                  s_spec],
        out_specs=[pl.BlockSpec((None, bb, C, DN_W), lambda d, b, c: (d, b, cidx(d, c), 0)), s_spec],
        out_shape=[jax.ShapeDtypeStruct((2, B, T, DN_W), F32),
                   jax.ShapeDtypeStruct((2, B, DN_HEADS, DN_DK, DN_DV), F32)],
        scratch_shapes=[pltpu.VMEM((bb, DN_HEADS, DN_DK, DN_DV), F32)],
        compiler_params=_cparams("parallel", "parallel", "arbitrary"),
        name="dn_rec",
    )(uw, qk, gl, s0)


def _s5_param_body(are_ref, aim_ref, ldt_ref, bre_ref, bim_ref, lam_ref, bd_ref):
    a_re = are_ref[...]
    a_im = aim_ref[...]
    dt = jnp.exp(ldt_ref[...])
    mag = jnp.exp(a_re * dt)
    lam_re = mag * jnp.cos(a_im * dt)
    lam_im = mag * jnp.sin(a_im * dt)
    den = a_re * a_re + a_im * a_im
    nr = lam_re - 1.0
    ni = lam_im
    coef_re = (nr * a_re + ni * a_im) / den
    coef_im = (ni * a_re - nr * a_im) / den
    lam_ref[0] = lam_re
    lam_ref[1] = lam_im
    b_re = bre_ref[...]
    b_im = bim_ref[...]
    for d in range(2):
        cr = coef_re[d:d + 1, :]
        ci = coef_im[d:d + 1, :]
        bd_ref[d, 0] = cr * b_re - ci * b_im
        bd_ref[d, 1] = cr * b_im + ci * b_re


def _s5_params(A_re, A_im, log_dt, B_re, B_im):
    a_re = A_re.reshape(2, S5_NSTATE)
    a_im = A_im.reshape(2, S5_NSTATE)
    ldt = jnp.repeat(log_dt, S5_STATE, axis=1)
    b_re_t = jnp.transpose(B_re, (2, 0, 1)).reshape(S5_GROUP, S5_NSTATE)
    b_im_t = jnp.transpose(B_im, (2, 0, 1)).reshape(S5_GROUP, S5_NSTATE)
    lam, bd = pl.pallas_call(
        _s5_param_body,
        out_shape=[jax.ShapeDtypeStruct((2, 2, S5_NSTATE), F32),
                   jax.ShapeDtypeStruct((2, 2, S5_GROUP, S5_NSTATE), F32)],
        name="s5_params",
    )(a_re, a_im, ldt, b_re_t, b_im_t)
    return lam, bd


def _s5_block_weights(bd, C_re, C_im):
    eye = jnp.eye(S5_GBLK, dtype=F32)
    bd6 = bd.reshape(2, 2, S5_GROUP, S5_NBLK, S5_GBLK, S5_STATE)
    w = jnp.einsum('dchjmp,lm->djlhcmp', bd6, eye)
    w_drive = w.reshape(2, S5_NBLK, S5_GBLK * S5_GROUP, 2 * S5_SBLK).astype(BF16)
    cc = jnp.stack([C_re, -C_im], axis=0).reshape(2, S5_NBLK, S5_GBLK, S5_GROUP, S5_STATE)
    cm = jnp.einsum('cjmhp,lm->jcmplh', cc, eye)
    w_read = cm.reshape(S5_NBLK, 2, S5_SBLK, S5_GBLK * S5_GROUP).astype(BF16)
    return w_drive, w_read


def _s5_scan_body(u_ref, pin_ref, pout_ref, wd_ref, wr_ref, lam_ref, h0_ref, y_ref, hout_ref, xre, xim, hst, ytb,
                  *, B, Tc, nch, lb):
    d = pl.program_id(0)
    c = pl.program_id(1)

    @pl.when(c == 0)
    def _():
        hst[...] = h0_ref[0]

    n = Tc * B
    ub = _dot(pin_ref[...], u_ref[...].reshape(n, S5_WIDTH).astype(BF16)).astype(BF16)
    for j in range(S5_NBLK):
        drv = _dot(ub[:, j * LANES:(j + 1) * LANES], wd_ref[0, j])
        xre[:, j * S5_SBLK:(j + 1) * S5_SBLK] = drv[:, :S5_SBLK]
        xim[:, j * S5_SBLK:(j + 1) * S5_SBLK] = drv[:, S5_SBLK:]

    for lbi in range(S5_NSTATE // lb):
        ls = slice(lbi * lb, (lbi + 1) * lb)
        lr = jnp.broadcast_to(lam_ref[0, 0, :, ls], (B, lb))
        li = jnp.broadcast_to(lam_ref[1, 0, :, ls], (B, lb))

        def step(s, carry):
            hr, hi = carry
            t = s + d * (Tc - 1 - 2 * s)
            r0 = pl.multiple_of(t * B, B)
            nr = lr * hr - li * hi + xre[pl.ds(r0, B), ls]
            ni = lr * hi + li * hr + xim[pl.ds(r0, B), ls]
            xre[pl.ds(r0, B), ls] = nr
            xim[pl.ds(r0, B), ls] = ni
            return nr, ni

        hr, hi = lax.fori_loop(0, Tc, step, (hst[0, :, ls], hst[1, :, ls]), unroll=4)
        hst[0, :, ls] = hr
        hst[1, :, ls] = hi

    for j in range(S5_NBLK):
        ss = slice(j * S5_SBLK, (j + 1) * S5_SBLK)
        y = _dot(xre[:, ss].astype(BF16), wr_ref[j, 0]) + _dot(xim[:, ss].astype(BF16), wr_ref[j, 1])
        ytb[:, j * LANES:(j + 1) * LANES] = y.astype(BF16)
    y_ref[0] = _dot(pout_ref[...], ytb[...]).astype(BF16).reshape(B, Tc, S5_WIDTH)

    @pl.when(c == nch - 1)
    def _():
        hout_ref[0] = hst[...]


def _s5_scan(u, w_drive, w_read, lam, h0, Tc):
    B, T, _ = u.shape
    Tc = _tile(T, Tc)
    nch = T // Tc
    lam4 = lam.reshape(2, 2, 1, S5_NSTATE)
    n = Tc * B
    r = jnp.arange(n, dtype=jnp.int32)
    p_in = ((r[:, None] // B == r[None, :] % Tc) & (r[:, None] % B == r[None, :] // Tc)).astype(BF16)
    p_out = p_in.T

    def cidx(d, c):
        return c + d * (nch - 1 - 2 * c)

    return pl.pallas_call(
        functools.partial(_s5_scan_body, B=B, Tc=Tc, nch=nch, lb=256),
        grid=(2, nch),
        in_specs=[pl.BlockSpec((B, Tc, S5_WIDTH), lambda d, c: (0, cidx(d, c), 0)),
                  pl.BlockSpec((n, n), lambda d, c: (0, 0)),
                  pl.BlockSpec((n, n), lambda d, c: (0, 0)),
                  pl.BlockSpec((1, S5_NBLK, LANES, 2 * S5_SBLK), lambda d, c: (d, 0, 0, 0)),
                  pl.BlockSpec((S5_NBLK, 2, S5_SBLK, LANES), lambda d, c: (0, 0, 0, 0)),
                  pl.BlockSpec((2, 1, 1, S5_NSTATE), lambda d, c: (0, d, 0, 0)),
                  pl.BlockSpec((1, 2, B, S5_NSTATE), lambda d, c: (d, 0, 0, 0))],
        out_specs=[pl.BlockSpec((1, B, Tc, S5_WIDTH), lambda d, c: (d, 0, cidx(d, c), 0)),
                   pl.BlockSpec((1, 2, B, S5_NSTATE), lambda d, c: (d, 0, 0, 0))],
        out_shape=[jax.ShapeDtypeStruct((2, B, T, S5_WIDTH), BF16),
                   jax.ShapeDtypeStruct((2, 2, B, S5_NSTATE), F32)],
        scratch_shapes=[pltpu.VMEM((n, S5_NSTATE), F32),
                        pltpu.VMEM((n, S5_NSTATE), F32),
                        pltpu.VMEM((2, B, S5_NSTATE), F32),
                        pltpu.VMEM((n, S5_WIDTH), BF16)],
        compiler_params=_cparams("parallel", "arbitrary"),
        name="s5_scan",
    )(u, p_in, p_out, w_drive, w_read, lam4, h0)


def _ab_out_body(x_ref, mod_ref, o_ref, z_ref, u_ref, y_ref, ng_ref, dsk_ref, gw_ref, gb_ref,
                 woa_ref, wob_ref, out_ref):
    o = o_ref[0, 0] + o_ref[1, 0]
    z = z_ref[0]
    parts = []
    for h in range(DN_HEADS):
        sl = slice(h * DN_DV, (h + 1) * DN_DV)
        parts.append(_rms(o[:, sl]) * ng_ref[...] * _silu(z[:, sl]))
    a_out = jnp.concatenate(parts, axis=1)
    y = y_ref[0, 0].astype(F32) + y_ref[1, 0].astype(F32) + dsk_ref[...] * u_ref[0]
    y = _gelu_tanh(y)
    b_out = y * jax.nn.sigmoid(_dot(y.astype(BF16), gw_ref[...]) + gb_ref[...])
    mix = _dot(a_out.astype(BF16), woa_ref[...]) + _dot(b_out.astype(BF16), wob_ref[...])
    out_ref[0] = x_ref[0] + mod_ref[0, 2:3, :] * mix


def _ab_out(x, mod, o_dn, z, u, y_s5, dn_norm_g, d_skip, glu_w, glu_b, w_out, tm):
    B, T, D = x.shape
    tm = _tile(T, tm)
    per_batch_mod = mod.shape[0] != 1
    full = lambda *shape: pl.BlockSpec(shape, lambda b, i: (0,) * len(shape))
    return pl.pallas_call(
        _ab_out_body,
        grid=(B, T // tm),
        in_specs=[pl.BlockSpec((1, tm, D), lambda b, i: (b, i, 0)),
                  pl.BlockSpec((1, 6, D), (lambda b, i: (b, 0, 0)) if per_batch_mod else (lambda b, i: (0, 0, 0))),
                  pl.BlockSpec((2, 1, tm, DN_W), lambda b, i: (0, b, i, 0)),
                  pl.BlockSpec((1, tm, DN_W), lambda b, i: (b, i, 0)),
                  pl.BlockSpec((1, tm, S5_WIDTH), lambda b, i: (b, i, 0)),
                  pl.BlockSpec((2, 1, tm, S5_WIDTH), lambda b, i: (0, b, i, 0)),
                  full(1, DN_DV), full(1, S5_WIDTH), full(S5_WIDTH, S5_WIDTH), full(1, S5_WIDTH),
                  full(DN_W, D), full(S5_WIDTH, D)],
        out_specs=pl.BlockSpec((1, tm, D), lambda b, i: (b, i, 0)),
        out_shape=jax.ShapeDtypeStruct((B, T, D), F32),
        compiler_params=_cparams("parallel", "parallel"),
        name="ab_out",
    )(x, mod, o_dn, z, u, y_s5, dn_norm_g.reshape(1, DN_DV), d_skip.reshape(1, S5_WIDTH),
      glu_w.astype(BF16), glu_b.reshape(1, S5_WIDTH), w_out[:DN_W].astype(BF16), w_out[DN_W:].astype(BF16))


def _route(h, rw_ref, rb_ref):
    logits = _dot_nt(rw_ref[...], h, precision=HIGHEST)
    scores = jax.nn.sigmoid(logits)
    choice = scores + rb_ref[...]
    rows = [choice[e:e + 1, :] for e in range(N_EXPERTS)]
    neg_inf = jnp.float32(-jnp.inf)
    gs = []
    for g in range(N_EXPERT_GROUPS):
        r = rows[g * EXPERTS_PER_GROUP:(g + 1) * EXPERTS_PER_GROUP]
        best = None
        for a in range(EXPERTS_PER_GROUP):
            for b in range(a + 1, EXPERTS_PER_GROUP):
                s = r[a] + r[b]
                best = s if best is None else jnp.maximum(best, s)
        gs.append(best)
    best_val = gs[0]
    best_g = jnp.zeros_like(best_val, dtype=jnp.int32)
    for g in range(1, N_EXPERT_GROUPS):
        better = gs[g] > best_val
        best_val = jnp.where(better, gs[g], best_val)
        best_g = jnp.where(better, g, best_g)
    masked = [jnp.where(best_g == e // EXPERTS_PER_GROUP, rows[e], neg_inf) for e in range(N_EXPERTS)]
    m1 = masked[0]
    for e in range(1, N_EXPERTS):
        m1 = jnp.maximum(m1, masked[e])
    i1 = jnp.full_like(best_g, N_EXPERTS)
    for e in reversed(range(N_EXPERTS)):
        i1 = jnp.where(masked[e] == m1, e, i1)
    rest = [jnp.where(i1 == e, neg_inf, masked[e]) for e in range(N_EXPERTS)]
    m2 = rest[0]
    for e in range(1, N_EXPERTS):
        m2 = jnp.maximum(m2, rest[e])
    i2 = jnp.full_like(best_g, N_EXPERTS)
    for e in reversed(range(N_EXPERTS)):
        i2 = jnp.where(rest[e] == m2, e, i2)
    eidx = lax.broadcasted_iota(jnp.int32, scores.shape, 0)
    sel1 = eidx == i1
    sel2 = eidx == i2
    w1 = jnp.sum(jnp.where(sel1, scores, 0.0), axis=0, keepdims=True)
    w2 = jnp.sum(jnp.where(sel2, scores, 0.0), axis=0, keepdims=True)
    inv = 1.0 / (w1 + w2)
    return jnp.where(sel1, w1 * inv, 0.0) + jnp.where(sel2, w2 * inv, 0.0), best_g, i1, i2


def _moe_route_body(x_ref, mod_ref, g_ref, rw_ref, rb_ref, tab_ref, cls_ref):
    h = _rms(x_ref[0]) * g_ref[...]
    h = h * (1.0 + mod_ref[0, 4:5, :]) + mod_ref[0, 3:4, :]
    for c in range(N_PLANES):
        tab_ref[c] = h[:, c * LANES:(c + 1) * LANES]
    gates_t, best_g, i1, i2 = _route(h, rw_ref, rb_ref)
    rows = []
    for k in range(EXPERTS_PER_GROUP):
        gk = jnp.zeros_like(gates_t[0:1, :])
        for g in range(N_EXPERT_GROUPS):
            e = g * EXPERTS_PER_GROUP + k
            gk = jnp.where(best_g == g, gates_t[e:e + 1, :], gk)
        rows.append(gk)
    rows.append(jnp.zeros((LANES - EXPERTS_PER_GROUP, gates_t.shape[1]), F32))
    tab_ref[N_PLANES] = jnp.transpose(jnp.concatenate(rows, axis=0))
    lo = jnp.minimum(i1, i2) % EXPERTS_PER_GROUP
    hi = jnp.maximum(i1, i2) % EXPERTS_PER_GROUP
    pair = lo * (7 - lo) // 2 + (hi - lo - 1)
    cls_ref[...] = jnp.broadcast_to(best_g * PAIRS_PER_GROUP + pair, cls_ref.shape)


def _moe_route(x, mod, norm_g, router_w, router_bias, tm):
    Bx, T, D = x.shape
    tm = _tile(T, tm)
    nt = T // tm
    N = Bx * T
    per_batch_mod = mod.shape[0] != 1
    full = lambda *shape: pl.BlockSpec(shape, lambda b, i: (0,) * len(shape))
    return pl.pallas_call(
        _moe_route_body,
        grid=(Bx, nt),
        in_specs=[pl.BlockSpec((1, tm, D), lambda b, i: (b, i, 0)),
                  pl.BlockSpec((1, 6, D), (lambda b, i: (b, 0, 0)) if per_batch_mod else (lambda b, i: (0, 0, 0))),
                  full(1, D), full(N_EXPERTS, D), full(N_EXPERTS, 1)],
        out_specs=[pl.BlockSpec((N_PLANES + 1, tm, LANES), lambda b, i: (0, b * nt + i, 0)),
                   pl.BlockSpec((8, tm), lambda b, i: (0, b * nt + i))],
        out_shape=[jax.ShapeDtypeStruct((N_PLANES + 1, N, LANES), F32),
                   jax.ShapeDtypeStruct((8, N), jnp.int32)],
        compiler_params=_cparams("parallel", "parallel"),
        name="moe_route",
    )(x, mod, norm_g.reshape(1, D), router_w.T, router_bias.reshape(N_EXPERTS, 1))


def _sc_move_rows(table, idx, n_out, scatter):
    n_planes = table.shape[0]
    n = idx.shape[0]
    ch = SC_CHUNK_ROWS if n % (2 * SC_WORKERS * SC_CHUNK_ROWS) == 0 else SC_CHUNK_ROWS // 2
    n_chunks = n // (SC_WORKERS * ch)
    assert n_chunks * SC_WORKERS * ch == n and n_chunks % 2 == 0
    assert n_out == n if not scatter else table.shape[1] == n
    items = [(dj, c) for dj in range(2) for c in range(n_planes)]
    mesh = plsc.VectorSubcoreMesh(core_axis_name="c", subcore_axis_name="s")

    @functools.partial(
        pl.kernel, mesh=mesh,
        out_type=jax.ShapeDtypeStruct((n_planes, n_out, LANES), table.dtype),
        scratch_types=[pltpu.VMEM((n_chunks, ch), jnp.int32),
                       pltpu.VMEM((2, ch, LANES), table.dtype),
                       pltpu.SemaphoreType.DMA((2,)), pltpu.SemaphoreType.DMA((2,))])
    def move_kernel(table_hbm, idx_hbm, out_hbm, idx_v, rows_v, isem, osem):
        wid = lax.axis_index("s") * SC_CORES + lax.axis_index("c")
        wbase = wid * (n_chunks * ch)
        pltpu.sync_copy(idx_hbm.at[pl.ds(wid * n_chunks, n_chunks)], idx_v)

        def load(j, c, slot):
            rows = idx_v.at[j] if not scatter else pl.ds(wbase + j * ch, ch)
            return pltpu.make_async_copy(table_hbm.at[c].at[rows], rows_v.at[slot], isem.at[slot])

        def store(j, c, slot):
            rows = idx_v.at[j] if scatter else pl.ds(wbase + j * ch, ch)
            return pltpu.make_async_copy(rows_v.at[slot], out_hbm.at[c].at[rows], osem.at[slot])

        load(0, 0, 0).start()

        @pl.loop(0, n_chunks, step=2)
        def _(j):
            for it, (dj, c) in enumerate(items):
                s = it % 2
                load(j + dj, c, s).wait()

                def refill(it=it, s=s):
                    if it == 0:
                        @pl.when(j > 0)
                        def _():
                            store(j - 1, n_planes - 1, 1 - s).wait()
                    else:
                        store(j + items[it - 1][0], items[it - 1][1], 1 - s).wait()
                    if it + 1 < len(items):
                        load(j + items[it + 1][0], items[it + 1][1], 1 - s).start()
                    else:
                        load(j + 2, 0, 1 - s).start()

                if it + 1 < len(items):
                    refill()
                else:
                    pl.when(j + 2 < n_chunks)(refill)
                store(j + dj, c, s).start()

        store(n_chunks - 2 + items[-2][0], items[-2][1], 0).wait()
        store(n_chunks - 1, n_planes - 1, 1).wait()

    return move_kernel(table, idx.reshape(SC_WORKERS * n_chunks, ch))


FFN_LIVE, FFN_FIRST, FFN_LAST = 1, 2, 4


def _moe_ffn_body(st_ref, se_ref, sk_ref, sf_ref, tn_ref, xs_ref, wg_ref, wu_ref, wd_ref, ys_ref, xb, gcol, acc):
    s = pl.program_id(0)
    flags = sf_ref[s]
    k = sk_ref[s]

    @pl.when((flags & FFN_FIRST) != 0)
    def _():
        keep = lax.broadcasted_iota(jnp.int32, (xs_ref.shape[1], LANES), 0) < tn_ref[st_ref[s]]
        for c in range(N_PLANES):
            xb[:, c * LANES:(c + 1) * LANES] = jnp.where(keep, xs_ref[c], 0.0).astype(BF16)
        gcol[...] = jnp.where(keep, xs_ref[N_PLANES], 0.0)
        acc[...] = jnp.zeros_like(acc)

    @pl.when((flags & FFN_LIVE) != 0)
    def _():
        hb = xb[...]
        act = _silu(_dot(hb, wg_ref[0].astype(BF16))) * _dot(hb, wu_ref[0].astype(BF16))
        lane = lax.broadcasted_iota(jnp.int32, gcol.shape, 1)
        ge = jnp.sum(jnp.where(lane == k, gcol[...], 0.0), axis=1, keepdims=True)
        acc[...] += _dot((act * ge).astype(BF16), wd_ref[0].astype(BF16))

    @pl.when((flags & FFN_LAST) != 0)
    def _():
        for c in range(N_PLANES):
            ys_ref[c] = acc[:, c * LANES:(c + 1) * LANES]


def _moe_ffn(step_tile, step_expert, step_k, step_flags, tile_rows, xs, wg, wu, wd, tm):
    P = xs.shape[1]
    D = N_PLANES * LANES
    n_steps = step_tile.shape[0]

    def wspec(shape):
        return pl.BlockSpec(shape, lambda s, st, se, sk, sf, tn: (se[s], 0, 0))

    return pl.pallas_call(
        _moe_ffn_body,
        grid_spec=pltpu.PrefetchScalarGridSpec(
            num_scalar_prefetch=5,
            grid=(n_steps,),
            in_specs=[pl.BlockSpec((N_PLANES + 1, tm, LANES), lambda s, st, se, sk, sf, tn: (0, st[s], 0)),
                      wspec((1, D, D_EXPERT)), wspec((1, D, D_EXPERT)), wspec((1, D_EXPERT, D))],
            out_specs=pl.BlockSpec((N_PLANES, tm, LANES), lambda s, st, se, sk, sf, tn: (0, st[s], 0)),
            scratch_shapes=[pltpu.VMEM((tm, D), BF16), pltpu.VMEM((tm, LANES), F32), pltpu.VMEM((tm, D), F32)]),
        out_shape=jax.ShapeDtypeStruct((N_PLANES, P, LANES), F32),
        compiler_params=_cparams("arbitrary"),
        name="moe_ffn",
    )(step_tile, step_expert, step_k, step_flags, tile_rows, xs, wg, wu, wd)


def _moe_combine_body(x_ref, mod_ref, y_ref, fg_ref, out_ref, *, final_norm):
    y = jnp.concatenate([y_ref[c] for c in range(N_PLANES)], axis=1)
    r = x_ref[0] + mod_ref[0, 5:6, :] * y
    if final_norm:
        r = _rms(r) * fg_ref[...]
    out_ref[0] = r


def _moe_combine(x, mod, y_tok, final_g, tm):
    Bx, T, D = x.shape
    tm = _tile(T, tm)
    nt = T // tm
    per_batch_mod = mod.shape[0] != 1
    final_norm = final_g is not None
    fg = (final_g if final_norm else jnp.ones((D,), F32)).reshape(1, D)
    return pl.pallas_call(
        functools.partial(_moe_combine_body, final_norm=final_norm),
        grid=(Bx, nt),
        in_specs=[pl.BlockSpec((1, tm, D), lambda b, i: (b, i, 0)),
                  pl.BlockSpec((1, 6, D), (lambda b, i: (b, 0, 0)) if per_batch_mod else (lambda b, i: (0, 0, 0))),
                  pl.BlockSpec((N_PLANES, tm, LANES), lambda b, i: (0, b * nt + i, 0)),
                  pl.BlockSpec((1, D), lambda b, i: (0, 0))],
        out_specs=pl.BlockSpec((1, tm, D), lambda b, i: (b, i, 0)),
        out_shape=jax.ShapeDtypeStruct((Bx, T, D), F32),
        compiler_params=_cparams("parallel", "parallel"),
        name="moe_combine",
    )(x, mod, y_tok, fg)


def _moe(x, mod, norm_g, router_w, router_bias, wg, wu, wd, expert_base, final_g):
    Bx, T, D = x.shape
    N = Bx * T
    tm = MOE_TILE
    G, K, NP = N_EXPERT_GROUPS, EXPERTS_PER_GROUP, PAIRS_PER_GROUP
    table, cls_rows = _moe_route(x, mod, norm_g, router_w, router_bias, 512)
    cls = cls_rows[0]
    onehot = (cls[:, None] == jnp.arange(G * NP, dtype=jnp.int32)[None, :]).astype(jnp.int32)
    csum = jnp.cumsum(onehot, axis=0)
    c_count = csum[-1]
    g_count = c_count.reshape(G, NP).sum(axis=1)
    g_padded = (g_count + tm - 1) // tm * tm
    g_end = jnp.cumsum(g_padded)
    g_start = g_end - g_padded
    in_group = jnp.cumsum(c_count.reshape(G, NP), axis=1) - c_count.reshape(G, NP)
    c_start = (g_start[:, None] + in_group).reshape(G * NP)
    rank = jnp.sum(csum * onehot, axis=1) - 1
    pos = (jnp.sum(c_start[None, :] * onehot, axis=1) + rank).astype(jnp.int32)
    P = N + G * tm
    nt = P // tm
    tile_start = jnp.arange(nt, dtype=jnp.int32) * tm
    tile_group = jnp.minimum(jnp.sum(tile_start[:, None] >= g_end[None, :], axis=1), G - 1)
    filled = (g_start + g_count)[tile_group]
    tile_rows = jnp.clip(filled - tile_start, 0, tm).astype(jnp.int32)
    overlap = ((c_start[None, :] < tile_start[:, None] + tm) & (c_start + c_count > tile_start[:, None])
               & (c_count[None, :] > 0)).astype(jnp.int32)
    pairs = [(a, b) for a in range(K) for b in range(a + 1, K)]
    member = jnp.array([[int(k in pairs[c % NP]) for k in range(K)] for c in range(G * NP)], jnp.int32)
    used = (jnp.dot(overlap, member) > 0).astype(jnp.int32)
    seen = jnp.cumsum(used, axis=1)
    first = used * (seen == 1)
    last = used * (seen == seen[:, -1:])
    n_steps = nt * K
    order = jnp.argsort(1 - used.reshape(n_steps), stable=True).astype(jnp.int32)
    n_used = jnp.sum(used)
    live = jnp.arange(n_steps, dtype=jnp.int32) < n_used
    src = jnp.where(live, order, order[jnp.maximum(n_used - 1, 0)])
    step_tile = src // K
    step_k = src % K
    step_expert = (expert_base + tile_group[step_tile] * K + step_k).astype(jnp.int32)
    step_flags = jnp.where(live, FFN_LIVE + FFN_FIRST * first.reshape(n_steps)[src]
                           + FFN_LAST * last.reshape(n_steps)[src], 0).astype(jnp.int32)
    xs = _sc_move_rows(table, pos, P, scatter=True)
    ys = _moe_ffn(step_tile, step_expert, step_k, step_flags, tile_rows, xs, wg, wu, wd, tm)
    y_tok = _sc_move_rows(ys, pos, N, scatter=False)
    return _moe_combine(x, mod, y_tok, final_g, 512)


def _mla_q_body(cq_ref, g_ref, wn_ref, wa_ref, wb_ref, cos_ref, sin_ref, q_ref):
    hb = (_rms(cq_ref[0]) * g_ref[...]).astype(BF16)
    qn = _dot(hb, wn_ref[...]) * MLA_Q_SCALE
    ra = _dot(hb, wa_ref[...])
    rb = _dot(hb, wb_ref[...])
    cos = cos_ref[...]
    sin = sin_ref[...]
    lane = lax.broadcasted_iota(jnp.int32, cos.shape, 1)
    for h in range(MLA_HEADS):
        p = h // 2
        rot = (ra[:, p * LANES:(p + 1) * LANES] * cos + rb[:, p * LANES:(p + 1) * LANES] * sin) * MLA_Q_SCALE
        mine = (lane < MLA_ROPE) if h % 2 == 0 else (lane >= MLA_ROPE)
        q_ref[0, :, h * MLA_QK:h * MLA_QK + MLA_NOPE] = qn[:, h * MLA_NOPE:(h + 1) * MLA_NOPE].astype(BF16)
        q_ref[0, :, h * MLA_QK + MLA_NOPE:(h + 1) * MLA_QK] = jnp.where(mine, rot, 0.0).astype(BF16)


def _mla_q(cq, g, wn, wa, wb, cos_t, sin_t, tm):
    B, T, R = cq.shape
    tm = _tile(T, tm)
    full = lambda *shape: pl.BlockSpec(shape, lambda b, i: (0,) * len(shape))
    return pl.pallas_call(
        _mla_q_body,
        grid=(B, T // tm),
        in_specs=[pl.BlockSpec((1, tm, R), lambda b, i: (b, i, 0)), full(1, R),
                  full(R, MLA_HEADS * MLA_NOPE), full(R, MLA_HEADS * MLA_ROPE), full(R, MLA_HEADS * MLA_ROPE),
                  pl.BlockSpec((tm, LANES), lambda b, i: (i, 0)),
                  pl.BlockSpec((tm, LANES), lambda b, i: (i, 0))],
        out_specs=pl.BlockSpec((1, tm, MLA_HEADS * MLA_QK), lambda b, i: (b, i, 0)),
        out_shape=jax.ShapeDtypeStruct((B, T, MLA_HEADS * MLA_QK), BF16),
        compiler_params=_cparams("parallel", "parallel"),
        name="mla_q",
    )(cq, g.reshape(1, R), wn, wa, wb, cos_t, sin_t)


def _mla_kv_body(ckv_ref, ka_ref, kb_ref, g_ref, wk_ref, wv_ref, cos_ref, sin_ref, k_ref, v_ref):
    hb = (_rms(ckv_ref[0]) * g_ref[...]).astype(BF16)
    kn = _dot(hb, wk_ref[...]).astype(BF16)
    v_ref[0] = _dot(hb, wv_ref[...]).astype(BF16)
    kr =(ka_ref[0] * cos_ref[...] + kb_ref[0] * sin_ref[...]).astype(BF16)
    for h in range(MLA_HEADS):
        k_ref[0, :, h * MLA_QK:h * MLA_QK + MLA_NOPE] = kn[:, h * MLA_NOPE:(h + 1) * MLA_NOPE]
        k_ref[0, :, h * MLA_QK + MLA_NOPE:(h + 1) * MLA_QK] = kr


def _mla_kv(ckv, kra, krb, g, wk, wv, cos_t, sin_t, tm):
    B, T, R = ckv.shape
    tm = _tile(T, tm)
    W = MLA_HEADS * MLA_NOPE
    full = lambda *shape: pl.BlockSpec(shape, lambda b, i: (0,) * len(shape))
    tok = lambda n: pl.BlockSpec((1, tm, n), lambda b, i: (b, i, 0))
    return pl.pallas_call(
        _mla_kv_body,
        grid=(B, T // tm),
        in_specs=[tok(R), tok(LANES), tok(LANES), full(1, R), full(R, W), full(R, W),
                  pl.BlockSpec((tm, LANES), lambda b, i: (i, 0)),
                  pl.BlockSpec((tm, LANES), lambda b, i: (i, 0))],
        out_specs=[tok(MLA_HEADS * MLA_QK), tok(W)],
        out_shape=[jax.ShapeDtypeStruct((B, T, MLA_HEADS * MLA_QK), BF16), jax.ShapeDtypeStruct((B, T, W), BF16)],
        compiler_params=_cparams("parallel", "parallel"),
        name="mla_kv",
    )(ckv, kra, krb, g.reshape(1, R), wk, wv, cos_t, sin_t)


def _mla_attn_body(q_ref, k_ref, v_ref, o_ref):
    for h in range(MLA_HEADS):
        qk = slice(h * MLA_QK, (h + 1) * MLA_QK)
        sl = slice(h * MLA_V, (h + 1) * MLA_V)
        s = _dot_nt(q_ref[0, :, qk], k_ref[0, :, qk])
        m = jnp.max(s, axis=-1, keepdims=True)
        e = jnp.exp2(s - m)
        l = jnp.sum(e, axis=-1, keepdims=True)
        o = _dot(e.astype(BF16), v_ref[0, :, sl]) / l
        o_ref[0, :, sl] = o.astype(o_ref.dtype)


def _mla_attn(q, k, v, tq):
    B, T, WQ = q.shape
    Tk, W = v.shape[1], v.shape[2]
    tq = _tile(T, tq)
    return pl.pallas_call(
        _mla_attn_body,
        grid=(B, T // tq),
        in_specs=[pl.BlockSpec((1, tq, WQ), lambda b, i: (b, i, 0)),
                  pl.BlockSpec((1, Tk, WQ), lambda b, i: (b, 0, 0)),
                  pl.BlockSpec((1, Tk, W), lambda b, i: (b, 0, 0))],
        out_specs=pl.BlockSpec((1, tq, W), lambda b, i: (b, i, 0)),
        out_shape=jax.ShapeDtypeStruct((B, T, W), BF16),
        compiler_params=_cparams("parallel", "arbitrary"),
        name="mla_attn",
    )(q, k, v)


def _proj_res_body(x_ref, mod_ref, a_ref, w_ref, out_ref):
    out_ref[0] = x_ref[0] + mod_ref[0, 2:3, :] * _dot(a_ref[0], w_ref[...])


def _proj_res(x, mod, a, w, tm):
    B, T, D = x.shape
    K = a.shape[2]
    tm = _tile(T, tm)
    per_batch_mod = mod.shape[0] != 1
    return pl.pallas_call(
        _proj_res_body,
        grid=(B, T // tm),
        in_specs=[pl.BlockSpec((1, tm, D), lambda b, i: (b, i, 0)),
                  pl.BlockSpec((1, 6, D), (lambda b, i: (b, 0, 0)) if per_batch_mod else (lambda b, i: (0, 0, 0))),
                  pl.BlockSpec((1, tm, K), lambda b, i: (b, i, 0)),
                  pl.BlockSpec((K, D), lambda b, i: (0, 0))],
        out_specs=pl.BlockSpec((1, tm, D), lambda b, i: (b, i, 0)),
        out_shape=jax.ShapeDtypeStruct((B, T, D), F32),
        compiler_params=_cparams("parallel", "parallel"),
        name="proj_res",
    )(x, mod, a, w)


def _rope_tables(T, n_ctx):
    rows = T // GRID_W
    row = jnp.repeat(jnp.arange(rows, dtype=F32), GRID_W)
    col = jnp.tile(jnp.arange(GRID_W, dtype=F32), rows)
    n_freq = MLA_ROPE // 4
    inv = ROPE_THETA ** (-jnp.arange(n_freq, dtype=F32) / n_freq)
    ang = jnp.concatenate([row[:, None] * inv, col[:, None] * inv], axis=-1)
    cos, sin = jnp.cos(ang), jnp.sin(ang)
    cos_t = jnp.concatenate([cos, cos, cos, cos], axis=-1)
    sin_t = jnp.concatenate([-sin, sin, -sin, sin], axis=-1)
    cos_k = jnp.concatenate([jnp.ones((n_ctx, LANES), F32), cos_t], axis=0)
    sin_k = jnp.concatenate([jnp.zeros((n_ctx, LANES), F32), sin_t], axis=0)
    return cos_t, sin_t, cos_k, sin_k


def _layer_ab(x, ctx, mod_l, mod_c, norm1_g, w_in, conv_w, a_log, dt_bias, dn_norm_g, A_re, A_im, log_dt,
              B_re, B_im, C_re, C_im, D_skip, glu_w, glu_b, w_out):
    B, T, D = x.shape
    Tc = ctx.shape[1]
    q0, k0, v0, z0, a0, b0, u0 = 0, 512, 1024, 1536, 2048, 2056, 2064
    w_qkv = w_in[:, q0:z0].astype(BF16)
    w_z = w_in[:, z0:a0].astype(BF16)
    w_ab = jnp.zeros((D, LANES), F32).at[:, :16].set(w_in[:, a0:u0]).astype(BF16)
    w_u = w_in[:, u0:].astype(BF16)
    w_abt = w_in[:, a0:u0].T.astype(BF16)
    ws = [w_qkv, w_z, w_ab, w_abt, w_u]
    dts = [F32] * 5
    kinds = ["n", "n", "n", "t", "n"]
    dn_group = 4

    lam, bd = _s5_params(A_re, A_im, log_dt, B_re, B_im)
    w_drive, w_read = _s5_block_weights(bd, C_re, C_im)

    streams = []
    dn_state = jnp.zeros((2, B, DN_HEADS, DN_DK, DN_DV), F32)
    s5_state = jnp.zeros((2, 2, B, S5_NSTATE), F32)
    for xs, mod in ((ctx, mod_c), (x, mod_l)):
        qkv, z, ab, abt, u = _modmm(xs, mod, norm1_g, ws, dts, kinds, 0, 512)
        qkv = _dn_prep(qkv, conv_w, 512)
        uw, qk, gl = _dn_chunk(qkv, ab, abt, a_log, dt_bias, dn_group)
        o_dn, dn_state = _dn_rec(uw, qk, gl, dn_state, 8, dn_group)
        y_s5, s5_state = _s5_scan(u, w_drive, w_read, lam, s5_state, 32)
        streams.append(_ab_out(xs, mod, o_dn, z, u, y_s5, dn_norm_g, D_skip.reshape(-1), glu_w, glu_b,
                               w_out, 512))
    return streams[1], streams[0]


def _layer_mla(x, ctx, mod_l, mod_c, norm1_g, w_in, q_norm_g, w_q_up, kv_norm_g, w_kv_up, w_out, need_ctx):
    assert not need_ctx, "context attention output is only needed when a later layer follows"
    B, T, D = x.shape
    n_ctx = ctx.shape[1]
    qr, kvr = MLA_Q_RANK, MLA_KV_RANK
    half = MLA_ROPE // 2
    w_cq = w_in[:, :qr].astype(BF16)
    w_ckv = w_in[:, qr:qr + kvr].astype(BF16)
    wk1 = w_in[:, qr + kvr:qr + kvr + half]
    wk2 = w_in[:, qr + kvr + half:]
    w_ka = jnp.concatenate([wk1, wk2, wk1, wk2], axis=1).astype(BF16)
    w_kb = jnp.concatenate([wk2, wk1, wk2, wk1], axis=1).astype(BF16)
    ws = [w_cq, w_ckv, w_ka, w_kb]
    cq_l, ckv_l, ka_l, kb_l = _modmm(x, mod_l, norm1_g, ws, [F32] * 4, ["n"] * 4, 0, 512)
    _, ckv_c, ka_c, kb_c = _modmm(ctx, mod_c, norm1_g, ws, [F32] * 4, ["n"] * 4, 0, 512)

    wq = w_q_up.reshape(qr, MLA_HEADS, MLA_NOPE + MLA_ROPE)
    wq_n = wq[:, :, :MLA_NOPE].reshape(qr, MLA_HEADS * MLA_NOPE).astype(BF16)
    x1 = wq[:, :, MLA_NOPE:MLA_NOPE + half]
    x2 = wq[:, :, MLA_NOPE + half:]
    wq_a = jnp.concatenate([x1, x2], axis=2).reshape(qr, MLA_HEADS * MLA_ROPE).astype(BF16)
    wq_b = jnp.concatenate([x2, x1], axis=2).reshape(qr, MLA_HEADS * MLA_ROPE).astype(BF16)
    wkv = w_kv_up.reshape(kvr, MLA_HEADS, MLA_NOPE + MLA_V)
    wk_n = wkv[:, :, :MLA_NOPE].reshape(kvr, MLA_HEADS * MLA_NOPE).astype(BF16)
    wv = wkv[:, :, MLA_NOPE:].reshape(kvr, MLA_HEADS * MLA_V).astype(BF16)

    cos_t, sin_t, cos_k, sin_k = _rope_tables(T, n_ctx)
    q = _mla_q(cq_l, q_norm_g, wq_n, wq_a, wq_b, cos_t, sin_t, 512)
    ckv = jnp.concatenate([ckv_c, ckv_l], axis=1)
    ka = jnp.concatenate([ka_c, ka_l], axis=1)
    kb = jnp.concatenate([kb_c, kb_l], axis=1)
    k, v = _mla_kv(ckv, ka, kb, kv_norm_g, wk_n, wv, cos_k, sin_k, 256)
    o = _mla_attn(q, k, v, 256)
    return _proj_res(x, mod_l, o, w_out.astype(BF16), 512)


def kernel(x, c, ctx, c_ctx, ada_w, ada_b, norm1_g, norm2_g, ab_w_in, dn_conv_w, dn_A_log, dn_dt_bias, dn_norm_g, s5_A_re, s5_A_im, s5_log_dt, s5_B_re, s5_B_im, s5_C_re, s5_C_im, s5_D, s5_glu_w, s5_glu_b, ab_w_out, mla_w_in, mla_q_norm_g, mla_w_q_up, mla_kv_norm_g, mla_w_kv_up, mla_w_out, router_w, router_bias, moe_w_gate, moe_w_up, moe_w_down, final_norm_g):
    B, T, D = x.shape
    n_ctx = ctx.shape[1]
    depth = ada_w.shape[0]
    n_cond = -(-(B + 1) // 8) * 8
    cond = jnp.zeros((n_cond, D), F32).at[:B].set(c).at[B].set(c_ctx)
    mods = _adaln_all(cond, ada_w, ada_b).reshape(depth, n_cond, 6, D)
    n_exp = moe_w_gate.shape[1]
    wg = moe_w_gate.reshape((depth * n_exp,) + moe_w_gate.shape[2:])
    wu = moe_w_up.reshape((depth * n_exp,) + moe_w_up.shape[2:])
    wd = moe_w_down.reshape((depth * n_exp,) + moe_w_down.shape[2:])
    for i in range(depth):
        last = i == depth - 1
        j = i // 2
        mod_l = mods[i, :B]
        mod_c = mods[i, B:B + 1]
        if i % 2 == 0:
            x, ctx_new = _layer_ab(x, ctx, mod_l, mod_c, norm1_g[i], ab_w_in[j], dn_conv_w[j], dn_A_log[j],
                                   dn_dt_bias[j], dn_norm_g[j], s5_A_re[j], s5_A_im[j], s5_log_dt[j],
                                   s5_B_re[j], s5_B_im[j], s5_C_re[j], s5_C_im[j], s5_D[j], s5_glu_w[j],
                                   s5_glu_b[j], ab_w_out[j])
        else:
            x = _layer_mla(x, ctx, mod_l, mod_c, norm1_g[i], mla_w_in[j], mla_q_norm_g[j], mla_w_q_up[j],
                           mla_kv_norm_g[j], mla_w_kv_up[j], mla_w_out[j], not last)
            ctx_new = None
        x = _moe(x, mod_l, norm2_g[i], router_w, router_bias, wg, wu, wd, i * n_exp,
                 final_norm_g if last else None)
        if not last:
            ctx_flat = _moe(ctx_new.reshape(1, B * n_ctx, D), mod_c, norm2_g[i], router_w, router_bias,
                            wg, wu, wd, i * n_exp, None)
            ctx = ctx_flat.reshape(B, n_ctx, D)
    return x
```

```python
import functools
import math

import jax
import jax.numpy as jnp
from jax import lax
from jax.experimental import pallas as pl
from jax.experimental.pallas import tpu as pltpu
from jax.experimental.pallas import tpu_sc as plsc

F32 = jnp.float32
BF16 = jnp.bfloat16
HIGHEST = lax.Precision.HIGHEST

NORM_EPS = 1e-6
GRID_W = 64
ROPE_THETA = 10000.0

DN_HEADS = 4
DN_DK = 128
DN_DV = 128
DN_CONV = 5
DN_CHUNK = 64
DN_W = DN_HEADS * DN_DK

S5_WIDTH = 512
S5_GROUP = 16
S5_GROUPS = 32
S5_STATE = 64
S5_NSTATE = S5_GROUPS * S5_STATE
S5_GBLK = 8
S5_NBLK = S5_GROUPS // S5_GBLK
S5_SBLK = S5_GBLK * S5_STATE

MLA_HEADS = 8
MLA_Q_RANK = 384
MLA_KV_RANK = 256
MLA_NOPE = 128
MLA_ROPE = 64
MLA_V = 128
MLA_Q_SCALE = (MLA_NOPE + MLA_ROPE) ** -0.5 * math.log2(math.e)

N_EXPERTS = 16
N_EXPERT_GROUPS = 4
EXPERTS_PER_GROUP = 4
D_EXPERT = 512
PAIRS_PER_GROUP = EXPERTS_PER_GROUP * (EXPERTS_PER_GROUP - 1) // 2
MOE_TILE = 1024

N_PLANES = 4
SC_CORES = 2
SC_WORKERS = 32
SC_CHUNK_ROWS = 128

LANES = 128
MLA_QK = MLA_NOPE + LANES
VMEM_LIMIT_BYTES = 56 * 1024 * 1024


def _tile(n, pref):
    t = min(pref, n)
    while n % t or t % 8:
        t -= 1
    return t


def _cparams(*sem):
    return pltpu.CompilerParams(dimension_semantics=sem, vmem_limit_bytes=VMEM_LIMIT_BYTES)


def _silu(x):
    return x * jax.nn.sigmoid(x)


def _softplus(x):
    return jnp.maximum(x, 0.0) + jnp.log(1.0 + jnp.exp(-jnp.abs(x)))


def _gelu_tanh(x):
    return 0.5 * x * (1.0 + jnp.tanh(math.sqrt(2.0 / math.pi) * (x + 0.044715 * (x * x * x))))


def _rms(x):
    return x * lax.rsqrt(jnp.mean(x * x, axis=-1, keepdims=True) + NORM_EPS)


def _dot(a, b):
    return jnp.dot(a, b, preferred_element_type=F32)


def _dot_nt(a, b, precision=None):
    return lax.dot_general(a, b, (((1,), (1,)), ((), ())), preferred_element_type=F32,
                           precision=precision)


def _dot_tn(a, b):
    return lax.dot_general(a, b, (((0,), (0,)), ((), ())), preferred_element_type=F32)


def _ada_body(c_ref, w_ref, b_ref, o_ref):
    c = c_ref[...]
    o_ref[0] = _dot(_silu(c).astype(BF16), w_ref[0].astype(BF16)) + b_ref[0]


def _adaln_all(cond, ada_w, ada_b):
    L, D, D6 = ada_w.shape
    R = cond.shape[0]
    tn = 1536
    return pl.pallas_call(
        _ada_body,
        grid=(L, D6 // tn),
        in_specs=[pl.BlockSpec((R, D), lambda l, j: (0, 0)),
                  pl.BlockSpec((1, D, tn), lambda l, j: (l, 0, j)),
                  pl.BlockSpec((1, 1, tn), lambda l, j: (l, 0, j))],
        out_specs=pl.BlockSpec((1, R, tn), lambda l, j: (l, 0, j)),
        out_shape=jax.ShapeDtypeStruct((L, R, D6), F32),
        compiler_params=_cparams("parallel", "parallel"),
        name="adaln",
    )(cond, ada_w, ada_b.reshape(L, 1, D6))


def _modmm_body(x_ref, mod_ref, g_ref, *refs, kinds, shift_row):
    n_out = len(kinds)
    w_refs, o_refs = refs[:n_out], refs[n_out:]
    h = _rms(x_ref[0]) * g_ref[...]
    h = h * (1.0 + mod_ref[0, shift_row + 1:shift_row + 2, :]) + mod_ref[0, shift_row:shift_row + 1, :]
    hb = h.astype(BF16)
    for w_ref, o_ref, kind in zip(w_refs, o_refs, kinds):
        if kind == "t":
            o_ref[0] = _dot_nt(w_ref[...], hb).astype(o_ref.dtype)
        else:
            o_ref[0] = _dot(hb, w_ref[...]).astype(o_ref.dtype)


def _modmm(x, mod, g, ws, out_dtypes, kinds, shift_row, tm):
    Bx, T, D = x.shape
    tm = _tile(T, tm)
    per_batch_mod = mod.shape[0] != 1
    in_specs = [pl.BlockSpec((1, tm, D), lambda b, i: (b, i, 0)),
                pl.BlockSpec((1, 6, D), (lambda b, i: (b, 0, 0)) if per_batch_mod else (lambda b, i: (0, 0, 0))),
                pl.BlockSpec((1, D), lambda b, i: (0, 0))]
    out_specs, out_shape = [], []
    for w, dt, kind in zip(ws, out_dtypes, kinds):
        in_specs.append(pl.BlockSpec(w.shape, lambda b, i: (0, 0)))
        if kind == "t":
            n = w.shape[0]
            out_specs.append(pl.BlockSpec((1, n, tm), lambda b, i: (b, 0, i)))
            out_shape.append(jax.ShapeDtypeStruct((Bx, n, T), dt))
        else:
            n = w.shape[1]
            out_specs.append(pl.BlockSpec((1, tm, n), lambda b, i: (b, i, 0)))
            out_shape.append(jax.ShapeDtypeStruct((Bx, T, n), dt))
    return pl.pallas_call(
        functools.partial(_modmm_body, kinds=tuple(kinds), shift_row=shift_row),
        grid=(Bx, T // tm),
        in_specs=in_specs, out_specs=out_specs, out_shape=out_shape,
        compiler_params=_cparams("parallel", "parallel"),
        name="modmm",
    )(x, mod, g.reshape(1, D), *ws)


def _dn_prep_body(x_ref, xp_ref, xn_ref, w_ref, o_ref, buf, *, tm, nt):
    i = pl.program_id(1)
    j = pl.program_id(2)
    buf[0:8, :] = jnp.where(i == 0, 0.0, xp_ref[0])
    buf[8:8 + tm, :] = x_ref[0]
    buf[8 + tm:16 + tm, :] = jnp.where(i == nt - 1, 0.0, xn_ref[0])
    acc = buf[pl.ds(8 - DN_CONV // 2, tm), :] * w_ref[0:1, :]
    for kk in range(1, DN_CONV):
        acc = acc + buf[pl.ds(8 - DN_CONV // 2 + kk, tm), :] * w_ref[kk:kk + 1, :]
    y = _silu(acc)
    q_scale = jnp.where(j == 0, DN_DK ** -0.5, 1.0)
    for h in range(DN_HEADS):
        yh = y[:, h * DN_DK:(h + 1) * DN_DK]
        r = lax.rsqrt(jnp.sum(yh * yh, axis=-1, keepdims=True) + NORM_EPS) * q_scale
        o_ref[0, :, h * DN_DK:(h + 1) * DN_DK] = yh * jnp.where(j == 2, 1.0, r)


def _dn_prep(qkv, conv_w, tm):
    B, T, W3 = qkv.shape
    tm = _tile(T, tm)
    nt = T // tm
    r8 = tm // 8
    return pl.pallas_call(
        functools.partial(_dn_prep_body, tm=tm, nt=nt),
        grid=(B, nt, 3),
        in_specs=[pl.BlockSpec((1, tm, DN_W), lambda b, i, j: (b, i, j)),
                  pl.BlockSpec((1, 8, DN_W), lambda b, i, j: (b, jnp.maximum(i * r8 - 1, 0), j)),
                  pl.BlockSpec((1, 8, DN_W), lambda b, i, j: (b, jnp.minimum((i + 1) * r8, T // 8 - 1), j)),
                  pl.BlockSpec((DN_CONV, DN_W), lambda b, i, j: (0, j))],
        out_specs=pl.BlockSpec((1, tm, DN_W), lambda b, i, j: (b, i, j)),
        out_shape=jax.ShapeDtypeStruct((B, T, W3), F32),
        scratch_shapes=[pltpu.VMEM((tm + 16, DN_W), F32)],
        compiler_params=_cparams("parallel", "parallel", "parallel"),
        name="dn_prep",
    )(qkv, qkv, qkv, conv_w)


def _heads_to_lanes(cols, width):
    return jnp.concatenate([jnp.broadcast_to(c, (c.shape[0], width)) for c in cols], axis=1)


def _block_diag(x, nblk):
    C, W = x.shape
    w = W // nblk
    t = jnp.concatenate([x] * nblk, axis=0)
    rb = lax.broadcasted_iota(jnp.int32, t.shape, 0) // C
    cb = lax.broadcasted_iota(jnp.int32, t.shape, 1) // w
    return jnp.where(rb == cb, t, jnp.zeros_like(t))


def _dn_chunk_body(q_ref, k_ref, v_ref, ab_ref, abt_ref, arow_ref, drow_ref, acol_ref, dcol_ref,
                   uw_ref, qk_ref, gl_ref, *, G):
    C, H = DN_CHUNK, DN_HEADS
    Tg = G * C
    gl_ref[...] = jnp.zeros_like(gl_ref)
    ab = ab_ref[0]
    abt = abt_ref[0]
    g_all = -jnp.exp(arow_ref[...]) * _softplus(ab + drow_ref[...])
    gt_all = -jnp.exp(acol_ref[...]) * _softplus(abt + dcol_ref[...])
    beta_all = jax.nn.sigmoid(ab)
    pos_s = lax.broadcasted_iota(jnp.int32, (Tg, LANES), 0) % C
    pos_l = lax.broadcasted_iota(jnp.int32, (2 * H * 2, Tg), 1) % C
    gc_all, gct_all = g_all, gt_all
    s = 1
    while s < C:
        gc_all = gc_all + jnp.where(pos_s >= s, pltpu.roll(gc_all, s, 0), 0.0)
        gct_all = gct_all + jnp.where(pos_l >= s, pltpu.roll(gct_all, s, 1), 0.0)
        s *= 2

    ri = lax.broadcasted_iota(jnp.int32, (C, H * C), 0)
    cj = lax.broadcasted_iota(jnp.int32, (C, H * C), 1) % C
    eye_side = (ri == cj).astype(F32)

    chains = []
    for ci in range(G):
        rows = slice(ci * C, (ci + 1) * C)
        q = q_ref[0, rows, :]
        k = k_ref[0, rows, :]
        v = v_ref[0, rows, :]
        kbd = _block_diag(k.astype(BF16), H)
        g_c, gc_f, beta_c = g_all[rows], gc_all[rows], beta_all[rows]
        gt_c, gct_f = gt_all[:, rows], gct_all[:, rows]
        gtot = gc_f[C - 1:C, :]
        gtot_t = gct_f[:, C - 1:C]
        for d in range(2):
            if d == 0:
                gc, gct = gc_f, gct_f
                incl, strict = ri >= cj, ri > cj
            else:
                gc, gct = gtot - gc_f + g_c, gtot_t - gct_f + gt_c
                incl, strict = ri <= cj, ri < cj
            lanes = [d * H + h for h in range(H)]
            gcol = [gc[:, l:l + 1] for l in lanes]
            diff = _heads_to_lanes(gcol, C) - jnp.concatenate([gct[l:l + 1, :] for l in lanes], axis=1)
            decay = jnp.where(incl, jnp.exp(jnp.where(incl, diff, 0.0)), 0.0)
            beta_b = _heads_to_lanes([beta_c[:, 2 * H + l:2 * H + l + 1] for l in lanes], DN_DK)
            egc_b = _heads_to_lanes([jnp.exp(c) for c in gcol], DN_DK)
            ekd_b = _heads_to_lanes([jnp.exp(gtot[:, l:l + 1] - gc[:, l:l + 1]) for l in lanes], DN_DK)
            kb = k * beta_b
            a_low = jnp.where(strict, _dot_nt(kb.astype(BF16), kbd) * decay, 0.0)
            qk_ref[d, 0, rows, :] = jnp.where(incl, _dot_nt(q.astype(BF16), kbd) * decay, 0.0).astype(BF16)
            uw_ref[d, 0, rows, 2 * DN_W:3 * DN_W] = (q * egc_b).astype(BF16)
            uw_ref[d, 0, rows, 3 * DN_W:4 * DN_W] = (k * ekd_b).astype(BF16)
            gl_ref[d, 0, 0, ci:ci + 1, :] = jnp.exp(gtot)
            rhs = jnp.concatenate([v * beta_b, kb * egc_b], axis=1).astype(BF16)
            chains.append(dict(d=d, rows=rows, pw=-a_low, inv=eye_side - a_low, rhs=rhs))

    n_lvl = int(math.log2(C)) - 1
    for ch in chains:
        pwb = ch["pw"].astype(BF16)
        ch["pw"] = _dot(pwb, _block_diag(pwb, H))
    for lvl in range(1, n_lvl + 1):
        for ch in chains:
            pwb = ch["pw"].astype(BF16)
            pbd = _block_diag(pwb, H)
            if lvl < n_lvl:
                st = _dot(jnp.concatenate([ch["inv"].astype(BF16), pwb], axis=0), pbd)
                ch["inv"] = ch["inv"] + st[:C]
                ch["pw"] = st[C:]
            else:
                ch["inv"] = ch["inv"] + _dot(ch["inv"].astype(BF16), pbd)
    rb = lax.broadcasted_iota(jnp.int32, (H * C, 2 * DN_W), 0) // C
    cb = (lax.broadcasted_iota(jnp.int32, (H * C, 2 * DN_W), 1) // DN_DK) % H
    for ch in chains:
        rhs_bd = jnp.where(rb == cb, jnp.concatenate([ch["rhs"]] * H, axis=0), jnp.zeros((), BF16))
        sol = _dot(ch["inv"].astype(BF16), rhs_bd)
        uw_ref[ch["d"], 0, ch["rows"], 0:2 * DN_W] = sol.astype(BF16)


def _dn_chunk(qkv, ab, abt, a_log, dt_bias, G):
    B, T, _ = qkv.shape
    Tg = G * DN_CHUNK
    ns = T // Tg
    nl = 2 * DN_HEADS
    alog = a_log.reshape(-1)
    dtb = dt_bias.reshape(-1)
    arow = jnp.zeros((1, LANES), F32).at[0, :nl].set(alog)
    drow = jnp.zeros((1, LANES), F32).at[0, :nl].set(dtb)
    acol = jnp.zeros((2 * nl, 1), F32).at[:nl, 0].set(alog)
    dcol = jnp.zeros((2 * nl, 1), F32).at[:nl, 0].set(dtb)
    full = lambda *shape: pl.BlockSpec(shape, lambda b, i: (0,) * len(shape))
    return pl.pallas_call(
        functools.partial(_dn_chunk_body, G=G),
        grid=(B, ns),
        in_specs=[pl.BlockSpec((1, Tg, DN_W), lambda b, i: (b, i, 0)),
                  pl.BlockSpec((1, Tg, DN_W), lambda b, i: (b, i, 1)),
                  pl.BlockSpec((1, Tg, DN_W), lambda b, i: (b, i, 2)),
                  pl.BlockSpec((1, Tg, LANES), lambda b, i: (b, i, 0)),
                  pl.BlockSpec((1, 2 * nl, Tg), lambda b, i: (b, 0, i)),
                  full(1, LANES), full(1, LANES), full(2 * nl, 1), full(2 * nl, 1)],
        out_specs=[pl.BlockSpec((2, 1, Tg, 4 * DN_W), lambda b, i: (0, b, i, 0)),
                   pl.BlockSpec((2, 1, Tg, DN_HEADS * DN_CHUNK), lambda b, i: (0, b, i, 0)),
                   pl.BlockSpec((2, 1, 1, 8, LANES), lambda b, i: (0, b, i, 0, 0))],
        out_shape=[jax.ShapeDtypeStruct((2, B, T, 4 * DN_W), BF16),
                   jax.ShapeDtypeStruct((2, B, T, DN_HEADS * DN_CHUNK), BF16),
                   jax.ShapeDtypeStruct((2, B, ns, 8, LANES), F32)],
        compiler_params=_cparams("parallel", "parallel"),
        name="dn_chunk",
    )(qkv, qkv, qkv, ab, abt, arow, drow, acol, dcol)


def _dn_rec_body(uw_ref, qk_ref, gl_ref, s0_ref, o_ref, sout_ref, S_ref, *, bb, nch, G):
    C, H = DN_CHUNK, DN_HEADS
    d = pl.program_id(0)
    c = pl.program_id(2)

    @pl.when(c == 0)
    def _():
        S_ref[...] = s0_ref[...]

    r = (c + d * (nch - 1 - 2 * c)) % G
    heads = [(b, h) for b in range(bb) for h in range(H)]
    ts = []
    for b, h in heads:
        wq = jnp.concatenate([uw_ref[b, :, DN_W + h * DN_DK:DN_W + (h + 1) * DN_DK],
                              uw_ref[b, :, 2 * DN_W + h * DN_DK:2 * DN_W + (h + 1) * DN_DK]], axis=0)
        ts.append(_dot(wq, S_ref[b, h].astype(BF16)))
    for (b, h), t in zip(heads, ts):
        sl = slice(h * DN_DV, (h + 1) * DN_DV)
        v_new = (uw_ref[b, :, sl].astype(F32) - t[:C]).astype(BF16)
        o_ref[b, :, sl] = t[C:] + _dot(qk_ref[b, :, h * C:(h + 1) * C], v_new)
        gl_row = gl_ref[b, 0, pl.ds(r, 1), :]
        gl = jnp.where(d == 0, gl_row[:, h:h + 1], gl_row[:, H + h:H + h + 1])
        kd = uw_ref[b, :, 3 * DN_W + h * DN_DK:3 * DN_W + (h + 1) * DN_DK]
        S_ref[b, h] = S_ref[b, h] * gl + _dot_tn(kd, v_new)

    @pl.when(c == nch - 1)
    def _():
        sout_ref[...] = S_ref[...]


def _dn_rec(uw, qk, gl, s0, bb, G):
    _, B, T, _ = uw.shape
    C = DN_CHUNK
    nch = T // C

    def cidx(d, c):
        return c + d * (nch - 1 - 2 * c)

    s_spec = pl.BlockSpec((None, bb, DN_HEADS, DN_DK, DN_DV), lambda d, b, c: (d, b, 0, 0, 0))
    return pl.pallas_call(
        functools.partial(_dn_rec_body, bb=bb, nch=nch, G=G),
        grid=(2, B // bb, nch),
        in_specs=[pl.BlockSpec((None, bb, C, 4 * DN_W), lambda d, b, c: (d, b, cidx(d, c), 0)),
                  pl.BlockSpec((None, bb, C, DN_HEADS * C), lambda d, b, c: (d, b, cidx(d, c), 0)),
                  pl.BlockSpec((None, bb, 1, 8, LANES), lambda d, b, c: (d, b, cidx(d, c) // G, 0, 0)),
                  s_spec],
        out_specs=[pl.BlockSpec((None, bb, C, DN_W), lambda d, b, c: (d, b, cidx(d, c), 0)), s_spec],
        out_shape=[jax.ShapeDtypeStruct((2, B, T, DN_W), F32),
                   jax.ShapeDtypeStruct((2, B, DN_HEADS, DN_DK, DN_DV), F32)],
        scratch_shapes=[pltpu.VMEM((bb, DN_HEADS, DN_DK, DN_DV), F32)],
        compiler_params=_cparams("parallel", "parallel", "arbitrary"),
        name="dn_rec",
    )(uw, qk, gl, s0)


def _s5_param_body(are_ref, aim_ref, ldt_ref, bre_ref, bim_ref, lam_ref, bd_ref):
    a_re = are_ref[...]
    a_im = aim_ref[...]
    dt = jnp.exp(ldt_ref[...])
    mag = jnp.exp(a_re * dt)
    lam_re = mag * jnp.cos(a_im * dt)
    lam_im = mag * jnp.sin(a_im * dt)
    den = a_re * a_re + a_im * a_im
    nr = lam_re - 1.0
    ni = lam_im
    coef_re = (nr * a_re + ni * a_im) / den
    coef_im = (ni * a_re - nr * a_im) / den
    lam_ref[0] = lam_re
    lam_ref[1] = lam_im
    b_re = bre_ref[...]
    b_im = bim_ref[...]
    for d in range(2):
        cr = coef_re[d:d + 1, :]
        ci = coef_im[d:d + 1, :]
        bd_ref[d, 0] = cr * b_re - ci * b_im
        bd_ref[d, 1] = cr * b_im + ci * b_re


def _s5_params(A_re, A_im, log_dt, B_re, B_im):
    a_re = A_re.reshape(2, S5_NSTATE)
    a_im = A_im.reshape(2, S5_NSTATE)
    ldt = jnp.repeat(log_dt, S5_STATE, axis=1)
    b_re_t = jnp.transpose(B_re, (2, 0, 1)).reshape(S5_GROUP, S5_NSTATE)
    b_im_t = jnp.transpose(B_im, (2, 0, 1)).reshape(S5_GROUP, S5_NSTATE)
    lam, bd = pl.pallas_call(
        _s5_param_body,
        out_shape=[jax.ShapeDtypeStruct((2, 2, S5_NSTATE), F32),
                   jax.ShapeDtypeStruct((2, 2, S5_GROUP, S5_NSTATE), F32)],
        name="s5_params",
    )(a_re, a_im, ldt, b_re_t, b_im_t)
    return lam, bd


def _s5_block_weights(bd, C_re, C_im):
    eye = jnp.eye(S5_GBLK, dtype=F32)
    bd6 = bd.reshape(2, 2, S5_GROUP, S5_NBLK, S5_GBLK, S5_STATE)
    w = jnp.einsum('dchjmp,lm->djlhcmp', bd6, eye)
    w_drive = w.reshape(2, S5_NBLK, S5_GBLK * S5_GROUP, 2 * S5_SBLK).astype(BF16)
    cc = jnp.stack([C_re, -C_im], axis=0).reshape(2, S5_NBLK, S5_GBLK, S5_GROUP, S5_STATE)
    cm = jnp.einsum('cjmhp,lm->jcmplh', cc, eye)
    w_read = cm.reshape(S5_NBLK, 2, S5_SBLK, S5_GBLK * S5_GROUP).astype(BF16)
    return w_drive, w_read


def _s5_scan_body(u_ref, pin_ref, pout_ref, wd_ref, wr_ref, lam_ref, h0_ref, y_ref, hout_ref,
                  xre, xim, sre, sim, hst, ytb,
                  *, B, Tc, nch, lb):
    d = pl.program_id(0)
    c = pl.program_id(1)

    @pl.when(c == 0)
    def _():
        hst[...] = h0_ref[0]

    n = Tc * B
    ub = _dot(pin_ref[...], u_ref[...].reshape(n, S5_WIDTH).astype(BF16)).astype(BF16)
    for j in range(S5_NBLK):
        drv = _dot(ub[:, j * LANES:(j + 1) * LANES], wd_ref[0, j])
        xre[:, j * S5_SBLK:(j + 1) * S5_SBLK] = drv[:, :S5_SBLK]
        xim[:, j * S5_SBLK:(j + 1) * S5_SBLK] = drv[:, S5_SBLK:]

    for lbi in range(S5_NSTATE // lb):
        ls = slice(lbi * lb, (lbi + 1) * lb)
        lr = jnp.broadcast_to(lam_ref[0, 0, :, ls], (B, lb))
        li = jnp.broadcast_to(lam_ref[1, 0, :, ls], (B, lb))

        def step(s, carry):
            hr, hi = carry
            t = s + d * (Tc - 1 - 2 * s)
            r0 = pl.multiple_of(t * B, B)
            nr = lr * hr - li * hi + xre[pl.ds(r0, B), ls]
            ni = lr * hi + li * hr + xim[pl.ds(r0, B), ls]
            sre[pl.ds(r0, B), ls] = nr.astype(BF16)
            sim[pl.ds(r0, B), ls] = ni.astype(BF16)
            return nr, ni

        hr, hi = lax.fori_loop(0, Tc, step, (hst[0, :, ls], hst[1, :, ls]), unroll=4)
        hst[0, :, ls] = hr
        hst[1, :, ls] = hi

    for j in range(S5_NBLK):
        ss = slice(j * S5_SBLK, (j + 1) * S5_SBLK)
        y = _dot(sre[:, ss], wr_ref[j, 0]) + _dot(sim[:, ss], wr_ref[j, 1])
        ytb[:, j * LANES:(j + 1) * LANES] = y.astype(BF16)
    y_ref[0] = _dot(pout_ref[...], ytb[...]).astype(BF16).reshape(B, Tc, S5_WIDTH)

    @pl.when(c == nch - 1)
    def _():
        hout_ref[0] = hst[...]


def _s5_scan(u, w_drive, w_read, lam, h0, Tc):
    B, T, _ = u.shape
    Tc = _tile(T, Tc)
    nch = T // Tc
    lam4 = lam.reshape(2, 2, 1, S5_NSTATE)
    n = Tc * B
    r = jnp.arange(n, dtype=jnp.int32)
    p_in = ((r[:, None] // B == r[None, :] % Tc) & (r[:, None] % B == r[None, :] // Tc)).astype(BF16)
    p_out = p_in.T

    def cidx(d, c):
        return c + d * (nch - 1 - 2 * c)

    return pl.pallas_call(
        functools.partial(_s5_scan_body, B=B, Tc=Tc, nch=nch, lb=256),
        grid=(2, nch),
        in_specs=[pl.BlockSpec((B, Tc, S5_WIDTH), lambda d, c: (0, cidx(d, c), 0)),
                  pl.BlockSpec((n, n), lambda d, c: (0, 0)),
                  pl.BlockSpec((n, n), lambda d, c: (0, 0)),
                  pl.BlockSpec((1, S5_NBLK, LANES, 2 * S5_SBLK), lambda d, c: (d, 0, 0, 0)),
                  pl.BlockSpec((S5_NBLK, 2, S5_SBLK, LANES), lambda d, c: (0, 0, 0, 0)),
                  pl.BlockSpec((2, 1, 1, S5_NSTATE), lambda d, c: (0, d, 0, 0)),
                  pl.BlockSpec((1, 2, B, S5_NSTATE), lambda d, c: (d, 0, 0, 0))],
        out_specs=[pl.BlockSpec((1, B, Tc, S5_WIDTH), lambda d, c: (d, 0, cidx(d, c), 0)),
                   pl.BlockSpec((1, 2, B, S5_NSTATE), lambda d, c: (d, 0, 0, 0))],
        out_shape=[jax.ShapeDtypeStruct((2, B, T, S5_WIDTH), BF16),
                   jax.ShapeDtypeStruct((2, 2, B, S5_NSTATE), F32)],
        scratch_shapes=[pltpu.VMEM((n, S5_NSTATE), F32),
                        pltpu.VMEM((n, S5_NSTATE), F32),
                        pltpu.VMEM((n, S5_NSTATE), BF16),
                        pltpu.VMEM((n, S5_NSTATE), BF16),
                        pltpu.VMEM((2, B, S5_NSTATE), F32),
                        pltpu.VMEM((n, S5_WIDTH), BF16)],
        compiler_params=_cparams("parallel", "arbitrary"),
        name="s5_scan",
    )(u, p_in, p_out, w_drive, w_read, lam4, h0)


def _ab_out_body(x_ref, mod_ref, o_ref, z_ref, u_ref, y_ref, ng_ref, dsk_ref, gw_ref, gb_ref,
                 woa_ref, wob_ref, out_ref):
    o = o_ref[0, 0] + o_ref[1, 0]
    z = z_ref[0]
    parts = []
    for h in range(DN_HEADS):
        sl = slice(h * DN_DV, (h + 1) * DN_DV)
        parts.append(_rms(o[:, sl]) * ng_ref[...] * _silu(z[:, sl]))
    a_out = jnp.concatenate(parts, axis=1)
    y = y_ref[0, 0].astype(F32) + y_ref[1, 0].astype(F32) + dsk_ref[...] * u_ref[0]
    y = _gelu_tanh(y)
    b_out = y * jax.nn.sigmoid(_dot(y.astype(BF16), gw_ref[...]) + gb_ref[...])
    mix = _dot(a_out.astype(BF16), woa_ref[...]) + _dot(b_out.astype(BF16), wob_ref[...])
    out_ref[0] = x_ref[0] + mod_ref[0, 2:3, :] * mix


def _ab_out(x, mod, o_dn, z, u, y_s5, dn_norm_g, d_skip, glu_w, glu_b, w_out, tm):
    B, T, D = x.shape
    tm = _tile(T, tm)
    per_batch_mod = mod.shape[0] != 1
    full = lambda *shape: pl.BlockSpec(shape, lambda b, i: (0,) * len(shape))
    return pl.pallas_call(
        _ab_out_body,
        grid=(B, T // tm),
        in_specs=[pl.BlockSpec((1, tm, D), lambda b, i: (b, i, 0)),
                  pl.BlockSpec((1, 6, D), (lambda b, i: (b, 0, 0)) if per_batch_mod else (lambda b, i: (0, 0, 0))),
                  pl.BlockSpec((2, 1, tm, DN_W), lambda b, i: (0, b, i, 0)),
                  pl.BlockSpec((1, tm, DN_W), lambda b, i: (b, i, 0)),
                  pl.BlockSpec((1, tm, S5_WIDTH), lambda b, i: (b, i, 0)),
                  pl.BlockSpec((2, 1, tm, S5_WIDTH), lambda b, i: (0, b, i, 0)),
                  full(1, DN_DV), full(1, S5_WIDTH), full(S5_WIDTH, S5_WIDTH), full(1, S5_WIDTH),
                  full(DN_W, D), full(S5_WIDTH, D)],
        out_specs=pl.BlockSpec((1, tm, D), lambda b, i: (b, i, 0)),
        out_shape=jax.ShapeDtypeStruct((B, T, D), F32),
        compiler_params=_cparams("parallel", "parallel"),
        name="ab_out",
    )(x, mod, o_dn, z, u, y_s5, dn_norm_g.reshape(1, DN_DV), d_skip.reshape(1, S5_WIDTH),
      glu_w.astype(BF16), glu_b.reshape(1, S5_WIDTH), w_out[:DN_W].astype(BF16), w_out[DN_W:].astype(BF16))


def _route(h, rw_ref, rb_ref):
    logits = _dot_nt(rw_ref[...], h, precision=HIGHEST)
    scores = jax.nn.sigmoid(logits)
    choice = scores + rb_ref[...]
    rows = [choice[e:e + 1, :] for e in range(N_EXPERTS)]
    neg_inf = jnp.float32(-jnp.inf)
    gs = []
    for g in range(N_EXPERT_GROUPS):
        r = rows[g * EXPERTS_PER_GROUP:(g + 1) * EXPERTS_PER_GROUP]
        best = None
        for a in range(EXPERTS_PER_GROUP):
            for b in range(a + 1, EXPERTS_PER_GROUP):
                s = r[a] + r[b]
                best = s if best is None else jnp.maximum(best, s)
        gs.append(best)
    best_val = gs[0]
    best_g = jnp.zeros_like(best_val, dtype=jnp.int32)
    for g in range(1, N_EXPERT_GROUPS):
        better = gs[g] > best_val
        best_val = jnp.where(better, gs[g], best_val)
        best_g = jnp.where(better, g, best_g)
    masked = [jnp.where(best_g == e // EXPERTS_PER_GROUP, rows[e], neg_inf) for e in range(N_EXPERTS)]
    m1 = masked[0]
    for e in range(1, N_EXPERTS):
        m1 = jnp.maximum(m1, masked[e])
    i1 = jnp.full_like(best_g, N_EXPERTS)
    for e in reversed(range(N_EXPERTS)):
        i1 = jnp.where(masked[e] == m1, e, i1)
    rest = [jnp.where(i1 == e, neg_inf, masked[e]) for e in range(N_EXPERTS)]
    m2 = rest[0]
    for e in range(1, N_EXPERTS):
        m2 = jnp.maximum(m2, rest[e])
    i2 = jnp.full_like(best_g, N_EXPERTS)
    for e in reversed(range(N_EXPERTS)):
        i2 = jnp.where(rest[e] == m2, e, i2)
    eidx = lax.broadcasted_iota(jnp.int32, scores.shape, 0)
    sel1 = eidx == i1
    sel2 = eidx == i2
    w1 = jnp.sum(jnp.where(sel1, scores, 0.0), axis=0, keepdims=True)
    w2 = jnp.sum(jnp.where(sel2, scores, 0.0), axis=0, keepdims=True)
    inv = 1.0 / (w1 + w2)
    return jnp.where(sel1, w1 * inv, 0.0) + jnp.where(sel2, w2 * inv, 0.0), best_g, i1, i2


def _pack_planes(x):
    planes = []
    for c in range(N_PLANES):
        lo = x[:, 2 * c * LANES:(2 * c + 1) * LANES]
        hi = x[:, (2 * c + 1) * LANES:(2 * c + 2) * LANES]
        planes.append(lax.bitcast_convert_type(pltpu.pack_elementwise([lo, hi], packed_dtype=BF16), F32))
    return planes


def _unpack_planes(planes):
    parts = []
    for p in planes:
        w = lax.bitcast_convert_type(p, jnp.uint32)
        for idx in range(2):
            parts.append(pltpu.unpack_elementwise(w, index=idx, packed_dtype=BF16, unpacked_dtype=F32))
    return jnp.concatenate(parts, axis=1)


def _moe_route_body(x_ref, mod_ref, g_ref, rw_ref, rb_ref, tab_ref, cls_ref):
    h = _rms(x_ref[0]) * g_ref[...]
    h = h * (1.0 + mod_ref[0, 4:5, :]) + mod_ref[0, 3:4, :]
    for c, plane in enumerate(_pack_planes(h)):
        tab_ref[c] = plane
    gates_t, best_g, i1, i2 = _route(h, rw_ref, rb_ref)
    rows = []
    for k in range(EXPERTS_PER_GROUP):
        gk = jnp.zeros_like(gates_t[0:1, :])
        for g in range(N_EXPERT_GROUPS):
            e = g * EXPERTS_PER_GROUP + k
            gk = jnp.where(best_g == g, gates_t[e:e + 1, :], gk)
        rows.append(gk)
    rows.append(jnp.zeros((LANES - EXPERTS_PER_GROUP, gates_t.shape[1]), F32))
    tab_ref[N_PLANES] = jnp.transpose(jnp.concatenate(rows, axis=0))
    lo = jnp.minimum(i1, i2) % EXPERTS_PER_GROUP
    hi = jnp.maximum(i1, i2) % EXPERTS_PER_GROUP
    pair = lo * (7 - lo) // 2 + (hi - lo - 1)
    cls_ref[...] = jnp.broadcast_to(best_g * PAIRS_PER_GROUP + pair, cls_ref.shape)


def _moe_route(x, mod, norm_g, router_w, router_bias, tm):
    Bx, T, D = x.shape
    tm = _tile(T, tm)
    nt = T // tm
    N = Bx * T
    per_batch_mod = mod.shape[0] != 1
    full = lambda *shape: pl.BlockSpec(shape, lambda b, i: (0,) * len(shape))
    return pl.pallas_call(
        _moe_route_body,
        grid=(Bx, nt),
        in_specs=[pl.BlockSpec((1, tm, D), lambda b, i: (b, i, 0)),
                  pl.BlockSpec((1, 6, D), (lambda b, i: (b, 0, 0)) if per_batch_mod else (lambda b, i: (0, 0, 0))),
                  full(1, D), full(N_EXPERTS, D), full(N_EXPERTS, 1)],
        out_specs=[pl.BlockSpec((N_PLANES + 1, tm, LANES), lambda b, i: (0, b * nt + i, 0)),
                   pl.BlockSpec((8, tm), lambda b, i: (0, b * nt + i))],
        out_shape=[jax.ShapeDtypeStruct((N_PLANES + 1, N, LANES), F32),
                   jax.ShapeDtypeStruct((8, N), jnp.int32)],
        compiler_params=_cparams("parallel", "parallel"),
        name="moe_route",
    )(x, mod, norm_g.reshape(1, D), router_w.T, router_bias.reshape(N_EXPERTS, 1))


def _sc_move_rows(table, idx, n_out, scatter):
    n_planes = table.shape[0]
    n = idx.shape[0]
    ch = SC_CHUNK_ROWS if n % (2 * SC_WORKERS * SC_CHUNK_ROWS) == 0 else SC_CHUNK_ROWS // 2
    n_chunks = n // (SC_WORKERS * ch)
    assert n_chunks * SC_WORKERS * ch == n and n_chunks % 2 == 0
    assert n_out == n if not scatter else table.shape[1] == n
    items = [(dj, c) for dj in range(2) for c in range(n_planes)]
    mesh = plsc.VectorSubcoreMesh(core_axis_name="c", subcore_axis_name="s")

    @functools.partial(
        pl.kernel, mesh=mesh,
        out_type=jax.ShapeDtypeStruct((n_planes, n_out, LANES), table.dtype),
        scratch_types=[pltpu.VMEM((n_chunks, ch), jnp.int32),
                       pltpu.VMEM((2, ch, LANES), table.dtype),
                       pltpu.SemaphoreType.DMA((2,)), pltpu.SemaphoreType.DMA((2,))])
    def move_kernel(table_hbm, idx_hbm, out_hbm, idx_v, rows_v, isem, osem):
        wid = lax.axis_index("s") * SC_CORES + lax.axis_index("c")
        wbase = wid * (n_chunks * ch)
        pltpu.sync_copy(idx_hbm.at[pl.ds(wid * n_chunks, n_chunks)], idx_v)

        def load(j, c, slot):
            rows = idx_v.at[j] if not scatter else pl.ds(wbase + j * ch, ch)
            return pltpu.make_async_copy(table_hbm.at[c].at[rows], rows_v.at[slot], isem.at[slot])

        def store(j, c, slot):
            rows = idx_v.at[j] if scatter else pl.ds(wbase + j * ch, ch)
            return pltpu.make_async_copy(rows_v.at[slot], out_hbm.at[c].at[rows], osem.at[slot])

        load(0, 0, 0).start()

        @pl.loop(0, n_chunks, step=2)
        def _(j):
            for it, (dj, c) in enumerate(items):
                s = it % 2
                load(j + dj, c, s).wait()

                def refill(it=it, s=s):
                    if it == 0:
                        @pl.when(j > 0)
                        def _():
                            store(j - 1, n_planes - 1, 1 - s).wait()
                    else:
                        store(j + items[it - 1][0], items[it - 1][1], 1 - s).wait()
                    if it + 1 < len(items):
                        load(j + items[it + 1][0], items[it + 1][1], 1 - s).start()
                    else:
                        load(j + 2, 0, 1 - s).start()

                if it + 1 < len(items):
                    refill()
                else:
                    pl.when(j + 2 < n_chunks)(refill)
                store(j + dj, c, s).start()

        store(n_chunks - 2 + items[-2][0], items[-2][1], 0).wait()
        store(n_chunks - 1, n_planes - 1, 1).wait()

    return move_kernel(table, idx.reshape(SC_WORKERS * n_chunks, ch))


FFN_LIVE, FFN_FIRST, FFN_LAST = 1, 2, 4


def _moe_ffn_body(st_ref, se_ref, sk_ref, sf_ref, tn_ref, xs_ref, wg_ref, wu_ref, wd_ref, ys_ref, xb, gcol, acc):
    s = pl.program_id(0)
    flags = sf_ref[s]
    k = sk_ref[s]

    @pl.when((flags & FFN_FIRST) != 0)
    def _():
        keep = lax.broadcasted_iota(jnp.int32, (xs_ref.shape[1], LANES), 0) < tn_ref[st_ref[s]]
        xb[...] = _unpack_planes([jnp.where(keep, xs_ref[c], 0.0) for c in range(N_PLANES)]).astype(BF16)
        gcol[...] = jnp.where(keep, xs_ref[N_PLANES], 0.0)
        acc[...] = jnp.zeros_like(acc)

    @pl.when((flags & FFN_LIVE) != 0)
    def _():
        hb = xb[...]
        act = _silu(_dot(hb, wg_ref[0].astype(BF16))) * _dot(hb, wu_ref[0].astype(BF16))
        lane = lax.broadcasted_iota(jnp.int32, gcol.shape, 1)
        ge = jnp.sum(jnp.where(lane == k, gcol[...], 0.0), axis=1, keepdims=True)
        acc[...] += _dot((act * ge).astype(BF16), wd_ref[0].astype(BF16))

    @pl.when((flags & FFN_LAST) != 0)
    def _():
        for c, plane in enumerate(_pack_planes(acc[...])):
            ys_ref[c] = plane


def _moe_ffn(step_tile, step_expert, step_k, step_flags, tile_rows, xs, wg, wu, wd, tm):
    P = xs.shape[1]
    D = 2 * N_PLANES * LANES
    n_steps = step_tile.shape[0]

    def wspec(shape):
        return pl.BlockSpec(shape, lambda s, st, se, sk, sf, tn: (se[s], 0, 0))

    return pl.pallas_call(
        _moe_ffn_body,
        grid_spec=pltpu.PrefetchScalarGridSpec(
            num_scalar_prefetch=5,
            grid=(n_steps,),
            in_specs=[pl.BlockSpec((N_PLANES + 1, tm, LANES), lambda s, st, se, sk, sf, tn: (0, st[s], 0)),
                      wspec((1, D, D_EXPERT)), wspec((1, D, D_EXPERT)), wspec((1, D_EXPERT, D))],
            out_specs=pl.BlockSpec((N_PLANES, tm, LANES), lambda s, st, se, sk, sf, tn: (0, st[s], 0)),
            scratch_shapes=[pltpu.VMEM((tm, D), BF16), pltpu.VMEM((tm, LANES), F32), pltpu.VMEM((tm, D), F32)]),
        out_shape=jax.ShapeDtypeStruct((N_PLANES, P, LANES), F32),
        compiler_params=_cparams("arbitrary"),
        name="moe_ffn",
    )(step_tile, step_expert, step_k, step_flags, tile_rows, xs, wg, wu, wd)


def _moe_combine_body(x_ref, mod_ref, y_ref, fg_ref, out_ref, *, final_norm):
    y = _unpack_planes([y_ref[c] for c in range(N_PLANES)])
    r = x_ref[0] + mod_ref[0, 5:6, :] * y
    if final_norm:
        r = _rms(r) * fg_ref[...]
    out_ref[0] = r


def _moe_combine(x, mod, y_tok, final_g, tm):
    Bx, T, D = x.shape
    tm = _tile(T, tm)
    nt = T // tm
    per_batch_mod = mod.shape[0] != 1
    final_norm = final_g is not None
    fg = (final_g if final_norm else jnp.ones((D,), F32)).reshape(1, D)
    return pl.pallas_call(
        functools.partial(_moe_combine_body, final_norm=final_norm),
        grid=(Bx, nt),
        in_specs=[pl.BlockSpec((1, tm, D), lambda b, i: (b, i, 0)),
                  pl.BlockSpec((1, 6, D), (lambda b, i: (b, 0, 0)) if per_batch_mod else (lambda b, i: (0, 0, 0))),
                  pl.BlockSpec((N_PLANES, tm, LANES), lambda b, i: (0, b * nt + i, 0)),
                  pl.BlockSpec((1, D), lambda b, i: (0, 0))],
        out_specs=pl.BlockSpec((1, tm, D), lambda b, i: (b, i, 0)),
        out_shape=jax.ShapeDtypeStruct((Bx, T, D), F32),
        compiler_params=_cparams("parallel", "parallel"),
        name="moe_combine",
    )(x, mod, y_tok, fg)


def _moe(x, mod, norm_g, router_w, router_bias, wg, wu, wd, expert_base, final_g):
    Bx, T, D = x.shape
    N = Bx * T
    tm = MOE_TILE
    G, K, NP = N_EXPERT_GROUPS, EXPERTS_PER_GROUP, PAIRS_PER_GROUP
    table, cls_rows = _moe_route(x, mod, norm_g, router_w, router_bias, 512)
    cls = cls_rows[0]
    onehot = (cls[:, None] == jnp.arange(G * NP, dtype=jnp.int32)[None, :]).astype(jnp.int32)
    csum = jnp.cumsum(onehot, axis=0)
    c_count = csum[-1]
    g_count = c_count.reshape(G, NP).sum(axis=1)
    g_padded = (g_count + tm - 1) // tm * tm
    g_end = jnp.cumsum(g_padded)
    g_start = g_end - g_padded
    in_group = jnp.cumsum(c_count.reshape(G, NP), axis=1) - c_count.reshape(G, NP)
    c_start = (g_start[:, None] + in_group).reshape(G * NP)
    rank = jnp.sum(csum * onehot, axis=1) - 1
    pos = (jnp.sum(c_start[None, :] * onehot, axis=1) + rank).astype(jnp.int32)
    P = N + G * tm
    nt = P // tm
    tile_start = jnp.arange(nt, dtype=jnp.int32) * tm
    tile_group = jnp.minimum(jnp.sum(tile_start[:, None] >= g_end[None, :], axis=1), G - 1)
    filled = (g_start + g_count)[tile_group]
    tile_rows = jnp.clip(filled - tile_start, 0, tm).astype(jnp.int32)
    overlap = ((c_start[None, :] < tile_start[:, None] + tm) & (c_start + c_count > tile_start[:, None])
               & (c_count[None, :] > 0)).astype(jnp.int32)
    pairs = [(a, b) for a in range(K) for b in range(a + 1, K)]
    member = jnp.array([[int(k in pairs[c % NP]) for k in range(K)] for c in range(G * NP)], jnp.int32)
    used = (jnp.dot(overlap, member) > 0).astype(jnp.int32)
    seen = jnp.cumsum(used, axis=1)
    first = used * (seen == 1)
    last = used * (seen == seen[:, -1:])
    n_steps = nt * K
    order = jnp.argsort(1 - used.reshape(n_steps), stable=True).astype(jnp.int32)
    n_used = jnp.sum(used)
    live = jnp.arange(n_steps, dtype=jnp.int32) < n_used
    src = jnp.where(live, order, order[jnp.maximum(n_used - 1, 0)])
    step_tile = src // K
    step_k = src % K
    step_expert = (expert_base + tile_group[step_tile] * K + step_k).astype(jnp.int32)
    step_flags = jnp.where(live, FFN_LIVE + FFN_FIRST * first.reshape(n_steps)[src]
                           + FFN_LAST * last.reshape(n_steps)[src], 0).astype(jnp.int32)
    xs = _sc_move_rows(table, pos, P, scatter=True)
    ys = _moe_ffn(step_tile, step_expert, step_k, step_flags, tile_rows, xs, wg, wu, wd, tm)
    y_tok = _sc_move_rows(ys, pos, N, scatter=False)
    return _moe_combine(x, mod, y_tok, final_g, 512)


def _mla_q_body(cq_ref, g_ref, wn_ref, wa_ref, wb_ref, cos_ref, sin_ref, q_ref):
    hb = (_rms(cq_ref[0]) * g_ref[...]).astype(BF16)
    qn = _dot(hb, wn_ref[...]) * MLA_Q_SCALE
    ra = _dot(hb, wa_ref[...])
    rb = _dot(hb, wb_ref[...])
    cos = cos_ref[...]
    sin = sin_ref[...]
    lane = lax.broadcasted_iota(jnp.int32, cos.shape, 1)
    for h in range(MLA_HEADS):
        p = h // 2
        rot = (ra[:, p * LANES:(p + 1) * LANES] * cos + rb[:, p * LANES:(p + 1) * LANES] * sin) * MLA_Q_SCALE
        mine = (lane < MLA_ROPE) if h % 2 == 0 else (lane >= MLA_ROPE)
        q_ref[0, :, h * MLA_QK:h * MLA_QK + MLA_NOPE] = qn[:, h * MLA_NOPE:(h + 1) * MLA_NOPE].astype(BF16)
        q_ref[0, :, h * MLA_QK + MLA_NOPE:(h + 1) * MLA_QK] = jnp.where(mine, rot, 0.0).astype(BF16)


def _mla_q(cq, g, wn, wa, wb, cos_t, sin_t, tm):
    B, T, R = cq.shape
    tm = _tile(T, tm)
    full = lambda *shape: pl.BlockSpec(shape, lambda b, i: (0,) * len(shape))
    return pl.pallas_call(
        _mla_q_body,
        grid=(B, T // tm),
        in_specs=[pl.BlockSpec((1, tm, R), lambda b, i: (b, i, 0)), full(1, R),
                  full(R, MLA_HEADS * MLA_NOPE), full(R, MLA_HEADS * MLA_ROPE), full(R, MLA_HEADS * MLA_ROPE),
                  pl.BlockSpec((tm, LANES), lambda b, i: (i, 0)),
                  pl.BlockSpec((tm, LANES), lambda b, i: (i, 0))],
        out_specs=pl.BlockSpec((1, tm, MLA_HEADS * MLA_QK), lambda b, i: (b, i, 0)),
        out_shape=jax.ShapeDtypeStruct((B, T, MLA_HEADS * MLA_QK), BF16),
        compiler_params=_cparams("parallel", "parallel"),
        name="mla_q",
    )(cq, g.reshape(1, R), wn, wa, wb, cos_t, sin_t)


def _mla_kv_body(ckv_ref, ka_ref, kb_ref, g_ref, wk_ref, wv_ref, cos_ref, sin_ref, k_ref, vt_ref):
    hb = (_rms(ckv_ref[0]) * g_ref[...]).astype(BF16)
    kn = _dot(hb, wk_ref[...]).astype(BF16)
    vt_ref[0] = _dot_nt(wv_ref[...], hb).astype(BF16)
    kr =(ka_ref[0] * cos_ref[...] + kb_ref[0] * sin_ref[...]).astype(BF16)
    for h in range(MLA_HEADS):
        k_ref[0, :, h * MLA_QK:h * MLA_QK + MLA_NOPE] = kn[:, h * MLA_NOPE:(h + 1) * MLA_NOPE]
        k_ref[0, :, h * MLA_QK + MLA_NOPE:(h + 1) * MLA_QK] = kr


def _mla_kv(ckv, kra, krb, g, wk, wv, cos_t, sin_t, tm):
    B, T, R = ckv.shape
    tm = _tile(T, tm)
    W = MLA_HEADS * MLA_NOPE
    full = lambda *shape: pl.BlockSpec(shape, lambda b, i: (0,) * len(shape))
    tok = lambda n: pl.BlockSpec((1, tm, n), lambda b, i: (b, i, 0))
    return pl.pallas_call(
        _mla_kv_body,
        grid=(B, T // tm),
        in_specs=[tok(R), tok(LANES), tok(LANES), full(1, R), full(R, W), full(W, R),
                  pl.BlockSpec((tm, LANES), lambda b, i: (i, 0)),
                  pl.BlockSpec((tm, LANES), lambda b, i: (i, 0))],
        out_specs=[tok(MLA_HEADS * MLA_QK), pl.BlockSpec((1, W, tm), lambda b, i: (b, 0, i))],
        out_shape=[jax.ShapeDtypeStruct((B, T, MLA_HEADS * MLA_QK), BF16), jax.ShapeDtypeStruct((B, W, T), BF16)],
        compiler_params=_cparams("parallel", "parallel"),
        name="mla_kv",
    )(ckv, kra, krb, g.reshape(1, R), wk, wv, cos_t, sin_t)


def _mla_attn_body(q_ref, k_ref, v_ref, o_ref):
    for h in range(MLA_HEADS):
        qk = slice(h * MLA_QK, (h + 1) * MLA_QK)
        sl = slice(h * MLA_V, (h + 1) * MLA_V)
        s = _dot_nt(q_ref[0, :, qk], k_ref[0, :, qk])
        m = jnp.max(s, axis=-1, keepdims=True)
        e = jnp.exp2(s - m)
        l = jnp.sum(e, axis=-1, keepdims=True)
        ot = _dot_nt(v_ref[0, sl, :], e.astype(BF16))
        o_ref[0, :, sl] = (jnp.transpose(ot) / l).astype(o_ref.dtype)


def _mla_attn(q, k, v, tq):
    B, T, WQ = q.shape
    W, Tk = v.shape[1], v.shape[2]
    tq = _tile(T, tq)
    return pl.pallas_call(
        _mla_attn_body,
        grid=(B, T // tq),
        in_specs=[pl.BlockSpec((1, tq, WQ), lambda b, i: (b, i, 0)),
                  pl.BlockSpec((1, Tk, WQ), lambda b, i: (b, 0, 0)),
                  pl.BlockSpec((1, W, Tk), lambda b, i: (b, 0, 0))],
        out_specs=pl.BlockSpec((1, tq, W), lambda b, i: (b, i, 0)),
        out_shape=jax.ShapeDtypeStruct((B, T, W), BF16),
        compiler_params=_cparams("parallel", "arbitrary"),
        name="mla_attn",
    )(q, k, v)


def _proj_res_body(x_ref, mod_ref, a_ref, w_ref, out_ref):
    out_ref[0] = x_ref[0] + mod_ref[0, 2:3, :] * _dot(a_ref[0], w_ref[...])


def _proj_res(x, mod, a, w, tm):
    B, T, D = x.shape
    K = a.shape[2]
    tm = _tile(T, tm)
    per_batch_mod = mod.shape[0] != 1
    return pl.pallas_call(
        _proj_res_body,
        grid=(B, T // tm),
        in_specs=[pl.BlockSpec((1, tm, D), lambda b, i: (b, i, 0)),
                  pl.BlockSpec((1, 6, D), (lambda b, i: (b, 0, 0)) if per_batch_mod else (lambda b, i: (0, 0, 0))),
                  pl.BlockSpec((1, tm, K), lambda b, i: (b, i, 0)),
                  pl.BlockSpec((K, D), lambda b, i: (0, 0))],
        out_specs=pl.BlockSpec((1, tm, D), lambda b, i: (b, i, 0)),
        out_shape=jax.ShapeDtypeStruct((B, T, D), F32),
        compiler_params=_cparams("parallel", "parallel"),
        name="proj_res",
    )(x, mod, a, w)


def _rope_tables(T, n_ctx):
    rows = T // GRID_W
    row = jnp.repeat(jnp.arange(rows, dtype=F32), GRID_W)
    col = jnp.tile(jnp.arange(GRID_W, dtype=F32), rows)
    n_freq = MLA_ROPE // 4
    inv = ROPE_THETA ** (-jnp.arange(n_freq, dtype=F32) / n_freq)
    ang = jnp.concatenate([row[:, None] * inv, col[:, None] * inv], axis=-1)
    cos, sin = jnp.cos(ang), jnp.sin(ang)
    cos_t = jnp.concatenate([cos, cos, cos, cos], axis=-1)
    sin_t = jnp.concatenate([-sin, sin, -sin, sin], axis=-1)
    cos_k = jnp.concatenate([jnp.ones((n_ctx, LANES), F32), cos_t], axis=0)
    sin_k = jnp.concatenate([jnp.zeros((n_ctx, LANES), F32), sin_t], axis=0)
    return cos_t, sin_t, cos_k, sin_k


def _layer_ab(x, ctx, mod_l, mod_c, norm1_g, w_in, conv_w, a_log, dt_bias, dn_norm_g, A_re, A_im, log_dt,
              B_re, B_im, C_re, C_im, D_skip, glu_w, glu_b, w_out):
    B, T, D = x.shape
    Tc = ctx.shape[1]
    q0, k0, v0, z0, a0, b0, u0 = 0, 512, 1024, 1536, 2048, 2056, 2064
    w_qkv = w_in[:, q0:z0].astype(BF16)
    w_z = w_in[:, z0:a0].astype(BF16)
    w_ab = jnp.zeros((D, LANES), F32).at[:, :16].set(w_in[:, a0:u0]).astype(BF16)
    w_u = w_in[:, u0:].astype(BF16)
    w_abt = w_in[:, a0:u0].T.astype(BF16)
    ws = [w_qkv, w_z, w_ab, w_abt, w_u]
    dts = [F32] * 5
    kinds = ["n", "n", "n", "t", "n"]
    dn_group = 4

    lam, bd = _s5_params(A_re, A_im, log_dt, B_re, B_im)
    w_drive, w_read = _s5_block_weights(bd, C_re, C_im)

    streams = []
    dn_state = jnp.zeros((2, B, DN_HEADS, DN_DK, DN_DV), F32)
    s5_state = jnp.zeros((2, 2, B, S5_NSTATE), F32)
    for xs, mod in ((ctx, mod_c), (x, mod_l)):
        qkv, z, ab, abt, u = _modmm(xs, mod, norm1_g, ws, dts, kinds, 0, 512)
        qkv = _dn_prep(qkv, conv_w, 512)
        uw, qk, gl = _dn_chunk(qkv, ab, abt, a_log, dt_bias, dn_group)
        o_dn, dn_state = _dn_rec(uw, qk, gl, dn_state, 8, dn_group)
        y_s5, s5_state = _s5_scan(u, w_drive, w_read, lam, s5_state, 32)
        streams.append(_ab_out(xs, mod, o_dn, z, u, y_s5, dn_norm_g, D_skip.reshape(-1), glu_w, glu_b,
                               w_out, 512))
    return streams[1], streams[0]


def _layer_mla(x, ctx, mod_l, mod_c, norm1_g, w_in, q_norm_g, w_q_up, kv_norm_g, w_kv_up, w_out, need_ctx):
    assert not need_ctx, "context attention output is only needed when a later layer follows"
    B, T, D = x.shape
    n_ctx = ctx.shape[1]
    qr, kvr = MLA_Q_RANK, MLA_KV_RANK
    half = MLA_ROPE // 2
    w_cq = w_in[:, :qr].astype(BF16)
    w_ckv = w_in[:, qr:qr + kvr].astype(BF16)
    wk1 = w_in[:, qr + kvr:qr + kvr + half]
    wk2 = w_in[:, qr + kvr + half:]
    w_ka = jnp.concatenate([wk1, wk2, wk1, wk2], axis=1).astype(BF16)
    w_kb = jnp.concatenate([wk2, wk1, wk2, wk1], axis=1).astype(BF16)
    ws = [w_cq, w_ckv, w_ka, w_kb]
    cq_l, ckv_l, ka_l, kb_l = _modmm(x, mod_l, norm1_g, ws, [F32] * 4, ["n"] * 4, 0, 512)
    _, ckv_c, ka_c, kb_c = _modmm(ctx, mod_c, norm1_g, ws, [F32] * 4, ["n"] * 4, 0, 512)

    wq = w_q_up.reshape(qr, MLA_HEADS, MLA_NOPE + MLA_ROPE)
    wq_n = wq[:, :, :MLA_NOPE].reshape(qr, MLA_HEADS * MLA_NOPE).astype(BF16)
    x1 = wq[:, :, MLA_NOPE:MLA_NOPE + half]
    x2 = wq[:, :, MLA_NOPE + half:]
    wq_a = jnp.concatenate([x1, x2], axis=2).reshape(qr, MLA_HEADS * MLA_ROPE).astype(BF16)
    wq_b = jnp.concatenate([x2, x1], axis=2).reshape(qr, MLA_HEADS * MLA_ROPE).astype(BF16)
    wkv = w_kv_up.reshape(kvr, MLA_HEADS, MLA_NOPE + MLA_V)
    wk_n = wkv[:, :, :MLA_NOPE].reshape(kvr, MLA_HEADS * MLA_NOPE).astype(BF16)
    wv_t = wkv[:, :, MLA_NOPE:].reshape(kvr, MLA_HEADS * MLA_V).T.astype(BF16)

    cos_t, sin_t, cos_k, sin_k = _rope_tables(T, n_ctx)
    q = _mla_q(cq_l, q_norm_g, wq_n, wq_a, wq_b, cos_t, sin_t, 512)
    ckv = jnp.concatenate([ckv_c, ckv_l], axis=1)
    ka = jnp.concatenate([ka_c, ka_l], axis=1)
    kb = jnp.concatenate([kb_c, kb_l], axis=1)
    k, vt = _mla_kv(ckv, ka, kb, kv_norm_g, wk_n, wv_t, cos_k, sin_k, 256)
    o = _mla_attn(q, k, vt, 256)
    return _proj_res(x, mod_l, o, w_out.astype(BF16), 512)


def kernel(x, c, ctx, c_ctx, ada_w, ada_b, norm1_g, norm2_g, ab_w_in, dn_conv_w, dn_A_log, dn_dt_bias, dn_norm_g, s5_A_re, s5_A_im, s5_log_dt, s5_B_re, s5_B_im, s5_C_re, s5_C_im, s5_D, s5_glu_w, s5_glu_b, ab_w_out, mla_w_in, mla_q_norm_g, mla_w_q_up, mla_kv_norm_g, mla_w_kv_up, mla_w_out, router_w, router_bias, moe_w_gate, moe_w_up, moe_w_down, final_norm_g):
    B, T, D = x.shape
    n_ctx = ctx.shape[1]
    depth = ada_w.shape[0]
    n_cond = -(-(B + 1) // 8) * 8
    cond = jnp.zeros((n_cond, D), F32).at[:B].set(c).at[B].set(c_ctx)
    mods = _adaln_all(cond, ada_w, ada_b).reshape(depth, n_cond, 6, D)
    n_exp = moe_w_gate.shape[1]
    wg = moe_w_gate.reshape((depth * n_exp,) + moe_w_gate.shape[2:])
    wu = moe_w_up.reshape((depth * n_exp,) + moe_w_up.shape[2:])
    wd = moe_w_down.reshape((depth * n_exp,) + moe_w_down.shape[2:])
    for i in range(depth):
        last = i == depth - 1
        j = i // 2
        mod_l = mods[i, :B]
        mod_c = mods[i, B:B + 1]
        if i % 2 == 0:
            x, ctx_new = _layer_ab(x, ctx, mod_l, mod_c, norm1_g[i], ab_w_in[j], dn_conv_w[j], dn_A_log[j],
                                   dn_dt_bias[j], dn_norm_g[j], s5_A_re[j], s5_A_im[j], s5_log_dt[j],
                                   s5_B_re[j], s5_B_im[j], s5_C_re[j], s5_C_im[j], s5_D[j], s5_glu_w[j],
                                   s5_glu_b[j], ab_w_out[j])
        else:
            x = _layer_mla(x, ctx, mod_l, mod_c, norm1_g[i], mla_w_in[j], mla_q_norm_g[j], mla_w_q_up[j],
                           mla_kv_norm_g[j], mla_w_kv_up[j], mla_w_out[j], not last)
            ctx_new = None
        x = _moe(x, mod_l, norm2_g[i], router_w, router_bias, wg, wu, wd, i * n_exp,
                 final_norm_g if last else None)
        if not last:
            ctx_flat = _moe(ctx_new.reshape(1, B * n_ctx, D), mod_c, norm2_g[i], router_w, router_bias,
                            wg, wu, wd, i * n_exp, None)
            ctx = ctx_flat.reshape(B, n_ctx, D)
    return x
```

```python
import functools
import math

import jax
import jax.numpy as jnp
from jax import lax
from jax.experimental import pallas as pl
from jax.experimental.pallas import tpu as pltpu
from jax.experimental.pallas import tpu_sc as plsc

F32 = jnp.float32
BF16 = jnp.bfloat16
HIGHEST = lax.Precision.HIGHEST

NORM_EPS = 1e-6
GRID_W = 64
ROPE_THETA = 10000.0

DN_HEADS = 4
DN_DK = 128
DN_DV = 128
DN_CONV = 5
DN_CHUNK = 64
DN_W = DN_HEADS * DN_DK

S5_WIDTH = 512
S5_GROUP = 16
S5_GROUPS = 32
S5_STATE = 64
S5_NSTATE = S5_GROUPS * S5_STATE
S5_GBLK = 8
S5_NBLK = S5_GROUPS // S5_GBLK
S5_SBLK = S5_GBLK * S5_STATE

MLA_HEADS = 8
MLA_Q_RANK = 384
MLA_KV_RANK = 256
MLA_NOPE = 128
MLA_ROPE = 64
MLA_V = 128
MLA_Q_SCALE = (MLA_NOPE + MLA_ROPE) ** -0.5 * math.log2(math.e)

N_EXPERTS = 16
N_EXPERT_GROUPS = 4
EXPERTS_PER_GROUP = 4
D_EXPERT = 512
PAIRS_PER_GROUP = EXPERTS_PER_GROUP * (EXPERTS_PER_GROUP - 1) // 2
MOE_TILE = 1024

N_PLANES = 4
SC_CORES = 2
SC_WORKERS = 32
SC_CHUNK_ROWS = 128

LANES = 128
MLA_QK = MLA_NOPE + LANES
VMEM_LIMIT_BYTES = 56 * 1024 * 1024


def _tile(n, pref):
    t = min(pref, n)
    while n % t or t % 8:
        t -= 1
    return t


def _cparams(*sem):
    return pltpu.CompilerParams(dimension_semantics=sem, vmem_limit_bytes=VMEM_LIMIT_BYTES)


def _silu(x):
    return x * jax.nn.sigmoid(x)


def _softplus(x):
    return jnp.maximum(x, 0.0) + jnp.log(1.0 + jnp.exp(-jnp.abs(x)))


def _gelu_tanh(x):
    return 0.5 * x * (1.0 + jnp.tanh(math.sqrt(2.0 / math.pi) * (x + 0.044715 * (x * x * x))))


def _rms(x):
    return x * lax.rsqrt(jnp.mean(x * x, axis=-1, keepdims=True) + NORM_EPS)


def _dot(a, b):
    return jnp.dot(a, b, preferred_element_type=F32)


def _dot_nt(a, b, precision=None):
    return lax.dot_general(a, b, (((1,), (1,)), ((), ())), preferred_element_type=F32,
                           precision=precision)


def _dot_tn(a, b):
    return lax.dot_general(a, b, (((0,), (0,)), ((), ())), preferred_element_type=F32)


def _ada_body(c_ref, w_ref, b_ref, o_ref):
    c = c_ref[...]
    o_ref[0] = _dot(_silu(c).astype(BF16), w_ref[0].astype(BF16)) + b_ref[0]


def _adaln_all(cond, ada_w, ada_b):
    L, D, D6 = ada_w.shape
    R = cond.shape[0]
    tn = 1536
    return pl.pallas_call(
        _ada_body,
        grid=(L, D6 // tn),
        in_specs=[pl.BlockSpec((R, D), lambda l, j: (0, 0)),
                  pl.BlockSpec((1, D, tn), lambda l, j: (l, 0, j)),
                  pl.BlockSpec((1, 1, tn), lambda l, j: (l, 0, j))],
        out_specs=pl.BlockSpec((1, R, tn), lambda l, j: (l, 0, j)),
        out_shape=jax.ShapeDtypeStruct((L, R, D6), F32),
        compiler_params=_cparams("parallel", "parallel"),
        name="adaln",
    )(cond, ada_w, ada_b.reshape(L, 1, D6))


def _modmm_body(x_ref, mod_ref, g_ref, *refs, kinds, shift_row):
    n_out = len(kinds)
    w_refs, o_refs = refs[:n_out], refs[n_out:]
    h = _rms(x_ref[0]) * g_ref[...]
    h = h * (1.0 + mod_ref[0, shift_row + 1:shift_row + 2, :]) + mod_ref[0, shift_row:shift_row + 1, :]
    hb = h.astype(BF16)
    for w_ref, o_ref, kind in zip(w_refs, o_refs, kinds):
        if kind == "t":
            o_ref[0] = _dot_nt(w_ref[...], hb).astype(o_ref.dtype)
        else:
            o_ref[0] = _dot(hb, w_ref[...]).astype(o_ref.dtype)


def _modmm(x, mod, g, ws, out_dtypes, kinds, shift_row, tm):
    Bx, T, D = x.shape
    tm = _tile(T, tm)
    per_batch_mod = mod.shape[0] != 1
    in_specs = [pl.BlockSpec((1, tm, D), lambda b, i: (b, i, 0)),
                pl.BlockSpec((1, 6, D), (lambda b, i: (b, 0, 0)) if per_batch_mod else (lambda b, i: (0, 0, 0))),
                pl.BlockSpec((1, D), lambda b, i: (0, 0))]
    out_specs, out_shape = [], []
    for w, dt, kind in zip(ws, out_dtypes, kinds):
        in_specs.append(pl.BlockSpec(w.shape, lambda b, i: (0, 0)))
        if kind == "t":
            n = w.shape[0]
            out_specs.append(pl.BlockSpec((1, n, tm), lambda b, i: (b, 0, i)))
            out_shape.append(jax.ShapeDtypeStruct((Bx, n, T), dt))
        else:
            n = w.shape[1]
            out_specs.append(pl.BlockSpec((1, tm, n), lambda b, i: (b, i, 0)))
            out_shape.append(jax.ShapeDtypeStruct((Bx, T, n), dt))
    return pl.pallas_call(
        functools.partial(_modmm_body, kinds=tuple(kinds), shift_row=shift_row),
        grid=(Bx, T // tm),
        in_specs=in_specs, out_specs=out_specs, out_shape=out_shape,
        compiler_params=_cparams("parallel", "parallel"),
        name="modmm",
    )(x, mod, g.reshape(1, D), *ws)


def _dn_prep_body(x_ref, xp_ref, xn_ref, w_ref, o_ref, buf, *, tm, nt):
    i = pl.program_id(1)
    j = pl.program_id(2)
    buf[0:8, :] = jnp.where(i == 0, 0.0, xp_ref[0])
    buf[8:8 + tm, :] = x_ref[0]
    buf[8 + tm:16 + tm, :] = jnp.where(i == nt - 1, 0.0, xn_ref[0])
    acc = buf[pl.ds(8 - DN_CONV // 2, tm), :] * w_ref[0:1, :]
    for kk in range(1, DN_CONV):
        acc = acc + buf[pl.ds(8 - DN_CONV // 2 + kk, tm), :] * w_ref[kk:kk + 1, :]
    y = _silu(acc)
    q_scale = jnp.where(j == 0, DN_DK ** -0.5, 1.0)
    for h in range(DN_HEADS):
        yh = y[:, h * DN_DK:(h + 1) * DN_DK]
        r = lax.rsqrt(jnp.sum(yh * yh, axis=-1, keepdims=True) + NORM_EPS) * q_scale
        o_ref[0, :, h * DN_DK:(h + 1) * DN_DK] = yh * jnp.where(j == 2, 1.0, r)


def _dn_prep(qkv, conv_w, tm):
    B, T, W3 = qkv.shape
    tm = _tile(T, tm)
    nt = T // tm
    r8 = tm // 8
    return pl.pallas_call(
        functools.partial(_dn_prep_body, tm=tm, nt=nt),
        grid=(B, nt, 3),
        in_specs=[pl.BlockSpec((1, tm, DN_W), lambda b, i, j: (b, i, j)),
                  pl.BlockSpec((1, 8, DN_W), lambda b, i, j: (b, jnp.maximum(i * r8 - 1, 0), j)),
                  pl.BlockSpec((1, 8, DN_W), lambda b, i, j: (b, jnp.minimum((i + 1) * r8, T // 8 - 1), j)),
                  pl.BlockSpec((DN_CONV, DN_W), lambda b, i, j: (0, j))],
        out_specs=pl.BlockSpec((1, tm, DN_W), lambda b, i, j: (b, i, j)),
        out_shape=jax.ShapeDtypeStruct((B, T, W3), F32),
        scratch_shapes=[pltpu.VMEM((tm + 16, DN_W), F32)],
        compiler_params=_cparams("parallel", "parallel", "parallel"),
        name="dn_prep",
    )(qkv, qkv, qkv, conv_w)


def _heads_to_lanes(cols, width):
    return jnp.concatenate([jnp.broadcast_to(c, (c.shape[0], width)) for c in cols], axis=1)


def _block_diag(x, nblk):
    C, W = x.shape
    w = W // nblk
    t = jnp.concatenate([x] * nblk, axis=0)
    rb = lax.broadcasted_iota(jnp.int32, t.shape, 0) // C
    cb = lax.broadcasted_iota(jnp.int32, t.shape, 1) // w
    return jnp.where(rb == cb, t, jnp.zeros_like(t))


def _dn_chunk_body(q_ref, k_ref, v_ref, ab_ref, abt_ref, arow_ref, drow_ref, acol_ref, dcol_ref,
                   uw_ref, qk_ref, gl_ref, *, G):
    C, H = DN_CHUNK, DN_HEADS
    Tg = G * C
    gl_ref[...] = jnp.zeros_like(gl_ref)
    ab = ab_ref[0]
    abt = abt_ref[0]
    g_all = -jnp.exp(arow_ref[...]) * _softplus(ab + drow_ref[...])
    gt_all = -jnp.exp(acol_ref[...]) * _softplus(abt + dcol_ref[...])
    beta_all = jax.nn.sigmoid(ab)
    pos_s = lax.broadcasted_iota(jnp.int32, (Tg, LANES), 0) % C
    pos_l = lax.broadcasted_iota(jnp.int32, (2 * H * 2, Tg), 1) % C
    gc_all, gct_all = g_all, gt_all
    s = 1
    while s < C:
        gc_all = gc_all + jnp.where(pos_s >= s, pltpu.roll(gc_all, s, 0), 0.0)
        gct_all = gct_all + jnp.where(pos_l >= s, pltpu.roll(gct_all, s, 1), 0.0)
        s *= 2

    ri = lax.broadcasted_iota(jnp.int32, (C, H * C), 0)
    cj = lax.broadcasted_iota(jnp.int32, (C, H * C), 1) % C
    eye_side = (ri == cj).astype(F32)

    chains = []
    for ci in range(G):
        rows = slice(ci * C, (ci + 1) * C)
        q = q_ref[0, rows, :]
        k = k_ref[0, rows, :]
        v = v_ref[0, rows, :]
        kbd = _block_diag(k.astype(BF16), H)
        g_c, gc_f, beta_c = g_all[rows], gc_all[rows], beta_all[rows]
        gt_c, gct_f = gt_all[:, rows], gct_all[:, rows]
        gtot = gc_f[C - 1:C, :]
        gtot_t = gct_f[:, C - 1:C]
        for d in range(2):
            if d == 0:
                gc, gct = gc_f, gct_f
                incl, strict = ri >= cj, ri > cj
            else:
                gc, gct = gtot - gc_f + g_c, gtot_t - gct_f + gt_c
                incl, strict = ri <= cj, ri < cj
            lanes = [d * H + h for h in range(H)]
            gcol = [gc[:, l:l + 1] for l in lanes]
            diff = _heads_to_lanes(gcol, C) - jnp.concatenate([gct[l:l + 1, :] for l in lanes], axis=1)
            decay = jnp.where(incl, jnp.exp(jnp.where(incl, diff, 0.0)), 0.0)
            beta_b = _heads_to_lanes([beta_c[:, 2 * H + l:2 * H + l + 1] for l in lanes], DN_DK)
            egc_b = _heads_to_lanes([jnp.exp(c) for c in gcol], DN_DK)
            ekd_b = _heads_to_lanes([jnp.exp(gtot[:, l:l + 1] - gc[:, l:l + 1]) for l in lanes], DN_DK)
            kb = k * beta_b
            a_low = jnp.where(strict, _dot_nt(kb.astype(BF16), kbd) * decay, 0.0)
            qk_ref[d, 0, rows, :] = jnp.where(incl, _dot_nt(q.astype(BF16), kbd) * decay, 0.0).astype(BF16)
            uw_ref[d, 0, rows, 2 * DN_W:3 * DN_W] = (q * egc_b).astype(BF16)
            uw_ref[d, 0, rows, 3 * DN_W:4 * DN_W] = (k * ekd_b).astype(BF16)
            gl_ref[d, 0, 0, ci:ci + 1, :] = jnp.exp(gtot)
            rhs = jnp.concatenate([v * beta_b, kb * egc_b], axis=1).astype(BF16)
            chains.append(dict(d=d, rows=rows, pw=-a_low, inv=eye_side - a_low, rhs=rhs))

    n_lvl = int(math.log2(C)) - 1
    for ch in chains:
        pwb = ch["pw"].astype(BF16)
        ch["pw"] = _dot(pwb, _block_diag(pwb, H))
    for lvl in range(1, n_lvl + 1):
        for ch in chains:
            pwb = ch["pw"].astype(BF16)
            pbd = _block_diag(pwb, H)
            if lvl < n_lvl:
                st = _dot(jnp.concatenate([ch["inv"].astype(BF16), pwb], axis=0), pbd)
                ch["inv"] = ch["inv"] + st[:C]
                ch["pw"] = st[C:]
            else:
                ch["inv"] = ch["inv"] + _dot(ch["inv"].astype(BF16), pbd)
    rb = lax.broadcasted_iota(jnp.int32, (H * C, 2 * DN_W), 0) // C
    cb = (lax.broadcasted_iota(jnp.int32, (H * C, 2 * DN_W), 1) // DN_DK) % H
    for ch in chains:
        rhs_bd = jnp.where(rb == cb, jnp.concatenate([ch["rhs"]] * H, axis=0), jnp.zeros((), BF16))
        sol = _dot(ch["inv"].astype(BF16), rhs_bd)
        uw_ref[ch["d"], 0, ch["rows"], 0:2 * DN_W] = sol.astype(BF16)


def _dn_chunk(qkv, ab, abt, a_log, dt_bias, G):
    B, T, _ = qkv.shape
    Tg = G * DN_CHUNK
    ns = T // Tg
    nl = 2 * DN_HEADS
    alog = a_log.reshape(-1)
    dtb = dt_bias.reshape(-1)
    arow = jnp.zeros((1, LANES), F32).at[0, :nl].set(alog)
    drow = jnp.zeros((1, LANES), F32).at[0, :nl].set(dtb)
    acol = jnp.zeros((2 * nl, 1), F32).at[:nl, 0].set(alog)
    dcol = jnp.zeros((2 * nl, 1), F32).at[:nl, 0].set(dtb)
    full = lambda *shape: pl.BlockSpec(shape, lambda b, i: (0,) * len(shape))
    return pl.pallas_call(
        functools.partial(_dn_chunk_body, G=G),
        grid=(B, ns),
        in_specs=[pl.BlockSpec((1, Tg, DN_W), lambda b, i: (b, i, 0)),
                  pl.BlockSpec((1, Tg, DN_W), lambda b, i: (b, i, 1)),
                  pl.BlockSpec((1, Tg, DN_W), lambda b, i: (b, i, 2)),
                  pl.BlockSpec((1, Tg, LANES), lambda b, i: (b, i, 0)),
                  pl.BlockSpec((1, 2 * nl, Tg), lambda b, i: (b, 0, i)),
                  full(1, LANES), full(1, LANES), full(2 * nl, 1), full(2 * nl, 1)],
        out_specs=[pl.BlockSpec((2, 1, Tg, 4 * DN_W), lambda b, i: (0, b, i, 0)),
                   pl.BlockSpec((2, 1, Tg, DN_HEADS * DN_CHUNK), lambda b, i: (0, b, i, 0)),
                   pl.BlockSpec((2, 1, 1, 8, LANES), lambda b, i: (0, b, i, 0, 0))],
        out_shape=[jax.ShapeDtypeStruct((2, B, T, 4 * DN_W), BF16),
                   jax.ShapeDtypeStruct((2, B, T, DN_HEADS * DN_CHUNK), BF16),
                   jax.ShapeDtypeStruct((2, B, ns, 8, LANES), F32)],
        compiler_params=_cparams("parallel", "parallel"),
        name="dn_chunk",
    )(qkv, qkv, qkv, ab, abt, arow, drow, acol, dcol)


def _dn_rec_body(uw_ref, qk_ref, gl_ref, s0_ref, o_ref, sout_ref, S_ref, *, bb, nch, G):
    C, H = DN_CHUNK, DN_HEADS
    d = pl.program_id(0)
    c = pl.program_id(2)

    @pl.when(c == 0)
    def _():
        S_ref[...] = s0_ref[...]

    r = (c + d * (nch - 1 - 2 * c)) % G
    heads = [(b, h) for b in range(bb) for h in range(H)]
    ts = []
    for b, h in heads:
        wq = jnp.concatenate([uw_ref[b, :, DN_W + h * DN_DK:DN_W + (h + 1) * DN_DK],
                              uw_ref[b, :, 2 * DN_W + h * DN_DK:2 * DN_W + (h + 1) * DN_DK]], axis=0)
        ts.append(_dot(wq, S_ref[b, h].astype(BF16)))
    for (b, h), t in zip(heads, ts):
        sl = slice(h * DN_DV, (h + 1) * DN_DV)
        v_new = (uw_ref[b, :, sl].astype(F32) - t[:C]).astype(BF16)
        o_ref[b, :, sl] = t[C:] + _dot(qk_ref[b, :, h * C:(h + 1) * C], v_new)
        gl_row = gl_ref[b, 0, pl.ds(r, 1), :]
        gl = jnp.where(d == 0, gl_row[:, h:h + 1], gl_row[:, H + h:H + h + 1])
        kd = uw_ref[b, :, 3 * DN_W + h * DN_DK:3 * DN_W + (h + 1) * DN_DK]
        S_ref[b, h] = S_ref[b, h] * gl + _dot_tn(kd, v_new)

    @pl.when(c == nch - 1)
    def _():
        sout_ref[...] = S_ref[...]


def _dn_rec(uw, qk, gl, s0, bb, G):
    _, B, T, _ = uw.shape
    C = DN_CHUNK
    nch = T // C

    def cidx(d, c):
        return c + d * (nch - 1 - 2 * c)

    s_spec = pl.BlockSpec((None, bb, DN_HEADS, DN_DK, DN_DV), lambda d, b, c: (d, b, 0, 0, 0))
    return pl.pallas_call(
        functools.partial(_dn_rec_body, bb=bb, nch=nch, G=G),
        grid=(2, B // bb, nch),
        in_specs=[pl.BlockSpec((None, bb, C, 4 * DN_W), lambda d, b, c: (d, b, cidx(d, c), 0)),
                  pl.BlockSpec((None, bb, C, DN_HEADS * C), lambda d, b, c: (d, b, cidx(d, c), 0)),
                  pl.BlockSpec((None, bb, 1, 8, LANES), lambda d, b, c: (d, b, cidx(d, c) // G, 0, 0)),
                  s_spec],
        out_specs=[pl.BlockSpec((None, bb, C, DN_W), lambda d, b, c: (d, b, cidx(d, c), 0)), s_spec],
        out_shape=[jax.ShapeDtypeStruct((2, B, T, DN_W), F32),
                   jax.ShapeDtypeStruct((2, B, DN_HEADS, DN_DK, DN_DV), F32)],
        scratch_shapes=[pltpu.VMEM((bb, DN_HEADS, DN_DK, DN_DV), F32)],
        compiler_params=_cparams("parallel", "parallel", "arbitrary"),
        name="dn_rec",
    )(uw, qk, gl, s0)


def _s5_param_body(are_ref, aim_ref, ldt_ref, bre_ref, bim_ref, lam_ref, bd_ref):
    a_re = are_ref[...]
    a_im = aim_ref[...]
    dt = jnp.exp(ldt_ref[...])
    mag = jnp.exp(a_re * dt)
    lam_re = mag * jnp.cos(a_im * dt)
    lam_im = mag * jnp.sin(a_im * dt)
    den = a_re * a_re + a_im * a_im
    nr = lam_re - 1.0
    ni = lam_im
    coef_re = (nr * a_re + ni * a_im) / den
    coef_im = (ni * a_re - nr * a_im) / den
    lam_ref[0] = lam_re
    lam_ref[1] = lam_im
    b_re = bre_ref[...]
    b_im = bim_ref[...]
    for d in range(2):
        cr = coef_re[d:d + 1, :]
        ci = coef_im[d:d + 1, :]
        bd_ref[d, 0] = cr * b_re - ci * b_im
        bd_ref[d, 1] = cr * b_im + ci * b_re


def _s5_params(A_re, A_im, log_dt, B_re, B_im):
    a_re = A_re.reshape(2, S5_NSTATE)
    a_im = A_im.reshape(2, S5_NSTATE)
    ldt = jnp.repeat(log_dt, S5_STATE, axis=1)
    b_re_t = jnp.transpose(B_re, (2, 0, 1)).reshape(S5_GROUP, S5_NSTATE)
    b_im_t = jnp.transpose(B_im, (2, 0, 1)).reshape(S5_GROUP, S5_NSTATE)
    lam, bd = pl.pallas_call(
        _s5_param_body,
        out_shape=[jax.ShapeDtypeStruct((2, 2, S5_NSTATE), F32),
                   jax.ShapeDtypeStruct((2, 2, S5_GROUP, S5_NSTATE), F32)],
        name="s5_params",
    )(a_re, a_im, ldt, b_re_t, b_im_t)
    return lam, bd


def _s5_block_weights(bd, C_re, C_im):
    eye = jnp.eye(S5_GBLK, dtype=F32)
    bd6 = bd.reshape(2, 2, S5_GROUP, S5_NBLK, S5_GBLK, S5_STATE)
    w = jnp.einsum('dchjmp,lm->djlhcmp', bd6, eye)
    w_drive = w.reshape(2, S5_NBLK, S5_GBLK * S5_GROUP, 2 * S5_SBLK).astype(BF16)
    cc = jnp.stack([C_re, -C_im], axis=0).reshape(2, S5_NBLK, S5_GBLK, S5_GROUP, S5_STATE)
    cm = jnp.einsum('cjmhp,lm->jcmplh', cc, eye)
    w_read = cm.reshape(S5_NBLK, 2, S5_SBLK, S5_GBLK * S5_GROUP).astype(BF16)
    return w_drive, w_read


def _s5_scan_body(u_ref, pin_ref, pout_ref, wd_ref, wr_ref, lam_ref, h0_ref, y_ref, hout_ref,
                  xre, xim, sre, sim, hst, ytb,
                  *, B, Tc, nch, lb):
    d = pl.program_id(0)
    c = pl.program_id(1)

    @pl.when(c == 0)
    def _():
        hst[...] = h0_ref[0]

    n = Tc * B
    ub = _dot(pin_ref[...], u_ref[...].reshape(n, S5_WIDTH).astype(BF16)).astype(BF16)
    for j in range(S5_NBLK):
        drv = _dot(ub[:, j * LANES:(j + 1) * LANES], wd_ref[0, j])
        xre[:, j * S5_SBLK:(j + 1) * S5_SBLK] = drv[:, :S5_SBLK]
        xim[:, j * S5_SBLK:(j + 1) * S5_SBLK] = drv[:, S5_SBLK:]

    for lbi in range(S5_NSTATE // lb):
        ls = slice(lbi * lb, (lbi + 1) * lb)
        lr = jnp.broadcast_to(lam_ref[0, 0, :, ls], (B, lb))
        li = jnp.broadcast_to(lam_ref[1, 0, :, ls], (B, lb))

        def step(s, carry):
            hr, hi = carry
            t = s + d * (Tc - 1 - 2 * s)
            r0 = pl.multiple_of(t * B, B)
            nr = lr * hr - li * hi + xre[pl.ds(r0, B), ls]
            ni = lr * hi + li * hr + xim[pl.ds(r0, B), ls]
            sre[pl.ds(r0, B), ls] = nr.astype(BF16)
            sim[pl.ds(r0, B), ls] = ni.astype(BF16)
            return nr, ni

        hr, hi = lax.fori_loop(0, Tc, step, (hst[0, :, ls], hst[1, :, ls]), unroll=4)
        hst[0, :, ls] = hr
        hst[1, :, ls] = hi

    for j in range(S5_NBLK):
        ss = slice(j * S5_SBLK, (j + 1) * S5_SBLK)
        y = _dot(sre[:, ss], wr_ref[j, 0]) + _dot(sim[:, ss], wr_ref[j, 1])
        ytb[:, j * LANES:(j + 1) * LANES] = y.astype(BF16)
    y_ref[0] = _dot(pout_ref[...], ytb[...]).astype(BF16).reshape(B, Tc, S5_WIDTH)

    @pl.when(c == nch - 1)
    def _():
        hout_ref[0] = hst[...]


def _s5_scan(u, w_drive, w_read, lam, h0, Tc):
    B, T, _ = u.shape
    Tc = _tile(T, Tc)
    nch = T // Tc
    lam4 = lam.reshape(2, 2, 1, S5_NSTATE)
    n = Tc * B
    r = jnp.arange(n, dtype=jnp.int32)
    p_in = ((r[:, None] // B == r[None, :] % Tc) & (r[:, None] % B == r[None, :] // Tc)).astype(BF16)
    p_out = p_in.T

    def cidx(d, c):
        return c + d * (nch - 1 - 2 * c)

    return pl.pallas_call(
        functools.partial(_s5_scan_body, B=B, Tc=Tc, nch=nch, lb=256),
        grid=(2, nch),
        in_specs=[pl.BlockSpec((B, Tc, S5_WIDTH), lambda d, c: (0, cidx(d, c), 0)),
                  pl.BlockSpec((n, n), lambda d, c: (0, 0)),
                  pl.BlockSpec((n, n), lambda d, c: (0, 0)),
                  pl.BlockSpec((1, S5_NBLK, LANES, 2 * S5_SBLK), lambda d, c: (d, 0, 0, 0)),
                  pl.BlockSpec((S5_NBLK, 2, S5_SBLK, LANES), lambda d, c: (0, 0, 0, 0)),
                  pl.BlockSpec((2, 1, 1, S5_NSTATE), lambda d, c: (0, d, 0, 0)),
                  pl.BlockSpec((1, 2, B, S5_NSTATE), lambda d, c: (d, 0, 0, 0))],
        out_specs=[pl.BlockSpec((1, B, Tc, S5_WIDTH), lambda d, c: (d, 0, cidx(d, c), 0)),
                   pl.BlockSpec((1, 2, B, S5_NSTATE), lambda d, c: (d, 0, 0, 0))],
        out_shape=[jax.ShapeDtypeStruct((2, B, T, S5_WIDTH), BF16),
                   jax.ShapeDtypeStruct((2, 2, B, S5_NSTATE), F32)],
        scratch_shapes=[pltpu.VMEM((n, S5_NSTATE), F32),
                        pltpu.VMEM((n, S5_NSTATE), F32),
                        pltpu.VMEM((n, S5_NSTATE), BF16),
                        pltpu.VMEM((n, S5_NSTATE), BF16),
                        pltpu.VMEM((2, B, S5_NSTATE), F32),
                        pltpu.VMEM((n, S5_WIDTH), BF16)],
        compiler_params=_cparams("parallel", "arbitrary"),
        name="s5_scan",
    )(u, p_in, p_out, w_drive, w_read, lam4, h0)


def _ab_out_body(x_ref, mod_ref, o_ref, z_ref, u_ref, y_ref, ng_ref, dsk_ref, gw_ref, gb_ref,
                 woa_ref, wob_ref, out_ref):
    o = o_ref[0, 0] + o_ref[1, 0]
    z = z_ref[0]
    parts = []
    for h in range(DN_HEADS):
        sl = slice(h * DN_DV, (h + 1) * DN_DV)
        parts.append(_rms(o[:, sl]) * ng_ref[...] * _silu(z[:, sl]))
    a_out = jnp.concatenate(parts, axis=1)
    y = y_ref[0, 0].astype(F32) + y_ref[1, 0].astype(F32) + dsk_ref[...] * u_ref[0]
    y = _gelu_tanh(y)
    b_out = y * jax.nn.sigmoid(_dot(y.astype(BF16), gw_ref[...]) + gb_ref[...])
    mix = _dot(a_out.astype(BF16), woa_ref[...]) + _dot(b_out.astype(BF16), wob_ref[...])
    out_ref[0] = x_ref[0] + mod_ref[0, 2:3, :] * mix


def _ab_out(x, mod, o_dn, z, u, y_s5, dn_norm_g, d_skip, glu_w, glu_b, w_out, tm):
    B, T, D = x.shape
    tm = _tile(T, tm)
    per_batch_mod = mod.shape[0] != 1
    full = lambda *shape: pl.BlockSpec(shape, lambda b, i: (0,) * len(shape))
    return pl.pallas_call(
        _ab_out_body,
        grid=(B, T // tm),
        in_specs=[pl.BlockSpec((1, tm, D), lambda b, i: (b, i, 0)),
                  pl.BlockSpec((1, 6, D), (lambda b, i: (b, 0, 0)) if per_batch_mod else (lambda b, i: (0, 0, 0))),
                  pl.BlockSpec((2, 1, tm, DN_W), lambda b, i: (0, b, i, 0)),
                  pl.BlockSpec((1, tm, DN_W), lambda b, i: (b, i, 0)),
                  pl.BlockSpec((1, tm, S5_WIDTH), lambda b, i: (b, i, 0)),
                  pl.BlockSpec((2, 1, tm, S5_WIDTH), lambda b, i: (0, b, i, 0)),
                  full(1, DN_DV), full(1, S5_WIDTH), full(S5_WIDTH, S5_WIDTH), full(1, S5_WIDTH),
                  full(DN_W, D), full(S5_WIDTH, D)],
        out_specs=pl.BlockSpec((1, tm, D), lambda b, i: (b, i, 0)),
        out_shape=jax.ShapeDtypeStruct((B, T, D), F32),
        compiler_params=_cparams("parallel", "parallel"),
        name="ab_out",
    )(x, mod, o_dn, z, u, y_s5, dn_norm_g.reshape(1, DN_DV), d_skip.reshape(1, S5_WIDTH),
      glu_w.astype(BF16), glu_b.reshape(1, S5_WIDTH), w_out[:DN_W].astype(BF16), w_out[DN_W:].astype(BF16))


def _route(h, rw_ref, rb_ref):
    logits = _dot_nt(rw_ref[...], h, precision=HIGHEST)
    scores = jax.nn.sigmoid(logits)
    choice = scores + rb_ref[...]
    rows = [choice[e:e + 1, :] for e in range(N_EXPERTS)]
    neg_inf = jnp.float32(-jnp.inf)
    gs = []
    for g in range(N_EXPERT_GROUPS):
        r = rows[g * EXPERTS_PER_GROUP:(g + 1) * EXPERTS_PER_GROUP]
        best = None
        for a in range(EXPERTS_PER_GROUP):
            for b in range(a + 1, EXPERTS_PER_GROUP):
                s = r[a] + r[b]
                best = s if best is None else jnp.maximum(best, s)
        gs.append(best)
    best_val = gs[0]
    best_g = jnp.zeros_like(best_val, dtype=jnp.int32)
    for g in range(1, N_EXPERT_GROUPS):
        better = gs[g] > best_val
        best_val = jnp.where(better, gs[g], best_val)
        best_g = jnp.where(better, g, best_g)
    masked = [jnp.where(best_g == e // EXPERTS_PER_GROUP, rows[e], neg_inf) for e in range(N_EXPERTS)]
    m1 = masked[0]
    for e in range(1, N_EXPERTS):
        m1 = jnp.maximum(m1, masked[e])
    i1 = jnp.full_like(best_g, N_EXPERTS)
    for e in reversed(range(N_EXPERTS)):
        i1 = jnp.where(masked[e] == m1, e, i1)
    rest = [jnp.where(i1 == e, neg_inf, masked[e]) for e in range(N_EXPERTS)]
    m2 = rest[0]
    for e in range(1, N_EXPERTS):
        m2 = jnp.maximum(m2, rest[e])
    i2 = jnp.full_like(best_g, N_EXPERTS)
    for e in reversed(range(N_EXPERTS)):
        i2 = jnp.where(rest[e] == m2, e, i2)
    eidx = lax.broadcasted_iota(jnp.int32, scores.shape, 0)
    sel1 = eidx == i1
    sel2 = eidx == i2
    w1 = jnp.sum(jnp.where(sel1, scores, 0.0), axis=0, keepdims=True)
    w2 = jnp.sum(jnp.where(sel2, scores, 0.0), axis=0, keepdims=True)
    inv = 1.0 / (w1 + w2)
    return jnp.where(sel1, w1 * inv, 0.0) + jnp.where(sel2, w2 * inv, 0.0), best_g, i1, i2


def _pack_planes(x):
    planes = []
    for c in range(N_PLANES):
        lo = x[:, 2 * c * LANES:(2 * c + 1) * LANES]
        hi = x[:, (2 * c + 1) * LANES:(2 * c + 2) * LANES]
        planes.append(lax.bitcast_convert_type(pltpu.pack_elementwise([lo, hi], packed_dtype=BF16), F32))
    return planes


def _unpack_planes(planes):
    parts = []
    for p in planes:
        w = lax.bitcast_convert_type(p, jnp.uint32)
        for idx in range(2):
            parts.append(pltpu.unpack_elementwise(w, index=idx, packed_dtype=BF16, unpacked_dtype=F32))
    return jnp.concatenate(parts, axis=1)


def _moe_route_body(x_ref, mod_ref, g_ref, rw_ref, rb_ref, tab_ref, cls_ref):
    h = _rms(x_ref[0]) * g_ref[...]
    h = h * (1.0 + mod_ref[0, 4:5, :]) + mod_ref[0, 3:4, :]
    for c, plane in enumerate(_pack_planes(h)):
        tab_ref[c] = plane
    gates_t, best_g, i1, i2 = _route(h, rw_ref, rb_ref)
    rows = []
    for k in range(EXPERTS_PER_GROUP):
        gk = jnp.zeros_like(gates_t[0:1, :])
        for g in range(N_EXPERT_GROUPS):
            e = g * EXPERTS_PER_GROUP + k
            gk = jnp.where(best_g == g, gates_t[e:e + 1, :], gk)
        rows.append(gk)
    rows.append(jnp.zeros((LANES - EXPERTS_PER_GROUP, gates_t.shape[1]), F32))
    tab_ref[N_PLANES] = jnp.transpose(jnp.concatenate(rows, axis=0))
    lo = jnp.minimum(i1, i2) % EXPERTS_PER_GROUP
    hi = jnp.maximum(i1, i2) % EXPERTS_PER_GROUP
    pair = lo * (7 - lo) // 2 + (hi - lo - 1)
    cls_ref[...] = jnp.broadcast_to(best_g * PAIRS_PER_GROUP + pair, cls_ref.shape)


def _moe_route(x, mod, norm_g, router_w, router_bias, tm):
    Bx, T, D = x.shape
    tm = _tile(T, tm)
    nt = T // tm
    N = Bx * T
    per_batch_mod = mod.shape[0] != 1
    full = lambda *shape: pl.BlockSpec(shape, lambda b, i: (0,) * len(shape))
    return pl.pallas_call(
        _moe_route_body,
        grid=(Bx, nt),
        in_specs=[pl.BlockSpec((1, tm, D), lambda b, i: (b, i, 0)),
                  pl.BlockSpec((1, 6, D), (lambda b, i: (b, 0, 0)) if per_batch_mod else (lambda b, i: (0, 0, 0))),
                  full(1, D), full(N_EXPERTS, D), full(N_EXPERTS, 1)],
        out_specs=[pl.BlockSpec((N_PLANES + 1, tm, LANES), lambda b, i: (0, b * nt + i, 0)),
                   pl.BlockSpec((8, tm), lambda b, i: (0, b * nt + i))],
        out_shape=[jax.ShapeDtypeStruct((N_PLANES + 1, N, LANES), F32),
                   jax.ShapeDtypeStruct((8, N), jnp.int32)],
        compiler_params=_cparams("parallel", "parallel"),
        name="moe_route",
    )(x, mod, norm_g.reshape(1, D), router_w.T, router_bias.reshape(N_EXPERTS, 1))


def _sc_move_rows(table, idx, n_out, scatter):
    n_planes = table.shape[0]
    n = idx.shape[0]
    ch = SC_CHUNK_ROWS if n % (2 * SC_WORKERS * SC_CHUNK_ROWS) == 0 else SC_CHUNK_ROWS // 2
    n_chunks = n // (SC_WORKERS * ch)
    assert n_chunks * SC_WORKERS * ch == n and n_chunks % 2 == 0
    assert n_out == n if not scatter else table.shape[1] == n
    items = [(dj, c) for dj in range(2) for c in range(n_planes)]
    mesh = plsc.VectorSubcoreMesh(core_axis_name="c", subcore_axis_name="s")

    @functools.partial(
        pl.kernel, mesh=mesh,
        out_type=jax.ShapeDtypeStruct((n_planes, n_out, LANES), table.dtype),
        scratch_types=[pltpu.VMEM((n_chunks, ch), jnp.int32),
                       pltpu.VMEM((2, ch, LANES), table.dtype),
                       pltpu.SemaphoreType.DMA((2,)), pltpu.SemaphoreType.DMA((2,))])
    def move_kernel(table_hbm, idx_hbm, out_hbm, idx_v, rows_v, isem, osem):
        wid = lax.axis_index("s") * SC_CORES + lax.axis_index("c")
        wbase = wid * (n_chunks * ch)
        pltpu.sync_copy(idx_hbm.at[pl.ds(wid * n_chunks, n_chunks)], idx_v)

        def load(j, c, slot):
            rows = idx_v.at[j] if not scatter else pl.ds(wbase + j * ch, ch)
            return pltpu.make_async_copy(table_hbm.at[c].at[rows], rows_v.at[slot], isem.at[slot])

        def store(j, c, slot):
            rows = idx_v.at[j] if scatter else pl.ds(wbase + j * ch, ch)
            return pltpu.make_async_copy(rows_v.at[slot], out_hbm.at[c].at[rows], osem.at[slot])

        load(0, 0, 0).start()

        @pl.loop(0, n_chunks, step=2)
        def _(j):
            for it, (dj, c) in enumerate(items):
                s = it % 2
                load(j + dj, c, s).wait()

                def refill(it=it, s=s):
                    if it == 0:
                        @pl.when(j > 0)
                        def _():
                            store(j - 1, n_planes - 1, 1 - s).wait()
                    else:
                        store(j + items[it - 1][0], items[it - 1][1], 1 - s).wait()
                    if it + 1 < len(items):
                        load(j + items[it + 1][0], items[it + 1][1], 1 - s).start()
                    else:
                        load(j + 2, 0, 1 - s).start()

                if it + 1 < len(items):
                    refill()
                else:
                    pl.when(j + 2 < n_chunks)(refill)
                store(j + dj, c, s).start()

        store(n_chunks - 2 + items[-2][0], items[-2][1], 0).wait()
        store(n_chunks - 1, n_planes - 1, 1).wait()

    return move_kernel(table, idx.reshape(SC_WORKERS * n_chunks, ch))


FFN_LIVE, FFN_FIRST, FFN_LAST = 1, 2, 4


def _moe_ffn_body(st_ref, se_ref, sk_ref, sf_ref, tn_ref, xs_ref, wg_ref, wu_ref, wd_ref, ys_ref, xb, gcol, acc):
    s = pl.program_id(0)
    flags = sf_ref[s]
    k = sk_ref[s]

    @pl.when((flags & FFN_FIRST) != 0)
    def _():
        keep = lax.broadcasted_iota(jnp.int32, (xs_ref.shape[1], LANES), 0) < tn_ref[st_ref[s]]
        xb[...] = _unpack_planes([jnp.where(keep, xs_ref[c], 0.0) for c in range(N_PLANES)]).astype(BF16)
        gcol[...] = jnp.where(keep, xs_ref[N_PLANES], 0.0)
        acc[...] = jnp.zeros_like(acc)

    @pl.when((flags & FFN_LIVE) != 0)
    def _():
        hb = xb[...]
        act = _silu(_dot(hb, wg_ref[0].astype(BF16))) * _dot(hb, wu_ref[0].astype(BF16))
        lane = lax.broadcasted_iota(jnp.int32, gcol.shape, 1)
        ge = jnp.sum(jnp.where(lane == k, gcol[...], 0.0), axis=1, keepdims=True)
        acc[...] += _dot((act * ge).astype(BF16), wd_ref[0].astype(BF16))

    @pl.when((flags & FFN_LAST) != 0)
    def _():
        for c, plane in enumerate(_pack_planes(acc[...])):
            ys_ref[c] = plane


def _moe_ffn(step_tile, step_expert, step_k, step_flags, tile_rows, xs, wg, wu, wd, tm):
    P = xs.shape[1]
    D = 2 * N_PLANES * LANES
    n_steps = step_tile.shape[0]

    def wspec(shape):
        return pl.BlockSpec(shape, lambda s, st, se, sk, sf, tn: (se[s], 0, 0))

    return pl.pallas_call(
        _moe_ffn_body,
        grid_spec=pltpu.PrefetchScalarGridSpec(
            num_scalar_prefetch=5,
            grid=(n_steps,),
            in_specs=[pl.BlockSpec((N_PLANES + 1, tm, LANES), lambda s, st, se, sk, sf, tn: (0, st[s], 0)),
                      wspec((1, D, D_EXPERT)), wspec((1, D, D_EXPERT)), wspec((1, D_EXPERT, D))],
            out_specs=pl.BlockSpec((N_PLANES, tm, LANES), lambda s, st, se, sk, sf, tn: (0, st[s], 0)),
            scratch_shapes=[pltpu.VMEM((tm, D), BF16), pltpu.VMEM((tm, LANES), F32), pltpu.VMEM((tm, D), F32)]),
        out_shape=jax.ShapeDtypeStruct((N_PLANES, P, LANES), F32),
        compiler_params=_cparams("arbitrary"),
        name="moe_ffn",
    )(step_tile, step_expert, step_k, step_flags, tile_rows, xs, wg, wu, wd)


def _moe_combine_body(x_ref, mod_ref, y_ref, fg_ref, out_ref, *, final_norm):
    y = _unpack_planes([y_ref[c] for c in range(N_PLANES)])
    r = x_ref[0] + mod_ref[0, 5:6, :] * y
    if final_norm:
        r = _rms(r) * fg_ref[...]
    out_ref[0] = r


def _moe_combine(x, mod, y_tok, final_g, tm):
    Bx, T, D = x.shape
    tm = _tile(T, tm)
    nt = T // tm
    per_batch_mod = mod.shape[0] != 1
    final_norm = final_g is not None
    fg = (final_g if final_norm else jnp.ones((D,), F32)).reshape(1, D)
    return pl.pallas_call(
        functools.partial(_moe_combine_body, final_norm=final_norm),
        grid=(Bx, nt),
        in_specs=[pl.BlockSpec((1, tm, D), lambda b, i: (b, i, 0)),
                  pl.BlockSpec((1, 6, D), (lambda b, i: (b, 0, 0)) if per_batch_mod else (lambda b, i: (0, 0, 0))),
                  pl.BlockSpec((N_PLANES, tm, LANES), lambda b, i: (0, b * nt + i, 0)),
                  pl.BlockSpec((1, D), lambda b, i: (0, 0))],
        out_specs=pl.BlockSpec((1, tm, D), lambda b, i: (b, i, 0)),
        out_shape=jax.ShapeDtypeStruct((Bx, T, D), F32),
        compiler_params=_cparams("parallel", "parallel"),
        name="moe_combine",
    )(x, mod, y_tok, fg)


def _moe(x, mod, norm_g, router_w, router_bias, wg, wu, wd, expert_base, final_g):
    Bx, T, D = x.shape
    N = Bx * T
    tm = MOE_TILE
    G, K, NP = N_EXPERT_GROUPS, EXPERTS_PER_GROUP, PAIRS_PER_GROUP
    table, cls_rows = _moe_route(x, mod, norm_g, router_w, router_bias, 512)
    cls = cls_rows[0]
    onehot = (cls[:, None] == jnp.arange(G * NP, dtype=jnp.int32)[None, :]).astype(jnp.int32)
    csum = jnp.cumsum(onehot, axis=0)
    c_count = csum[-1]
    g_count = c_count.reshape(G, NP).sum(axis=1)
    g_padded = (g_count + tm - 1) // tm * tm
    g_end = jnp.cumsum(g_padded)
    g_start = g_end - g_padded
    in_group = jnp.cumsum(c_count.reshape(G, NP), axis=1) - c_count.reshape(G, NP)
    c_start = (g_start[:, None] + in_group).reshape(G * NP)
    rank = jnp.sum(csum * onehot, axis=1) - 1
    pos = (jnp.sum(c_start[None, :] * onehot, axis=1) + rank).astype(jnp.int32)
    P = N + G * tm
    nt = P // tm
    tile_start = jnp.arange(nt, dtype=jnp.int32) * tm
    tile_group = jnp.minimum(jnp.sum(tile_start[:, None] >= g_end[None, :], axis=1), G - 1)
    filled = (g_start + g_count)[tile_group]
    tile_rows = jnp.clip(filled - tile_start, 0, tm).astype(jnp.int32)
    overlap = ((c_start[None, :] < tile_start[:, None] + tm) & (c_start + c_count > tile_start[:, None])
               & (c_count[None, :] > 0)).astype(jnp.int32)
    pairs = [(a, b) for a in range(K) for b in range(a + 1, K)]
    member = jnp.array([[int(k in pairs[c % NP]) for k in range(K)] for c in range(G * NP)], jnp.int32)
    used = (jnp.dot(overlap, member) > 0).astype(jnp.int32)
    seen = jnp.cumsum(used, axis=1)
    first = used * (seen == 1)
    last = used * (seen == seen[:, -1:])
    n_steps = nt * K
    order = jnp.argsort(1 - used.reshape(n_steps), stable=True).astype(jnp.int32)
    n_used = jnp.sum(used)
    live = jnp.arange(n_steps, dtype=jnp.int32) < n_used
    src = jnp.where(live, order, order[jnp.maximum(n_used - 1, 0)])
    step_tile = src // K
    step_k = src % K
    step_expert = (expert_base + tile_group[step_tile] * K + step_k).astype(jnp.int32)
    step_flags = jnp.where(live, FFN_LIVE + FFN_FIRST * first.reshape(n_steps)[src]
                           + FFN_LAST * last.reshape(n_steps)[src], 0).astype(jnp.int32)
    xs = _sc_move_rows(table, pos, P, scatter=True)
    ys = _moe_ffn(step_tile, step_expert, step_k, step_flags, tile_rows, xs, wg, wu, wd, tm)
    y_tok = _sc_move_rows(ys, pos, N, scatter=False)
    return _moe_combine(x, mod, y_tok, final_g, 512)


def _mla_in_body(x_ref, mod_ref, g_ref, win_ref, qg_ref, wn_ref, wa_ref, wb_ref, kg_ref, wk_ref, wvt_ref,
                 cos_ref, sin_ref, *rest, with_q):
    k_ref, vt_ref = rest[-2:]
    h = _rms(x_ref[0]) * g_ref[...]
    h = h * (1.0 + mod_ref[0, 1:2, :]) + mod_ref[0, 0:1, :]
    low = _dot(h.astype(BF16), win_ref[...])
    cos = cos_ref[...]
    sin = sin_ref[...]
    c0 = MLA_Q_RANK + MLA_KV_RANK
    kvb = (_rms(low[:, MLA_Q_RANK:c0]) * kg_ref[...]).astype(BF16)
    kn = _dot(kvb, wk_ref[...]).astype(BF16)
    vt_ref[0] = _dot_nt(wvt_ref[...], kvb).astype(BF16)
    kr = (low[:, c0:c0 + LANES] * cos + low[:, c0 + LANES:c0 + 2 * LANES] * sin).astype(BF16)
    for hd in range(MLA_HEADS):
        k_ref[0, :, hd * MLA_QK:hd * MLA_QK + MLA_NOPE] = kn[:, hd * MLA_NOPE:(hd + 1) * MLA_NOPE]
        k_ref[0, :, hd * MLA_QK + MLA_NOPE:(hd + 1) * MLA_QK] = kr
    if with_q:
        q_ref = rest[-3]
        qb = (_rms(low[:, :MLA_Q_RANK]) * qg_ref[...]).astype(BF16)
        qn = _dot(qb, wn_ref[...]) * MLA_Q_SCALE
        ra = _dot(qb, wa_ref[...])
        rb = _dot(qb, wb_ref[...])
        lane = lax.broadcasted_iota(jnp.int32, cos.shape, 1)
        for hd in range(MLA_HEADS):
            p = hd // 2
            rot = (ra[:, p * LANES:(p + 1) * LANES] * cos + rb[:, p * LANES:(p + 1) * LANES] * sin) * MLA_Q_SCALE
            mine = (lane < MLA_ROPE) if hd % 2 == 0 else (lane >= MLA_ROPE)
            q_ref[0, :, hd * MLA_QK:hd * MLA_QK + MLA_NOPE] = qn[:, hd * MLA_NOPE:(hd + 1) * MLA_NOPE].astype(BF16)
            q_ref[0, :, hd * MLA_QK + MLA_NOPE:(hd + 1) * MLA_QK] = jnp.where(mine, rot, 0.0).astype(BF16)


def _mla_in(x, mod, norm_g, weights, cos_t, sin_t, tm, n_keys, key_block0, kv_bufs):
    B, T, D = x.shape
    W = MLA_HEADS * MLA_V
    with_q = kv_bufs is not None
    per_batch_mod = mod.shape[0] != 1
    full = lambda a: pl.BlockSpec(a.shape, lambda b, i: (0,) * a.ndim)
    in_specs = [pl.BlockSpec((1, tm, D), lambda b, i: (b, i, 0)),
                pl.BlockSpec((1, 6, D), (lambda b, i: (b, 0, 0)) if per_batch_mod else (lambda b, i: (0, 0, 0))),
                pl.BlockSpec((1, D), lambda b, i: (0, 0))]
    in_specs += [full(w) for w in weights]
    in_specs += [pl.BlockSpec((tm, LANES), lambda b, i: (i, 0)), pl.BlockSpec((tm, LANES), lambda b, i: (i, 0))]
    args = [x, mod, norm_g.reshape(1, D), *weights, cos_t, sin_t]
    out_specs = [pl.BlockSpec((1, tm, MLA_HEADS * MLA_QK), lambda b, i: (b, i + key_block0, 0)),
                 pl.BlockSpec((1, W, tm), lambda b, i: (b, 0, i + key_block0))]
    out_shape = [jax.ShapeDtypeStruct((B, n_keys, MLA_HEADS * MLA_QK), BF16),
                 jax.ShapeDtypeStruct((B, W, n_keys), BF16)]
    aliases = {}
    if with_q:
        in_specs += [pl.BlockSpec(memory_space=pl.ANY)] * 2
        aliases = {len(args): 1, len(args) + 1: 2}
        args += list(kv_bufs)
        out_specs.insert(0, pl.BlockSpec((1, tm, MLA_HEADS * MLA_QK), lambda b, i: (b, i, 0)))
        out_shape.insert(0, jax.ShapeDtypeStruct((B, T, MLA_HEADS * MLA_QK), BF16))
    return pl.pallas_call(
        functools.partial(_mla_in_body, with_q=with_q),
        grid=(B, T // tm),
        in_specs=in_specs, out_specs=out_specs, out_shape=out_shape,
        input_output_aliases=aliases,
        compiler_params=_cparams("parallel", "parallel"),
        name="mla_in",
    )(*args)


def _mla_attn_body(q_ref, k_ref, v_ref, o_ref):
    for h in range(MLA_HEADS):
        qk = slice(h * MLA_QK, (h + 1) * MLA_QK)
        sl = slice(h * MLA_V, (h + 1) * MLA_V)
        s = _dot_nt(q_ref[0, :, qk], k_ref[0, :, qk])
        m = jnp.max(s, axis=-1, keepdims=True)
        e = jnp.exp2(s - m)
        l = jnp.sum(e, axis=-1, keepdims=True)
        ot = _dot_nt(v_ref[0, sl, :], e.astype(BF16))
        o_ref[0, :, sl] = (jnp.transpose(ot) / l).astype(o_ref.dtype)


def _mla_attn(q, k, v, tq):
    B, T, WQ = q.shape
    W, Tk = v.shape[1], v.shape[2]
    tq = _tile(T, tq)
    return pl.pallas_call(
        _mla_attn_body,
        grid=(B, T // tq),
        in_specs=[pl.BlockSpec((1, tq, WQ), lambda b, i: (b, i, 0)),
                  pl.BlockSpec((1, Tk, WQ), lambda b, i: (b, 0, 0)),
                  pl.BlockSpec((1, W, Tk), lambda b, i: (b, 0, 0))],
        out_specs=pl.BlockSpec((1, tq, W), lambda b, i: (b, i, 0)),
        out_shape=jax.ShapeDtypeStruct((B, T, W), BF16),
        compiler_params=_cparams("parallel", "arbitrary"),
        name="mla_attn",
    )(q, k, v)


def _proj_res_body(x_ref, mod_ref, a_ref, w_ref, out_ref):
    out_ref[0] = x_ref[0] + mod_ref[0, 2:3, :] * _dot(a_ref[0], w_ref[...])


def _proj_res(x, mod, a, w, tm):
    B, T, D = x.shape
    K = a.shape[2]
    tm = _tile(T, tm)
    per_batch_mod = mod.shape[0] != 1
    return pl.pallas_call(
        _proj_res_body,
        grid=(B, T // tm),
        in_specs=[pl.BlockSpec((1, tm, D), lambda b, i: (b, i, 0)),
                  pl.BlockSpec((1, 6, D), (lambda b, i: (b, 0, 0)) if per_batch_mod else (lambda b, i: (0, 0, 0))),
                  pl.BlockSpec((1, tm, K), lambda b, i: (b, i, 0)),
                  pl.BlockSpec((K, D), lambda b, i: (0, 0))],
        out_specs=pl.BlockSpec((1, tm, D), lambda b, i: (b, i, 0)),
        out_shape=jax.ShapeDtypeStruct((B, T, D), F32),
        compiler_params=_cparams("parallel", "parallel"),
        name="proj_res",
    )(x, mod, a, w)


def _rope_tables(T, n_ctx):
    rows = T // GRID_W
    row = jnp.repeat(jnp.arange(rows, dtype=F32), GRID_W)
    col = jnp.tile(jnp.arange(GRID_W, dtype=F32), rows)
    n_freq = MLA_ROPE // 4
    inv = ROPE_THETA ** (-jnp.arange(n_freq, dtype=F32) / n_freq)
    ang = jnp.concatenate([row[:, None] * inv, col[:, None] * inv], axis=-1)
    cos, sin = jnp.cos(ang), jnp.sin(ang)
    cos_t = jnp.concatenate([cos, cos, cos, cos], axis=-1)
    sin_t = jnp.concatenate([-sin, sin, -sin, sin], axis=-1)
    return cos_t, sin_t, jnp.ones((n_ctx, LANES), F32), jnp.zeros((n_ctx, LANES), F32)


def _layer_ab(x, ctx, mod_l, mod_c, norm1_g, w_in, conv_w, a_log, dt_bias, dn_norm_g, A_re, A_im, log_dt,
              B_re, B_im, C_re, C_im, D_skip, glu_w, glu_b, w_out):
    B, T, D = x.shape
    Tc = ctx.shape[1]
    q0, k0, v0, z0, a0, b0, u0 = 0, 512, 1024, 1536, 2048, 2056, 2064
    w_qkv = w_in[:, q0:z0].astype(BF16)
    w_z = w_in[:, z0:a0].astype(BF16)
    w_ab = jnp.zeros((D, LANES), F32).at[:, :16].set(w_in[:, a0:u0]).astype(BF16)
    w_u = w_in[:, u0:].astype(BF16)
    w_abt = w_in[:, a0:u0].T.astype(BF16)
    ws = [w_qkv, w_z, w_ab, w_abt, w_u]
    dts = [F32] * 5
    kinds = ["n", "n", "n", "t", "n"]
    dn_group = 4

    lam, bd = _s5_params(A_re, A_im, log_dt, B_re, B_im)
    w_drive, w_read = _s5_block_weights(bd, C_re, C_im)

    streams = []
    dn_state = jnp.zeros((2, B, DN_HEADS, DN_DK, DN_DV), F32)
    s5_state = jnp.zeros((2, 2, B, S5_NSTATE), F32)
    for xs, mod in ((ctx, mod_c), (x, mod_l)):
        qkv, z, ab, abt, u = _modmm(xs, mod, norm1_g, ws, dts, kinds, 0, 512)
        qkv = _dn_prep(qkv, conv_w, 512)
        uw, qk, gl = _dn_chunk(qkv, ab, abt, a_log, dt_bias, dn_group)
        o_dn, dn_state = _dn_rec(uw, qk, gl, dn_state, 8, dn_group)
        y_s5, s5_state = _s5_scan(u, w_drive, w_read, lam, s5_state, 32)
        streams.append(_ab_out(xs, mod, o_dn, z, u, y_s5, dn_norm_g, D_skip.reshape(-1), glu_w, glu_b,
                               w_out, 512))
    return streams[1], streams[0]


def _layer_mla(x, ctx, mod_l, mod_c, norm1_g, w_in, q_norm_g, w_q_up, kv_norm_g, w_kv_up, w_out, need_ctx):
    assert not need_ctx, "context attention output is only needed when a later layer follows"
    B, T, D = x.shape
    n_ctx = ctx.shape[1]
    qr, kvr = MLA_Q_RANK, MLA_KV_RANK
    half = MLA_ROPE // 2
    wk1 = w_in[:, qr + kvr:qr + kvr + half]
    wk2 = w_in[:, qr + kvr + half:]
    w_low = jnp.concatenate([w_in[:, :qr + kvr], wk1, wk2, wk1, wk2, wk2, wk1, wk2, wk1], axis=1).astype(BF16)

    wq = w_q_up.reshape(qr, MLA_HEADS, MLA_NOPE + MLA_ROPE)
    wq_n = wq[:, :, :MLA_NOPE].reshape(qr, MLA_HEADS * MLA_NOPE).astype(BF16)
    x1 = wq[:, :, MLA_NOPE:MLA_NOPE + half]
    x2 = wq[:, :, MLA_NOPE + half:]
    wq_a = jnp.concatenate([x1, x2], axis=2).reshape(qr, MLA_HEADS * MLA_ROPE).astype(BF16)
    wq_b = jnp.concatenate([x2, x1], axis=2).reshape(qr, MLA_HEADS * MLA_ROPE).astype(BF16)
    wkv = w_kv_up.reshape(kvr, MLA_HEADS, MLA_NOPE + MLA_V)
    wk_n = wkv[:, :, :MLA_NOPE].reshape(kvr, MLA_HEADS * MLA_NOPE).astype(BF16)
    wv_t = wkv[:, :, MLA_NOPE:].reshape(kvr, MLA_HEADS * MLA_V).T.astype(BF16)

    tm = _tile(math.gcd(T, n_ctx), 256)
    weights = [w_low, q_norm_g.reshape(1, qr), wq_n, wq_a, wq_b, kv_norm_g.reshape(1, kvr), wk_n, wv_t]
    cos_t, sin_t, cos_c, sin_c = _rope_tables(T, n_ctx)
    kv_ctx = _mla_in(ctx, mod_c, norm1_g, weights, cos_c, sin_c, tm, n_ctx + T, 0, None)
    q, k, vt = _mla_in(x, mod_l, norm1_g, weights, cos_t, sin_t, tm, n_ctx + T, n_ctx // tm, kv_ctx)
    o = _mla_attn(q, k, vt, 256)
    return _proj_res(x, mod_l, o, w_out.astype(BF16), 512)


def kernel(x, c, ctx, c_ctx, ada_w, ada_b, norm1_g, norm2_g, ab_w_in, dn_conv_w, dn_A_log, dn_dt_bias, dn_norm_g, s5_A_re, s5_A_im, s5_log_dt, s5_B_re, s5_B_im, s5_C_re, s5_C_im, s5_D, s5_glu_w, s5_glu_b, ab_w_out, mla_w_in, mla_q_norm_g, mla_w_q_up, mla_kv_norm_g, mla_w_kv_up, mla_w_out, router_w, router_bias, moe_w_gate, moe_w_up, moe_w_down, final_norm_g):
    B, T, D = x.shape
    n_ctx = ctx.shape[1]
    depth = ada_w.shape[0]
    n_cond = -(-(B + 1) // 8) * 8
    cond = jnp.zeros((n_cond, D), F32).at[:B].set(c).at[B].set(c_ctx)
    mods = _adaln_all(cond, ada_w, ada_b).reshape(depth, n_cond, 6, D)
    n_exp = moe_w_gate.shape[1]
    wg = moe_w_gate.reshape((depth * n_exp,) + moe_w_gate.shape[2:])
    wu = moe_w_up.reshape((depth * n_exp,) + moe_w_up.shape[2:])
    wd = moe_w_down.reshape((depth * n_exp,) + moe_w_down.shape[2:])
    for i in range(depth):
        last = i == depth - 1
        j = i // 2
        mod_l = mods[i, :B]
        mod_c = mods[i, B:B + 1]
        if i % 2 == 0:
            x, ctx_new = _layer_ab(x, ctx, mod_l, mod_c, norm1_g[i], ab_w_in[j], dn_conv_w[j], dn_A_log[j],
                                   dn_dt_bias[j], dn_norm_g[j], s5_A_re[j], s5_A_im[j], s5_log_dt[j],
                                   s5_B_re[j], s5_B_im[j], s5_C_re[j], s5_C_im[j], s5_D[j], s5_glu_w[j],
                                   s5_glu_b[j], ab_w_out[j])
        else:
            x = _layer_mla(x, ctx, mod_l, mod_c, norm1_g[i], mla_w_in[j], mla_q_norm_g[j], mla_w_q_up[j],
                           mla_kv_norm_g[j], mla_w_kv_up[j], mla_w_out[j], not last)
            ctx_new = None
        x = _moe(x, mod_l, norm2_g[i], router_w, router_bias, wg, wu, wd, i * n_exp,
                 final_norm_g if last else None)
        if not last:
            ctx_flat = _moe(ctx_new.reshape(1, B * n_ctx, D), mod_c, norm2_g[i], router_w, router_bias,
                            wg, wu, wd, i * n_exp, None)
            ctx = ctx_flat.reshape(B, n_ctx, D)
    return x
```

```python
import functools
import math

import jax
import jax.numpy as jnp
from jax import lax
from jax.experimental import pallas as pl
from jax.experimental.pallas import tpu as pltpu
from jax.experimental.pallas import tpu_sc as plsc

F32 = jnp.float32
BF16 = jnp.bfloat16
HIGHEST = lax.Precision.HIGHEST

NORM_EPS = 1e-6
GRID_W = 64
ROPE_THETA = 10000.0

DN_HEADS = 4
DN_DK = 128
DN_DV = 128
DN_CONV = 5
DN_CHUNK = 64
DN_W = DN_HEADS * DN_DK

S5_WIDTH = 512
S5_GROUP = 16
S5_GROUPS = 32
S5_STATE = 64
S5_NSTATE = S5_GROUPS * S5_STATE
S5_GBLK = 8
S5_NBLK = S5_GROUPS // S5_GBLK
S5_SBLK = S5_GBLK * S5_STATE

MLA_HEADS = 8
MLA_Q_RANK = 384
MLA_KV_RANK = 256
MLA_NOPE = 128
MLA_ROPE = 64
MLA_V = 128
MLA_Q_SCALE = (MLA_NOPE + MLA_ROPE) ** -0.5 * math.log2(math.e)

N_EXPERTS = 16
N_EXPERT_GROUPS = 4
EXPERTS_PER_GROUP = 4
D_EXPERT = 512
PAIRS_PER_GROUP = EXPERTS_PER_GROUP * (EXPERTS_PER_GROUP - 1) // 2
MOE_TILE = 1024

N_PLANES = 4
SC_CORES = 2
SC_WORKERS = 32
SC_CHUNK_ROWS = 128

LANES = 128
MLA_QK = MLA_NOPE + LANES
VMEM_LIMIT_BYTES = 56 * 1024 * 1024


def _tile(n, pref):
    t = min(pref, n)
    while n % t or t % 8:
        t -= 1
    return t


def _cparams(*sem):
    return pltpu.CompilerParams(dimension_semantics=sem, vmem_limit_bytes=VMEM_LIMIT_BYTES)


def _silu(x):
    return x * jax.nn.sigmoid(x)


def _softplus(x):
    return jnp.maximum(x, 0.0) + jnp.log(1.0 + jnp.exp(-jnp.abs(x)))


def _gelu_tanh(x):
    return 0.5 * x * (1.0 + jnp.tanh(math.sqrt(2.0 / math.pi) * (x + 0.044715 * (x * x * x))))


def _rms(x):
    return x * lax.rsqrt(jnp.mean(x * x, axis=-1, keepdims=True) + NORM_EPS)


def _dot(a, b):
    return jnp.dot(a, b, preferred_element_type=F32)


def _dot_nt(a, b, precision=None):
    return lax.dot_general(a, b, (((1,), (1,)), ((), ())), preferred_element_type=F32,
                           precision=precision)


def _dot_tn(a, b):
    return lax.dot_general(a, b, (((0,), (0,)), ((), ())), preferred_element_type=F32)


def _ada_body(c_ref, w_ref, b_ref, o_ref):
    c = c_ref[...]
    o_ref[0] = _dot(_silu(c).astype(BF16), w_ref[0].astype(BF16)) + b_ref[0]


def _adaln_all(cond, ada_w, ada_b):
    L, D, D6 = ada_w.shape
    R = cond.shape[0]
    tn = 1536
    return pl.pallas_call(
        _ada_body,
        grid=(L, D6 // tn),
        in_specs=[pl.BlockSpec((R, D), lambda l, j: (0, 0)),
                  pl.BlockSpec((1, D, tn), lambda l, j: (l, 0, j)),
                  pl.BlockSpec((1, 1, tn), lambda l, j: (l, 0, j))],
        out_specs=pl.BlockSpec((1, R, tn), lambda l, j: (l, 0, j)),
        out_shape=jax.ShapeDtypeStruct((L, R, D6), F32),
        compiler_params=_cparams("parallel", "parallel"),
        name="adaln",
    )(cond, ada_w, ada_b.reshape(L, 1, D6))


def _modmm_body(x_ref, mod_ref, g_ref, *refs, kinds, shift_row):
    n_out = len(kinds)
    w_refs, o_refs = refs[:n_out], refs[n_out:]
    h = _rms(x_ref[0]) * g_ref[...]
    h = h * (1.0 + mod_ref[0, shift_row + 1:shift_row + 2, :]) + mod_ref[0, shift_row:shift_row + 1, :]
    hb = h.astype(BF16)
    for w_ref, o_ref, kind in zip(w_refs, o_refs, kinds):
        if kind == "t":
            o_ref[0] = _dot_nt(w_ref[...], hb).astype(o_ref.dtype)
        else:
            o_ref[0] = _dot(hb, w_ref[...]).astype(o_ref.dtype)


def _modmm(x, mod, g, ws, out_dtypes, kinds, shift_row, tm):
    Bx, T, D = x.shape
    tm = _tile(T, tm)
    per_batch_mod = mod.shape[0] != 1
    in_specs = [pl.BlockSpec((1, tm, D), lambda b, i: (b, i, 0)),
                pl.BlockSpec((1, 6, D), (lambda b, i: (b, 0, 0)) if per_batch_mod else (lambda b, i: (0, 0, 0))),
                pl.BlockSpec((1, D), lambda b, i: (0, 0))]
    out_specs, out_shape = [], []
    for w, dt, kind in zip(ws, out_dtypes, kinds):
        in_specs.append(pl.BlockSpec(w.shape, lambda b, i: (0, 0)))
        if kind == "t":
            n = w.shape[0]
            out_specs.append(pl.BlockSpec((1, n, tm), lambda b, i: (b, 0, i)))
            out_shape.append(jax.ShapeDtypeStruct((Bx, n, T), dt))
        else:
            n = w.shape[1]
            out_specs.append(pl.BlockSpec((1, tm, n), lambda b, i: (b, i, 0)))
            out_shape.append(jax.ShapeDtypeStruct((Bx, T, n), dt))
    return pl.pallas_call(
        functools.partial(_modmm_body, kinds=tuple(kinds), shift_row=shift_row),
        grid=(Bx, T // tm),
        in_specs=in_specs, out_specs=out_specs, out_shape=out_shape,
        compiler_params=_cparams("parallel", "parallel"),
        name="modmm",
    )(x, mod, g.reshape(1, D), *ws)


def _dn_prep_body(x_ref, xp_ref, xn_ref, w_ref, o_ref, buf, *, tm, nt):
    i = pl.program_id(1)
    j = pl.program_id(2)
    buf[0:8, :] = jnp.where(i == 0, 0.0, xp_ref[0])
    buf[8:8 + tm, :] = x_ref[0]
    buf[8 + tm:16 + tm, :] = jnp.where(i == nt - 1, 0.0, xn_ref[0])
    acc = buf[pl.ds(8 - DN_CONV // 2, tm), :] * w_ref[0:1, :]
    for kk in range(1, DN_CONV):
        acc = acc + buf[pl.ds(8 - DN_CONV // 2 + kk, tm), :] * w_ref[kk:kk + 1, :]
    y = _silu(acc)
    q_scale = jnp.where(j == 0, DN_DK ** -0.5, 1.0)
    for h in range(DN_HEADS):
        yh = y[:, h * DN_DK:(h + 1) * DN_DK]
        r = lax.rsqrt(jnp.sum(yh * yh, axis=-1, keepdims=True) + NORM_EPS) * q_scale
        o_ref[0, :, h * DN_DK:(h + 1) * DN_DK] = yh * jnp.where(j == 2, 1.0, r)


def _dn_prep(qkv, conv_w, tm):
    B, T, W3 = qkv.shape
    tm = _tile(T, tm)
    nt = T // tm
    r8 = tm // 8
    return pl.pallas_call(
        functools.partial(_dn_prep_body, tm=tm, nt=nt),
        grid=(B, nt, 3),
        in_specs=[pl.BlockSpec((1, tm, DN_W), lambda b, i, j: (b, i, j)),
                  pl.BlockSpec((1, 8, DN_W), lambda b, i, j: (b, jnp.maximum(i * r8 - 1, 0), j)),
                  pl.BlockSpec((1, 8, DN_W), lambda b, i, j: (b, jnp.minimum((i + 1) * r8, T // 8 - 1), j)),
                  pl.BlockSpec((DN_CONV, DN_W), lambda b, i, j: (0, j))],
        out_specs=pl.BlockSpec((1, tm, DN_W), lambda b, i, j: (b, i, j)),
        out_shape=jax.ShapeDtypeStruct((B, T, W3), F32),
        scratch_shapes=[pltpu.VMEM((tm + 16, DN_W), F32)],
        compiler_params=_cparams("parallel", "parallel", "parallel"),
        name="dn_prep",
    )(qkv, qkv, qkv, conv_w)


def _heads_to_lanes(cols, width):
    return jnp.concatenate([jnp.broadcast_to(c, (c.shape[0], width)) for c in cols], axis=1)


def _block_diag(x, nblk):
    C, W = x.shape
    w = W // nblk
    t = jnp.concatenate([x] * nblk, axis=0)
    rb = lax.broadcasted_iota(jnp.int32, t.shape, 0) // C
    cb = lax.broadcasted_iota(jnp.int32, t.shape, 1) // w
    return jnp.where(rb == cb, t, jnp.zeros_like(t))


def _dn_chunk_body(q_ref, k_ref, v_ref, ab_ref, abt_ref, arow_ref, drow_ref, acol_ref, dcol_ref,
                   uw_ref, qk_ref, gl_ref, *, G):
    C, H = DN_CHUNK, DN_HEADS
    Tg = G * C
    gl_ref[...] = jnp.zeros_like(gl_ref)
    ab = ab_ref[0]
    abt = abt_ref[0]
    g_all = -jnp.exp(arow_ref[...]) * _softplus(ab + drow_ref[...])
    gt_all = -jnp.exp(acol_ref[...]) * _softplus(abt + dcol_ref[...])
    beta_all = jax.nn.sigmoid(ab)
    pos_s = lax.broadcasted_iota(jnp.int32, (Tg, LANES), 0) % C
    pos_l = lax.broadcasted_iota(jnp.int32, (2 * H * 2, Tg), 1) % C
    gc_all, gct_all = g_all, gt_all
    s = 1
    while s < C:
        gc_all = gc_all + jnp.where(pos_s >= s, pltpu.roll(gc_all, s, 0), 0.0)
        gct_all = gct_all + jnp.where(pos_l >= s, pltpu.roll(gct_all, s, 1), 0.0)
        s *= 2

    ri = lax.broadcasted_iota(jnp.int32, (C, H * C), 0)
    cj = lax.broadcasted_iota(jnp.int32, (C, H * C), 1) % C
    eye_side = (ri == cj).astype(F32)

    chains = []
    for ci in range(G):
        rows = slice(ci * C, (ci + 1) * C)
        q = q_ref[0, rows, :]
        k = k_ref[0, rows, :]
        v = v_ref[0, rows, :]
        kbd = _block_diag(k.astype(BF16), H)
        g_c, gc_f, beta_c = g_all[rows], gc_all[rows], beta_all[rows]
        gt_c, gct_f = gt_all[:, rows], gct_all[:, rows]
        gtot = gc_f[C - 1:C, :]
        gtot_t = gct_f[:, C - 1:C]
        for d in range(2):
            if d == 0:
                gc, gct = gc_f, gct_f
                incl, strict = ri >= cj, ri > cj
            else:
                gc, gct = gtot - gc_f + g_c, gtot_t - gct_f + gt_c
                incl, strict = ri <= cj, ri < cj
            lanes = [d * H + h for h in range(H)]
            gcol = [gc[:, l:l + 1] for l in lanes]
            diff = _heads_to_lanes(gcol, C) - jnp.concatenate([gct[l:l + 1, :] for l in lanes], axis=1)
            decay = jnp.where(incl, jnp.exp(jnp.where(incl, diff, 0.0)), 0.0)
            beta_b = _heads_to_lanes([beta_c[:, 2 * H + l:2 * H + l + 1] for l in lanes], DN_DK)
            egc_b = _heads_to_lanes([jnp.exp(c) for c in gcol], DN_DK)
            ekd_b = _heads_to_lanes([jnp.exp(gtot[:, l:l + 1] - gc[:, l:l + 1]) for l in lanes], DN_DK)
            kb = k * beta_b
            a_low = jnp.where(strict, _dot_nt(kb.astype(BF16), kbd) * decay, 0.0)
            qk_ref[d, 0, rows, :] = jnp.where(incl, _dot_nt(q.astype(BF16), kbd) * decay, 0.0).astype(BF16)
            uw_ref[d, 0, rows, 2 * DN_W:3 * DN_W] = (q * egc_b).astype(BF16)
            uw_ref[d, 0, rows, 3 * DN_W:4 * DN_W] = (k * ekd_b).astype(BF16)
            gl_ref[d, 0, 0, ci:ci + 1, :] = jnp.exp(gtot)
            rhs = jnp.concatenate([v * beta_b, kb * egc_b], axis=1).astype(BF16)
            chains.append(dict(d=d, rows=rows, pw=-a_low, inv=eye_side - a_low, rhs=rhs))

    n_lvl = int(math.log2(C)) - 1
    for ch in chains:
        pwb = ch["pw"].astype(BF16)
        ch["pw"] = _dot(pwb, _block_diag(pwb, H))
    for lvl in range(1, n_lvl + 1):
        for ch in chains:
            pwb = ch["pw"].astype(BF16)
            pbd = _block_diag(pwb, H)
            if lvl < n_lvl:
                st = _dot(jnp.concatenate([ch["inv"].astype(BF16), pwb], axis=0), pbd)
                ch["inv"] = ch["inv"] + st[:C]
                ch["pw"] = st[C:]
            else:
                ch["inv"] = ch["inv"] + _dot(ch["inv"].astype(BF16), pbd)
    rb = lax.broadcasted_iota(jnp.int32, (H * C, 2 * DN_W), 0) // C
    cb = (lax.broadcasted_iota(jnp.int32, (H * C, 2 * DN_W), 1) // DN_DK) % H
    for ch in chains:
        rhs_bd = jnp.where(rb == cb, jnp.concatenate([ch["rhs"]] * H, axis=0), jnp.zeros((), BF16))
        sol = _dot(ch["inv"].astype(BF16), rhs_bd)
        uw_ref[ch["d"], 0, ch["rows"], 0:2 * DN_W] = sol.astype(BF16)


def _dn_chunk(qkv, ab, abt, a_log, dt_bias, G):
    B, T, _ = qkv.shape
    Tg = G * DN_CHUNK
    ns = T // Tg
    nl = 2 * DN_HEADS
    alog = a_log.reshape(-1)
    dtb = dt_bias.reshape(-1)
    arow = jnp.zeros((1, LANES), F32).at[0, :nl].set(alog)
    drow = jnp.zeros((1, LANES), F32).at[0, :nl].set(dtb)
    acol = jnp.zeros((2 * nl, 1), F32).at[:nl, 0].set(alog)
    dcol = jnp.zeros((2 * nl, 1), F32).at[:nl, 0].set(dtb)
    full = lambda *shape: pl.BlockSpec(shape, lambda b, i: (0,) * len(shape))
    return pl.pallas_call(
        functools.partial(_dn_chunk_body, G=G),
        grid=(B, ns),
        in_specs=[pl.BlockSpec((1, Tg, DN_W), lambda b, i: (b, i, 0)),
                  pl.BlockSpec((1, Tg, DN_W), lambda b, i: (b, i, 1)),
                  pl.BlockSpec((1, Tg, DN_W), lambda b, i: (b, i, 2)),
                  pl.BlockSpec((1, Tg, LANES), lambda b, i: (b, i, 0)),
                  pl.BlockSpec((1, 2 * nl, Tg), lambda b, i: (b, 0, i)),
                  full(1, LANES), full(1, LANES), full(2 * nl, 1), full(2 * nl, 1)],
        out_specs=[pl.BlockSpec((2, 1, Tg, 4 * DN_W), lambda b, i: (0, b, i, 0)),
                   pl.BlockSpec((2, 1, Tg, DN_HEADS * DN_CHUNK), lambda b, i: (0, b, i, 0)),
                   pl.BlockSpec((2, 1, 1, 8, LANES), lambda b, i: (0, b, i, 0, 0))],
        out_shape=[jax.ShapeDtypeStruct((2, B, T, 4 * DN_W), BF16),
                   jax.ShapeDtypeStruct((2, B, T, DN_HEADS * DN_CHUNK), BF16),
                   jax.ShapeDtypeStruct((2, B, ns, 8, LANES), F32)],
        compiler_params=_cparams("parallel", "parallel"),
        name="dn_chunk",
    )(qkv, qkv, qkv, ab, abt, arow, drow, acol, dcol)


def _dn_rec_body(uw_ref, qk_ref, gl_ref, s0_ref, o_ref, sout_ref, S_ref, *, bb, nch, G):
    C, H = DN_CHUNK, DN_HEADS
    d = pl.program_id(0)
    c = pl.program_id(2)

    @pl.when(c == 0)
    def _():
        S_ref[...] = s0_ref[...]

    r = (c + d * (nch - 1 - 2 * c)) % G
    heads = [(b, h) for b in range(bb) for h in range(H)]
    ts = []
    for b, h in heads:
        wq = jnp.concatenate([uw_ref[b, :, DN_W + h * DN_DK:DN_W + (h + 1) * DN_DK],
                              uw_ref[b, :, 2 * DN_W + h * DN_DK:2 * DN_W + (h + 1) * DN_DK]], axis=0)
        ts.append(_dot(wq, S_ref[b, h].astype(BF16)))
    for (b, h), t in zip(heads, ts):
        sl = slice(h * DN_DV, (h + 1) * DN_DV)
        v_new = (uw_ref[b, :, sl].astype(F32) - t[:C]).astype(BF16)
        o_ref[b, :, sl] = t[C:] + _dot(qk_ref[b, :, h * C:(h + 1) * C], v_new)
        gl_row = gl_ref[b, 0, pl.ds(r, 1), :]
        gl = jnp.where(d == 0, gl_row[:, h:h + 1], gl_row[:, H + h:H + h + 1])
        kd = uw_ref[b, :, 3 * DN_W + h * DN_DK:3 * DN_W + (h + 1) * DN_DK]
        S_ref[b, h] = S_ref[b, h] * gl + _dot_tn(kd, v_new)

    @pl.when(c == nch - 1)
    def _():
        sout_ref[...] = S_ref[...]


def _dn_rec(uw, qk, gl, s0, bb, G):
    _, B, T, _ = uw.shape
    C = DN_CHUNK
    nch = T // C

    def cidx(d, c):
        return c + d * (nch - 1 - 2 * c)

    s_spec = pl.BlockSpec((None, bb, DN_HEADS, DN_DK, DN_DV), lambda d, b, c: (d, b, 0, 0, 0))
    return pl.pallas_call(
        functools.partial(_dn_rec_body, bb=bb, nch=nch, G=G),
        grid=(2, B // bb, nch),
        in_specs=[pl.BlockSpec((None, bb, C, 4 * DN_W), lambda d, b, c: (d, b, cidx(d, c), 0)),
                  pl.BlockSpec((None, bb, C, DN_HEADS * C), lambda d, b, c: (d, b, cidx(d, c), 0)),
                  pl.BlockSpec((None, bb, 1, 8, LANES), lambda d, b, c: (d, b, cidx(d, c) // G, 0, 0)),
                  s_spec],
        out_specs=[pl.BlockSpec((None, bb, C, DN_W), lambda d, b, c: (d, b, cidx(d, c), 0)), s_spec],
        out_shape=[jax.ShapeDtypeStruct((2, B, T, DN_W), F32),
                   jax.ShapeDtypeStruct((2, B, DN_HEADS, DN_DK, DN_DV), F32)],
        scratch_shapes=[pltpu.VMEM((bb, DN_HEADS, DN_DK, DN_DV), F32)],
        compiler_params=_cparams("parallel", "parallel", "arbitrary"),
        name="dn_rec",
    )(uw, qk, gl, s0)


def _s5_param_body(are_ref, aim_ref, ldt_ref, bre_ref, bim_ref, lam_ref, bd_ref):
    a_re = are_ref[...]
    a_im = aim_ref[...]
    dt = jnp.exp(ldt_ref[...])
    mag = jnp.exp(a_re * dt)
    lam_re = mag * jnp.cos(a_im * dt)
    lam_im = mag * jnp.sin(a_im * dt)
    den = a_re * a_re + a_im * a_im
    nr = lam_re - 1.0
    ni = lam_im
    coef_re = (nr * a_re + ni * a_im) / den
    coef_im = (ni * a_re - nr * a_im) / den
    lam_ref[0] = lam_re
    lam_ref[1] = lam_im
    b_re = bre_ref[...]
    b_im = bim_ref[...]
    for d in range(2):
        cr = coef_re[d:d + 1, :]
        ci = coef_im[d:d + 1, :]
        bd_ref[d, 0] = cr * b_re - ci * b_im
        bd_ref[d, 1] = cr * b_im + ci * b_re


def _s5_params(A_re, A_im, log_dt, B_re, B_im):
    a_re = A_re.reshape(2, S5_NSTATE)
    a_im = A_im.reshape(2, S5_NSTATE)
    ldt = jnp.repeat(log_dt, S5_STATE, axis=1)
    b_re_t = jnp.transpose(B_re, (2, 0, 1)).reshape(S5_GROUP, S5_NSTATE)
    b_im_t = jnp.transpose(B_im, (2, 0, 1)).reshape(S5_GROUP, S5_NSTATE)
    lam, bd = pl.pallas_call(
        _s5_param_body,
        out_shape=[jax.ShapeDtypeStruct((2, 2, S5_NSTATE), F32),
                   jax.ShapeDtypeStruct((2, 2, S5_GROUP, S5_NSTATE), F32)],
        name="s5_params",
    )(a_re, a_im, ldt, b_re_t, b_im_t)
    return lam, bd


def _s5_block_weights(bd, C_re, C_im):
    eye = jnp.eye(S5_GBLK, dtype=F32)
    bd6 = bd.reshape(2, 2, S5_GROUP, S5_NBLK, S5_GBLK, S5_STATE)
    w = jnp.einsum('dchjmp,lm->djlhcmp', bd6, eye)
    w_drive = w.reshape(2, S5_NBLK, S5_GBLK * S5_GROUP, 2 * S5_SBLK).astype(BF16)
    cc = jnp.stack([C_re, -C_im], axis=0).reshape(2, S5_NBLK, S5_GBLK, S5_GROUP, S5_STATE)
    cm = jnp.einsum('cjmhp,lm->jcmplh', cc, eye)
    w_read = cm.reshape(S5_NBLK, 2, S5_SBLK, S5_GBLK * S5_GROUP).astype(BF16)
    return w_drive, w_read


def _s5_scan_body(u_ref, pin_ref, pout_ref, wd_ref, wr_ref, lam_ref, h0_ref, y_ref, hout_ref,
                  xre, xim, sre, sim, hst, ytb,
                  *, B, Tc, nch, lb):
    d = pl.program_id(0)
    c = pl.program_id(1)

    @pl.when(c == 0)
    def _():
        hst[...] = h0_ref[0]

    n = Tc * B
    ub = _dot(pin_ref[...], u_ref[...].reshape(n, S5_WIDTH).astype(BF16)).astype(BF16)
    for j in range(S5_NBLK):
        drv = _dot(ub[:, j * LANES:(j + 1) * LANES], wd_ref[0, j])
        xre[:, j * S5_SBLK:(j + 1) * S5_SBLK] = drv[:, :S5_SBLK]
        xim[:, j * S5_SBLK:(j + 1) * S5_SBLK] = drv[:, S5_SBLK:]

    for lbi in range(S5_NSTATE // lb):
        ls = slice(lbi * lb, (lbi + 1) * lb)
        lr = jnp.broadcast_to(lam_ref[0, 0, :, ls], (B, lb))
        li = jnp.broadcast_to(lam_ref[1, 0, :, ls], (B, lb))

        def step(s, carry):
            hr, hi = carry
            t = s + d * (Tc - 1 - 2 * s)
            r0 = pl.multiple_of(t * B, B)
            nr = lr * hr - li * hi + xre[pl.ds(r0, B), ls]
            ni = lr * hi + li * hr + xim[pl.ds(r0, B), ls]
            sre[pl.ds(r0, B), ls] = nr.astype(BF16)
            sim[pl.ds(r0, B), ls] = ni.astype(BF16)
            return nr, ni

        hr, hi = lax.fori_loop(0, Tc, step, (hst[0, :, ls], hst[1, :, ls]), unroll=4)
        hst[0, :, ls] = hr
        hst[1, :, ls] = hi

    for j in range(S5_NBLK):
        ss = slice(j * S5_SBLK, (j + 1) * S5_SBLK)
        y = _dot(sre[:, ss], wr_ref[j, 0]) + _dot(sim[:, ss], wr_ref[j, 1])
        ytb[:, j * LANES:(j + 1) * LANES] = y.astype(BF16)
    y_ref[0] = _dot(pout_ref[...], ytb[...]).astype(BF16).reshape(B, Tc, S5_WIDTH)

    @pl.when(c == nch - 1)
    def _():
        hout_ref[0] = hst[...]


def _s5_scan(u, w_drive, w_read, lam, h0, Tc):
    B, T, _ = u.shape
    Tc = _tile(T, Tc)
    nch = T // Tc
    lam4 = lam.reshape(2, 2, 1, S5_NSTATE)
    n = Tc * B
    r = jnp.arange(n, dtype=jnp.int32)
    p_in = ((r[:, None] // B == r[None, :] % Tc) & (r[:, None] % B == r[None, :] // Tc)).astype(BF16)
    p_out = p_in.T

    def cidx(d, c):
        return c + d * (nch - 1 - 2 * c)

    return pl.pallas_call(
        functools.partial(_s5_scan_body, B=B, Tc=Tc, nch=nch, lb=256),
        grid=(2, nch),
        in_specs=[pl.BlockSpec((B, Tc, S5_WIDTH), lambda d, c: (0, cidx(d, c), 0)),
                  pl.BlockSpec((n, n), lambda d, c: (0, 0)),
                  pl.BlockSpec((n, n), lambda d, c: (0, 0)),
                  pl.BlockSpec((1, S5_NBLK, LANES, 2 * S5_SBLK), lambda d, c: (d, 0, 0, 0)),
                  pl.BlockSpec((S5_NBLK, 2, S5_SBLK, LANES), lambda d, c: (0, 0, 0, 0)),
                  pl.BlockSpec((2, 1, 1, S5_NSTATE), lambda d, c: (0, d, 0, 0)),
                  pl.BlockSpec((1, 2, B, S5_NSTATE), lambda d, c: (d, 0, 0, 0))],
        out_specs=[pl.BlockSpec((1, B, Tc, S5_WIDTH), lambda d, c: (d, 0, cidx(d, c), 0)),
                   pl.BlockSpec((1, 2, B, S5_NSTATE), lambda d, c: (d, 0, 0, 0))],
        out_shape=[jax.ShapeDtypeStruct((2, B, T, S5_WIDTH), BF16),
                   jax.ShapeDtypeStruct((2, 2, B, S5_NSTATE), F32)],
        scratch_shapes=[pltpu.VMEM((n, S5_NSTATE), F32),
                        pltpu.VMEM((n, S5_NSTATE), F32),
                        pltpu.VMEM((n, S5_NSTATE), BF16),
                        pltpu.VMEM((n, S5_NSTATE), BF16),
                        pltpu.VMEM((2, B, S5_NSTATE), F32),
                        pltpu.VMEM((n, S5_WIDTH), BF16)],
        compiler_params=_cparams("parallel", "arbitrary"),
        name="s5_scan",
    )(u, p_in, p_out, w_drive, w_read, lam4, h0)


def _ab_out_body(x_ref, mod_ref, o_ref, z_ref, u_ref, y_ref, ng_ref, dsk_ref, gw_ref, gb_ref,
                 woa_ref, wob_ref, g2_ref, rw_ref, rb_ref, out_ref, tab_ref, cls_ref):
    o = o_ref[0, 0] + o_ref[1, 0]
    z = z_ref[0]
    parts = []
    for h in range(DN_HEADS):
        sl = slice(h * DN_DV, (h + 1) * DN_DV)
        parts.append(_rms(o[:, sl]) * ng_ref[...] * _silu(z[:, sl]))
    a_out = jnp.concatenate(parts, axis=1)
    y = y_ref[0, 0].astype(F32) + y_ref[1, 0].astype(F32) + dsk_ref[...] * u_ref[0]
    y = _gelu_tanh(y)
    b_out = y * jax.nn.sigmoid(_dot(y.astype(BF16), gw_ref[...]) + gb_ref[...])
    mix = _dot(a_out.astype(BF16), woa_ref[...]) + _dot(b_out.astype(BF16), wob_ref[...])
    r = x_ref[0] + mod_ref[0, 2:3, :] * mix
    out_ref[0] = r
    _route_tile(r, mod_ref, g2_ref, rw_ref, rb_ref, tab_ref, cls_ref)


def _ab_out(x, mod, o_dn, z, u, y_s5, dn_norm_g, d_skip, glu_w, glu_b, w_out, norm2_g, router_w, router_bias, tm):
    B, T, D = x.shape
    tm = _tile(T, tm)
    per_batch_mod = mod.shape[0] != 1
    full = lambda *shape: pl.BlockSpec(shape, lambda b, i: (0,) * len(shape))
    r_in, r_args, r_out, r_shape = _route_io(B, T, tm, norm2_g, router_w, router_bias)
    return pl.pallas_call(
        _ab_out_body,
        grid=(B, T // tm),
        in_specs=[pl.BlockSpec((1, tm, D), lambda b, i: (b, i, 0)),
                  pl.BlockSpec((1, 6, D), (lambda b, i: (b, 0, 0)) if per_batch_mod else (lambda b, i: (0, 0, 0))),
                  pl.BlockSpec((2, 1, tm, DN_W), lambda b, i: (0, b, i, 0)),
                  pl.BlockSpec((1, tm, DN_W), lambda b, i: (b, i, 0)),
                  pl.BlockSpec((1, tm, S5_WIDTH), lambda b, i: (b, i, 0)),
                  pl.BlockSpec((2, 1, tm, S5_WIDTH), lambda b, i: (0, b, i, 0)),
                  full(1, DN_DV), full(1, S5_WIDTH), full(S5_WIDTH, S5_WIDTH), full(1, S5_WIDTH),
                  full(DN_W, D), full(S5_WIDTH, D)] + r_in,
        out_specs=[pl.BlockSpec((1, tm, D), lambda b, i: (b, i, 0))] + r_out,
        out_shape=[jax.ShapeDtypeStruct((B, T, D), F32)] + r_shape,
        compiler_params=_cparams("parallel", "parallel"),
        name="ab_out",
    )(x, mod, o_dn, z, u, y_s5, dn_norm_g.reshape(1, DN_DV), d_skip.reshape(1, S5_WIDTH),
      glu_w.astype(BF16), glu_b.reshape(1, S5_WIDTH), w_out[:DN_W].astype(BF16), w_out[DN_W:].astype(BF16),
      *r_args)


def _route(h, rw_ref, rb_ref):
    logits = _dot_nt(rw_ref[...], h, precision=HIGHEST)
    scores = jax.nn.sigmoid(logits)
    choice = scores + rb_ref[...]
    rows = [choice[e:e + 1, :] for e in range(N_EXPERTS)]
    neg_inf = jnp.float32(-jnp.inf)
    gs = []
    for g in range(N_EXPERT_GROUPS):
        r = rows[g * EXPERTS_PER_GROUP:(g + 1) * EXPERTS_PER_GROUP]
        best = None
        for a in range(EXPERTS_PER_GROUP):
            for b in range(a + 1, EXPERTS_PER_GROUP):
                s = r[a] + r[b]
                best = s if best is None else jnp.maximum(best, s)
        gs.append(best)
    best_val = gs[0]
    best_g = jnp.zeros_like(best_val, dtype=jnp.int32)
    for g in range(1, N_EXPERT_GROUPS):
        better = gs[g] > best_val
        best_val = jnp.where(better, gs[g], best_val)
        best_g = jnp.where(better, g, best_g)
    masked = [jnp.where(best_g == e // EXPERTS_PER_GROUP, rows[e], neg_inf) for e in range(N_EXPERTS)]
    m1 = masked[0]
    for e in range(1, N_EXPERTS):
        m1 = jnp.maximum(m1, masked[e])
    i1 = jnp.full_like(best_g, N_EXPERTS)
    for e in reversed(range(N_EXPERTS)):
        i1 = jnp.where(masked[e] == m1, e, i1)
    rest = [jnp.where(i1 == e, neg_inf, masked[e]) for e in range(N_EXPERTS)]
    m2 = rest[0]
    for e in range(1, N_EXPERTS):
        m2 = jnp.maximum(m2, rest[e])
    i2 = jnp.full_like(best_g, N_EXPERTS)
    for e in reversed(range(N_EXPERTS)):
        i2 = jnp.where(rest[e] == m2, e, i2)
    eidx = lax.broadcasted_iota(jnp.int32, scores.shape, 0)
    sel1 = eidx == i1
    sel2 = eidx == i2
    w1 = jnp.sum(jnp.where(sel1, scores, 0.0), axis=0, keepdims=True)
    w2 = jnp.sum(jnp.where(sel2, scores, 0.0), axis=0, keepdims=True)
    inv = 1.0 / (w1 + w2)
    return jnp.where(sel1, w1 * inv, 0.0) + jnp.where(sel2, w2 * inv, 0.0), best_g, i1, i2


def _pack_planes(x):
    planes = []
    for c in range(N_PLANES):
        lo = x[:, 2 * c * LANES:(2 * c + 1) * LANES]
        hi = x[:, (2 * c + 1) * LANES:(2 * c + 2) * LANES]
        planes.append(lax.bitcast_convert_type(pltpu.pack_elementwise([lo, hi], packed_dtype=BF16), F32))
    return planes


def _unpack_planes(planes):
    parts = []
    for p in planes:
        w = lax.bitcast_convert_type(p, jnp.uint32)
        for idx in range(2):
            parts.append(pltpu.unpack_elementwise(w, index=idx, packed_dtype=BF16, unpacked_dtype=F32))
    return jnp.concatenate(parts, axis=1)


def _route_tile(x, mod_ref, g_ref, rw_ref, rb_ref, tab_ref, cls_ref):
    h = _rms(x) * g_ref[...]
    h = h * (1.0 + mod_ref[0, 4:5, :]) + mod_ref[0, 3:4, :]
    for c, plane in enumerate(_pack_planes(h)):
        tab_ref[c] = plane
    gates_t, best_g, i1, i2 = _route(h, rw_ref, rb_ref)
    rows = []
    for k in range(EXPERTS_PER_GROUP):
        gk = jnp.zeros_like(gates_t[0:1, :])
        for g in range(N_EXPERT_GROUPS):
            e = g * EXPERTS_PER_GROUP + k
            gk = jnp.where(best_g == g, gates_t[e:e + 1, :], gk)
        rows.append(gk)
    rows.append(jnp.zeros((LANES - EXPERTS_PER_GROUP, gates_t.shape[1]), F32))
    tab_ref[N_PLANES] = jnp.transpose(jnp.concatenate(rows, axis=0))
    lo = jnp.minimum(i1, i2) % EXPERTS_PER_GROUP
    hi = jnp.maximum(i1, i2) % EXPERTS_PER_GROUP
    pair = lo * (7 - lo) // 2 + (hi - lo - 1)
    cls_ref[...] = jnp.broadcast_to(best_g * PAIRS_PER_GROUP + pair, cls_ref.shape)


def _route_io(B, T, tm, norm_g, router_w, router_bias):
    D = norm_g.shape[0]
    nt = T // tm
    N = B * T
    full = lambda *shape: pl.BlockSpec(shape, lambda b, i: (0,) * len(shape))
    in_specs = [full(1, D), full(N_EXPERTS, D), full(N_EXPERTS, 1)]
    args = [norm_g.reshape(1, D), router_w.T, router_bias.reshape(N_EXPERTS, 1)]
    out_specs = [pl.BlockSpec((N_PLANES + 1, tm, LANES), lambda b, i: (0, b * nt + i, 0)),
                 pl.BlockSpec((8, tm), lambda b, i: (0, b * nt + i))]
    out_shape = [jax.ShapeDtypeStruct((N_PLANES + 1, N, LANES), F32), jax.ShapeDtypeStruct((8, N), jnp.int32)]
    return in_specs, args, out_specs, out_shape


def _sc_move_rows(table, idx, n_out, scatter):
    n_planes = table.shape[0]
    n = idx.shape[0]
    ch = SC_CHUNK_ROWS if n % (2 * SC_WORKERS * SC_CHUNK_ROWS) == 0 else SC_CHUNK_ROWS // 2
    n_chunks = n // (SC_WORKERS * ch)
    assert n_chunks * SC_WORKERS * ch == n and n_chunks % 2 == 0
    assert n_out == n if not scatter else table.shape[1] == n
    items = [(dj, c) for dj in range(2) for c in range(n_planes)]
    mesh = plsc.VectorSubcoreMesh(core_axis_name="c", subcore_axis_name="s")

    @functools.partial(
        pl.kernel, mesh=mesh,
        out_type=jax.ShapeDtypeStruct((n_planes, n_out, LANES), table.dtype),
        scratch_types=[pltpu.VMEM((n_chunks, ch), jnp.int32),
                       pltpu.VMEM((2, ch, LANES), table.dtype),
                       pltpu.SemaphoreType.DMA((2,)), pltpu.SemaphoreType.DMA((2,))])
    def move_kernel(table_hbm, idx_hbm, out_hbm, idx_v, rows_v, isem, osem):
        wid = lax.axis_index("s") * SC_CORES + lax.axis_index("c")
        wbase = wid * (n_chunks * ch)
        pltpu.sync_copy(idx_hbm.at[pl.ds(wid * n_chunks, n_chunks)], idx_v)

        def load(j, c, slot):
            rows = idx_v.at[j] if not scatter else pl.ds(wbase + j * ch, ch)
            return pltpu.make_async_copy(table_hbm.at[c].at[rows], rows_v.at[slot], isem.at[slot])

        def store(j, c, slot):
            rows = idx_v.at[j] if scatter else pl.ds(wbase + j * ch, ch)
            return pltpu.make_async_copy(rows_v.at[slot], out_hbm.at[c].at[rows], osem.at[slot])

        load(0, 0, 0).start()

        @pl.loop(0, n_chunks, step=2)
        def _(j):
            for it, (dj, c) in enumerate(items):
                s = it % 2
                load(j + dj, c, s).wait()

                def refill(it=it, s=s):
                    if it == 0:
                        @pl.when(j > 0)
                        def _():
                            store(j - 1, n_planes - 1, 1 - s).wait()
                    else:
                        store(j + items[it - 1][0], items[it - 1][1], 1 - s).wait()
                    if it + 1 < len(items):
                        load(j + items[it + 1][0], items[it + 1][1], 1 - s).start()
                    else:
                        load(j + 2, 0, 1 - s).start()

                if it + 1 < len(items):
                    refill()
                else:
                    pl.when(j + 2 < n_chunks)(refill)
                store(j + dj, c, s).start()

        store(n_chunks - 2 + items[-2][0], items[-2][1], 0).wait()
        store(n_chunks - 1, n_planes - 1, 1).wait()

    return move_kernel(table, idx.reshape(SC_WORKERS * n_chunks, ch))


FFN_LIVE, FFN_FIRST, FFN_LAST = 1, 2, 4


def _moe_ffn_body(st_ref, se_ref, sk_ref, sf_ref, tn_ref, xs_ref, wg_ref, wu_ref, wd_ref, ys_ref, xb, gcol, acc):
    s = pl.program_id(0)
    flags = sf_ref[s]
    k = sk_ref[s]

    @pl.when((flags & FFN_FIRST) != 0)
    def _():
        keep = lax.broadcasted_iota(jnp.int32, (xs_ref.shape[1], LANES), 0) < tn_ref[st_ref[s]]
        xb[...] = _unpack_planes([jnp.where(keep, xs_ref[c], 0.0) for c in range(N_PLANES)]).astype(BF16)
        gcol[...] = jnp.where(keep, xs_ref[N_PLANES], 0.0)
        acc[...] = jnp.zeros_like(acc)

    @pl.when((flags & FFN_LIVE) != 0)
    def _():
        hb = xb[...]
        act = _silu(_dot(hb, wg_ref[0].astype(BF16))) * _dot(hb, wu_ref[0].astype(BF16))
        lane = lax.broadcasted_iota(jnp.int32, gcol.shape, 1)
        ge = jnp.sum(jnp.where(lane == k, gcol[...], 0.0), axis=1, keepdims=True)
        acc[...] += _dot((act * ge).astype(BF16), wd_ref[0].astype(BF16))

    @pl.when((flags & FFN_LAST) != 0)
    def _():
        for c, plane in enumerate(_pack_planes(acc[...])):
            ys_ref[c] = plane


def _moe_ffn(step_tile, step_expert, step_k, step_flags, tile_rows, xs, wg, wu, wd, tm):
    P = xs.shape[1]
    D = 2 * N_PLANES * LANES
    n_steps = step_tile.shape[0]

    def wspec(shape):
        return pl.BlockSpec(shape, lambda s, st, se, sk, sf, tn: (se[s], 0, 0))

    return pl.pallas_call(
        _moe_ffn_body,
        grid_spec=pltpu.PrefetchScalarGridSpec(
            num_scalar_prefetch=5,
            grid=(n_steps,),
            in_specs=[pl.BlockSpec((N_PLANES + 1, tm, LANES), lambda s, st, se, sk, sf, tn: (0, st[s], 0)),
                      wspec((1, D, D_EXPERT)), wspec((1, D, D_EXPERT)), wspec((1, D_EXPERT, D))],
            out_specs=pl.BlockSpec((N_PLANES, tm, LANES), lambda s, st, se, sk, sf, tn: (0, st[s], 0)),
            scratch_shapes=[pltpu.VMEM((tm, D), BF16), pltpu.VMEM((tm, LANES), F32), pltpu.VMEM((tm, D), F32)]),
        out_shape=jax.ShapeDtypeStruct((N_PLANES, P, LANES), F32),
        compiler_params=_cparams("arbitrary"),
        name="moe_ffn",
    )(step_tile, step_expert, step_k, step_flags, tile_rows, xs, wg, wu, wd)


def _moe_combine_body(x_ref, mod_ref, y_ref, fg_ref, out_ref, *, final_norm):
    y = _unpack_planes([y_ref[c] for c in range(N_PLANES)])
    r = x_ref[0] + mod_ref[0, 5:6, :] * y
    if final_norm:
        r = _rms(r) * fg_ref[...]
    out_ref[0] = r


def _moe_combine(x, mod, y_tok, final_g, tm):
    Bx, T, D = x.shape
    tm = _tile(T, tm)
    nt = T // tm
    per_batch_mod = mod.shape[0] != 1
    final_norm = final_g is not None
    fg = (final_g if final_norm else jnp.ones((D,), F32)).reshape(1, D)
    return pl.pallas_call(
        functools.partial(_moe_combine_body, final_norm=final_norm),
        grid=(Bx, nt),
        in_specs=[pl.BlockSpec((1, tm, D), lambda b, i: (b, i, 0)),
                  pl.BlockSpec((1, 6, D), (lambda b, i: (b, 0, 0)) if per_batch_mod else (lambda b, i: (0, 0, 0))),
                  pl.BlockSpec((N_PLANES, tm, LANES), lambda b, i: (0, b * nt + i, 0)),
                  pl.BlockSpec((1, D), lambda b, i: (0, 0))],
        out_specs=pl.BlockSpec((1, tm, D), lambda b, i: (b, i, 0)),
        out_shape=jax.ShapeDtypeStruct((Bx, T, D), F32),
        compiler_params=_cparams("parallel", "parallel"),
        name="moe_combine",
    )(x, mod, y_tok, fg)


def _moe(x, mod, table, cls_rows, wg, wu, wd, expert_base, final_g):
    Bx, T, D = x.shape
    N = Bx * T
    tm = MOE_TILE
    G, K, NP = N_EXPERT_GROUPS, EXPERTS_PER_GROUP, PAIRS_PER_GROUP
    cls = cls_rows[0]
    onehot = (cls[:, None] == jnp.arange(G * NP, dtype=jnp.int32)[None, :]).astype(jnp.int32)
    csum = jnp.cumsum(onehot, axis=0)
    c_count = csum[-1]
    g_count = c_count.reshape(G, NP).sum(axis=1)
    g_padded = (g_count + tm - 1) // tm * tm
    g_end = jnp.cumsum(g_padded)
    g_start = g_end - g_padded
    in_group = jnp.cumsum(c_count.reshape(G, NP), axis=1) - c_count.reshape(G, NP)
    c_start = (g_start[:, None] + in_group).reshape(G * NP)
    rank = jnp.sum(csum * onehot, axis=1) - 1
    pos = (jnp.sum(c_start[None, :] * onehot, axis=1) + rank).astype(jnp.int32)
    P = N + G * tm
    nt = P // tm
    tile_start = jnp.arange(nt, dtype=jnp.int32) * tm
    tile_group = jnp.minimum(jnp.sum(tile_start[:, None] >= g_end[None, :], axis=1), G - 1)
    filled = (g_start + g_count)[tile_group]
    tile_rows = jnp.clip(filled - tile_start, 0, tm).astype(jnp.int32)
    overlap = ((c_start[None, :] < tile_start[:, None] + tm) & (c_start + c_count > tile_start[:, None])
               & (c_count[None, :] > 0)).astype(jnp.int32)
    pairs = [(a, b) for a in range(K) for b in range(a + 1, K)]
    member = jnp.array([[int(k in pairs[c % NP]) for k in range(K)] for c in range(G * NP)], jnp.int32)
    used = (jnp.dot(overlap, member) > 0).astype(jnp.int32)
    seen = jnp.cumsum(used, axis=1)
    first = used * (seen == 1)
    last = used * (seen == seen[:, -1:])
    n_steps = nt * K
    order = jnp.argsort(1 - used.reshape(n_steps), stable=True).astype(jnp.int32)
    n_used = jnp.sum(used)
    live = jnp.arange(n_steps, dtype=jnp.int32) < n_used
    src = jnp.where(live, order, order[jnp.maximum(n_used - 1, 0)])
    step_tile = src // K
    step_k = src % K
    step_expert = (expert_base + tile_group[step_tile] * K + step_k).astype(jnp.int32)
    step_flags = jnp.where(live, FFN_LIVE + FFN_FIRST * first.reshape(n_steps)[src]
                           + FFN_LAST * last.reshape(n_steps)[src], 0).astype(jnp.int32)
    xs = _sc_move_rows(table, pos, P, scatter=True)
    ys = _moe_ffn(step_tile, step_expert, step_k, step_flags, tile_rows, xs, wg, wu, wd, tm)
    y_tok = _sc_move_rows(ys, pos, N, scatter=False)
    return _moe_combine(x, mod, y_tok, final_g, 512)


def _mla_in_body(x_ref, mod_ref, g_ref, win_ref, qg_ref, wn_ref, wa_ref, wb_ref, kg_ref, wk_ref, wvt_ref,
                 cos_ref, sin_ref, *rest, with_q):
    k_ref, vt_ref = rest[-2:]
    h = _rms(x_ref[0]) * g_ref[...]
    h = h * (1.0 + mod_ref[0, 1:2, :]) + mod_ref[0, 0:1, :]
    low = _dot(h.astype(BF16), win_ref[...])
    cos = cos_ref[...]
    sin = sin_ref[...]
    c0 = MLA_Q_RANK + MLA_KV_RANK
    kvb = (_rms(low[:, MLA_Q_RANK:c0]) * kg_ref[...]).astype(BF16)
    kn = _dot(kvb, wk_ref[...]).astype(BF16)
    vt_ref[0] = _dot_nt(wvt_ref[...], kvb).astype(BF16)
    kr = (low[:, c0:c0 + LANES] * cos + low[:, c0 + LANES:c0 + 2 * LANES] * sin).astype(BF16)
    for hd in range(MLA_HEADS):
        k_ref[0, :, hd * MLA_QK:hd * MLA_QK + MLA_NOPE] = kn[:, hd * MLA_NOPE:(hd + 1) * MLA_NOPE]
        k_ref[0, :, hd * MLA_QK + MLA_NOPE:(hd + 1) * MLA_QK] = kr
    if with_q:
        q_ref = rest[-3]
        qb = (_rms(low[:, :MLA_Q_RANK]) * qg_ref[...]).astype(BF16)
        qn = _dot(qb, wn_ref[...]) * MLA_Q_SCALE
        ra = _dot(qb, wa_ref[...])
        rb = _dot(qb, wb_ref[...])
        lane = lax.broadcasted_iota(jnp.int32, cos.shape, 1)
        for hd in range(MLA_HEADS):
            p = hd // 2
            rot = (ra[:, p * LANES:(p + 1) * LANES] * cos + rb[:, p * LANES:(p + 1) * LANES] * sin) * MLA_Q_SCALE
            mine = (lane < MLA_ROPE) if hd % 2 == 0 else (lane >= MLA_ROPE)
            q_ref[0, :, hd * MLA_QK:hd * MLA_QK + MLA_NOPE] = qn[:, hd * MLA_NOPE:(hd + 1) * MLA_NOPE].astype(BF16)
            q_ref[0, :, hd * MLA_QK + MLA_NOPE:(hd + 1) * MLA_QK] = jnp.where(mine, rot, 0.0).astype(BF16)


def _mla_in(x, mod, norm_g, weights, cos_t, sin_t, tm, n_keys, key_block0, kv_bufs):
    B, T, D = x.shape
    W = MLA_HEADS * MLA_V
    with_q = kv_bufs is not None
    per_batch_mod = mod.shape[0] != 1
    full = lambda a: pl.BlockSpec(a.shape, lambda b, i: (0,) * a.ndim)
    in_specs = [pl.BlockSpec((1, tm, D), lambda b, i: (b, i, 0)),
                pl.BlockSpec((1, 6, D), (lambda b, i: (b, 0, 0)) if per_batch_mod else (lambda b, i: (0, 0, 0))),
                pl.BlockSpec((1, D), lambda b, i: (0, 0))]
    in_specs += [full(w) for w in weights]
    in_specs += [pl.BlockSpec((tm, LANES), lambda b, i: (i, 0)), pl.BlockSpec((tm, LANES), lambda b, i: (i, 0))]
    args = [x, mod, norm_g.reshape(1, D), *weights, cos_t, sin_t]
    out_specs = [pl.BlockSpec((1, tm, MLA_HEADS * MLA_QK), lambda b, i: (b, i + key_block0, 0)),
                 pl.BlockSpec((1, W, tm), lambda b, i: (b, 0, i + key_block0))]
    out_shape = [jax.ShapeDtypeStruct((B, n_keys, MLA_HEADS * MLA_QK), BF16),
                 jax.ShapeDtypeStruct((B, W, n_keys), BF16)]
    aliases = {}
    if with_q:
        in_specs += [pl.BlockSpec(memory_space=pl.ANY)] * 2
        aliases = {len(args): 1, len(args) + 1: 2}
        args += list(kv_bufs)
        out_specs.insert(0, pl.BlockSpec((1, tm, MLA_HEADS * MLA_QK), lambda b, i: (b, i, 0)))
        out_shape.insert(0, jax.ShapeDtypeStruct((B, T, MLA_HEADS * MLA_QK), BF16))
    return pl.pallas_call(
        functools.partial(_mla_in_body, with_q=with_q),
        grid=(B, T // tm),
        in_specs=in_specs, out_specs=out_specs, out_shape=out_shape,
        input_output_aliases=aliases,
        compiler_params=_cparams("parallel", "parallel"),
        name="mla_in",
    )(*args)


def _mla_attn_body(q_ref, k_ref, v_ref, o_ref):
    for h in range(MLA_HEADS):
        qk = slice(h * MLA_QK, (h + 1) * MLA_QK)
        sl = slice(h * MLA_V, (h + 1) * MLA_V)
        s = _dot_nt(q_ref[0, :, qk], k_ref[0, :, qk])
        m = jnp.max(s, axis=-1, keepdims=True)
        e = jnp.exp2(s - m)
        l = jnp.sum(e, axis=-1, keepdims=True)
        ot = _dot_nt(v_ref[0, sl, :], e.astype(BF16))
        o_ref[0, :, sl] = (jnp.transpose(ot) / l).astype(o_ref.dtype)


def _mla_attn(q, k, v, tq):
    B, T, WQ = q.shape
    W, Tk = v.shape[1], v.shape[2]
    tq = _tile(T, tq)
    return pl.pallas_call(
        _mla_attn_body,
        grid=(B, T // tq),
        in_specs=[pl.BlockSpec((1, tq, WQ), lambda b, i: (b, i, 0)),
                  pl.BlockSpec((1, Tk, WQ), lambda b, i: (b, 0, 0)),
                  pl.BlockSpec((1, W, Tk), lambda b, i: (b, 0, 0))],
        out_specs=pl.BlockSpec((1, tq, W), lambda b, i: (b, i, 0)),
        out_shape=jax.ShapeDtypeStruct((B, T, W), BF16),
        compiler_params=_cparams("parallel", "arbitrary"),
        name="mla_attn",
    )(q, k, v)


def _proj_res_body(x_ref, mod_ref, a_ref, w_ref, g2_ref, rw_ref, rb_ref, out_ref, tab_ref, cls_ref):
    r = x_ref[0] + mod_ref[0, 2:3, :] * _dot(a_ref[0], w_ref[...])
    out_ref[0] = r
    _route_tile(r, mod_ref, g2_ref, rw_ref, rb_ref, tab_ref, cls_ref)


def _proj_res(x, mod, a, w, norm2_g, router_w, router_bias, tm):
    B, T, D = x.shape
    K = a.shape[2]
    tm = _tile(T, tm)
    per_batch_mod = mod.shape[0] != 1
    r_in, r_args, r_out, r_shape = _route_io(B, T, tm, norm2_g, router_w, router_bias)
    return pl.pallas_call(
        _proj_res_body,
        grid=(B, T // tm),
        in_specs=[pl.BlockSpec((1, tm, D), lambda b, i: (b, i, 0)),
                  pl.BlockSpec((1, 6, D), (lambda b, i: (b, 0, 0)) if per_batch_mod else (lambda b, i: (0, 0, 0))),
                  pl.BlockSpec((1, tm, K), lambda b, i: (b, i, 0)),
                  pl.BlockSpec((K, D), lambda b, i: (0, 0))] + r_in,
        out_specs=[pl.BlockSpec((1, tm, D), lambda b, i: (b, i, 0))] + r_out,
        out_shape=[jax.ShapeDtypeStruct((B, T, D), F32)] + r_shape,
        compiler_params=_cparams("parallel", "parallel"),
        name="proj_res",
    )(x, mod, a, w, *r_args)


def _rope_tables(T, n_ctx):
    rows = T // GRID_W
    row = jnp.repeat(jnp.arange(rows, dtype=F32), GRID_W)
    col = jnp.tile(jnp.arange(GRID_W, dtype=F32), rows)
    n_freq = MLA_ROPE // 4
    inv = ROPE_THETA ** (-jnp.arange(n_freq, dtype=F32) / n_freq)
    ang = jnp.concatenate([row[:, None] * inv, col[:, None] * inv], axis=-1)
    cos, sin = jnp.cos(ang), jnp.sin(ang)
    cos_t = jnp.concatenate([cos, cos, cos, cos], axis=-1)
    sin_t = jnp.concatenate([-sin, sin, -sin, sin], axis=-1)
    return cos_t, sin_t, jnp.ones((n_ctx, LANES), F32), jnp.zeros((n_ctx, LANES), F32)


def _layer_ab(x, ctx, mod_l, mod_c, norm1_g, w_in, conv_w, a_log, dt_bias, dn_norm_g, A_re, A_im, log_dt,
              B_re, B_im, C_re, C_im, D_skip, glu_w, glu_b, w_out, norm2_g, router_w, router_bias):
    B, T, D = x.shape
    Tc = ctx.shape[1]
    q0, k0, v0, z0, a0, b0, u0 = 0, 512, 1024, 1536, 2048, 2056, 2064
    w_qkv = w_in[:, q0:z0].astype(BF16)
    w_z = w_in[:, z0:a0].astype(BF16)
    w_ab = jnp.zeros((D, LANES), F32).at[:, :16].set(w_in[:, a0:u0]).astype(BF16)
    w_u = w_in[:, u0:].astype(BF16)
    w_abt = w_in[:, a0:u0].T.astype(BF16)
    ws = [w_qkv, w_z, w_ab, w_abt, w_u]
    dts = [F32] * 5
    kinds = ["n", "n", "n", "t", "n"]
    dn_group = 4

    lam, bd = _s5_params(A_re, A_im, log_dt, B_re, B_im)
    w_drive, w_read = _s5_block_weights(bd, C_re, C_im)

    streams = []
    dn_state = jnp.zeros((2, B, DN_HEADS, DN_DK, DN_DV), F32)
    s5_state = jnp.zeros((2, 2, B, S5_NSTATE), F32)
    for xs, mod in ((ctx, mod_c), (x, mod_l)):
        qkv, z, ab, abt, u = _modmm(xs, mod, norm1_g, ws, dts, kinds, 0, 512)
        qkv = _dn_prep(qkv, conv_w, 512)
        uw, qk, gl = _dn_chunk(qkv, ab, abt, a_log, dt_bias, dn_group)
        o_dn, dn_state = _dn_rec(uw, qk, gl, dn_state, 8, dn_group)
        y_s5, s5_state = _s5_scan(u, w_drive, w_read, lam, s5_state, 32)
        streams.append(_ab_out(xs, mod, o_dn, z, u, y_s5, dn_norm_g, D_skip.reshape(-1), glu_w, glu_b,
                               w_out, norm2_g, router_w, router_bias, 512))
    return streams[1], streams[0]


def _layer_mla(x, ctx, mod_l, mod_c, norm1_g, w_in, q_norm_g, w_q_up, kv_norm_g, w_kv_up, w_out,
               norm2_g, router_w, router_bias, need_ctx):
    assert not need_ctx, "context attention output is only needed when a later layer follows"
    B, T, D = x.shape
    n_ctx = ctx.shape[1]
    qr, kvr = MLA_Q_RANK, MLA_KV_RANK
    half = MLA_ROPE // 2
    wk1 = w_in[:, qr + kvr:qr + kvr + half]
    wk2 = w_in[:, qr + kvr + half:]
    w_low = jnp.concatenate([w_in[:, :qr + kvr], wk1, wk2, wk1, wk2, wk2, wk1, wk2, wk1], axis=1).astype(BF16)

    wq = w_q_up.reshape(qr, MLA_HEADS, MLA_NOPE + MLA_ROPE)
    wq_n = wq[:, :, :MLA_NOPE].reshape(qr, MLA_HEADS * MLA_NOPE).astype(BF16)
    x1 = wq[:, :, MLA_NOPE:MLA_NOPE + half]
    x2 = wq[:, :, MLA_NOPE + half:]
    wq_a = jnp.concatenate([x1, x2], axis=2).reshape(qr, MLA_HEADS * MLA_ROPE).astype(BF16)
    wq_b = jnp.concatenate([x2, x1], axis=2).reshape(qr, MLA_HEADS * MLA_ROPE).astype(BF16)
    wkv = w_kv_up.reshape(kvr, MLA_HEADS, MLA_NOPE + MLA_V)
    wk_n = wkv[:, :, :MLA_NOPE].reshape(kvr, MLA_HEADS * MLA_NOPE).astype(BF16)
    wv_t = wkv[:, :, MLA_NOPE:].reshape(kvr, MLA_HEADS * MLA_V).T.astype(BF16)

    tm = _tile(math.gcd(T, n_ctx), 256)
    weights = [w_low, q_norm_g.reshape(1, qr), wq_n, wq_a, wq_b, kv_norm_g.reshape(1, kvr), wk_n, wv_t]
    cos_t, sin_t, cos_c, sin_c = _rope_tables(T, n_ctx)
    kv_ctx = _mla_in(ctx, mod_c, norm1_g, weights, cos_c, sin_c, tm, n_ctx + T, 0, None)
    q, k, vt = _mla_in(x, mod_l, norm1_g, weights, cos_t, sin_t, tm, n_ctx + T, n_ctx // tm, kv_ctx)
    o = _mla_attn(q, k, vt, 256)
    return _proj_res(x, mod_l, o, w_out.astype(BF16), norm2_g, router_w, router_bias, 512)


def kernel(x, c, ctx, c_ctx, ada_w, ada_b, norm1_g, norm2_g, ab_w_in, dn_conv_w, dn_A_log, dn_dt_bias, dn_norm_g, s5_A_re, s5_A_im, s5_log_dt, s5_B_re, s5_B_im, s5_C_re, s5_C_im, s5_D, s5_glu_w, s5_glu_b, ab_w_out, mla_w_in, mla_q_norm_g, mla_w_q_up, mla_kv_norm_g, mla_w_kv_up, mla_w_out, router_w, router_bias, moe_w_gate, moe_w_up, moe_w_down, final_norm_g):
    B, T, D = x.shape
    n_ctx = ctx.shape[1]
    depth = ada_w.shape[0]
    n_cond = -(-(B + 1) // 8) * 8
    cond = jnp.zeros((n_cond, D), F32).at[:B].set(c).at[B].set(c_ctx)
    mods = _adaln_all(cond, ada_w, ada_b).reshape(depth, n_cond, 6, D)
    n_exp = moe_w_gate.shape[1]
    wg = moe_w_gate.reshape((depth * n_exp,) + moe_w_gate.shape[2:])
    wu = moe_w_up.reshape((depth * n_exp,) + moe_w_up.shape[2:])
    wd = moe_w_down.reshape((depth * n_exp,) + moe_w_down.shape[2:])
    for i in range(depth):
        last = i == depth - 1
        j = i // 2
        mod_l = mods[i, :B]
        mod_c = mods[i, B:B + 1]
        route = (norm2_g[i], router_w, router_bias)
        if i % 2 == 0:
            lat, ctx_new = _layer_ab(x, ctx, mod_l, mod_c, norm1_g[i], ab_w_in[j], dn_conv_w[j], dn_A_log[j],
                                     dn_dt_bias[j], dn_norm_g[j], s5_A_re[j], s5_A_im[j], s5_log_dt[j],
                                     s5_B_re[j], s5_B_im[j], s5_C_re[j], s5_C_im[j], s5_D[j], s5_glu_w[j],
                                     s5_glu_b[j], ab_w_out[j], *route)
        else:
            lat = _layer_mla(x, ctx, mod_l, mod_c, norm1_g[i], mla_w_in[j], mla_q_norm_g[j], mla_w_q_up[j],
                             mla_kv_norm_g[j], mla_w_kv_up[j], mla_w_out[j], *route, not last)
            ctx_new = None
        x = _moe(lat[0], mod_l, lat[1], lat[2], wg, wu, wd, i * n_exp, final_norm_g if last else None)
        if not last:
            ctx_flat = _moe(ctx_new[0].reshape(1, B * n_ctx, D), mod_c, ctx_new[1], ctx_new[2],
                            wg, wu, wd, i * n_exp, None)
            ctx = ctx_flat.reshape(B, n_ctx, D)
    return x
```

```python
import functools
import math

import jax
import jax.numpy as jnp
from jax import lax
from jax.experimental import pallas as pl
from jax.experimental.pallas import tpu as pltpu
from jax.experimental.pallas import tpu_sc as plsc

F32 = jnp.float32
BF16 = jnp.bfloat16
HIGHEST = lax.Precision.HIGHEST

NORM_EPS = 1e-6
GRID_W = 64
ROPE_THETA = 10000.0

DN_HEADS = 4
DN_DK = 128
DN_DV = 128
DN_CONV = 5
DN_CHUNK = 64
DN_W = DN_HEADS * DN_DK

S5_WIDTH = 512
S5_GROUP = 16
S5_GROUPS = 32
S5_STATE = 64
S5_NSTATE = S5_GROUPS * S5_STATE
S5_GBLK = 8
S5_NBLK = S5_GROUPS // S5_GBLK
S5_SBLK = S5_GBLK * S5_STATE

MLA_HEADS = 8
MLA_Q_RANK = 384
MLA_KV_RANK = 256
MLA_NOPE = 128
MLA_ROPE = 64
MLA_V = 128
MLA_Q_SCALE = (MLA_NOPE + MLA_ROPE) ** -0.5 * math.log2(math.e)

N_EXPERTS = 16
N_EXPERT_GROUPS = 4
EXPERTS_PER_GROUP = 4
D_EXPERT = 512
PAIRS_PER_GROUP = EXPERTS_PER_GROUP * (EXPERTS_PER_GROUP - 1) // 2
MOE_TILE = 1024

N_PLANES = 4
SC_CORES = 2
SC_WORKERS = 32
SC_CHUNK_ROWS = 128

LANES = 128
MLA_QK = MLA_NOPE + LANES
VMEM_LIMIT_BYTES = 56 * 1024 * 1024


def _tile(n, pref):
    t = min(pref, n)
    while n % t or t % 8:
        t -= 1
    return t


def _cparams(*sem):
    return pltpu.CompilerParams(dimension_semantics=sem, vmem_limit_bytes=VMEM_LIMIT_BYTES)


def _silu(x):
    return x * jax.nn.sigmoid(x)


def _softplus(x):
    return jnp.maximum(x, 0.0) + jnp.log(1.0 + jnp.exp(-jnp.abs(x)))


def _gelu_tanh(x):
    return 0.5 * x * (1.0 + jnp.tanh(math.sqrt(2.0 / math.pi) * (x + 0.044715 * (x * x * x))))


def _rms(x):
    return x * lax.rsqrt(jnp.mean(x * x, axis=-1, keepdims=True) + NORM_EPS)


def _dot(a, b):
    return jnp.dot(a, b, preferred_element_type=F32)


def _dot_nt(a, b, precision=None):
    return lax.dot_general(a, b, (((1,), (1,)), ((), ())), preferred_element_type=F32,
                           precision=precision)


def _dot_tn(a, b):
    return lax.dot_general(a, b, (((0,), (0,)), ((), ())), preferred_element_type=F32)


def _ada_body(c_ref, w_ref, b_ref, o_ref):
    c = c_ref[...]
    o_ref[0] = _dot(_silu(c).astype(BF16), w_ref[0].astype(BF16)) + b_ref[0]


def _adaln_all(cond, ada_w, ada_b):
    L, D, D6 = ada_w.shape
    R = cond.shape[0]
    tn = 1536
    return pl.pallas_call(
        _ada_body,
        grid=(L, D6 // tn),
        in_specs=[pl.BlockSpec((R, D), lambda l, j: (0, 0)),
                  pl.BlockSpec((1, D, tn), lambda l, j: (l, 0, j)),
                  pl.BlockSpec((1, 1, tn), lambda l, j: (l, 0, j))],
        out_specs=pl.BlockSpec((1, R, tn), lambda l, j: (l, 0, j)),
        out_shape=jax.ShapeDtypeStruct((L, R, D6), F32),
        compiler_params=_cparams("parallel", "parallel"),
        name="adaln",
    )(cond, ada_w, ada_b.reshape(L, 1, D6))


def _modmm_body(x_ref, mod_ref, g_ref, *refs, kinds, shift_row):
    n_out = len(kinds)
    w_refs, o_refs = refs[:n_out], refs[n_out:]
    h = _rms(x_ref[0]) * g_ref[...]
    h = h * (1.0 + mod_ref[0, shift_row + 1:shift_row + 2, :]) + mod_ref[0, shift_row:shift_row + 1, :]
    hb = h.astype(BF16)
    for w_ref, o_ref, kind in zip(w_refs, o_refs, kinds):
        if kind == "t":
            o_ref[0] = _dot_nt(w_ref[...], hb).astype(o_ref.dtype)
        else:
            o_ref[0] = _dot(hb, w_ref[...]).astype(o_ref.dtype)


def _modmm(x, mod, g, ws, out_dtypes, kinds, shift_row, tm):
    Bx, T, D = x.shape
    tm = _tile(T, tm)
    per_batch_mod = mod.shape[0] != 1
    in_specs = [pl.BlockSpec((1, tm, D), lambda b, i: (b, i, 0)),
                pl.BlockSpec((1, 6, D), (lambda b, i: (b, 0, 0)) if per_batch_mod else (lambda b, i: (0, 0, 0))),
                pl.BlockSpec((1, D), lambda b, i: (0, 0))]
    out_specs, out_shape = [], []
    for w, dt, kind in zip(ws, out_dtypes, kinds):
        in_specs.append(pl.BlockSpec(w.shape, lambda b, i: (0, 0)))
        if kind == "t":
            n = w.shape[0]
            out_specs.append(pl.BlockSpec((1, n, tm), lambda b, i: (b, 0, i)))
            out_shape.append(jax.ShapeDtypeStruct((Bx, n, T), dt))
        else:
            n = w.shape[1]
            out_specs.append(pl.BlockSpec((1, tm, n), lambda b, i: (b, i, 0)))
            out_shape.append(jax.ShapeDtypeStruct((Bx, T, n), dt))
    return pl.pallas_call(
        functools.partial(_modmm_body, kinds=tuple(kinds), shift_row=shift_row),
        grid=(Bx, T // tm),
        in_specs=in_specs, out_specs=out_specs, out_shape=out_shape,
        compiler_params=_cparams("parallel", "parallel"),
        name="modmm",
    )(x, mod, g.reshape(1, D), *ws)


def _dn_prep_body(x_ref, xp_ref, xn_ref, w_ref, o_ref, buf, *, tm, nt):
    i = pl.program_id(1)
    j = pl.program_id(2)
    buf[0:8, :] = jnp.where(i == 0, 0.0, xp_ref[0])
    buf[8:8 + tm, :] = x_ref[0]
    buf[8 + tm:16 + tm, :] = jnp.where(i == nt - 1, 0.0, xn_ref[0])
    acc = buf[pl.ds(8 - DN_CONV // 2, tm), :] * w_ref[0:1, :]
    for kk in range(1, DN_CONV):
        acc = acc + buf[pl.ds(8 - DN_CONV // 2 + kk, tm), :] * w_ref[kk:kk + 1, :]
    y = _silu(acc)
    q_scale = jnp.where(j == 0, DN_DK ** -0.5, 1.0)
    for h in range(DN_HEADS):
        yh = y[:, h * DN_DK:(h + 1) * DN_DK]
        r = lax.rsqrt(jnp.sum(yh * yh, axis=-1, keepdims=True) + NORM_EPS) * q_scale
        o_ref[0, :, h * DN_DK:(h + 1) * DN_DK] = yh * jnp.where(j == 2, 1.0, r)


def _dn_prep(qkv, conv_w, tm):
    B, T, W3 = qkv.shape
    tm = _tile(T, tm)
    nt = T // tm
    r8 = tm // 8
    return pl.pallas_call(
        functools.partial(_dn_prep_body, tm=tm, nt=nt),
        grid=(B, nt, 3),
        in_specs=[pl.BlockSpec((1, tm, DN_W), lambda b, i, j: (b, i, j)),
                  pl.BlockSpec((1, 8, DN_W), lambda b, i, j: (b, jnp.maximum(i * r8 - 1, 0), j)),
                  pl.BlockSpec((1, 8, DN_W), lambda b, i, j: (b, jnp.minimum((i + 1) * r8, T // 8 - 1), j)),
                  pl.BlockSpec((DN_CONV, DN_W), lambda b, i, j: (0, j))],
        out_specs=pl.BlockSpec((1, tm, DN_W), lambda b, i, j: (b, i, j)),
        out_shape=jax.ShapeDtypeStruct((B, T, W3), F32),
        scratch_shapes=[pltpu.VMEM((tm + 16, DN_W), F32)],
        compiler_params=_cparams("parallel", "parallel", "parallel"),
        name="dn_prep",
    )(qkv, qkv, qkv, conv_w)


def _heads_to_lanes(cols, width):
    return jnp.concatenate([jnp.broadcast_to(c, (c.shape[0], width)) for c in cols], axis=1)


def _block_diag(x, nblk):
    C, W = x.shape
    w = W // nblk
    t = jnp.concatenate([x] * nblk, axis=0)
    rb = lax.broadcasted_iota(jnp.int32, t.shape, 0) // C
    cb = lax.broadcasted_iota(jnp.int32, t.shape, 1) // w
    return jnp.where(rb == cb, t, jnp.zeros_like(t))


def _dn_chunk_body(q_ref, k_ref, v_ref, ab_ref, abt_ref, arow_ref, drow_ref, acol_ref, dcol_ref,
                   uw_ref, qk_ref, gl_ref, *, G):
    C, H = DN_CHUNK, DN_HEADS
    Tg = G * C
    gl_ref[...] = jnp.zeros_like(gl_ref)
    ab = ab_ref[0]
    abt = abt_ref[0]
    g_all = -jnp.exp(arow_ref[...]) * _softplus(ab + drow_ref[...])
    gt_all = -jnp.exp(acol_ref[...]) * _softplus(abt + dcol_ref[...])
    beta_all = jax.nn.sigmoid(ab)
    pos_s = lax.broadcasted_iota(jnp.int32, (Tg, LANES), 0) % C
    pos_l = lax.broadcasted_iota(jnp.int32, (2 * H * 2, Tg), 1) % C
    gc_all, gct_all = g_all, gt_all
    s = 1
    while s < C:
        gc_all = gc_all + jnp.where(pos_s >= s, pltpu.roll(gc_all, s, 0), 0.0)
        gct_all = gct_all + jnp.where(pos_l >= s, pltpu.roll(gct_all, s, 1), 0.0)
        s *= 2

    ri = lax.broadcasted_iota(jnp.int32, (C, H * C), 0)
    cj = lax.broadcasted_iota(jnp.int32, (C, H * C), 1) % C
    eye_side = (ri == cj).astype(F32)

    chains = []
    for ci in range(G):
        rows = slice(ci * C, (ci + 1) * C)
        q = q_ref[0, rows, :]
        k = k_ref[0, rows, :]
        v = v_ref[0, rows, :]
        kbd = _block_diag(k.astype(BF16), H)
        g_c, gc_f, beta_c = g_all[rows], gc_all[rows], beta_all[rows]
        gt_c, gct_f = gt_all[:, rows], gct_all[:, rows]
        gtot = gc_f[C - 1:C, :]
        gtot_t = gct_f[:, C - 1:C]
        for d in range(2):
            if d == 0:
                gc, gct = gc_f, gct_f
                incl, strict = ri >= cj, ri > cj
            else:
                gc, gct = gtot - gc_f + g_c, gtot_t - gct_f + gt_c
                incl, strict = ri <= cj, ri < cj
            lanes = [d * H + h for h in range(H)]
            gcol = [gc[:, l:l + 1] for l in lanes]
            diff = _heads_to_lanes(gcol, C) - jnp.concatenate([gct[l:l + 1, :] for l in lanes], axis=1)
            decay = jnp.where(incl, jnp.exp(jnp.where(incl, diff, 0.0)), 0.0)
            beta_b = _heads_to_lanes([beta_c[:, 2 * H + l:2 * H + l + 1] for l in lanes], DN_DK)
            egc_b = _heads_to_lanes([jnp.exp(c) for c in gcol], DN_DK)
            ekd_b = _heads_to_lanes([jnp.exp(gtot[:, l:l + 1] - gc[:, l:l + 1]) for l in lanes], DN_DK)
            kb = k * beta_b
            a_low = jnp.where(strict, _dot_nt(kb.astype(BF16), kbd) * decay, 0.0)
            qk_ref[d, 0, rows, :] = jnp.where(incl, _dot_nt(q.astype(BF16), kbd) * decay, 0.0).astype(BF16)
            uw_ref[d, 0, rows, 2 * DN_W:3 * DN_W] = (q * egc_b).astype(BF16)
            uw_ref[d, 0, rows, 3 * DN_W:4 * DN_W] = (k * ekd_b).astype(BF16)
            gl_ref[d, 0, 0, ci:ci + 1, :] = jnp.exp(gtot)
            rhs = jnp.concatenate([v * beta_b, kb * egc_b], axis=1).astype(BF16)
            chains.append(dict(d=d, rows=rows, pw=-a_low, inv=eye_side - a_low, rhs=rhs))

    n_lvl = int(math.log2(C)) - 1
    for ch in chains:
        pwb = ch["pw"].astype(BF16)
        ch["pw"] = _dot(pwb, _block_diag(pwb, H))
    for lvl in range(1, n_lvl + 1):
        for ch in chains:
            pwb = ch["pw"].astype(BF16)
            pbd = _block_diag(pwb, H)
            if lvl < n_lvl:
                st = _dot(jnp.concatenate([ch["inv"].astype(BF16), pwb], axis=0), pbd)
                ch["inv"] = ch["inv"] + st[:C]
                ch["pw"] = st[C:]
            else:
                ch["inv"] = ch["inv"] + _dot(ch["inv"].astype(BF16), pbd)
    rb = lax.broadcasted_iota(jnp.int32, (H * C, 2 * DN_W), 0) // C
    cb = (lax.broadcasted_iota(jnp.int32, (H * C, 2 * DN_W), 1) // DN_DK) % H
    for ch in chains:
        rhs_bd = jnp.where(rb == cb, jnp.concatenate([ch["rhs"]] * H, axis=0), jnp.zeros((), BF16))
        sol = _dot(ch["inv"].astype(BF16), rhs_bd)
        uw_ref[ch["d"], 0, ch["rows"], 0:2 * DN_W] = sol.astype(BF16)


def _dn_chunk(qkv, ab, abt, a_log, dt_bias, G):
    B, T, _ = qkv.shape
    Tg = G * DN_CHUNK
    ns = T // Tg
    nl = 2 * DN_HEADS
    alog = a_log.reshape(-1)
    dtb = dt_bias.reshape(-1)
    arow = jnp.zeros((1, LANES), F32).at[0, :nl].set(alog)
    drow = jnp.zeros((1, LANES), F32).at[0, :nl].set(dtb)
    acol = jnp.zeros((2 * nl, 1), F32).at[:nl, 0].set(alog)
    dcol = jnp.zeros((2 * nl, 1), F32).at[:nl, 0].set(dtb)
    full = lambda *shape: pl.BlockSpec(shape, lambda b, i: (0,) * len(shape))
    return pl.pallas_call(
        functools.partial(_dn_chunk_body, G=G),
        grid=(B, ns),
        in_specs=[pl.BlockSpec((1, Tg, DN_W), lambda b, i: (b, i, 0)),
                  pl.BlockSpec((1, Tg, DN_W), lambda b, i: (b, i, 1)),
                  pl.BlockSpec((1, Tg, DN_W), lambda b, i: (b, i, 2)),
                  pl.BlockSpec((1, Tg, LANES), lambda b, i: (b, i, 0)),
                  pl.BlockSpec((1, 2 * nl, Tg), lambda b, i: (b, 0, i)),
                  full(1, LANES), full(1, LANES), full(2 * nl, 1), full(2 * nl, 1)],
        out_specs=[pl.BlockSpec((2, 1, Tg, 4 * DN_W), lambda b, i: (0, b, i, 0)),
                   pl.BlockSpec((2, 1, Tg, DN_HEADS * DN_CHUNK), lambda b, i: (0, b, i, 0)),
                   pl.BlockSpec((2, 1, 1, 8, LANES), lambda b, i: (0, b, i, 0, 0))],
        out_shape=[jax.ShapeDtypeStruct((2, B, T, 4 * DN_W), BF16),
                   jax.ShapeDtypeStruct((2, B, T, DN_HEADS * DN_CHUNK), BF16),
                   jax.ShapeDtypeStruct((2, B, ns, 8, LANES), F32)],
        compiler_params=_cparams("parallel", "parallel"),
        name="dn_chunk",
    )(qkv, qkv, qkv, ab, abt, arow, drow, acol, dcol)


def _dn_rec_body(uw_ref, qk_ref, gl_ref, s0_ref, o_ref, sout_ref, S_ref, *, bb, nch, G):
    C, H = DN_CHUNK, DN_HEADS
    d = pl.program_id(0)
    c = pl.program_id(2)

    @pl.when(c == 0)
    def _():
        S_ref[...] = s0_ref[...]

    r = (c + d * (nch - 1 - 2 * c)) % G
    heads = [(b, h) for b in range(bb) for h in range(H)]
    ts = []
    for b, h in heads:
        wq = jnp.concatenate([uw_ref[b, :, DN_W + h * DN_DK:DN_W + (h + 1) * DN_DK],
                              uw_ref[b, :, 2 * DN_W + h * DN_DK:2 * DN_W + (h + 1) * DN_DK]], axis=0)
        ts.append(_dot(wq, S_ref[b, h].astype(BF16)))
    for (b, h), t in zip(heads, ts):
        sl = slice(h * DN_DV, (h + 1) * DN_DV)
        v_new = (uw_ref[b, :, sl].astype(F32) - t[:C]).astype(BF16)
        o_ref[b, :, sl] = t[C:] + _dot(qk_ref[b, :, h * C:(h + 1) * C], v_new)
        gl_row = gl_ref[b, 0, pl.ds(r, 1), :]
        gl = jnp.where(d == 0, gl_row[:, h:h + 1], gl_row[:, H + h:H + h + 1])
        kd = uw_ref[b, :, 3 * DN_W + h * DN_DK:3 * DN_W + (h + 1) * DN_DK]
        S_ref[b, h] = S_ref[b, h] * gl + _dot_tn(kd, v_new)

    @pl.when(c == nch - 1)
    def _():
        sout_ref[...] = S_ref[...]


def _dn_rec(uw, qk, gl, s0, bb, G):
    _, B, T, _ = uw.shape
    C = DN_CHUNK
    nch = T // C

    def cidx(d, c):
        return c + d * (nch - 1 - 2 * c)

    s_spec = pl.BlockSpec((None, bb, DN_HEADS, DN_DK, DN_DV), lambda d, b, c: (d, b, 0, 0, 0))
    return pl.pallas_call(
        functools.partial(_dn_rec_body, bb=bb, nch=nch, G=G),
        grid=(2, B // bb, nch),
        in_specs=[pl.BlockSpec((None, bb, C, 4 * DN_W), lambda d, b, c: (d, b, cidx(d, c), 0)),
                  pl.BlockSpec((None, bb, C, DN_HEADS * C), lambda d, b, c: (d, b, cidx(d, c), 0)),
                  pl.BlockSpec((None, bb, 1, 8, LANES), lambda d, b, c: (d, b, cidx(d, c) // G, 0, 0)),
                  s_spec],
        out_specs=[pl.BlockSpec((None, bb, C, DN_W), lambda d, b, c: (d, b, cidx(d, c), 0)), s_spec],
        out_shape=[jax.ShapeDtypeStruct((2, B, T, DN_W), F32),
                   jax.ShapeDtypeStruct((2, B, DN_HEADS, DN_DK, DN_DV), F32)],
        scratch_shapes=[pltpu.VMEM((bb, DN_HEADS, DN_DK, DN_DV), F32)],
        compiler_params=_cparams("parallel", "parallel", "arbitrary"),
        name="dn_rec",
    )(uw, qk, gl, s0)


def _s5_param_body(are_ref, aim_ref, ldt_ref, bre_ref, bim_ref, lam_ref, bd_ref):
    a_re = are_ref[...]
    a_im = aim_ref[...]
    dt = jnp.exp(ldt_ref[...])
    mag = jnp.exp(a_re * dt)
    lam_re = mag * jnp.cos(a_im * dt)
    lam_im = mag * jnp.sin(a_im * dt)
    den = a_re * a_re + a_im * a_im
    nr = lam_re - 1.0
    ni = lam_im
    coef_re = (nr * a_re + ni * a_im) / den
    coef_im = (ni * a_re - nr * a_im) / den
    lam_ref[0] = lam_re
    lam_ref[1] = lam_im
    b_re = bre_ref[...]
    b_im = bim_ref[...]
    for d in range(2):
        cr = coef_re[d:d + 1, :]
        ci = coef_im[d:d + 1, :]
        bd_ref[d, 0] = cr * b_re - ci * b_im
        bd_ref[d, 1] = cr * b_im + ci * b_re


def _s5_params(A_re, A_im, log_dt, B_re, B_im):
    a_re = A_re.reshape(2, S5_NSTATE)
    a_im = A_im.reshape(2, S5_NSTATE)
    ldt = jnp.repeat(log_dt, S5_STATE, axis=1)
    b_re_t = jnp.transpose(B_re, (2, 0, 1)).reshape(S5_GROUP, S5_NSTATE)
    b_im_t = jnp.transpose(B_im, (2, 0, 1)).reshape(S5_GROUP, S5_NSTATE)
    lam, bd = pl.pallas_call(
        _s5_param_body,
        out_shape=[jax.ShapeDtypeStruct((2, 2, S5_NSTATE), F32),
                   jax.ShapeDtypeStruct((2, 2, S5_GROUP, S5_NSTATE), F32)],
        name="s5_params",
    )(a_re, a_im, ldt, b_re_t, b_im_t)
    return lam, bd


def _s5_block_weights(bd, C_re, C_im):
    eye = jnp.eye(S5_GBLK, dtype=F32)
    bd6 = bd.reshape(2, 2, S5_GROUP, S5_NBLK, S5_GBLK, S5_STATE)
    w = jnp.einsum('dchjmp,lm->djlhcmp', bd6, eye)
    w_drive = w.reshape(2, S5_NBLK, S5_GBLK * S5_GROUP, 2 * S5_SBLK).astype(BF16)
    cc = jnp.stack([C_re, -C_im], axis=0).reshape(2, S5_NBLK, S5_GBLK, S5_GROUP, S5_STATE)
    cm = jnp.einsum('cjmhp,lm->jcmplh', cc, eye)
    w_read = cm.reshape(S5_NBLK, 2, S5_SBLK, S5_GBLK * S5_GROUP).astype(BF16)
    return w_drive, w_read


def _s5_scan_body(u_ref, pin_ref, pout_ref, wd_ref, wr_ref, lam_ref, h0_ref, y_ref, hout_ref,
                  xre, xim, sre, sim, hst, ytb,
                  *, B, Tc, nch, lb):
    d = pl.program_id(0)
    c = pl.program_id(1)

    @pl.when(c == 0)
    def _():
        hst[...] = h0_ref[0]

    n = Tc * B
    ub = _dot(pin_ref[...], u_ref[...].reshape(n, S5_WIDTH).astype(BF16)).astype(BF16)
    for j in range(S5_NBLK):
        drv = _dot(ub[:, j * LANES:(j + 1) * LANES], wd_ref[0, j])
        xre[:, j * S5_SBLK:(j + 1) * S5_SBLK] = drv[:, :S5_SBLK]
        xim[:, j * S5_SBLK:(j + 1) * S5_SBLK] = drv[:, S5_SBLK:]

    for lbi in range(S5_NSTATE // lb):
        ls = slice(lbi * lb, (lbi + 1) * lb)
        lr = jnp.broadcast_to(lam_ref[0, 0, :, ls], (B, lb))
        li = jnp.broadcast_to(lam_ref[1, 0, :, ls], (B, lb))

        def step(s, carry):
            hr, hi = carry
            t = s + d * (Tc - 1 - 2 * s)
            r0 = pl.multiple_of(t * B, B)
            nr = lr * hr - li * hi + xre[pl.ds(r0, B), ls]
            ni = lr * hi + li * hr + xim[pl.ds(r0, B), ls]
            sre[pl.ds(r0, B), ls] = nr.astype(BF16)
            sim[pl.ds(r0, B), ls] = ni.astype(BF16)
            return nr, ni

        hr, hi = lax.fori_loop(0, Tc, step, (hst[0, :, ls], hst[1, :, ls]), unroll=4)
        hst[0, :, ls] = hr
        hst[1, :, ls] = hi

    for j in range(S5_NBLK):
        ss = slice(j * S5_SBLK, (j + 1) * S5_SBLK)
        y = _dot(sre[:, ss], wr_ref[j, 0]) + _dot(sim[:, ss], wr_ref[j, 1])
        ytb[:, j * LANES:(j + 1) * LANES] = y.astype(BF16)
    y_ref[0] = _dot(pout_ref[...], ytb[...]).astype(BF16).reshape(B, Tc, S5_WIDTH)

    @pl.when(c == nch - 1)
    def _():
        hout_ref[0] = hst[...]


def _s5_scan(u, w_drive, w_read, lam, h0, Tc):
    B, T, _ = u.shape
    Tc = _tile(T, Tc)
    nch = T // Tc
    lam4 = lam.reshape(2, 2, 1, S5_NSTATE)
    n = Tc * B
    r = jnp.arange(n, dtype=jnp.int32)
    p_in = ((r[:, None] // B == r[None, :] % Tc) & (r[:, None] % B == r[None, :] // Tc)).astype(BF16)
    p_out = p_in.T

    def cidx(d, c):
        return c + d * (nch - 1 - 2 * c)

    return pl.pallas_call(
        functools.partial(_s5_scan_body, B=B, Tc=Tc, nch=nch, lb=256),
        grid=(2, nch),
        in_specs=[pl.BlockSpec((B, Tc, S5_WIDTH), lambda d, c: (0, cidx(d, c), 0)),
                  pl.BlockSpec((n, n), lambda d, c: (0, 0)),
                  pl.BlockSpec((n, n), lambda d, c: (0, 0)),
                  pl.BlockSpec((1, S5_NBLK, LANES, 2 * S5_SBLK), lambda d, c: (d, 0, 0, 0)),
                  pl.BlockSpec((S5_NBLK, 2, S5_SBLK, LANES), lambda d, c: (0, 0, 0, 0)),
                  pl.BlockSpec((2, 1, 1, S5_NSTATE), lambda d, c: (0, d, 0, 0)),
                  pl.BlockSpec((1, 2, B, S5_NSTATE), lambda d, c: (d, 0, 0, 0))],
        out_specs=[pl.BlockSpec((1, B, Tc, S5_WIDTH), lambda d, c: (d, 0, cidx(d, c), 0)),
                   pl.BlockSpec((1, 2, B, S5_NSTATE), lambda d, c: (d, 0, 0, 0))],
        out_shape=[jax.ShapeDtypeStruct((2, B, T, S5_WIDTH), BF16),
                   jax.ShapeDtypeStruct((2, 2, B, S5_NSTATE), F32)],
        scratch_shapes=[pltpu.VMEM((n, S5_NSTATE), F32),
                        pltpu.VMEM((n, S5_NSTATE), F32),
                        pltpu.VMEM((n, S5_NSTATE), BF16),
                        pltpu.VMEM((n, S5_NSTATE), BF16),
                        pltpu.VMEM((2, B, S5_NSTATE), F32),
                        pltpu.VMEM((n, S5_WIDTH), BF16)],
        compiler_params=_cparams("parallel", "arbitrary"),
        name="s5_scan",
    )(u, p_in, p_out, w_drive, w_read, lam4, h0)


def _ab_out_body(x_ref, mod_ref, o_ref, z_ref, u_ref, y_ref, ng_ref, dsk_ref, gw_ref, gb_ref,
                 woa_ref, wob_ref, g2_ref, rw_ref, rb_ref, out_ref, tab_ref, cls_ref):
    o = o_ref[0, 0] + o_ref[1, 0]
    z = z_ref[0]
    parts = []
    for h in range(DN_HEADS):
        sl = slice(h * DN_DV, (h + 1) * DN_DV)
        parts.append(_rms(o[:, sl]) * ng_ref[...] * _silu(z[:, sl]))
    a_out = jnp.concatenate(parts, axis=1)
    y = y_ref[0, 0].astype(F32) + y_ref[1, 0].astype(F32) + dsk_ref[...] * u_ref[0]
    y = _gelu_tanh(y)
    b_out = y * jax.nn.sigmoid(_dot(y.astype(BF16), gw_ref[...]) + gb_ref[...])
    mix = _dot(a_out.astype(BF16), woa_ref[...]) + _dot(b_out.astype(BF16), wob_ref[...])
    r = x_ref[0] + mod_ref[0, 2:3, :] * mix
    out_ref[0] = r
    _route_tile(r, mod_ref, g2_ref, rw_ref, rb_ref, tab_ref, cls_ref)


def _ab_out(x, mod, o_dn, z, u, y_s5, dn_norm_g, d_skip, glu_w, glu_b, w_out, norm2_g, router_w, router_bias, tm):
    B, T, D = x.shape
    tm = _tile(T, tm)
    per_batch_mod = mod.shape[0] != 1
    full = lambda *shape: pl.BlockSpec(shape, lambda b, i: (0,) * len(shape))
    r_in, r_args, r_out, r_shape = _route_io(B, T, tm, norm2_g, router_w, router_bias)
    return pl.pallas_call(
        _ab_out_body,
        grid=(B, T // tm),
        in_specs=[pl.BlockSpec((1, tm, D), lambda b, i: (b, i, 0)),
                  pl.BlockSpec((1, 6, D), (lambda b, i: (b, 0, 0)) if per_batch_mod else (lambda b, i: (0, 0, 0))),
                  pl.BlockSpec((2, 1, tm, DN_W), lambda b, i: (0, b, i, 0)),
                  pl.BlockSpec((1, tm, DN_W), lambda b, i: (b, i, 0)),
                  pl.BlockSpec((1, tm, S5_WIDTH), lambda b, i: (b, i, 0)),
                  pl.BlockSpec((2, 1, tm, S5_WIDTH), lambda b, i: (0, b, i, 0)),
                  full(1, DN_DV), full(1, S5_WIDTH), full(S5_WIDTH, S5_WIDTH), full(1, S5_WIDTH),
                  full(DN_W, D), full(S5_WIDTH, D)] + r_in,
        out_specs=[pl.BlockSpec((1, tm, D), lambda b, i: (b, i, 0))] + r_out,
        out_shape=[jax.ShapeDtypeStruct((B, T, D), F32)] + r_shape,
        compiler_params=_cparams("parallel", "parallel"),
        name="ab_out",
    )(x, mod, o_dn, z, u, y_s5, dn_norm_g.reshape(1, DN_DV), d_skip.reshape(1, S5_WIDTH),
      glu_w.astype(BF16), glu_b.reshape(1, S5_WIDTH), w_out[:DN_W].astype(BF16), w_out[DN_W:].astype(BF16),
      *r_args)


def _route(h, rw_ref, rb_ref):
    logits = _dot_nt(rw_ref[...], h, precision=HIGHEST)
    scores = jax.nn.sigmoid(logits)
    choice = scores + rb_ref[...]
    rows = [choice[e:e + 1, :] for e in range(N_EXPERTS)]
    neg_inf = jnp.float32(-jnp.inf)
    gs = []
    for g in range(N_EXPERT_GROUPS):
        r = rows[g * EXPERTS_PER_GROUP:(g + 1) * EXPERTS_PER_GROUP]
        best = None
        for a in range(EXPERTS_PER_GROUP):
            for b in range(a + 1, EXPERTS_PER_GROUP):
                s = r[a] + r[b]
                best = s if best is None else jnp.maximum(best, s)
        gs.append(best)
    best_val = gs[0]
    best_g = jnp.zeros_like(best_val, dtype=jnp.int32)
    for g in range(1, N_EXPERT_GROUPS):
        better = gs[g] > best_val
        best_val = jnp.where(better, gs[g], best_val)
        best_g = jnp.where(better, g, best_g)
    masked = [jnp.where(best_g == e // EXPERTS_PER_GROUP, rows[e], neg_inf) for e in range(N_EXPERTS)]
    m1 = masked[0]
    for e in range(1, N_EXPERTS):
        m1 = jnp.maximum(m1, masked[e])
    i1 = jnp.full_like(best_g, N_EXPERTS)
    for e in reversed(range(N_EXPERTS)):
        i1 = jnp.where(masked[e] == m1, e, i1)
    rest = [jnp.where(i1 == e, neg_inf, masked[e]) for e in range(N_EXPERTS)]
    m2 = rest[0]
    for e in range(1, N_EXPERTS):
        m2 = jnp.maximum(m2, rest[e])
    i2 = jnp.full_like(best_g, N_EXPERTS)
    for e in reversed(range(N_EXPERTS)):
        i2 = jnp.where(rest[e] == m2, e, i2)
    eidx = lax.broadcasted_iota(jnp.int32, scores.shape, 0)
    sel1 = eidx == i1
    sel2 = eidx == i2
    w1 = jnp.sum(jnp.where(sel1, scores, 0.0), axis=0, keepdims=True)
    w2 = jnp.sum(jnp.where(sel2, scores, 0.0), axis=0, keepdims=True)
    inv = 1.0 / (w1 + w2)
    return jnp.where(sel1, w1 * inv, 0.0) + jnp.where(sel2, w2 * inv, 0.0), best_g, i1, i2


def _pack_planes(x):
    planes = []
    for c in range(N_PLANES):
        lo = x[:, 2 * c * LANES:(2 * c + 1) * LANES]
        hi = x[:, (2 * c + 1) * LANES:(2 * c + 2) * LANES]
        planes.append(lax.bitcast_convert_type(pltpu.pack_elementwise([lo, hi], packed_dtype=BF16), F32))
    return planes


def _unpack_planes(planes):
    parts = []
    for p in planes:
        w = lax.bitcast_convert_type(p, jnp.uint32)
        for idx in range(2):
            parts.append(pltpu.unpack_elementwise(w, index=idx, packed_dtype=BF16, unpacked_dtype=F32))
    return jnp.concatenate(parts, axis=1)


def _route_tile(x, mod_ref, g_ref, rw_ref, rb_ref, tab_ref, cls_ref):
    h = _rms(x) * g_ref[...]
    h = h * (1.0 + mod_ref[0, 4:5, :]) + mod_ref[0, 3:4, :]
    for c, plane in enumerate(_pack_planes(h)):
        tab_ref[c] = plane
    gates_t, best_g, i1, i2 = _route(h, rw_ref, rb_ref)
    rows = []
    for k in range(EXPERTS_PER_GROUP):
        gk = jnp.zeros_like(gates_t[0:1, :])
        for g in range(N_EXPERT_GROUPS):
            e = g * EXPERTS_PER_GROUP + k
            gk = jnp.where(best_g == g, gates_t[e:e + 1, :], gk)
        rows.append(gk)
    rows.append(jnp.zeros((LANES - EXPERTS_PER_GROUP, gates_t.shape[1]), F32))
    tab_ref[N_PLANES] = jnp.transpose(jnp.concatenate(rows, axis=0))
    lo = jnp.minimum(i1, i2) % EXPERTS_PER_GROUP
    hi = jnp.maximum(i1, i2) % EXPERTS_PER_GROUP
    pair = lo * (7 - lo) // 2 + (hi - lo - 1)
    cls_ref[...] = jnp.broadcast_to(best_g * PAIRS_PER_GROUP + pair, cls_ref.shape)


def _route_io(B, T, tm, norm_g, router_w, router_bias):
    D = norm_g.shape[0]
    nt = T // tm
    N = B * T
    full = lambda *shape: pl.BlockSpec(shape, lambda b, i: (0,) * len(shape))
    in_specs = [full(1, D), full(N_EXPERTS, D), full(N_EXPERTS, 1)]
    args = [norm_g.reshape(1, D), router_w.T, router_bias.reshape(N_EXPERTS, 1)]
    out_specs = [pl.BlockSpec((N_PLANES + 1, tm, LANES), lambda b, i: (0, b * nt + i, 0)),
                 pl.BlockSpec((8, tm), lambda b, i: (0, b * nt + i))]
    out_shape = [jax.ShapeDtypeStruct((N_PLANES + 1, N, LANES), F32), jax.ShapeDtypeStruct((8, N), jnp.int32)]
    return in_specs, args, out_specs, out_shape


def _sc_move_rows(table, idx, n_out, scatter):
    n_planes = table.shape[0]
    n = idx.shape[0]
    ch = SC_CHUNK_ROWS if n % (2 * SC_WORKERS * SC_CHUNK_ROWS) == 0 else SC_CHUNK_ROWS // 2
    n_chunks = n // (SC_WORKERS * ch)
    assert n_chunks * SC_WORKERS * ch == n and n_chunks % 2 == 0
    assert n_out == n if not scatter else table.shape[1] == n
    items = [(dj, c) for dj in range(2) for c in range(n_planes)]
    mesh = plsc.VectorSubcoreMesh(core_axis_name="c", subcore_axis_name="s")

    @functools.partial(
        pl.kernel, mesh=mesh,
        out_type=jax.ShapeDtypeStruct((n_planes, n_out, LANES), table.dtype),
        scratch_types=[pltpu.VMEM((n_chunks, ch), jnp.int32),
                       pltpu.VMEM((2, ch, LANES), table.dtype),
                       pltpu.SemaphoreType.DMA((2,)), pltpu.SemaphoreType.DMA((2,))])
    def move_kernel(table_hbm, idx_hbm, out_hbm, idx_v, rows_v, isem, osem):
        wid = lax.axis_index("s") * SC_CORES + lax.axis_index("c")
        wbase = wid * (n_chunks * ch)
        pltpu.sync_copy(idx_hbm.at[pl.ds(wid * n_chunks, n_chunks)], idx_v)

        def load(j, c, slot):
            rows = idx_v.at[j] if not scatter else pl.ds(wbase + j * ch, ch)
            return pltpu.make_async_copy(table_hbm.at[c].at[rows], rows_v.at[slot], isem.at[slot])

        def store(j, c, slot):
            rows = idx_v.at[j] if scatter else pl.ds(wbase + j * ch, ch)
            return pltpu.make_async_copy(rows_v.at[slot], out_hbm.at[c].at[rows], osem.at[slot])

        load(0, 0, 0).start()

        @pl.loop(0, n_chunks, step=2)
        def _(j):
            for it, (dj, c) in enumerate(items):
                s = it % 2
                load(j + dj, c, s).wait()

                def refill(it=it, s=s):
                    if it == 0:
                        @pl.when(j > 0)
                        def _():
                            store(j - 1, n_planes - 1, 1 - s).wait()
                    else:
                        store(j + items[it - 1][0], items[it - 1][1], 1 - s).wait()
                    if it + 1 < len(items):
                        load(j + items[it + 1][0], items[it + 1][1], 1 - s).start()
                    else:
                        load(j + 2, 0, 1 - s).start()

                if it + 1 < len(items):
                    refill()
                else:
                    pl.when(j + 2 < n_chunks)(refill)
                store(j + dj, c, s).start()

        store(n_chunks - 2 + items[-2][0], items[-2][1], 0).wait()
        store(n_chunks - 1, n_planes - 1, 1).wait()

    return move_kernel(table, idx.reshape(SC_WORKERS * n_chunks, ch))


FFN_LIVE, FFN_FIRST, FFN_LAST = 1, 2, 4


def _moe_ffn_body(st_ref, se_ref, sk_ref, sf_ref, tn_ref, xs_ref, wg_ref, wu_ref, wd_ref, ys_ref, xb, gcol, acc):
    s = pl.program_id(0)
    flags = sf_ref[s]
    k = sk_ref[s]

    @pl.when((flags & FFN_FIRST) != 0)
    def _():
        keep = lax.broadcasted_iota(jnp.int32, (xs_ref.shape[1], LANES), 0) < tn_ref[st_ref[s]]
        xb[...] = _unpack_planes([jnp.where(keep, xs_ref[c], 0.0) for c in range(N_PLANES)]).astype(BF16)
        gcol[...] = jnp.where(keep, xs_ref[N_PLANES], 0.0)
        acc[...] = jnp.zeros_like(acc)

    @pl.when((flags & FFN_LIVE) != 0)
    def _():
        hb = xb[...]
        act = _silu(_dot(hb, wg_ref[0].astype(BF16))) * _dot(hb, wu_ref[0].astype(BF16))
        lane = lax.broadcasted_iota(jnp.int32, gcol.shape, 1)
        ge = jnp.sum(jnp.where(lane == k, gcol[...], 0.0), axis=1, keepdims=True)
        acc[...] += _dot((act * ge).astype(BF16), wd_ref[0].astype(BF16))

    @pl.when((flags & FFN_LAST) != 0)
    def _():
        for c, plane in enumerate(_pack_planes(acc[...])):
            ys_ref[c] = plane


def _moe_ffn(step_tile, step_expert, step_k, step_flags, tile_rows, xs, wg, wu, wd, tm):
    P = xs.shape[1]
    D = 2 * N_PLANES * LANES
    n_steps = step_tile.shape[0]

    def wspec(shape):
        return pl.BlockSpec(shape, lambda s, st, se, sk, sf, tn: (se[s], 0, 0))

    return pl.pallas_call(
        _moe_ffn_body,
        grid_spec=pltpu.PrefetchScalarGridSpec(
            num_scalar_prefetch=5,
            grid=(n_steps,),
            in_specs=[pl.BlockSpec((N_PLANES + 1, tm, LANES), lambda s, st, se, sk, sf, tn: (0, st[s], 0)),
                      wspec((1, D, D_EXPERT)), wspec((1, D, D_EXPERT)), wspec((1, D_EXPERT, D))],
            out_specs=pl.BlockSpec((N_PLANES, tm, LANES), lambda s, st, se, sk, sf, tn: (0, st[s], 0)),
            scratch_shapes=[pltpu.VMEM((tm, D), BF16), pltpu.VMEM((tm, LANES), F32), pltpu.VMEM((tm, D), F32)]),
        out_shape=jax.ShapeDtypeStruct((N_PLANES, P, LANES), F32),
        compiler_params=_cparams("arbitrary"),
        name="moe_ffn",
    )(step_tile, step_expert, step_k, step_flags, tile_rows, xs, wg, wu, wd)


def _moe_combine_body(x_ref, mod_ref, y_ref, fg_ref, out_ref, *, final_norm):
    y = _unpack_planes([y_ref[c] for c in range(N_PLANES)])
    r = x_ref[0] + mod_ref[0, 5:6, :] * y
    if final_norm:
        r = _rms(r) * fg_ref[...]
    out_ref[0] = r


def _moe_combine(x, mod, y_tok, final_g, tm):
    Bx, T, D = x.shape
    tm = _tile(T, tm)
    nt = T // tm
    per_batch_mod = mod.shape[0] != 1
    final_norm = final_g is not None
    fg = (final_g if final_norm else jnp.ones((D,), F32)).reshape(1, D)
    return pl.pallas_call(
        functools.partial(_moe_combine_body, final_norm=final_norm),
        grid=(Bx, nt),
        in_specs=[pl.BlockSpec((1, tm, D), lambda b, i: (b, i, 0)),
                  pl.BlockSpec((1, 6, D), (lambda b, i: (b, 0, 0)) if per_batch_mod else (lambda b, i: (0, 0, 0))),
                  pl.BlockSpec((N_PLANES, tm, LANES), lambda b, i: (0, b * nt + i, 0)),
                  pl.BlockSpec((1, D), lambda b, i: (0, 0))],
        out_specs=pl.BlockSpec((1, tm, D), lambda b, i: (b, i, 0)),
        out_shape=jax.ShapeDtypeStruct((Bx, T, D), F32),
        compiler_params=_cparams("parallel", "parallel"),
        name="moe_combine",
    )(x, mod, y_tok, fg)


def _moe(x, mod, table, cls_rows, wg, wu, wd, expert_base, final_g):
    Bx, T, D = x.shape
    N = Bx * T
    tm = MOE_TILE
    G, K, NP = N_EXPERT_GROUPS, EXPERTS_PER_GROUP, PAIRS_PER_GROUP
    cls = cls_rows[0]
    onehot = (cls[:, None] == jnp.arange(G * NP, dtype=jnp.int32)[None, :]).astype(jnp.int32)
    csum = jnp.cumsum(onehot, axis=0)
    c_count = csum[-1]
    g_count = c_count.reshape(G, NP).sum(axis=1)
    g_padded = (g_count + tm - 1) // tm * tm
    g_end = jnp.cumsum(g_padded)
    g_start = g_end - g_padded
    in_group = jnp.cumsum(c_count.reshape(G, NP), axis=1) - c_count.reshape(G, NP)
    c_start = (g_start[:, None] + in_group).reshape(G * NP)
    rank = jnp.sum(csum * onehot, axis=1) - 1
    pos = (jnp.sum(c_start[None, :] * onehot, axis=1) + rank).astype(jnp.int32)
    P = N + G * tm
    nt = P // tm
    tile_start = jnp.arange(nt, dtype=jnp.int32) * tm
    tile_group = jnp.minimum(jnp.sum(tile_start[:, None] >= g_end[None, :], axis=1), G - 1)
    filled = (g_start + g_count)[tile_group]
    tile_rows = jnp.clip(filled - tile_start, 0, tm).astype(jnp.int32)
    overlap = ((c_start[None, :] < tile_start[:, None] + tm) & (c_start + c_count > tile_start[:, None])
               & (c_count[None, :] > 0)).astype(jnp.int32)
    pairs = [(a, b) for a in range(K) for b in range(a + 1, K)]
    member = jnp.array([[int(k in pairs[c % NP]) for k in range(K)] for c in range(G * NP)], jnp.int32)
    used = (jnp.dot(overlap, member) > 0).astype(jnp.int32)
    seen = jnp.cumsum(used, axis=1)
    first = used * (seen == 1)
    last = used * (seen == seen[:, -1:])
    n_steps = nt * K
    order = jnp.argsort(1 - used.reshape(n_steps), stable=True).astype(jnp.int32)
    n_used = jnp.sum(used)
    live = jnp.arange(n_steps, dtype=jnp.int32) < n_used
    src = jnp.where(live, order, order[jnp.maximum(n_used - 1, 0)])
    step_tile = src // K
    step_k = src % K
    step_expert = (expert_base + tile_group[step_tile] * K + step_k).astype(jnp.int32)
    step_flags = jnp.where(live, FFN_LIVE + FFN_FIRST * first.reshape(n_steps)[src]
                           + FFN_LAST * last.reshape(n_steps)[src], 0).astype(jnp.int32)
    xs = _sc_move_rows(table, pos, P, scatter=True)
    ys = _moe_ffn(step_tile, step_expert, step_k, step_flags, tile_rows, xs, wg, wu, wd, tm)
    y_tok = _sc_move_rows(ys, pos, N, scatter=False)
    return _moe_combine(x, mod, y_tok, final_g, 512)


def _mla_in_body(x_ref, mod_ref, g_ref, win_ref, qg_ref, wn_ref, wa_ref, wb_ref, kg_ref, wk_ref, wvt_ref,
                 cos_ref, sin_ref, *rest, with_q):
    k_ref, vt_ref = rest[-2:]
    h = _rms(x_ref[0]) * g_ref[...]
    h = h * (1.0 + mod_ref[0, 1:2, :]) + mod_ref[0, 0:1, :]
    low = _dot(h.astype(BF16), win_ref[...])
    cos = cos_ref[...]
    sin = sin_ref[...]
    c0 = MLA_Q_RANK + MLA_KV_RANK
    kvb = (_rms(low[:, MLA_Q_RANK:c0]) * kg_ref[...]).astype(BF16)
    kn = _dot(kvb, wk_ref[...]).astype(BF16)
    vt_ref[0] = _dot_nt(wvt_ref[...], kvb).astype(BF16)
    kr = (low[:, c0:c0 + LANES] * cos + low[:, c0 + LANES:c0 + 2 * LANES] * sin).astype(BF16)
    for hd in range(MLA_HEADS):
        k_ref[0, :, hd * MLA_QK:hd * MLA_QK + MLA_NOPE] = kn[:, hd * MLA_NOPE:(hd + 1) * MLA_NOPE]
        k_ref[0, :, hd * MLA_QK + MLA_NOPE:(hd + 1) * MLA_QK] = kr
    if with_q:
        q_ref = rest[-3]
        qb = (_rms(low[:, :MLA_Q_RANK]) * qg_ref[...]).astype(BF16)
        qn = _dot(qb, wn_ref[...]) * MLA_Q_SCALE
        ra = _dot(qb, wa_ref[...])
        rb = _dot(qb, wb_ref[...])
        lane = lax.broadcasted_iota(jnp.int32, cos.shape, 1)
        for hd in range(MLA_HEADS):
            p = hd // 2
            rot = (ra[:, p * LANES:(p + 1) * LANES] * cos + rb[:, p * LANES:(p + 1) * LANES] * sin) * MLA_Q_SCALE
            mine = (lane < MLA_ROPE) if hd % 2 == 0 else (lane >= MLA_ROPE)
            q_ref[0, :, hd * MLA_QK:hd * MLA_QK + MLA_NOPE] = qn[:, hd * MLA_NOPE:(hd + 1) * MLA_NOPE].astype(BF16)
            q_ref[0, :, hd * MLA_QK + MLA_NOPE:(hd + 1) * MLA_QK] = jnp.where(mine, rot, 0.0).astype(BF16)


def _mla_in(x, mod, norm_g, weights, cos_t, sin_t, tm, n_keys, key_block0, kv_bufs):
    B, T, D = x.shape
    W = MLA_HEADS * MLA_V
    with_q = kv_bufs is not None
    per_batch_mod = mod.shape[0] != 1
    full = lambda a: pl.BlockSpec(a.shape, lambda b, i: (0,) * a.ndim)
    in_specs = [pl.BlockSpec((1, tm, D), lambda b, i: (b, i, 0)),
                pl.BlockSpec((1, 6, D), (lambda b, i: (b, 0, 0)) if per_batch_mod else (lambda b, i: (0, 0, 0))),
                pl.BlockSpec((1, D), lambda b, i: (0, 0))]
    in_specs += [full(w) for w in weights]
    in_specs += [pl.BlockSpec((tm, LANES), lambda b, i: (i, 0)), pl.BlockSpec((tm, LANES), lambda b, i: (i, 0))]
    args = [x, mod, norm_g.reshape(1, D), *weights, cos_t, sin_t]
    out_specs = [pl.BlockSpec((1, tm, MLA_HEADS * MLA_QK), lambda b, i: (b, i + key_block0, 0)),
                 pl.BlockSpec((1, W, tm), lambda b, i: (b, 0, i + key_block0))]
    out_shape = [jax.ShapeDtypeStruct((B, n_keys, MLA_HEADS * MLA_QK), BF16),
                 jax.ShapeDtypeStruct((B, W, n_keys), BF16)]
    aliases = {}
    if with_q:
        in_specs += [pl.BlockSpec(memory_space=pl.ANY)] * 2
        aliases = {len(args): 1, len(args) + 1: 2}
        args += list(kv_bufs)
        out_specs.insert(0, pl.BlockSpec((1, tm, MLA_HEADS * MLA_QK), lambda b, i: (b, i, 0)))
        out_shape.insert(0, jax.ShapeDtypeStruct((B, T, MLA_HEADS * MLA_QK), BF16))
    return pl.pallas_call(
        functools.partial(_mla_in_body, with_q=with_q),
        grid=(B, T // tm),
        in_specs=in_specs, out_specs=out_specs, out_shape=out_shape,
        input_output_aliases=aliases,
        compiler_params=_cparams("parallel", "parallel"),
        name="mla_in",
    )(*args)


def _mla_attn_body(q_ref, k_ref, v_ref, o_ref):
    for h in range(MLA_HEADS):
        qk = slice(h * MLA_QK, (h + 1) * MLA_QK)
        sl = slice(h * MLA_V, (h + 1) * MLA_V)
        s = _dot_nt(q_ref[0, :, qk], k_ref[0, :, qk])
        m = jnp.max(s, axis=-1, keepdims=True)
        e = jnp.exp2(s - m)
        l = jnp.sum(e, axis=-1, keepdims=True)
        ot = _dot_nt(v_ref[0, sl, :], e.astype(BF16))
        o_ref[0, :, sl] = (jnp.transpose(ot) / l).astype(o_ref.dtype)


def _mla_attn(q, k, v, tq):
    B, T, WQ = q.shape
    W, Tk = v.shape[1], v.shape[2]
    tq = _tile(T, tq)
    return pl.pallas_call(
        _mla_attn_body,
        grid=(B, T // tq),
        in_specs=[pl.BlockSpec((1, tq, WQ), lambda b, i: (b, i, 0)),
                  pl.BlockSpec((1, Tk, WQ), lambda b, i: (b, 0, 0)),
                  pl.BlockSpec((1, W, Tk), lambda b, i: (b, 0, 0))],
        out_specs=pl.BlockSpec((1, tq, W), lambda b, i: (b, i, 0)),
        out_shape=jax.ShapeDtypeStruct((B, T, W), BF16),
        compiler_params=_cparams("parallel", "arbitrary"),
        name="mla_attn",
    )(q, k, v)


def _proj_res_body(x_ref, mod_ref, a_ref, w_ref, g2_ref, rw_ref, rb_ref, out_ref, tab_ref, cls_ref):
    r = x_ref[0] + mod_ref[0, 2:3, :] * _dot(a_ref[0], w_ref[...])
    out_ref[0] = r
    _route_tile(r, mod_ref, g2_ref, rw_ref, rb_ref, tab_ref, cls_ref)


def _proj_res(x, mod, a, w, norm2_g, router_w, router_bias, tm):
    B, T, D = x.shape
    K = a.shape[2]
    tm = _tile(T, tm)
    per_batch_mod = mod.shape[0] != 1
    r_in, r_args, r_out, r_shape = _route_io(B, T, tm, norm2_g, router_w, router_bias)
    return pl.pallas_call(
        _proj_res_body,
        grid=(B, T // tm),
        in_specs=[pl.BlockSpec((1, tm, D), lambda b, i: (b, i, 0)),
                  pl.BlockSpec((1, 6, D), (lambda b, i: (b, 0, 0)) if per_batch_mod else (lambda b, i: (0, 0, 0))),
                  pl.BlockSpec((1, tm, K), lambda b, i: (b, i, 0)),
                  pl.BlockSpec((K, D), lambda b, i: (0, 0))] + r_in,
        out_specs=[pl.BlockSpec((1, tm, D), lambda b, i: (b, i, 0))] + r_out,
        out_shape=[jax.ShapeDtypeStruct((B, T, D), F32)] + r_shape,
        compiler_params=_cparams("parallel", "parallel"),
        name="proj_res",
    )(x, mod, a, w, *r_args)


def _rope_tables(T, n_ctx):
    rows = T // GRID_W
    row = jnp.repeat(jnp.arange(rows, dtype=F32), GRID_W)
    col = jnp.tile(jnp.arange(GRID_W, dtype=F32), rows)
    n_freq = MLA_ROPE // 4
    inv = ROPE_THETA ** (-jnp.arange(n_freq, dtype=F32) / n_freq)
    ang = jnp.concatenate([row[:, None] * inv, col[:, None] * inv], axis=-1)
    cos, sin = jnp.cos(ang), jnp.sin(ang)
    cos_t = jnp.concatenate([cos, cos, cos, cos], axis=-1)
    sin_t = jnp.concatenate([-sin, sin, -sin, sin], axis=-1)
    return cos_t, sin_t, jnp.ones((n_ctx, LANES), F32), jnp.zeros((n_ctx, LANES), F32)


def _layer_ab(x, ctx, mod_l, mod_c, norm1_g, w_in, conv_w, a_log, dt_bias, dn_norm_g, A_re, A_im, log_dt,
              B_re, B_im, C_re, C_im, D_skip, glu_w, glu_b, w_out, norm2_g, router_w, router_bias):
    B, T, D = x.shape
    Tc = ctx.shape[1]
    q0, k0, v0, z0, a0, b0, u0 = 0, 512, 1024, 1536, 2048, 2056, 2064
    w_qkv = w_in[:, q0:z0].astype(BF16)
    w_z = w_in[:, z0:a0].astype(BF16)
    w_ab = jnp.zeros((D, LANES), F32).at[:, :16].set(w_in[:, a0:u0]).astype(BF16)
    w_u = w_in[:, u0:].astype(BF16)
    w_abt = w_in[:, a0:u0].T.astype(BF16)
    ws = [w_qkv, w_z, w_ab, w_abt, w_u]
    dts = [F32] * 5
    kinds = ["n", "n", "n", "t", "n"]
    dn_group = 4

    lam, bd = _s5_params(A_re, A_im, log_dt, B_re, B_im)
    w_drive, w_read = _s5_block_weights(bd, C_re, C_im)

    streams = []
    dn_state = jnp.zeros((2, B, DN_HEADS, DN_DK, DN_DV), F32)
    s5_state = jnp.zeros((2, 2, B, S5_NSTATE), F32)
    for xs, mod in ((ctx, mod_c), (x, mod_l)):
        qkv, z, ab, abt, u = _modmm(xs, mod, norm1_g, ws, dts, kinds, 0, 512)
        qkv = _dn_prep(qkv, conv_w, 512)
        uw, qk, gl = _dn_chunk(qkv, ab, abt, a_log, dt_bias, dn_group)
        o_dn, dn_state = _dn_rec(uw, qk, gl, dn_state, 16, dn_group)
        y_s5, s5_state = _s5_scan(u, w_drive, w_read, lam, s5_state, 32)
        streams.append(_ab_out(xs, mod, o_dn, z, u, y_s5, dn_norm_g, D_skip.reshape(-1), glu_w, glu_b,
                               w_out, norm2_g, router_w, router_bias, 512))
    return streams[1], streams[0]


def _layer_mla(x, ctx, mod_l, mod_c, norm1_g, w_in, q_norm_g, w_q_up, kv_norm_g, w_kv_up, w_out,
               norm2_g, router_w, router_bias, need_ctx):
    assert not need_ctx, "context attention output is only needed when a later layer follows"
    B, T, D = x.shape
    n_ctx = ctx.shape[1]
    qr, kvr = MLA_Q_RANK, MLA_KV_RANK
    half = MLA_ROPE // 2
    wk1 = w_in[:, qr + kvr:qr + kvr + half]
    wk2 = w_in[:, qr + kvr + half:]
    w_low = jnp.concatenate([w_in[:, :qr + kvr], wk1, wk2, wk1, wk2, wk2, wk1, wk2, wk1], axis=1).astype(BF16)

    wq = w_q_up.reshape(qr, MLA_HEADS, MLA_NOPE + MLA_ROPE)
    wq_n = wq[:, :, :MLA_NOPE].reshape(qr, MLA_HEADS * MLA_NOPE).astype(BF16)
    x1 = wq[:, :, MLA_NOPE:MLA_NOPE + half]
    x2 = wq[:, :, MLA_NOPE + half:]
    wq_a = jnp.concatenate([x1, x2], axis=2).reshape(qr, MLA_HEADS * MLA_ROPE).astype(BF16)
    wq_b = jnp.concatenate([x2, x1], axis=2).reshape(qr, MLA_HEADS * MLA_ROPE).astype(BF16)
    wkv = w_kv_up.reshape(kvr, MLA_HEADS, MLA_NOPE + MLA_V)
    wk_n = wkv[:, :, :MLA_NOPE].reshape(kvr, MLA_HEADS * MLA_NOPE).astype(BF16)
    wv_t = wkv[:, :, MLA_NOPE:].reshape(kvr, MLA_HEADS * MLA_V).T.astype(BF16)

    tm = _tile(math.gcd(T, n_ctx), 256)
    weights = [w_low, q_norm_g.reshape(1, qr), wq_n, wq_a, wq_b, kv_norm_g.reshape(1, kvr), wk_n, wv_t]
    cos_t, sin_t, cos_c, sin_c = _rope_tables(T, n_ctx)
    kv_ctx = _mla_in(ctx, mod_c, norm1_g, weights, cos_c, sin_c, tm, n_ctx + T, 0, None)
    q, k, vt = _mla_in(x, mod_l, norm1_g, weights, cos_t, sin_t, tm, n_ctx + T, n_ctx // tm, kv_ctx)
    o = _mla_attn(q, k, vt, 256)
    return _proj_res(x, mod_l, o, w_out.astype(BF16), norm2_g, router_w, router_bias, 512)


def kernel(x, c, ctx, c_ctx, ada_w, ada_b, norm1_g, norm2_g, ab_w_in, dn_conv_w, dn_A_log, dn_dt_bias, dn_norm_g, s5_A_re, s5_A_im, s5_log_dt, s5_B_re, s5_B_im, s5_C_re, s5_C_im, s5_D, s5_glu_w, s5_glu_b, ab_w_out, mla_w_in, mla_q_norm_g, mla_w_q_up, mla_kv_norm_g, mla_w_kv_up, mla_w_out, router_w, router_bias, moe_w_gate, moe_w_up, moe_w_down, final_norm_g):
    B, T, D = x.shape
    n_ctx = ctx.shape[1]
    depth = ada_w.shape[0]
    n_cond = -(-(B + 1) // 8) * 8
    cond = jnp.zeros((n_cond, D), F32).at[:B].set(c).at[B].set(c_ctx)
    mods = _adaln_all(cond, ada_w, ada_b).reshape(depth, n_cond, 6, D)
    n_exp = moe_w_gate.shape[1]
    wg = moe_w_gate.reshape((depth * n_exp,) + moe_w_gate.shape[2:])
    wu = moe_w_up.reshape((depth * n_exp,) + moe_w_up.shape[2:])
    wd = moe_w_down.reshape((depth * n_exp,) + moe_w_down.shape[2:])
    for i in range(depth):
        last = i == depth - 1
        j = i // 2
        mod_l = mods[i, :B]
        mod_c = mods[i, B:B + 1]
        route = (norm2_g[i], router_w, router_bias)
        if i % 2 == 0:
            lat, ctx_new = _layer_ab(x, ctx, mod_l, mod_c, norm1_g[i], ab_w_in[j], dn_conv_w[j], dn_A_log[j],
                                     dn_dt_bias[j], dn_norm_g[j], s5_A_re[j], s5_A_im[j], s5_log_dt[j],
                                     s5_B_re[j], s5_B_im[j], s5_C_re[j], s5_C_im[j], s5_D[j], s5_glu_w[j],
                                     s5_glu_b[j], ab_w_out[j], *route)
        else:
            lat = _layer_mla(x, ctx, mod_l, mod_c, norm1_g[i], mla_w_in[j], mla_q_norm_g[j], mla_w_q_up[j],
                             mla_kv_norm_g[j], mla_w_kv_up[j], mla_w_out[j], *route, not last)
            ctx_new = None
        x = _moe(lat[0], mod_l, lat[1], lat[2], wg, wu, wd, i * n_exp, final_norm_g if last else None)
        if not last:
            ctx_flat = _moe(ctx_new[0].reshape(1, B * n_ctx, D), mod_c, ctx_new[1], ctx_new[2],
                            wg, wu, wd, i * n_exp, None)
            ctx = ctx_flat.reshape(B, n_ctx, D)
    return x
```

```python
import functools
import math

import jax
import jax.numpy as jnp
from jax import lax
from jax.experimental import pallas as pl
from jax.experimental.pallas import tpu as pltpu
from jax.experimental.pallas import tpu_sc as plsc

F32 = jnp.float32
BF16 = jnp.bfloat16
HIGHEST = lax.Precision.HIGHEST

NORM_EPS = 1e-6
GRID_W = 64
ROPE_THETA = 10000.0

DN_HEADS = 4
DN_DK = 128
DN_DV = 128
DN_CONV = 5
DN_CHUNK = 64
DN_W = DN_HEADS * DN_DK

S5_WIDTH = 512
S5_GROUP = 16
S5_GROUPS = 32
S5_STATE = 64
S5_NSTATE = S5_GROUPS * S5_STATE
S5_GBLK = 8
S5_NBLK = S5_GROUPS // S5_GBLK
S5_SBLK = S5_GBLK * S5_STATE

MLA_HEADS = 8
MLA_Q_RANK = 384
MLA_KV_RANK = 256
MLA_NOPE = 128
MLA_ROPE = 64
MLA_V = 128
MLA_Q_SCALE = (MLA_NOPE + MLA_ROPE) ** -0.5 * math.log2(math.e)

N_EXPERTS = 16
N_EXPERT_GROUPS = 4
EXPERTS_PER_GROUP = 4
D_EXPERT = 512
PAIRS_PER_GROUP = EXPERTS_PER_GROUP * (EXPERTS_PER_GROUP - 1) // 2
MOE_TILE = 1024

N_PLANES = 4
SC_CORES = 2
SC_WORKERS = 32
SC_CHUNK_ROWS = 128

LANES = 128
MLA_QK = MLA_NOPE + LANES
VMEM_LIMIT_BYTES = 56 * 1024 * 1024


def _tile(n, pref):
    t = min(pref, n)
    while n % t or t % 8:
        t -= 1
    return t


def _cparams(*sem):
    return pltpu.CompilerParams(dimension_semantics=sem, vmem_limit_bytes=VMEM_LIMIT_BYTES)


def _silu(x):
    return x * jax.nn.sigmoid(x)


def _softplus(x):
    return jnp.maximum(x, 0.0) + jnp.log(1.0 + jnp.exp(-jnp.abs(x)))


def _gelu_tanh(x):
    return 0.5 * x * (1.0 + jnp.tanh(math.sqrt(2.0 / math.pi) * (x + 0.044715 * (x * x * x))))


def _rms(x):
    return x * lax.rsqrt(jnp.mean(x * x, axis=-1, keepdims=True) + NORM_EPS)


def _dot(a, b):
    return jnp.dot(a, b, preferred_element_type=F32)


def _dot_nt(a, b, precision=None):
    return lax.dot_general(a, b, (((1,), (1,)), ((), ())), preferred_element_type=F32,
                           precision=precision)


def _dot_tn(a, b):
    return lax.dot_general(a, b, (((0,), (0,)), ((), ())), preferred_element_type=F32)


def _ada_body(c_ref, w_ref, b_ref, o_ref):
    c = c_ref[...]
    o_ref[0] = _dot(_silu(c).astype(BF16), w_ref[0].astype(BF16)) + b_ref[0]


def _adaln_all(cond, ada_w, ada_b):
    L, D, D6 = ada_w.shape
    R = cond.shape[0]
    tn = 1536
    return pl.pallas_call(
        _ada_body,
        grid=(L, D6 // tn),
        in_specs=[pl.BlockSpec((R, D), lambda l, j: (0, 0)),
                  pl.BlockSpec((1, D, tn), lambda l, j: (l, 0, j)),
                  pl.BlockSpec((1, 1, tn), lambda l, j: (l, 0, j))],
        out_specs=pl.BlockSpec((1, R, tn), lambda l, j: (l, 0, j)),
        out_shape=jax.ShapeDtypeStruct((L, R, D6), F32),
        compiler_params=_cparams("parallel", "parallel"),
        name="adaln",
    )(cond, ada_w, ada_b.reshape(L, 1, D6))


def _modmm_body(x_ref, mod_ref, g_ref, *refs, kinds, shift_row):
    n_out = len(kinds)
    w_refs, o_refs = refs[:n_out], refs[n_out:]
    h = _rms(x_ref[0]) * g_ref[...]
    h = h * (1.0 + mod_ref[0, shift_row + 1:shift_row + 2, :]) + mod_ref[0, shift_row:shift_row + 1, :]
    hb = h.astype(BF16)
    for w_ref, o_ref, kind in zip(w_refs, o_refs, kinds):
        if kind == "t":
            o_ref[0] = _dot_nt(w_ref[...], hb).astype(o_ref.dtype)
        else:
            o_ref[0] = _dot(hb, w_ref[...]).astype(o_ref.dtype)


def _modmm(x, mod, g, ws, out_dtypes, kinds, shift_row, tm):
    Bx, T, D = x.shape
    tm = _tile(T, tm)
    per_batch_mod = mod.shape[0] != 1
    in_specs = [pl.BlockSpec((1, tm, D), lambda b, i: (b, i, 0)),
                pl.BlockSpec((1, 6, D), (lambda b, i: (b, 0, 0)) if per_batch_mod else (lambda b, i: (0, 0, 0))),
                pl.BlockSpec((1, D), lambda b, i: (0, 0))]
    out_specs, out_shape = [], []
    for w, dt, kind in zip(ws, out_dtypes, kinds):
        in_specs.append(pl.BlockSpec(w.shape, lambda b, i: (0, 0)))
        if kind == "t":
            n = w.shape[0]
            out_specs.append(pl.BlockSpec((1, n, tm), lambda b, i: (b, 0, i)))
            out_shape.append(jax.ShapeDtypeStruct((Bx, n, T), dt))
        else:
            n = w.shape[1]
            out_specs.append(pl.BlockSpec((1, tm, n), lambda b, i: (b, i, 0)))
            out_shape.append(jax.ShapeDtypeStruct((Bx, T, n), dt))
    return pl.pallas_call(
        functools.partial(_modmm_body, kinds=tuple(kinds), shift_row=shift_row),
        grid=(Bx, T // tm),
        in_specs=in_specs, out_specs=out_specs, out_shape=out_shape,
        compiler_params=_cparams("parallel", "parallel"),
        name="modmm",
    )(x, mod, g.reshape(1, D), *ws)


def _dn_prep_body(x_ref, xp_ref, xn_ref, w_ref, o_ref, buf, *, tm, nt):
    i = pl.program_id(1)
    j = pl.program_id(2)
    buf[0:8, :] = jnp.where(i == 0, 0.0, xp_ref[0])
    buf[8:8 + tm, :] = x_ref[0]
    buf[8 + tm:16 + tm, :] = jnp.where(i == nt - 1, 0.0, xn_ref[0])
    acc = buf[pl.ds(8 - DN_CONV // 2, tm), :] * w_ref[0:1, :]
    for kk in range(1, DN_CONV):
        acc = acc + buf[pl.ds(8 - DN_CONV // 2 + kk, tm), :] * w_ref[kk:kk + 1, :]
    y = _silu(acc)
    q_scale = jnp.where(j == 0, DN_DK ** -0.5, 1.0)
    for h in range(DN_HEADS):
        yh = y[:, h * DN_DK:(h + 1) * DN_DK]
        r = lax.rsqrt(jnp.sum(yh * yh, axis=-1, keepdims=True) + NORM_EPS) * q_scale
        o_ref[0, :, h * DN_DK:(h + 1) * DN_DK] = yh * jnp.where(j == 2, 1.0, r)


def _dn_prep(qkv, conv_w, tm):
    B, T, W3 = qkv.shape
    tm = _tile(T, tm)
    nt = T // tm
    r8 = tm // 8
    return pl.pallas_call(
        functools.partial(_dn_prep_body, tm=tm, nt=nt),
        grid=(B, nt, 3),
        in_specs=[pl.BlockSpec((1, tm, DN_W), lambda b, i, j: (b, i, j)),
                  pl.BlockSpec((1, 8, DN_W), lambda b, i, j: (b, jnp.maximum(i * r8 - 1, 0), j)),
                  pl.BlockSpec((1, 8, DN_W), lambda b, i, j: (b, jnp.minimum((i + 1) * r8, T // 8 - 1), j)),
                  pl.BlockSpec((DN_CONV, DN_W), lambda b, i, j: (0, j))],
        out_specs=pl.BlockSpec((1, tm, DN_W), lambda b, i, j: (b, i, j)),
        out_shape=jax.ShapeDtypeStruct((B, T, W3), F32),
        scratch_shapes=[pltpu.VMEM((tm + 16, DN_W), F32)],
        compiler_params=_cparams("parallel", "parallel", "parallel"),
        name="dn_prep",
    )(qkv, qkv, qkv, conv_w)


def _heads_to_lanes(cols, width):
    return jnp.concatenate([jnp.broadcast_to(c, (c.shape[0], width)) for c in cols], axis=1)


def _block_diag(x, nblk):
    C, W = x.shape
    w = W // nblk
    t = jnp.concatenate([x] * nblk, axis=0)
    rb = lax.broadcasted_iota(jnp.int32, t.shape, 0) // C
    cb = lax.broadcasted_iota(jnp.int32, t.shape, 1) // w
    return jnp.where(rb == cb, t, jnp.zeros_like(t))


def _dn_chunk_body(q_ref, k_ref, v_ref, ab_ref, abt_ref, arow_ref, drow_ref, acol_ref, dcol_ref,
                   uw_ref, qk_ref, gl_ref, *, G):
    C, H = DN_CHUNK, DN_HEADS
    Tg = G * C
    gl_ref[...] = jnp.zeros_like(gl_ref)
    ab = ab_ref[0]
    abt = abt_ref[0]
    g_all = -jnp.exp(arow_ref[...]) * _softplus(ab + drow_ref[...])
    gt_all = -jnp.exp(acol_ref[...]) * _softplus(abt + dcol_ref[...])
    beta_all = jax.nn.sigmoid(ab)
    pos_s = lax.broadcasted_iota(jnp.int32, (Tg, LANES), 0) % C
    pos_l = lax.broadcasted_iota(jnp.int32, (2 * H * 2, Tg), 1) % C
    gc_all, gct_all = g_all, gt_all
    s = 1
    while s < C:
        gc_all = gc_all + jnp.where(pos_s >= s, pltpu.roll(gc_all, s, 0), 0.0)
        gct_all = gct_all + jnp.where(pos_l >= s, pltpu.roll(gct_all, s, 1), 0.0)
        s *= 2

    ri = lax.broadcasted_iota(jnp.int32, (C, H * C), 0)
    cj = lax.broadcasted_iota(jnp.int32, (C, H * C), 1) % C
    eye_side = (ri == cj).astype(F32)

    chains = []
    for ci in range(G):
        rows = slice(ci * C, (ci + 1) * C)
        q = q_ref[0, rows, :]
        k = k_ref[0, rows, :]
        v = v_ref[0, rows, :]
        kbd = _block_diag(k.astype(BF16), H)
        g_c, gc_f, beta_c = g_all[rows], gc_all[rows], beta_all[rows]
        gt_c, gct_f = gt_all[:, rows], gct_all[:, rows]
        gtot = gc_f[C - 1:C, :]
        gtot_t = gct_f[:, C - 1:C]
        for d in range(2):
            if d == 0:
                gc, gct = gc_f, gct_f
                incl, strict = ri >= cj, ri > cj
            else:
                gc, gct = gtot - gc_f + g_c, gtot_t - gct_f + gt_c
                incl, strict = ri <= cj, ri < cj
            lanes = [d * H + h for h in range(H)]
            gcol = [gc[:, l:l + 1] for l in lanes]
            diff = _heads_to_lanes(gcol, C) - jnp.concatenate([gct[l:l + 1, :] for l in lanes], axis=1)
            decay = jnp.where(incl, jnp.exp(jnp.where(incl, diff, 0.0)), 0.0)
            beta_b = _heads_to_lanes([beta_c[:, 2 * H + l:2 * H + l + 1] for l in lanes], DN_DK)
            egc_b = _heads_to_lanes([jnp.exp(c) for c in gcol], DN_DK)
            ekd_b = _heads_to_lanes([jnp.exp(gtot[:, l:l + 1] - gc[:, l:l + 1]) for l in lanes], DN_DK)
            kb = k * beta_b
            a_low = jnp.where(strict, _dot_nt(kb.astype(BF16), kbd) * decay, 0.0)
            qk_ref[d, 0, rows, :] = jnp.where(incl, _dot_nt(q.astype(BF16), kbd) * decay, 0.0).astype(BF16)
            uw_ref[d, 0, rows, 2 * DN_W:3 * DN_W] = (q * egc_b).astype(BF16)
            uw_ref[d, 0, rows, 3 * DN_W:4 * DN_W] = (k * ekd_b).astype(BF16)
            gl_ref[d, 0, 0, ci:ci + 1, :] = jnp.exp(gtot)
            rhs = jnp.concatenate([v * beta_b, kb * egc_b], axis=1).astype(BF16)
            chains.append(dict(d=d, rows=rows, pw=-a_low, inv=eye_side - a_low, rhs=rhs))

    n_lvl = int(math.log2(C)) - 1
    for ch in chains:
        pwb = ch["pw"].astype(BF16)
        ch["pw"] = _dot(pwb, _block_diag(pwb, H))
    for lvl in range(1, n_lvl + 1):
        for ch in chains:
            pwb = ch["pw"].astype(BF16)
            pbd = _block_diag(pwb, H)
            if lvl < n_lvl:
                st = _dot(jnp.concatenate([ch["inv"].astype(BF16), pwb], axis=0), pbd)
                ch["inv"] = ch["inv"] + st[:C]
                ch["pw"] = st[C:]
            else:
                ch["inv"] = ch["inv"] + _dot(ch["inv"].astype(BF16), pbd)
    rb = lax.broadcasted_iota(jnp.int32, (H * C, 2 * DN_W), 0) // C
    cb = (lax.broadcasted_iota(jnp.int32, (H * C, 2 * DN_W), 1) // DN_DK) % H
    for ch in chains:
        rhs_bd = jnp.where(rb == cb, jnp.concatenate([ch["rhs"]] * H, axis=0), jnp.zeros((), BF16))
        sol = _dot(ch["inv"].astype(BF16), rhs_bd)
        uw_ref[ch["d"], 0, ch["rows"], 0:2 * DN_W] = sol.astype(BF16)


def _dn_chunk(qkv, ab, abt, a_log, dt_bias, G):
    B, T, _ = qkv.shape
    Tg = G * DN_CHUNK
    ns = T // Tg
    nl = 2 * DN_HEADS
    alog = a_log.reshape(-1)
    dtb = dt_bias.reshape(-1)
    arow = jnp.zeros((1, LANES), F32).at[0, :nl].set(alog)
    drow = jnp.zeros((1, LANES), F32).at[0, :nl].set(dtb)
    acol = jnp.zeros((2 * nl, 1), F32).at[:nl, 0].set(alog)
    dcol = jnp.zeros((2 * nl, 1), F32).at[:nl, 0].set(dtb)
    full = lambda *shape: pl.BlockSpec(shape, lambda b, i: (0,) * len(shape))
    return pl.pallas_call(
        functools.partial(_dn_chunk_body, G=G),
        grid=(B, ns),
        in_specs=[pl.BlockSpec((1, Tg, DN_W), lambda b, i: (b, i, 0)),
                  pl.BlockSpec((1, Tg, DN_W), lambda b, i: (b, i, 1)),
                  pl.BlockSpec((1, Tg, DN_W), lambda b, i: (b, i, 2)),
                  pl.BlockSpec((1, Tg, LANES), lambda b, i: (b, i, 0)),
                  pl.BlockSpec((1, 2 * nl, Tg), lambda b, i: (b, 0, i)),
                  full(1, LANES), full(1, LANES), full(2 * nl, 1), full(2 * nl, 1)],
        out_specs=[pl.BlockSpec((2, 1, Tg, 4 * DN_W), lambda b, i: (0, b, i, 0)),
                   pl.BlockSpec((2, 1, Tg, DN_HEADS * DN_CHUNK), lambda b, i: (0, b, i, 0)),
                   pl.BlockSpec((2, 1, 1, 8, LANES), lambda b, i: (0, b, i, 0, 0))],
        out_shape=[jax.ShapeDtypeStruct((2, B, T, 4 * DN_W), BF16),
                   jax.ShapeDtypeStruct((2, B, T, DN_HEADS * DN_CHUNK), BF16),
                   jax.ShapeDtypeStruct((2, B, ns, 8, LANES), F32)],
        compiler_params=_cparams("parallel", "parallel"),
        name="dn_chunk",
    )(qkv, qkv, qkv, ab, abt, arow, drow, acol, dcol)


def _dn_rec_body(uw_ref, qk_ref, gl_ref, s0_ref, o_ref, sout_ref, S_ref, *, bb, nch, G):
    C, H = DN_CHUNK, DN_HEADS
    d = pl.program_id(0)
    c = pl.program_id(2)

    @pl.when(c == 0)
    def _():
        S_ref[...] = s0_ref[...]

    r = (c + d * (nch - 1 - 2 * c)) % G
    heads = [(b, h) for b in range(bb) for h in range(H)]
    ts = []
    for b, h in heads:
        wq = jnp.concatenate([uw_ref[b, :, DN_W + h * DN_DK:DN_W + (h + 1) * DN_DK],
                              uw_ref[b, :, 2 * DN_W + h * DN_DK:2 * DN_W + (h + 1) * DN_DK]], axis=0)
        ts.append(_dot(wq, S_ref[b, h].astype(BF16)))
    for (b, h), t in zip(heads, ts):
        sl = slice(h * DN_DV, (h + 1) * DN_DV)
        v_new = (uw_ref[b, :, sl].astype(F32) - t[:C]).astype(BF16)
        o_ref[b, :, sl] = t[C:] + _dot(qk_ref[b, :, h * C:(h + 1) * C], v_new)
        gl_row = gl_ref[b, 0, pl.ds(r, 1), :]
        gl = jnp.where(d == 0, gl_row[:, h:h + 1], gl_row[:, H + h:H + h + 1])
        kd = uw_ref[b, :, 3 * DN_W + h * DN_DK:3 * DN_W + (h + 1) * DN_DK]
        S_ref[b, h] = S_ref[b, h] * gl + _dot_tn(kd, v_new)

    @pl.when(c == nch - 1)
    def _():
        sout_ref[...] = S_ref[...]


def _dn_rec(uw, qk, gl, s0, bb, G):
    _, B, T, _ = uw.shape
    C = DN_CHUNK
    nch = T // C

    def cidx(d, c):
        return c + d * (nch - 1 - 2 * c)

    s_spec = pl.BlockSpec((None, bb, DN_HEADS, DN_DK, DN_DV), lambda d, b, c: (d, b, 0, 0, 0))
    return pl.pallas_call(
        functools.partial(_dn_rec_body, bb=bb, nch=nch, G=G),
        grid=(2, B // bb, nch),
        in_specs=[pl.BlockSpec((None, bb, C, 4 * DN_W), lambda d, b, c: (d, b, cidx(d, c), 0)),
                  pl.BlockSpec((None, bb, C, DN_HEADS * C), lambda d, b, c: (d, b, cidx(d, c), 0)),
                  pl.BlockSpec((None, bb, 1, 8, LANES), lambda d, b, c: (d, b, cidx(d, c) // G, 0, 0)),
                  s_spec],
        out_specs=[pl.BlockSpec((None, bb, C, DN_W), lambda d, b, c: (d, b, cidx(d, c), 0)), s_spec],
        out_shape=[jax.ShapeDtypeStruct((2, B, T, DN_W), F32),
                   jax.ShapeDtypeStruct((2, B, DN_HEADS, DN_DK, DN_DV), F32)],
        scratch_shapes=[pltpu.VMEM((bb, DN_HEADS, DN_DK, DN_DV), F32)],
        compiler_params=_cparams("parallel", "parallel", "arbitrary"),
        name="dn_rec",
    )(uw, qk, gl, s0)


def _s5_param_body(are_ref, aim_ref, ldt_ref, bre_ref, bim_ref, lam_ref, bd_ref):
    a_re = are_ref[...]
    a_im = aim_ref[...]
    dt = jnp.exp(ldt_ref[...])
    mag = jnp.exp(a_re * dt)
    lam_re = mag * jnp.cos(a_im * dt)
    lam_im = mag * jnp.sin(a_im * dt)
    den = a_re * a_re + a_im * a_im
    nr = lam_re - 1.0
    ni = lam_im
    coef_re = (nr * a_re + ni * a_im) / den
    coef_im = (ni * a_re - nr * a_im) / den
    lam_ref[0] = lam_re
    lam_ref[1] = lam_im
    b_re = bre_ref[...]
    b_im = bim_ref[...]
    for d in range(2):
        cr = coef_re[d:d + 1, :]
        ci = coef_im[d:d + 1, :]
        bd_ref[d, 0] = cr * b_re - ci * b_im
        bd_ref[d, 1] = cr * b_im + ci * b_re


def _s5_params(A_re, A_im, log_dt, B_re, B_im):
    a_re = A_re.reshape(2, S5_NSTATE)
    a_im = A_im.reshape(2, S5_NSTATE)
    ldt = jnp.repeat(log_dt, S5_STATE, axis=1)
    b_re_t = jnp.transpose(B_re, (2, 0, 1)).reshape(S5_GROUP, S5_NSTATE)
    b_im_t = jnp.transpose(B_im, (2, 0, 1)).reshape(S5_GROUP, S5_NSTATE)
    lam, bd = pl.pallas_call(
        _s5_param_body,
        out_shape=[jax.ShapeDtypeStruct((2, 2, S5_NSTATE), F32),
                   jax.ShapeDtypeStruct((2, 2, S5_GROUP, S5_NSTATE), F32)],
        name="s5_params",
    )(a_re, a_im, ldt, b_re_t, b_im_t)
    return lam, bd


def _s5_block_weights(bd, C_re, C_im):
    eye = jnp.eye(S5_GBLK, dtype=F32)
    bd6 = bd.reshape(2, 2, S5_GROUP, S5_NBLK, S5_GBLK, S5_STATE)
    w = jnp.einsum('dchjmp,lm->djlhcmp', bd6, eye)
    w_drive = w.reshape(2, S5_NBLK, S5_GBLK * S5_GROUP, 2 * S5_SBLK).astype(BF16)
    cc = jnp.stack([C_re, -C_im], axis=0).reshape(2, S5_NBLK, S5_GBLK, S5_GROUP, S5_STATE)
    cm = jnp.einsum('cjmhp,lm->jcmplh', cc, eye)
    w_read = cm.reshape(S5_NBLK, 2, S5_SBLK, S5_GBLK * S5_GROUP).astype(BF16)
    return w_drive, w_read


def _s5_scan_body(u_ref, pin_ref, pout_ref, wd_ref, wr_ref, lam_ref, h0_ref, y_ref, hout_ref,
                  xre, xim, sre, sim, hst, ytb,
                  *, B, Tc, nch, lb):
    d = pl.program_id(0)
    c = pl.program_id(1)

    @pl.when(c == 0)
    def _():
        hst[...] = h0_ref[0]

    n = Tc * B
    ub = _dot(pin_ref[...], u_ref[...].reshape(n, S5_WIDTH).astype(BF16)).astype(BF16)
    for j in range(S5_NBLK):
        drv = _dot(ub[:, j * LANES:(j + 1) * LANES], wd_ref[0, j])
        xre[:, j * S5_SBLK:(j + 1) * S5_SBLK] = drv[:, :S5_SBLK]
        xim[:, j * S5_SBLK:(j + 1) * S5_SBLK] = drv[:, S5_SBLK:]

    for lbi in range(S5_NSTATE // lb):
        ls = slice(lbi * lb, (lbi + 1) * lb)
        lr = jnp.broadcast_to(lam_ref[0, 0, :, ls], (B, lb))
        li = jnp.broadcast_to(lam_ref[1, 0, :, ls], (B, lb))

        def step(s, carry):
            hr, hi = carry
            t = s + d * (Tc - 1 - 2 * s)
            r0 = pl.multiple_of(t * B, B)
            nr = lr * hr - li * hi + xre[pl.ds(r0, B), ls]
            ni = lr * hi + li * hr + xim[pl.ds(r0, B), ls]
            sre[pl.ds(r0, B), ls] = nr.astype(BF16)
            sim[pl.ds(r0, B), ls] = ni.astype(BF16)
            return nr, ni

        hr, hi = lax.fori_loop(0, Tc, step, (hst[0, :, ls], hst[1, :, ls]), unroll=4)
        hst[0, :, ls] = hr
        hst[1, :, ls] = hi

    for j in range(S5_NBLK):
        ss = slice(j * S5_SBLK, (j + 1) * S5_SBLK)
        y = _dot(sre[:, ss], wr_ref[j, 0]) + _dot(sim[:, ss], wr_ref[j, 1])
        ytb[:, j * LANES:(j + 1) * LANES] = y.astype(BF16)
    y_ref[0] = _dot(pout_ref[...], ytb[...]).astype(BF16).reshape(B, Tc, S5_WIDTH)

    @pl.when(c == nch - 1)
    def _():
        hout_ref[0] = hst[...]


def _s5_scan(u, w_drive, w_read, lam, h0, Tc):
    B, T, _ = u.shape
    Tc = _tile(T, Tc)
    nch = T // Tc
    lam4 = lam.reshape(2, 2, 1, S5_NSTATE)
    n = Tc * B
    r = jnp.arange(n, dtype=jnp.int32)
    p_in = ((r[:, None] // B == r[None, :] % Tc) & (r[:, None] % B == r[None, :] // Tc)).astype(BF16)
    p_out = p_in.T

    def cidx(d, c):
        return c + d * (nch - 1 - 2 * c)

    return pl.pallas_call(
        functools.partial(_s5_scan_body, B=B, Tc=Tc, nch=nch, lb=512),
        grid=(2, nch),
        in_specs=[pl.BlockSpec((B, Tc, S5_WIDTH), lambda d, c: (0, cidx(d, c), 0)),
                  pl.BlockSpec((n, n), lambda d, c: (0, 0)),
                  pl.BlockSpec((n, n), lambda d, c: (0, 0)),
                  pl.BlockSpec((1, S5_NBLK, LANES, 2 * S5_SBLK), lambda d, c: (d, 0, 0, 0)),
                  pl.BlockSpec((S5_NBLK, 2, S5_SBLK, LANES), lambda d, c: (0, 0, 0, 0)),
                  pl.BlockSpec((2, 1, 1, S5_NSTATE), lambda d, c: (0, d, 0, 0)),
                  pl.BlockSpec((1, 2, B, S5_NSTATE), lambda d, c: (d, 0, 0, 0))],
        out_specs=[pl.BlockSpec((1, B, Tc, S5_WIDTH), lambda d, c: (d, 0, cidx(d, c), 0)),
                   pl.BlockSpec((1, 2, B, S5_NSTATE), lambda d, c: (d, 0, 0, 0))],
        out_shape=[jax.ShapeDtypeStruct((2, B, T, S5_WIDTH), BF16),
                   jax.ShapeDtypeStruct((2, 2, B, S5_NSTATE), F32)],
        scratch_shapes=[pltpu.VMEM((n, S5_NSTATE), F32),
                        pltpu.VMEM((n, S5_NSTATE), F32),
                        pltpu.VMEM((n, S5_NSTATE), BF16),
                        pltpu.VMEM((n, S5_NSTATE), BF16),
                        pltpu.VMEM((2, B, S5_NSTATE), F32),
                        pltpu.VMEM((n, S5_WIDTH), BF16)],
        compiler_params=_cparams("parallel", "arbitrary"),
        name="s5_scan",
    )(u, p_in, p_out, w_drive, w_read, lam4, h0)


def _ab_out_body(x_ref, mod_ref, o_ref, z_ref, u_ref, y_ref, ng_ref, dsk_ref, gw_ref, gb_ref,
                 woa_ref, wob_ref, g2_ref, rw_ref, rb_ref, out_ref, tab_ref, cls_ref):
    o = o_ref[0, 0] + o_ref[1, 0]
    z = z_ref[0]
    parts = []
    for h in range(DN_HEADS):
        sl = slice(h * DN_DV, (h + 1) * DN_DV)
        parts.append(_rms(o[:, sl]) * ng_ref[...] * _silu(z[:, sl]))
    a_out = jnp.concatenate(parts, axis=1)
    y = y_ref[0, 0].astype(F32) + y_ref[1, 0].astype(F32) + dsk_ref[...] * u_ref[0]
    y = _gelu_tanh(y)
    b_out = y * jax.nn.sigmoid(_dot(y.astype(BF16), gw_ref[...]) + gb_ref[...])
    mix = _dot(a_out.astype(BF16), woa_ref[...]) + _dot(b_out.astype(BF16), wob_ref[...])
    r = x_ref[0] + mod_ref[0, 2:3, :] * mix
    out_ref[0] = r
    _route_tile(r, mod_ref, g2_ref, rw_ref, rb_ref, tab_ref, cls_ref)


def _ab_out(x, mod, o_dn, z, u, y_s5, dn_norm_g, d_skip, glu_w, glu_b, w_out, norm2_g, router_w, router_bias, tm):
    B, T, D = x.shape
    tm = _tile(T, tm)
    per_batch_mod = mod.shape[0] != 1
    full = lambda *shape: pl.BlockSpec(shape, lambda b, i: (0,) * len(shape))
    r_in, r_args, r_out, r_shape = _route_io(B, T, tm, norm2_g, router_w, router_bias)
    return pl.pallas_call(
        _ab_out_body,
        grid=(B, T // tm),
        in_specs=[pl.BlockSpec((1, tm, D), lambda b, i: (b, i, 0)),
                  pl.BlockSpec((1, 6, D), (lambda b, i: (b, 0, 0)) if per_batch_mod else (lambda b, i: (0, 0, 0))),
                  pl.BlockSpec((2, 1, tm, DN_W), lambda b, i: (0, b, i, 0)),
                  pl.BlockSpec((1, tm, DN_W), lambda b, i: (b, i, 0)),
                  pl.BlockSpec((1, tm, S5_WIDTH), lambda b, i: (b, i, 0)),
                  pl.BlockSpec((2, 1, tm, S5_WIDTH), lambda b, i: (0, b, i, 0)),
                  full(1, DN_DV), full(1, S5_WIDTH), full(S5_WIDTH, S5_WIDTH), full(1, S5_WIDTH),
                  full(DN_W, D), full(S5_WIDTH, D)] + r_in,
        out_specs=[pl.BlockSpec((1, tm, D), lambda b, i: (b, i, 0))] + r_out,
        out_shape=[jax.ShapeDtypeStruct((B, T, D), F32)] + r_shape,
        compiler_params=_cparams("parallel", "parallel"),
        name="ab_out",
    )(x, mod, o_dn, z, u, y_s5, dn_norm_g.reshape(1, DN_DV), d_skip.reshape(1, S5_WIDTH),
      glu_w.astype(BF16), glu_b.reshape(1, S5_WIDTH), w_out[:DN_W].astype(BF16), w_out[DN_W:].astype(BF16),
      *r_args)


def _route(h, rw_ref, rb_ref):
    logits = _dot_nt(rw_ref[...], h, precision=HIGHEST)
    scores = jax.nn.sigmoid(logits)
    choice = scores + rb_ref[...]
    rows = [choice[e:e + 1, :] for e in range(N_EXPERTS)]
    neg_inf = jnp.float32(-jnp.inf)
    gs = []
    for g in range(N_EXPERT_GROUPS):
        r = rows[g * EXPERTS_PER_GROUP:(g + 1) * EXPERTS_PER_GROUP]
        best = None
        for a in range(EXPERTS_PER_GROUP):
            for b in range(a + 1, EXPERTS_PER_GROUP):
                s = r[a] + r[b]
                best = s if best is None else jnp.maximum(best, s)
        gs.append(best)
    best_val = gs[0]
    best_g = jnp.zeros_like(best_val, dtype=jnp.int32)
    for g in range(1, N_EXPERT_GROUPS):
        better = gs[g] > best_val
        best_val = jnp.where(better, gs[g], best_val)
        best_g = jnp.where(better, g, best_g)
    masked = [jnp.where(best_g == e // EXPERTS_PER_GROUP, rows[e], neg_inf) for e in range(N_EXPERTS)]
    m1 = masked[0]
    for e in range(1, N_EXPERTS):
        m1 = jnp.maximum(m1, masked[e])
    i1 = jnp.full_like(best_g, N_EXPERTS)
    for e in reversed(range(N_EXPERTS)):
        i1 = jnp.where(masked[e] == m1, e, i1)
    rest = [jnp.where(i1 == e, neg_inf, masked[e]) for e in range(N_EXPERTS)]
    m2 = rest[0]
    for e in range(1, N_EXPERTS):
        m2 = jnp.maximum(m2, rest[e])
    i2 = jnp.full_like(best_g, N_EXPERTS)
    for e in reversed(range(N_EXPERTS)):
        i2 = jnp.where(rest[e] == m2, e, i2)
    eidx = lax.broadcasted_iota(jnp.int32, scores.shape, 0)
    sel1 = eidx == i1
    sel2 = eidx == i2
    w1 = jnp.sum(jnp.where(sel1, scores, 0.0), axis=0, keepdims=True)
    w2 = jnp.sum(jnp.where(sel2, scores, 0.0), axis=0, keepdims=True)
    inv = 1.0 / (w1 + w2)
    return jnp.where(sel1, w1 * inv, 0.0) + jnp.where(sel2, w2 * inv, 0.0), best_g, i1, i2


def _pack_planes(x):
    planes = []
    for c in range(N_PLANES):
        lo = x[:, 2 * c * LANES:(2 * c + 1) * LANES]
        hi = x[:, (2 * c + 1) * LANES:(2 * c + 2) * LANES]
        planes.append(lax.bitcast_convert_type(pltpu.pack_elementwise([lo, hi], packed_dtype=BF16), F32))
    return planes


def _unpack_planes(planes):
    parts = []
    for p in planes:
        w = lax.bitcast_convert_type(p, jnp.uint32)
        for idx in range(2):
            parts.append(pltpu.unpack_elementwise(w, index=idx, packed_dtype=BF16, unpacked_dtype=F32))
    return jnp.concatenate(parts, axis=1)


def _route_tile(x, mod_ref, g_ref, rw_ref, rb_ref, tab_ref, cls_ref):
    h = _rms(x) * g_ref[...]
    h = h * (1.0 + mod_ref[0, 4:5, :]) + mod_ref[0, 3:4, :]
    for c, plane in enumerate(_pack_planes(h)):
        tab_ref[c] = plane
    gates_t, best_g, i1, i2 = _route(h, rw_ref, rb_ref)
    rows = []
    for k in range(EXPERTS_PER_GROUP):
        gk = jnp.zeros_like(gates_t[0:1, :])
        for g in range(N_EXPERT_GROUPS):
            e = g * EXPERTS_PER_GROUP + k
            gk = jnp.where(best_g == g, gates_t[e:e + 1, :], gk)
        rows.append(gk)
    rows.append(jnp.zeros((LANES - EXPERTS_PER_GROUP, gates_t.shape[1]), F32))
    tab_ref[N_PLANES] = jnp.transpose(jnp.concatenate(rows, axis=0))
    lo = jnp.minimum(i1, i2) % EXPERTS_PER_GROUP
    hi = jnp.maximum(i1, i2) % EXPERTS_PER_GROUP
    pair = lo * (7 - lo) // 2 + (hi - lo - 1)
    cls_ref[...] = jnp.broadcast_to(best_g * PAIRS_PER_GROUP + pair, cls_ref.shape)


def _route_io(B, T, tm, norm_g, router_w, router_bias):
    D = norm_g.shape[0]
    nt = T // tm
    N = B * T
    full = lambda *shape: pl.BlockSpec(shape, lambda b, i: (0,) * len(shape))
    in_specs = [full(1, D), full(N_EXPERTS, D), full(N_EXPERTS, 1)]
    args = [norm_g.reshape(1, D), router_w.T, router_bias.reshape(N_EXPERTS, 1)]
    out_specs = [pl.BlockSpec((N_PLANES + 1, tm, LANES), lambda b, i: (0, b * nt + i, 0)),
                 pl.BlockSpec((8, tm), lambda b, i: (0, b * nt + i))]
    out_shape = [jax.ShapeDtypeStruct((N_PLANES + 1, N, LANES), F32), jax.ShapeDtypeStruct((8, N), jnp.int32)]
    return in_specs, args, out_specs, out_shape


def _sc_move_rows(table, idx, n_out, scatter):
    n_planes = table.shape[0]
    n = idx.shape[0]
    ch = SC_CHUNK_ROWS if n % (2 * SC_WORKERS * SC_CHUNK_ROWS) == 0 else SC_CHUNK_ROWS // 2
    n_chunks = n // (SC_WORKERS * ch)
    assert n_chunks * SC_WORKERS * ch == n and n_chunks % 2 == 0
    assert n_out == n if not scatter else table.shape[1] == n
    items = [(dj, c) for dj in range(2) for c in range(n_planes)]
    mesh = plsc.VectorSubcoreMesh(core_axis_name="c", subcore_axis_name="s")

    @functools.partial(
        pl.kernel, mesh=mesh,
        out_type=jax.ShapeDtypeStruct((n_planes, n_out, LANES), table.dtype),
        scratch_types=[pltpu.VMEM((n_chunks, ch), jnp.int32),
                       pltpu.VMEM((2, ch, LANES), table.dtype),
                       pltpu.SemaphoreType.DMA((2,)), pltpu.SemaphoreType.DMA((2,))])
    def move_kernel(table_hbm, idx_hbm, out_hbm, idx_v, rows_v, isem, osem):
        wid = lax.axis_index("s") * SC_CORES + lax.axis_index("c")
        wbase = wid * (n_chunks * ch)
        pltpu.sync_copy(idx_hbm.at[pl.ds(wid * n_chunks, n_chunks)], idx_v)

        def load(j, c, slot):
            rows = idx_v.at[j] if not scatter else pl.ds(wbase + j * ch, ch)
            return pltpu.make_async_copy(table_hbm.at[c].at[rows], rows_v.at[slot], isem.at[slot])

        def store(j, c, slot):
            rows = idx_v.at[j] if scatter else pl.ds(wbase + j * ch, ch)
            return pltpu.make_async_copy(rows_v.at[slot], out_hbm.at[c].at[rows], osem.at[slot])

        load(0, 0, 0).start()

        @pl.loop(0, n_chunks, step=2)
        def _(j):
            for it, (dj, c) in enumerate(items):
                s = it % 2
                load(j + dj, c, s).wait()

                def refill(it=it, s=s):
                    if it == 0:
                        @pl.when(j > 0)
                        def _():
                            store(j - 1, n_planes - 1, 1 - s).wait()
                    else:
                        store(j + items[it - 1][0], items[it - 1][1], 1 - s).wait()
                    if it + 1 < len(items):
                        load(j + items[it + 1][0], items[it + 1][1], 1 - s).start()
                    else:
                        load(j + 2, 0, 1 - s).start()

                if it + 1 < len(items):
                    refill()
                else:
                    pl.when(j + 2 < n_chunks)(refill)
                store(j + dj, c, s).start()

        store(n_chunks - 2 + items[-2][0], items[-2][1], 0).wait()
        store(n_chunks - 1, n_planes - 1, 1).wait()

    return move_kernel(table, idx.reshape(SC_WORKERS * n_chunks, ch))


FFN_LIVE, FFN_FIRST, FFN_LAST = 1, 2, 4


def _moe_ffn_body(st_ref, se_ref, sk_ref, sf_ref, tn_ref, xs_ref, wg_ref, wu_ref, wd_ref, ys_ref, xb, gcol, acc):
    s = pl.program_id(0)
    flags = sf_ref[s]
    k = sk_ref[s]

    @pl.when((flags & FFN_FIRST) != 0)
    def _():
        keep = lax.broadcasted_iota(jnp.int32, (xs_ref.shape[1], LANES), 0) < tn_ref[st_ref[s]]
        xb[...] = _unpack_planes([jnp.where(keep, xs_ref[c], 0.0) for c in range(N_PLANES)]).astype(BF16)
        gcol[...] = jnp.where(keep, xs_ref[N_PLANES], 0.0)
        acc[...] = jnp.zeros_like(acc)

    @pl.when((flags & FFN_LIVE) != 0)
    def _():
        hb = xb[...]
        act = _silu(_dot(hb, wg_ref[0].astype(BF16))) * _dot(hb, wu_ref[0].astype(BF16))
        lane = lax.broadcasted_iota(jnp.int32, gcol.shape, 1)
        ge = jnp.sum(jnp.where(lane == k, gcol[...], 0.0), axis=1, keepdims=True)
        acc[...] += _dot((act * ge).astype(BF16), wd_ref[0].astype(BF16))

    @pl.when((flags & FFN_LAST) != 0)
    def _():
        for c, plane in enumerate(_pack_planes(acc[...])):
            ys_ref[c] = plane


def _moe_ffn(step_tile, step_expert, step_k, step_flags, tile_rows, xs, wg, wu, wd, tm):
    P = xs.shape[1]
    D = 2 * N_PLANES * LANES
    n_steps = step_tile.shape[0]

    def wspec(shape):
        return pl.BlockSpec(shape, lambda s, st, se, sk, sf, tn: (se[s], 0, 0))

    return pl.pallas_call(
        _moe_ffn_body,
        grid_spec=pltpu.PrefetchScalarGridSpec(
            num_scalar_prefetch=5,
            grid=(n_steps,),
            in_specs=[pl.BlockSpec((N_PLANES + 1, tm, LANES), lambda s, st, se, sk, sf, tn: (0, st[s], 0)),
                      wspec((1, D, D_EXPERT)), wspec((1, D, D_EXPERT)), wspec((1, D_EXPERT, D))],
            out_specs=pl.BlockSpec((N_PLANES, tm, LANES), lambda s, st, se, sk, sf, tn: (0, st[s], 0)),
            scratch_shapes=[pltpu.VMEM((tm, D), BF16), pltpu.VMEM((tm, LANES), F32), pltpu.VMEM((tm, D), F32)]),
        out_shape=jax.ShapeDtypeStruct((N_PLANES, P, LANES), F32),
        compiler_params=_cparams("arbitrary"),
        name="moe_ffn",
    )(step_tile, step_expert, step_k, step_flags, tile_rows, xs, wg, wu, wd)


def _moe_combine_body(x_ref, mod_ref, y_ref, fg_ref, out_ref, *, final_norm):
    y = _unpack_planes([y_ref[c] for c in range(N_PLANES)])
    r = x_ref[0] + mod_ref[0, 5:6, :] * y
    if final_norm:
        r = _rms(r) * fg_ref[...]
    out_ref[0] = r


def _moe_combine(x, mod, y_tok, final_g, tm):
    Bx, T, D = x.shape
    tm = _tile(T, tm)
    nt = T // tm
    per_batch_mod = mod.shape[0] != 1
    final_norm = final_g is not None
    fg = (final_g if final_norm else jnp.ones((D,), F32)).reshape(1, D)
    return pl.pallas_call(
        functools.partial(_moe_combine_body, final_norm=final_norm),
        grid=(Bx, nt),
        in_specs=[pl.BlockSpec((1, tm, D), lambda b, i: (b, i, 0)),
                  pl.BlockSpec((1, 6, D), (lambda b, i: (b, 0, 0)) if per_batch_mod else (lambda b, i: (0, 0, 0))),
                  pl.BlockSpec((N_PLANES, tm, LANES), lambda b, i: (0, b * nt + i, 0)),
                  pl.BlockSpec((1, D), lambda b, i: (0, 0))],
        out_specs=pl.BlockSpec((1, tm, D), lambda b, i: (b, i, 0)),
        out_shape=jax.ShapeDtypeStruct((Bx, T, D), F32),
        compiler_params=_cparams("parallel", "parallel"),
        name="moe_combine",
    )(x, mod, y_tok, fg)


def _moe(x, mod, table, cls_rows, wg, wu, wd, expert_base, final_g):
    Bx, T, D = x.shape
    N = Bx * T
    tm = MOE_TILE
    G, K, NP = N_EXPERT_GROUPS, EXPERTS_PER_GROUP, PAIRS_PER_GROUP
    cls = cls_rows[0]
    onehot = (cls[:, None] == jnp.arange(G * NP, dtype=jnp.int32)[None, :]).astype(jnp.int32)
    csum = jnp.cumsum(onehot, axis=0)
    c_count = csum[-1]
    g_count = c_count.reshape(G, NP).sum(axis=1)
    g_padded = (g_count + tm - 1) // tm * tm
    g_end = jnp.cumsum(g_padded)
    g_start = g_end - g_padded
    in_group = jnp.cumsum(c_count.reshape(G, NP), axis=1) - c_count.reshape(G, NP)
    c_start = (g_start[:, None] + in_group).reshape(G * NP)
    rank = jnp.sum(csum * onehot, axis=1) - 1
    pos = (jnp.sum(c_start[None, :] * onehot, axis=1) + rank).astype(jnp.int32)
    P = N + G * tm
    nt = P // tm
    tile_start = jnp.arange(nt, dtype=jnp.int32) * tm
    tile_group = jnp.minimum(jnp.sum(tile_start[:, None] >= g_end[None, :], axis=1), G - 1)
    filled = (g_start + g_count)[tile_group]
    tile_rows = jnp.clip(filled - tile_start, 0, tm).astype(jnp.int32)
    overlap = ((c_start[None, :] < tile_start[:, None] + tm) & (c_start + c_count > tile_start[:, None])
               & (c_count[None, :] > 0)).astype(jnp.int32)
    pairs = [(a, b) for a in range(K) for b in range(a + 1, K)]
    member = jnp.array([[int(k in pairs[c % NP]) for k in range(K)] for c in range(G * NP)], jnp.int32)
    used = (jnp.dot(overlap, member) > 0).astype(jnp.int32)
    seen = jnp.cumsum(used, axis=1)
    first = used * (seen == 1)
    last = used * (seen == seen[:, -1:])
    n_steps = nt * K
    order = jnp.argsort(1 - used.reshape(n_steps), stable=True).astype(jnp.int32)
    n_used = jnp.sum(used)
    live = jnp.arange(n_steps, dtype=jnp.int32) < n_used
    src = jnp.where(live, order, order[jnp.maximum(n_used - 1, 0)])
    step_tile = src // K
    step_k = src % K
    step_expert = (expert_base + tile_group[step_tile] * K + step_k).astype(jnp.int32)
    step_flags = jnp.where(live, FFN_LIVE + FFN_FIRST * first.reshape(n_steps)[src]
                           + FFN_LAST * last.reshape(n_steps)[src], 0).astype(jnp.int32)
    xs = _sc_move_rows(table, pos, P, scatter=True)
    ys = _moe_ffn(step_tile, step_expert, step_k, step_flags, tile_rows, xs, wg, wu, wd, tm)
    y_tok = _sc_move_rows(ys, pos, N, scatter=False)
    return _moe_combine(x, mod, y_tok, final_g, 512)


def _mla_in_body(x_ref, mod_ref, g_ref, win_ref, qg_ref, wn_ref, wa_ref, wb_ref, kg_ref, wk_ref, wvt_ref,
                 cos_ref, sin_ref, *rest, with_q):
    k_ref, vt_ref = rest[-2:]
    h = _rms(x_ref[0]) * g_ref[...]
    h = h * (1.0 + mod_ref[0, 1:2, :]) + mod_ref[0, 0:1, :]
    low = _dot(h.astype(BF16), win_ref[...])
    cos = cos_ref[...]
    sin = sin_ref[...]
    c0 = MLA_Q_RANK + MLA_KV_RANK
    kvb = (_rms(low[:, MLA_Q_RANK:c0]) * kg_ref[...]).astype(BF16)
    kn = _dot(kvb, wk_ref[...]).astype(BF16)
    vt_ref[0] = _dot_nt(wvt_ref[...], kvb).astype(BF16)
    kr = (low[:, c0:c0 + LANES] * cos + low[:, c0 + LANES:c0 + 2 * LANES] * sin).astype(BF16)
    for hd in range(MLA_HEADS):
        k_ref[0, :, hd * MLA_QK:hd * MLA_QK + MLA_NOPE] = kn[:, hd * MLA_NOPE:(hd + 1) * MLA_NOPE]
        k_ref[0, :, hd * MLA_QK + MLA_NOPE:(hd + 1) * MLA_QK] = kr
    if with_q:
        q_ref = rest[-3]
        qb = (_rms(low[:, :MLA_Q_RANK]) * qg_ref[...]).astype(BF16)
        qn = _dot(qb, wn_ref[...]) * MLA_Q_SCALE
        ra = _dot(qb, wa_ref[...])
        rb = _dot(qb, wb_ref[...])
        lane = lax.broadcasted_iota(jnp.int32, cos.shape, 1)
        for hd in range(MLA_HEADS):
            p = hd // 2
            rot = (ra[:, p * LANES:(p + 1) * LANES] * cos + rb[:, p * LANES:(p + 1) * LANES] * sin) * MLA_Q_SCALE
            mine = (lane < MLA_ROPE) if hd % 2 == 0 else (lane >= MLA_ROPE)
            q_ref[0, :, hd * MLA_QK:hd * MLA_QK + MLA_NOPE] = qn[:, hd * MLA_NOPE:(hd + 1) * MLA_NOPE].astype(BF16)
            q_ref[0, :, hd * MLA_QK + MLA_NOPE:(hd + 1) * MLA_QK] = jnp.where(mine, rot, 0.0).astype(BF16)


def _mla_in(x, mod, norm_g, weights, cos_t, sin_t, tm, n_keys, key_block0, kv_bufs):
    B, T, D = x.shape
    W = MLA_HEADS * MLA_V
    with_q = kv_bufs is not None
    per_batch_mod = mod.shape[0] != 1
    full = lambda a: pl.BlockSpec(a.shape, lambda b, i: (0,) * a.ndim)
    in_specs = [pl.BlockSpec((1, tm, D), lambda b, i: (b, i, 0)),
                pl.BlockSpec((1, 6, D), (lambda b, i: (b, 0, 0)) if per_batch_mod else (lambda b, i: (0, 0, 0))),
                pl.BlockSpec((1, D), lambda b, i: (0, 0))]
    in_specs += [full(w) for w in weights]
    in_specs += [pl.BlockSpec((tm, LANES), lambda b, i: (i, 0)), pl.BlockSpec((tm, LANES), lambda b, i: (i, 0))]
    args = [x, mod, norm_g.reshape(1, D), *weights, cos_t, sin_t]
    out_specs = [pl.BlockSpec((1, tm, MLA_HEADS * MLA_QK), lambda b, i: (b, i + key_block0, 0)),
                 pl.BlockSpec((1, W, tm), lambda b, i: (b, 0, i + key_block0))]
    out_shape = [jax.ShapeDtypeStruct((B, n_keys, MLA_HEADS * MLA_QK), BF16),
                 jax.ShapeDtypeStruct((B, W, n_keys), BF16)]
    aliases = {}
    if with_q:
        in_specs += [pl.BlockSpec(memory_space=pl.ANY)] * 2
        aliases = {len(args): 1, len(args) + 1: 2}
        args += list(kv_bufs)
        out_specs.insert(0, pl.BlockSpec((1, tm, MLA_HEADS * MLA_QK), lambda b, i: (b, i, 0)))
        out_shape.insert(0, jax.ShapeDtypeStruct((B, T, MLA_HEADS * MLA_QK), BF16))
    return pl.pallas_call(
        functools.partial(_mla_in_body, with_q=with_q),
        grid=(B, T // tm),
        in_specs=in_specs, out_specs=out_specs, out_shape=out_shape,
        input_output_aliases=aliases,
        compiler_params=_cparams("parallel", "parallel"),
        name="mla_in",
    )(*args)


def _mla_attn_body(q_ref, k_ref, v_ref, o_ref):
    for h in range(MLA_HEADS):
        qk = slice(h * MLA_QK, (h + 1) * MLA_QK)
        sl = slice(h * MLA_V, (h + 1) * MLA_V)
        s = _dot_nt(q_ref[0, :, qk], k_ref[0, :, qk])
        m = jnp.max(s, axis=-1, keepdims=True)
        e = jnp.exp2(s - m)
        l = jnp.sum(e, axis=-1, keepdims=True)
        ot = _dot_nt(v_ref[0, sl, :], e.astype(BF16))
        o_ref[0, :, sl] = (jnp.transpose(ot) / l).astype(o_ref.dtype)


def _mla_attn(q, k, v, tq):
    B, T, WQ = q.shape
    W, Tk = v.shape[1], v.shape[2]
    tq = _tile(T, tq)
    return pl.pallas_call(
        _mla_attn_body,
        grid=(B, T // tq),
        in_specs=[pl.BlockSpec((1, tq, WQ), lambda b, i: (b, i, 0)),
                  pl.BlockSpec((1, Tk, WQ), lambda b, i: (b, 0, 0)),
                  pl.BlockSpec((1, W, Tk), lambda b, i: (b, 0, 0))],
        out_specs=pl.BlockSpec((1, tq, W), lambda b, i: (b, i, 0)),
        out_shape=jax.ShapeDtypeStruct((B, T, W), BF16),
        compiler_params=_cparams("parallel", "arbitrary"),
        name="mla_attn",
    )(q, k, v)


def _proj_res_body(x_ref, mod_ref, a_ref, w_ref, g2_ref, rw_ref, rb_ref, out_ref, tab_ref, cls_ref):
    r = x_ref[0] + mod_ref[0, 2:3, :] * _dot(a_ref[0], w_ref[...])
    out_ref[0] = r
    _route_tile(r, mod_ref, g2_ref, rw_ref, rb_ref, tab_ref, cls_ref)


def _proj_res(x, mod, a, w, norm2_g, router_w, router_bias, tm):
    B, T, D = x.shape
    K = a.shape[2]
    tm = _tile(T, tm)
    per_batch_mod = mod.shape[0] != 1
    r_in, r_args, r_out, r_shape = _route_io(B, T, tm, norm2_g, router_w, router_bias)
    return pl.pallas_call(
        _proj_res_body,
        grid=(B, T // tm),
        in_specs=[pl.BlockSpec((1, tm, D), lambda b, i: (b, i, 0)),
                  pl.BlockSpec((1, 6, D), (lambda b, i: (b, 0, 0)) if per_batch_mod else (lambda b, i: (0, 0, 0))),
                  pl.BlockSpec((1, tm, K), lambda b, i: (b, i, 0)),
                  pl.BlockSpec((K, D), lambda b, i: (0, 0))] + r_in,
        out_specs=[pl.BlockSpec((1, tm, D), lambda b, i: (b, i, 0))] + r_out,
        out_shape=[jax.ShapeDtypeStruct((B, T, D), F32)] + r_shape,
        compiler_params=_cparams("parallel", "parallel"),
        name="proj_res",
    )(x, mod, a, w, *r_args)


def _rope_tables(T, n_ctx):
    rows = T // GRID_W
    row = jnp.repeat(jnp.arange(rows, dtype=F32), GRID_W)
    col = jnp.tile(jnp.arange(GRID_W, dtype=F32), rows)
    n_freq = MLA_ROPE // 4
    inv = ROPE_THETA ** (-jnp.arange(n_freq, dtype=F32) / n_freq)
    ang = jnp.concatenate([row[:, None] * inv, col[:, None] * inv], axis=-1)
    cos, sin = jnp.cos(ang), jnp.sin(ang)
    cos_t = jnp.concatenate([cos, cos, cos, cos], axis=-1)
    sin_t = jnp.concatenate([-sin, sin, -sin, sin], axis=-1)
    return cos_t, sin_t, jnp.ones((n_ctx, LANES), F32), jnp.zeros((n_ctx, LANES), F32)


def _layer_ab(x, ctx, mod_l, mod_c, norm1_g, w_in, conv_w, a_log, dt_bias, dn_norm_g, A_re, A_im, log_dt,
              B_re, B_im, C_re, C_im, D_skip, glu_w, glu_b, w_out, norm2_g, router_w, router_bias):
    B, T, D = x.shape
    Tc = ctx.shape[1]
    q0, k0, v0, z0, a0, b0, u0 = 0, 512, 1024, 1536, 2048, 2056, 2064
    w_qkv = w_in[:, q0:z0].astype(BF16)
    w_z = w_in[:, z0:a0].astype(BF16)
    w_ab = jnp.zeros((D, LANES), F32).at[:, :16].set(w_in[:, a0:u0]).astype(BF16)
    w_u = w_in[:, u0:].astype(BF16)
    w_abt = w_in[:, a0:u0].T.astype(BF16)
    ws = [w_qkv, w_z, w_ab, w_abt, w_u]
    dts = [F32] * 5
    kinds = ["n", "n", "n", "t", "n"]
    dn_group = 4

    lam, bd = _s5_params(A_re, A_im, log_dt, B_re, B_im)
    w_drive, w_read = _s5_block_weights(bd, C_re, C_im)

    streams = []
    dn_state = jnp.zeros((2, B, DN_HEADS, DN_DK, DN_DV), F32)
    s5_state = jnp.zeros((2, 2, B, S5_NSTATE), F32)
    for xs, mod in ((ctx, mod_c), (x, mod_l)):
        qkv, z, ab, abt, u = _modmm(xs, mod, norm1_g, ws, dts, kinds, 0, 512)
        qkv = _dn_prep(qkv, conv_w, 512)
        uw, qk, gl = _dn_chunk(qkv, ab, abt, a_log, dt_bias, dn_group)
        o_dn, dn_state = _dn_rec(uw, qk, gl, dn_state, 16, dn_group)
        y_s5, s5_state = _s5_scan(u, w_drive, w_read, lam, s5_state, 32)
        streams.append(_ab_out(xs, mod, o_dn, z, u, y_s5, dn_norm_g, D_skip.reshape(-1), glu_w, glu_b,
                               w_out, norm2_g, router_w, router_bias, 512))
    return streams[1], streams[0]


def _layer_mla(x, ctx, mod_l, mod_c, norm1_g, w_in, q_norm_g, w_q_up, kv_norm_g, w_kv_up, w_out,
               norm2_g, router_w, router_bias, need_ctx):
    assert not need_ctx, "context attention output is only needed when a later layer follows"
    B, T, D = x.shape
    n_ctx = ctx.shape[1]
    qr, kvr = MLA_Q_RANK, MLA_KV_RANK
    half = MLA_ROPE // 2
    wk1 = w_in[:, qr + kvr:qr + kvr + half]
    wk2 = w_in[:, qr + kvr + half:]
    w_low = jnp.concatenate([w_in[:, :qr + kvr], wk1, wk2, wk1, wk2, wk2, wk1, wk2, wk1], axis=1).astype(BF16)

    wq = w_q_up.reshape(qr, MLA_HEADS, MLA_NOPE + MLA_ROPE)
    wq_n = wq[:, :, :MLA_NOPE].reshape(qr, MLA_HEADS * MLA_NOPE).astype(BF16)
    x1 = wq[:, :, MLA_NOPE:MLA_NOPE + half]
    x2 = wq[:, :, MLA_NOPE + half:]
    wq_a = jnp.concatenate([x1, x2], axis=2).reshape(qr, MLA_HEADS * MLA_ROPE).astype(BF16)
    wq_b = jnp.concatenate([x2, x1], axis=2).reshape(qr, MLA_HEADS * MLA_ROPE).astype(BF16)
    wkv = w_kv_up.reshape(kvr, MLA_HEADS, MLA_NOPE + MLA_V)
    wk_n = wkv[:, :, :MLA_NOPE].reshape(kvr, MLA_HEADS * MLA_NOPE).astype(BF16)
    wv_t = wkv[:, :, MLA_NOPE:].reshape(kvr, MLA_HEADS * MLA_V).T.astype(BF16)

    tm = _tile(math.gcd(T, n_ctx), 256)
    weights = [w_low, q_norm_g.reshape(1, qr), wq_n, wq_a, wq_b, kv_norm_g.reshape(1, kvr), wk_n, wv_t]
    cos_t, sin_t, cos_c, sin_c = _rope_tables(T, n_ctx)
    kv_ctx = _mla_in(ctx, mod_c, norm1_g, weights, cos_c, sin_c, tm, n_ctx + T, 0, None)
    q, k, vt = _mla_in(x, mod_l, norm1_g, weights, cos_t, sin_t, tm, n_ctx + T, n_ctx // tm, kv_ctx)
    o = _mla_attn(q, k, vt, 256)
    return _proj_res(x, mod_l, o, w_out.astype(BF16), norm2_g, router_w, router_bias, 512)


def kernel(x, c, ctx, c_ctx, ada_w, ada_b, norm1_g, norm2_g, ab_w_in, dn_conv_w, dn_A_log, dn_dt_bias, dn_norm_g, s5_A_re, s5_A_im, s5_log_dt, s5_B_re, s5_B_im, s5_C_re, s5_C_im, s5_D, s5_glu_w, s5_glu_b, ab_w_out, mla_w_in, mla_q_norm_g, mla_w_q_up, mla_kv_norm_g, mla_w_kv_up, mla_w_out, router_w, router_bias, moe_w_gate, moe_w_up, moe_w_down, final_norm_g):
    B, T, D = x.shape
    n_ctx = ctx.shape[1]
    depth = ada_w.shape[0]
    n_cond = -(-(B + 1) // 8) * 8
    cond = jnp.zeros((n_cond, D), F32).at[:B].set(c).at[B].set(c_ctx)
    mods = _adaln_all(cond, ada_w, ada_b).reshape(depth, n_cond, 6, D)
    n_exp = moe_w_gate.shape[1]
    wg = moe_w_gate.reshape((depth * n_exp,) + moe_w_gate.shape[2:])
    wu = moe_w_up.reshape((depth * n_exp,) + moe_w_up.shape[2:])
    wd = moe_w_down.reshape((depth * n_exp,) + moe_w_down.shape[2:])
    for i in range(depth):
        last = i == depth - 1
        j = i // 2
        mod_l = mods[i, :B]
        mod_c = mods[i, B:B + 1]
        route = (norm2_g[i], router_w, router_bias)
        if i % 2 == 0:
            lat, ctx_new = _layer_ab(x, ctx, mod_l, mod_c, norm1_g[i], ab_w_in[j], dn_conv_w[j], dn_A_log[j],
                                     dn_dt_bias[j], dn_norm_g[j], s5_A_re[j], s5_A_im[j], s5_log_dt[j],
                                     s5_B_re[j], s5_B_im[j], s5_C_re[j], s5_C_im[j], s5_D[j], s5_glu_w[j],
                                     s5_glu_b[j], ab_w_out[j], *route)
        else:
            lat = _layer_mla(x, ctx, mod_l, mod_c, norm1_g[i], mla_w_in[j], mla_q_norm_g[j], mla_w_q_up[j],
                             mla_kv_norm_g[j], mla_w_kv_up[j], mla_w_out[j], *route, not last)
            ctx_new = None
        x = _moe(lat[0], mod_l, lat[1], lat[2], wg, wu, wd, i * n_exp, final_norm_g if last else None)
        if not last:
            ctx_flat = _moe(ctx_new[0].reshape(1, B * n_ctx, D), mod_c, ctx_new[1], ctx_new[2],
                            wg, wu, wd, i * n_exp, None)
            ctx = ctx_flat.reshape(B, n_ctx, D)
    return x
```
